```python
import jax, jax.numpy as jnp
from jax import lax
import numpy as np

D_MODEL = 1024
BATCH = 2
SEQ = 8192
DEPTH = 1
DEC_BATCH = 8
DEC_SEQ = 64
PAST_LEN = 1024

CHUNK = 64
LEFT_CHUNKS = 8
A_WINDOW = LEFT_CHUNKS * CHUNK
H_A = 8
DH_A = 64
MAX_REL = 256
H_B = 8
DH_NOPE = 64
DH_ROPE = 32
DV_B = 64
D_C = 256
D_FF = 2816
CONV_W = 3
ROPE_BASE = 10000.0
EPS = 1e-6
Q_BLOCK = 128
A_SCALE = DH_A ** -0.5
MLA_SCALE = (DH_NOPE + DH_ROPE) ** -0.5
IN_SIZES = (H_A * DH_A, H_A * DH_A, H_A * DH_A, H_B * DH_NOPE, H_B * DH_ROPE, D_C, DH_ROPE, D_MODEL, D_MODEL)
IN_WIDTH = 3 * H_A * DH_A + H_B * (DH_NOPE + DH_ROPE) + D_C + DH_ROPE + 2 * D_MODEL

kernel_name = "hybrid_chunk_band_mla_convffn_step"


def rmsnorm(x, w):
    xf = x.astype(jnp.float32)
    y = xf * lax.rsqrt(jnp.mean(xf * xf, axis=-1, keepdims=True) + EPS)
    return (y * w.astype(jnp.float32)).astype(x.dtype)


def rope(x, pos):
    half = DH_ROPE // 2
    inv = ROPE_BASE ** (-jnp.arange(half, dtype=jnp.float32) / half)
    ang = pos.astype(jnp.float32)[:, None] * inv[None, :]
    shape = (ang.shape[0],) + (1,) * (x.ndim - 3) + (half,)
    c = jnp.cos(ang).reshape(shape)
    s = jnp.sin(ang).reshape(shape)
    xf = x.astype(jnp.float32)
    x1, x2 = xf[..., :half], xf[..., half:]
    return jnp.concatenate([x1 * c - x2 * s, x1 * s + x2 * c], axis=-1).astype(x.dtype)


def masked_softmax(s, mask):
    if mask is not None:
        s = jnp.where(mask, s, -1e30)
    return jax.nn.softmax(s, axis=-1)


def gather_rel_bias(table, d):
    return table[:, jnp.clip(d, -MAX_REL, MAX_REL) + MAX_REL].astype(jnp.float32)


def mixer_inputs(xn, pos, p):
    B, S = xn.shape[0], xn.shape[1]
    h = xn @ p["w_in"]
    offsets = np.cumsum(IN_SIZES)[:-1].tolist()
    qa, ka, va, qn, qr, ckv, kr, ga, gb = jnp.split(h, offsets, axis=-1)
    qa = qa.reshape(B, S, H_A, DH_A)
    ka = ka.reshape(B, S, H_A, DH_A)
    va = va.reshape(B, S, H_A, DH_A)
    qn = qn.reshape(B, S, H_B, DH_NOPE)
    qr = rope(qr.reshape(B, S, H_B, DH_ROPE), pos)
    ckv = rmsnorm(ckv, p["kv_norm"])
    kr = rope(kr, pos)
    return qa, ka, va, qn, qr, ckv, kr, ga, gb


def band_attention_prompt(qa, ka, va, table):
    B, S = qa.shape[0], qa.shape[1]
    nc = S // CHUNK
    qc = qa.reshape(B, nc, CHUNK, H_A, DH_A)
    pad = ((0, 0), (LEFT_CHUNKS, 0), (0, 0), (0, 0), (0, 0))
    kp = jnp.pad(ka.reshape(B, nc, CHUNK, H_A, DH_A), pad)
    vp = jnp.pad(va.reshape(B, nc, CHUNK, H_A, DH_A), pad)
    k_band = jnp.concatenate([kp[:, j:j + nc] for j in range(LEFT_CHUNKS + 1)], axis=2)
    v_band = jnp.concatenate([vp[:, j:j + nc] for j in range(LEFT_CHUNKS + 1)], axis=2)
    valid = (jnp.arange(nc)[:, None] - LEFT_CHUNKS + jnp.arange(LEFT_CHUNKS + 1)[None, :]) >= 0
    valid = jnp.repeat(valid, CHUNK, axis=1)
    d = A_WINDOW + jnp.arange(CHUNK)[:, None] - jnp.arange((LEFT_CHUNKS + 1) * CHUNK)[None, :]
    bias = gather_rel_bias(table, d)
    s = (jnp.einsum('bnqhd,bnkhd->bnhqk', qc, k_band) * A_SCALE).astype(jnp.float32) + bias[None, None]
    pr = masked_softmax(s, valid[None, :, None, None, :])
    o = jnp.einsum('bnhqk,bnkhd->bnqhd', pr.astype(va.dtype), v_band)
    return o.reshape(B, S, H_A * DH_A)


def band_attention_sample(qa, ka, va, cache_k, cache_v, table, past_len):
    B, T = qa.shape[0], qa.shape[1]
    W = cache_k.shape[1]
    k_all = jnp.concatenate([cache_k.astype(ka.dtype), ka], axis=1)
    v_all = jnp.concatenate([cache_v.astype(va.dtype), va], axis=1)
    q_pos = past_len + jnp.arange(T)
    k_pos = jnp.concatenate([past_len - W + jnp.arange(W), past_len + jnp.arange(T)])
    bias = gather_rel_bias(table, q_pos[:, None] - k_pos[None, :])
    s = (jnp.einsum('bqhd,bkhd->bhqk', qa, k_all) * A_SCALE).astype(jnp.float32) + bias[None]
    pr = masked_softmax(s, None)
    o = jnp.einsum('bhqk,bkhd->bqhd', pr.astype(va.dtype), v_all).reshape(B, T, H_A * DH_A)
    return o, k_all[:, -W:], v_all[:, -W:]


def mla_kv(ckv, p):
    k_nope = jnp.einsum('bsc,chd->bshd', ckv, p["w_uk"])
    v = jnp.einsum('bsc,chd->bshd', ckv, p["w_uv"])
    return k_nope, v


def mla_attend(qn, qr, k_nope, kr, v, mask):
    s = jnp.einsum('bqhd,bkhd->bhqk', qn, k_nope) + jnp.einsum('bqhd,bkd->bhqk', qr, kr)
    pr = masked_softmax((s * MLA_SCALE).astype(jnp.float32), mask)
    return jnp.einsum('bhqk,bkhd->bqhd', pr.astype(v.dtype), v)


def mla_prompt(qn, qr, ckv, kr, p):
    B, S = qn.shape[0], qn.shape[1]
    nb = S // Q_BLOCK
    k_nope, v = mla_kv(ckv, p)
    k_chunk = jnp.arange(S) // CHUNK
    qn_b = qn.reshape(B, nb, Q_BLOCK, H_B, DH_NOPE).transpose(1, 0, 2, 3, 4)
    qr_b = qr.reshape(B, nb, Q_BLOCK, H_B, DH_ROPE).transpose(1, 0, 2, 3, 4)

    def block(args):
        qn_i, qr_i, i = args
        q_chunk = (i * Q_BLOCK + jnp.arange(Q_BLOCK)) // CHUNK
        mask = k_chunk[None, :] <= q_chunk[:, None]
        return mla_attend(qn_i, qr_i, k_nope, kr, v, mask)

    o = lax.map(block, (qn_b, qr_b, jnp.arange(nb)))
    return o.transpose(1, 0, 2, 3, 4).reshape(B, S, H_B * DV_B)


def mla_sample(qn, qr, ckv, kr, cache_ckv, cache_kr, p):
    B, T = qn.shape[0], qn.shape[1]
    c_all = jnp.concatenate([cache_ckv.astype(ckv.dtype), ckv], axis=1)
    kr_all = jnp.concatenate([cache_kr.astype(kr.dtype), kr], axis=1)
    k_nope, v = mla_kv(c_all, p)
    o = mla_attend(qn, qr, k_nope, kr_all, v, None)
    return o.reshape(B, T, H_B * DV_B)


def merge_out(ya, yb, ga, gb, p):
    za = ya @ p["w_branch_a"]
    zb = yb @ p["w_branch_b"]
    return (jax.nn.sigmoid(ga) * za + jax.nn.sigmoid(gb) * zb) @ p["w_out"]


def conv_ffn(xn, left, p):
    T = xn.shape[1]
    g = xn @ p["w_ffn_gate"]
    u = xn @ p["w_ffn_up"]
    gp = jnp.concatenate([left.astype(g.dtype), g], axis=1)
    c = p["conv_b"] + p["conv_w"][0] * gp[:, 0:T]
    for j in range(1, CONV_W):
        c = c + p["conv_w"][j] * gp[:, j:j + T]
    h = jax.nn.gelu(c, approximate=True) * u
    return h @ p["w_ffn_down"], gp[:, -(CONV_W - 1):]


def prompt_layer(x, p):
    B, S = x.shape[0], x.shape[1]
    pos = jnp.arange(S, dtype=jnp.int32)
    xn = rmsnorm(x, p["norm_mix_pre"])
    qa, ka, va, qn, qr, ckv, kr, ga, gb = mixer_inputs(xn, pos, p)
    ya = band_attention_prompt(qa, ka, va, p["rel_bias_table"])
    yb = mla_prompt(qn, qr, ckv, kr, p)
    x = x + rmsnorm(merge_out(ya, yb, ga, gb, p), p["norm_mix_post"])
    left = jnp.zeros((B, CONV_W - 1, D_FF), x.dtype)
    f, conv_state = conv_ffn(rmsnorm(x, p["norm_ffn_pre"]), left, p)
    x = x + rmsnorm(f, p["norm_ffn_post"])
    keep = min(A_WINDOW, S)
    return x, ka[:, -keep:], va[:, -keep:], ckv, kr, conv_state


def sample_layer(x, cache_k, cache_v, cache_ckv, cache_kr, conv_state, p):
    T = x.shape[1]
    past_len = cache_ckv.shape[1]
    pos = past_len + jnp.arange(T, dtype=jnp.int32)
    xn = rmsnorm(x, p["norm_mix_pre"])
    qa, ka, va, qn, qr, ckv, kr, ga, gb = mixer_inputs(xn, pos, p)
    ya, new_k, new_v = band_attention_sample(qa, ka, va, cache_k, cache_v, p["rel_bias_table"], past_len)
    yb = mla_sample(qn, qr, ckv, kr, cache_ckv, cache_kr, p)
    x = x + rmsnorm(merge_out(ya, yb, ga, gb, p), p["norm_mix_post"])
    f, new_conv = conv_ffn(rmsnorm(x, p["norm_ffn_pre"]), conv_state, p)
    x = x + rmsnorm(f, p["norm_ffn_post"])
    return x, new_k, new_v, ckv, kr, new_conv


def setup_inputs(seed: int = 0) -> dict:
    key = jax.random.key(seed)
    ks = jax.random.split(key, 32)
    f32 = jnp.float32

    def nrm(k, shape, scale):
        return jax.random.normal(k, shape, f32) * scale

    a_cache = min(A_WINDOW, PAST_LEN)
    return {
        "x_prompt": nrm(ks[0], (BATCH, SEQ, D_MODEL), 1.0),
        "x_sample": nrm(ks[1], (DEC_BATCH, DEC_SEQ, D_MODEL), 1.0),
        "cache_a_k": nrm(ks[2], (DEPTH, DEC_BATCH, a_cache, H_A, DH_A), 1.0),
        "cache_a_v": nrm(ks[3], (DEPTH, DEC_BATCH, a_cache, H_A, DH_A), 1.0),
        "cache_mla_ckv": nrm(ks[4], (DEPTH, DEC_BATCH, PAST_LEN, D_C), 1.0),
        "cache_mla_krope": nrm(ks[5], (DEPTH, DEC_BATCH, PAST_LEN, DH_ROPE), 1.0),
        "state_ffn_conv": nrm(ks[6], (DEPTH, DEC_BATCH, CONV_W - 1, D_FF), 1.0),
        "norm_mix_pre": 1.0 + nrm(ks[7], (DEPTH, D_MODEL), 0.1),
        "norm_mix_post": 1.0 + nrm(ks[8], (DEPTH, D_MODEL), 0.1),
        "w_in": nrm(ks[9], (DEPTH, D_MODEL, IN_WIDTH), D_MODEL ** -0.5),
        "rel_bias_table": nrm(ks[10], (DEPTH, H_A, 2 * MAX_REL + 1), 0.5),
        "kv_norm": 1.0 + nrm(ks[11], (DEPTH, D_C), 0.1),
        "w_uk": nrm(ks[12], (DEPTH, D_C, H_B, DH_NOPE), D_C ** -0.5),
        "w_uv": nrm(ks[13], (DEPTH, D_C, H_B, DV_B), D_C ** -0.5),
        "w_branch_a": nrm(ks[14], (DEPTH, H_A * DH_A, D_MODEL), (H_A * DH_A) ** -0.5),
        "w_branch_b": nrm(ks[15], (DEPTH, H_B * DV_B, D_MODEL), (H_B * DV_B) ** -0.5),
        "w_out": nrm(ks[16], (DEPTH, D_MODEL, D_MODEL), D_MODEL ** -0.5),
        "norm_ffn_pre": 1.0 + nrm(ks[17], (DEPTH, D_MODEL), 0.1),
        "norm_ffn_post": 1.0 + nrm(ks[18], (DEPTH, D_MODEL), 0.1),
        "w_ffn_gate": nrm(ks[19], (DEPTH, D_MODEL, D_FF), D_MODEL ** -0.5),
        "w_ffn_up": nrm(ks[20], (DEPTH, D_MODEL, D_FF), D_MODEL ** -0.5),
        "conv_w": nrm(ks[21], (DEPTH, CONV_W, D_FF), CONV_W ** -0.5),
        "conv_b": nrm(ks[22], (DEPTH, D_FF), 0.01),
        "w_ffn_down": nrm(ks[23], (DEPTH, D_FF, D_MODEL), D_FF ** -0.5),
    }


def reference(x_prompt, x_sample, cache_a_k, cache_a_v, cache_mla_ckv, cache_mla_krope, state_ffn_conv,
              norm_mix_pre, norm_mix_post, w_in, rel_bias_table, kv_norm, w_uk, w_uv, w_branch_a, w_branch_b,
              w_out, norm_ffn_pre, norm_ffn_post, w_ffn_gate, w_ffn_up, conv_w, conv_b, w_ffn_down):
    xp = x_prompt
    xs = x_sample
    pk, pv, pc, pr, pcv = [], [], [], [], []
    sk, sv, sc, sr, scv = [], [], [], [], []
    for l in range(DEPTH):
        p = {
            "norm_mix_pre": norm_mix_pre[l], "norm_mix_post": norm_mix_post[l], "w_in": w_in[l],
            "rel_bias_table": rel_bias_table[l], "kv_norm": kv_norm[l], "w_uk": w_uk[l], "w_uv": w_uv[l],
            "w_branch_a": w_branch_a[l], "w_branch_b": w_branch_b[l], "w_out": w_out[l],
            "norm_ffn_pre": norm_ffn_pre[l], "norm_ffn_post": norm_ffn_post[l],
            "w_ffn_gate": w_ffn_gate[l], "w_ffn_up": w_ffn_up[l], "conv_w": conv_w[l], "conv_b": conv_b[l],
            "w_ffn_down": w_ffn_down[l],
        }
        xp, k1, v1, c1, r1, cv1 = prompt_layer(xp, p)
        pk.append(k1); pv.append(v1); pc.append(c1); pr.append(r1); pcv.append(cv1)
        xs, k2, v2, c2, r2, cv2 = sample_layer(xs, cache_a_k[l], cache_a_v[l], cache_mla_ckv[l],
                                               cache_mla_krope[l], state_ffn_conv[l], p)
        sk.append(k2); sv.append(v2); sc.append(c2); sr.append(r2); scv.append(cv2)
    return (xp, xs,
            jnp.stack(pk), jnp.stack(pv), jnp.stack(pc), jnp.stack(pr), jnp.stack(pcv),
            jnp.stack(sk), jnp.stack(sv), jnp.stack(sc), jnp.stack(sr), jnp.stack(scv))
```

```python
import functools

import jax
import jax.numpy as jnp
import numpy as np
from jax import lax
from jax.experimental import pallas as pl
from jax.experimental.pallas import tpu as pltpu

D_MODEL = 1024
CHUNK = 64
LEFT_CHUNKS = 8
A_WINDOW = LEFT_CHUNKS * CHUNK
H_A = 8
DH_A = 64
MAX_REL = 256
H_B = 8
DH_NOPE = 64
DH_ROPE = 32
DV_B = 64
D_C = 256
D_FF = 2816
CONV_W = 3
ROPE_BASE = 10000.0
EPS = 1e-6
A_SCALE = DH_A ** -0.5
MLA_SCALE = (DH_NOPE + DH_ROPE) ** -0.5
IN_SIZES = (H_A * DH_A, H_A * DH_A, H_A * DH_A, H_B * DH_NOPE, H_B * DH_ROPE, D_C, DH_ROPE, D_MODEL, D_MODEL)

HEAD_PAD = 128
D_A = H_A * DH_A
D_VB = H_B * DV_B
D_QF = H_B * HEAD_PAD
NEG = -1e30
VMEM_LIMIT_V7X = 56 * 1024 * 1024

F32 = jnp.float32
BF16 = jnp.bfloat16
_NT = (((1,), (1,)), ((), ()))


def _resident(shape):
    nd = len(shape)
    return pl.BlockSpec(shape, lambda *_: (0,) * nd, pipeline_mode=pl.Buffered(1))


def _rms(x, w):
    return x * lax.rsqrt(jnp.mean(x * x, axis=-1, keepdims=True) + EPS) * w


def _rope_lanes(x, c, s):
    lane = lax.broadcasted_iota(jnp.int32, x.shape, 1)
    partner = jnp.where(lane % 32 < 16, pltpu.roll(x, 128 - 16, 1), pltpu.roll(x, 16, 1))
    return x * c + partner * s


_C_QA, _C_KA, _C_QF, _C_CKV, _C_KR, _C_GA, _C_GB, _C_END = 0, 512, 1024, 2048, 2304, 2432, 3456, 4480


def _in_proj_kernel(x_ref, nw_ref, wm_ref, wvt_ref, wva_ref, kvn_ref, cq_ref, sq_ref, ck_ref, sk_ref,
                    qa_ref, ka_ref, vat_ref, kaf_ref, vaf_ref, qf_ref, ckv_ref, kr_ref, ga_ref, gb_ref):
    i = pl.program_id(1)
    xn = _rms(x_ref[0], nw_ref[...]).astype(BF16)

    def proj(lo, hi):
        return jnp.dot(xn, wm_ref[:, lo:hi], preferred_element_type=F32)

    qa_ref[0] = proj(_C_QA, _C_KA).astype(BF16)
    ka = proj(_C_KA, _C_QF)
    ka_ref[0] = ka.astype(BF16)
    kaf_ref[0] = ka
    vat_ref[0, 0] = lax.dot_general(wvt_ref[...], xn, _NT, preferred_element_type=F32).astype(BF16)

    @pl.when(i == pl.num_programs(1) - 1)
    def _():
        vaf_ref[0] = jnp.dot(xn, wva_ref[...], preferred_element_type=F32)

    cq, sq = cq_ref[...], sq_ref[...]
    for h in range(H_B):
        lo = _C_QF + h * HEAD_PAD
        qf_ref[0, :, h * HEAD_PAD:(h + 1) * HEAD_PAD] = _rope_lanes(proj(lo, lo + HEAD_PAD), cq, sq).astype(BF16)

    ckv_ref[0] = _rms(proj(_C_CKV, _C_KR), kvn_ref[...])
    kr_ref[0] = _rope_lanes(proj(_C_KR, _C_GA), ck_ref[...], sk_ref[...])[:, :DH_ROPE]
    ga_ref[0] = proj(_C_GA, _C_GB).astype(BF16)
    gb_ref[0] = proj(_C_GB, _C_END).astype(BF16)


def _in_proj(x, nw, wm, wvt, wva, kvn, cq, sq, ck, sk, *, tm):
    g, s, _ = x.shape
    n = s // tm
    tok = lambda w: pl.BlockSpec((1, tm, w), lambda a, b: (a, b, 0))
    tab = pl.BlockSpec((tm, HEAD_PAD), lambda a, b: (b, 0))
    tail = pl.BlockSpec((1, tm, D_A), lambda a, b: (a, 0, 0))
    out_shape = (
        jax.ShapeDtypeStruct((g, s, D_A), BF16),
        jax.ShapeDtypeStruct((g, s, D_A), BF16),
        jax.ShapeDtypeStruct((g, n, D_A, tm), BF16),
        jax.ShapeDtypeStruct((g, tm, D_A), F32),
        jax.ShapeDtypeStruct((g, tm, D_A), F32),
        jax.ShapeDtypeStruct((g, s, D_QF), BF16),
        jax.ShapeDtypeStruct((g, s, D_C), F32),
        jax.ShapeDtypeStruct((g, s, DH_ROPE), F32),
        jax.ShapeDtypeStruct((g, s, D_MODEL), BF16),
        jax.ShapeDtypeStruct((g, s, D_MODEL), BF16),
    )
    out_specs = (tok(D_A), tok(D_A), pl.BlockSpec((1, 1, D_A, tm), lambda a, b: (a, b, 0, 0)), tail, tail,
                 tok(D_QF), tok(D_C), tok(DH_ROPE), tok(D_MODEL), tok(D_MODEL))
    in_specs = [tok(D_MODEL), _resident(nw.shape), _resident(wm.shape), _resident(wvt.shape), _resident(wva.shape),
                _resident(kvn.shape), tab, tab, tab, tab]
    return pl.pallas_call(
        _in_proj_kernel, grid=(g, n), in_specs=in_specs, out_specs=out_specs, out_shape=out_shape,
        compiler_params=pltpu.CompilerParams(dimension_semantics=("arbitrary", "arbitrary"),
                                             vmem_limit_bytes=VMEM_LIMIT_V7X),
        name="in_proj")(x, nw, wm, wvt, wva, kvn, cq, sq, ck, sk)


def _kv_up_kernel(ckv_ref, kr_ref, wk_ref, ek_ref, wvt_ref, kf_ref, vt_ref, *, tk):
    c = ckv_ref[0].astype(BF16)
    kf = jnp.dot(c, wk_ref[...], preferred_element_type=F32)
    kf = kf + jnp.dot(kr_ref[0].astype(BF16), ek_ref[...], preferred_element_type=F32)
    kf_ref[0] = kf.astype(BF16)
    for j in range(c.shape[0] // tk):
        vt_ref[0, j] = lax.dot_general(wvt_ref[...], c[j * tk:(j + 1) * tk], _NT,
                                       preferred_element_type=F32).astype(BF16)


def _kv_up(ckv, kr, wk, ek, wvt, *, tm, tk):
    g, s, _ = ckv.shape
    return pl.pallas_call(
        functools.partial(_kv_up_kernel, tk=tk), grid=(g, s // tm),
        in_specs=[pl.BlockSpec((1, tm, D_C), lambda a, b: (a, b, 0)),
                  pl.BlockSpec((1, tm, DH_ROPE), lambda a, b: (a, b, 0)),
                  _resident(wk.shape), _resident(ek.shape), _resident(wvt.shape)],
        out_specs=(pl.BlockSpec((1, tm, D_QF), lambda a, b: (a, b, 0)),
                   pl.BlockSpec((1, tm // tk, D_VB, tk), lambda a, b: (a, b, 0, 0))),
        out_shape=(jax.ShapeDtypeStruct((g, s, D_QF), BF16),
                   jax.ShapeDtypeStruct((g, s // tk, D_VB, tk), BF16)),
        compiler_params=pltpu.CompilerParams(dimension_semantics=("arbitrary", "arbitrary"),
                                             vmem_limit_bytes=VMEM_LIMIT_V7X),
        name="kv_up")(ckv, kr, wk, ek, wvt)


def _band_heads(q, k, vt, bias_ref, pen):
    tq = q.shape[0]
    lane = lax.broadcasted_iota(jnp.int32, (tq, 2 * DH_A), 1)
    outs = []
    for h in range(H_A):
        pair = slice((h // 2) * 2 * DH_A, (h // 2 + 1) * 2 * DH_A)
        qm = jnp.where((lane >= DH_A) == (h % 2 == 1), q[:, pair], jnp.zeros((), BF16))
        s = lax.dot_general(k[:, pair], qm, _NT, preferred_element_type=F32)
        s = s + bias_ref[h]
        if pen is not None:
            s = s + pen
        m = jnp.max(s, axis=0, keepdims=True)
        p = jnp.exp(s - m)
        l = jnp.sum(p, axis=0, keepdims=True)
        o = jnp.dot(vt[h * DH_A:(h + 1) * DH_A], p.astype(BF16), preferred_element_type=F32)
        outs.append(o / l)
    return jnp.concatenate(outs, axis=0).T


def _band_prompt_kernel(q_ref, kp_ref, kc_ref, vp_ref, vc_ref, bias_ref, o_ref, *, tm, tq):
    i = pl.program_id(1)
    nk = A_WINDOW + tq
    row = lax.broadcasted_iota(jnp.int32, (nk, tq), 0)
    for sub in range(tm // tq):
        lo = sub * tq
        k = jnp.concatenate([kp_ref[0, lo:, :], kc_ref[0, :lo + tq, :]], axis=0)
        vt = jnp.concatenate([vp_ref[0, 0, :, lo:], vc_ref[0, 0, :, :lo + tq]], axis=1)
        pen = jnp.where(row < jnp.where(i == 0, tm - lo, 0), NEG, 0.0).astype(F32)
        o_ref[0, lo:lo + tq, :] = _band_heads(q_ref[0, lo:lo + tq, :], k, vt, bias_ref, pen).astype(BF16)


def _band_prompt(qa, ka, vat, bias, *, tm, tq):
    g, s, _ = qa.shape
    prev = lambda a, b: (a, jnp.maximum(b - 1, 0), 0)
    prev4 = lambda a, b: (a, jnp.maximum(b - 1, 0), 0, 0)
    return pl.pallas_call(
        functools.partial(_band_prompt_kernel, tm=tm, tq=tq), grid=(g, s // tm),
        in_specs=[pl.BlockSpec((1, tm, D_A), lambda a, b: (a, b, 0)),
                  pl.BlockSpec((1, tm, D_A), prev), pl.BlockSpec((1, tm, D_A), lambda a, b: (a, b, 0)),
                  pl.BlockSpec((1, 1, D_A, tm), prev4), pl.BlockSpec((1, 1, D_A, tm), lambda a, b: (a, b, 0, 0)),
                  _resident(bias.shape)],
        out_specs=pl.BlockSpec((1, tm, D_A), lambda a, b: (a, b, 0)),
        out_shape=jax.ShapeDtypeStruct((g, s, D_A), BF16),
        compiler_params=pltpu.CompilerParams(dimension_semantics=("arbitrary", "arbitrary"),
                                             vmem_limit_bytes=VMEM_LIMIT_V7X),
        name="band_prompt")(qa, ka, ka, vat, vat, bias)


def _band_sample_kernel(q_ref, k_ref, vt_ref, bias_ref, o_ref):
    o_ref[0] = _band_heads(q_ref[0], k_ref[0], vt_ref[0], bias_ref, None).astype(BF16)


def _band_sample(qa, k_all, vt_all, bias):
    g, t, _ = qa.shape
    nk = k_all.shape[1]
    return pl.pallas_call(
        _band_sample_kernel, grid=(g,),
        in_specs=[pl.BlockSpec((1, t, D_A), lambda a: (a, 0, 0)),
                  pl.BlockSpec((1, nk, D_A), lambda a: (a, 0, 0)),
                  pl.BlockSpec((1, D_A, nk), lambda a: (a, 0, 0)),
                  _resident(bias.shape)],
        out_specs=pl.BlockSpec((1, t, D_A), lambda a: (a, 0, 0)),
        out_shape=jax.ShapeDtypeStruct((g, t, D_A), BF16),
        compiler_params=pltpu.CompilerParams(dimension_semantics=("arbitrary",),
                                             vmem_limit_bytes=VMEM_LIMIT_V7X),
        name="band_sample")(qa, k_all, vt_all, bias)


def _mla_kernel(q_ref, k_ref, vt_ref, mask_ref, o_ref, *, tq, tk, causal):
    nq = q_ref.shape[1] // tq
    nk = k_ref.shape[1] // tk

    def q_tile(i, _):
        qrow = pl.multiple_of(i * tq, tq)
        qs = [q_ref[0, pl.ds(qrow, tq), hl * HEAD_PAD:(hl + 1) * HEAD_PAD] for hl in range(2)]

        def step(kk, carry, masked):
            krow = pl.multiple_of(kk * tk, tk)
            new = []
            for hl in range(2):
                m, l, acc = carry[3 * hl:3 * hl + 3]
                k = k_ref[0, pl.ds(krow, tk), hl * HEAD_PAD:(hl + 1) * HEAD_PAD]
                s = lax.dot_general(k, qs[hl], _NT, preferred_element_type=F32)
                if masked:
                    s = s + mask_ref[...]
                m_new = jnp.maximum(m, jnp.max(s, axis=0, keepdims=True))
                alpha = jnp.exp(m - m_new)
                p = jnp.exp(s - m_new)
                l = alpha * l + jnp.sum(p, axis=0, keepdims=True)
                v = vt_ref[0, kk, hl * DV_B:(hl + 1) * DV_B, :]
                acc = acc * alpha + jnp.dot(v, p.astype(BF16), preferred_element_type=F32)
                new += [m_new, l, acc]
            return tuple(new)

        init = (jnp.full((1, tq), NEG, F32), jnp.zeros((1, tq), F32), jnp.zeros((DV_B, tq), F32)) * 2
        carry = lax.fori_loop(0, i if causal else nk, functools.partial(step, masked=False), init)
        if causal:
            carry = step(i, carry, True)
        o = jnp.concatenate([carry[2] / carry[1], carry[5] / carry[4]], axis=0)
        o_ref[0, pl.ds(qrow, tq), :] = o.T.astype(BF16)
        return 0

    lax.fori_loop(0, nq, q_tile, 0)


def _mla_attn(qf, kf, vt, mask, *, tq, tk, causal):
    g, sq, _ = qf.shape
    sk = kf.shape[1]
    return pl.pallas_call(
        functools.partial(_mla_kernel, tq=tq, tk=tk, causal=causal), grid=(g, H_B // 2),
        in_specs=[pl.BlockSpec((1, sq, 2 * HEAD_PAD), lambda a, b: (a, 0, b)),
                  pl.BlockSpec((1, sk, 2 * HEAD_PAD), lambda a, b: (a, 0, b)),
                  pl.BlockSpec((1, sk // tk, 2 * DV_B, tk), lambda a, b: (a, 0, b, 0)),
                  _resident(mask.shape)],
        out_specs=pl.BlockSpec((1, sq, 2 * DV_B), lambda a, b: (a, 0, b)),
        out_shape=jax.ShapeDtypeStruct((g, sq, D_VB), BF16),
        compiler_params=pltpu.CompilerParams(dimension_semantics=("arbitrary", "arbitrary"),
                                             vmem_limit_bytes=VMEM_LIMIT_V7X),
        name="mla_attn")(qf, kf, vt, mask)


_G0 = 8


def _merge_ffn_kernel(x_ref, ya_ref, yb_ref, ga_ref, gb_ref, st_ref, wa_ref, wb_ref, wo_ref, n1_ref, n2_ref, n3_ref,
                      wg_ref, wu_ref, cw_ref, cb_ref, wd_ref, y_ref, cs_ref, gbuf, *, tm):
    i = pl.program_id(1)
    za = jnp.dot(ya_ref[0], wa_ref[...], preferred_element_type=F32)
    zb = jnp.dot(yb_ref[0], wb_ref[...], preferred_element_type=F32)
    mix = jax.nn.sigmoid(ga_ref[0].astype(F32)) * za + jax.nn.sigmoid(gb_ref[0].astype(F32)) * zb
    mo = jnp.dot(mix.astype(BF16), wo_ref[...], preferred_element_type=F32)
    x1 = x_ref[0] + _rms(mo, n1_ref[...])
    xn = _rms(x1, n2_ref[...]).astype(BF16)

    @pl.when(i == 0)
    def _():
        gbuf[_G0 - 2:_G0, :] = st_ref[0]

    @pl.when(i > 0)
    def _():
        gbuf[_G0 - 2:_G0, :] = gbuf[_G0 + tm - 2:_G0 + tm, :]

    gbuf[_G0:_G0 + tm, :] = jnp.dot(xn, wg_ref[...], preferred_element_type=F32)
    cs_ref[0] = gbuf[_G0 + tm - 2:_G0 + tm, :]
    u = jnp.dot(xn, wu_ref[...], preferred_element_type=F32)
    c = cb_ref[...] + cw_ref[0:1, :] * gbuf[_G0 - 2:_G0 - 2 + tm, :]
    c = c + cw_ref[1:2, :] * gbuf[_G0 - 1:_G0 - 1 + tm, :]
    c = c + cw_ref[2:3, :] * gbuf[_G0:_G0 + tm, :]
    hid = (jax.nn.gelu(c, approximate=True) * u).astype(BF16)
    f = jnp.dot(hid, wd_ref[...], preferred_element_type=F32)
    y_ref[0] = x1 + _rms(f, n3_ref[...])


def _merge_ffn(x, ya, yb, ga, gb, state, wa, wb, wo, n1, n2, n3, wg, wu, cw, cb, wd, *, tm):
    g, s, _ = x.shape
    tok = lambda w: pl.BlockSpec((1, tm, w), lambda a, b: (a, b, 0))
    per_group = pl.BlockSpec((1, CONV_W - 1, D_FF), lambda a, b: (a, 0, 0))
    weights = (wa, wb, wo, n1, n2, n3, wg, wu, cw, cb, wd)
    return pl.pallas_call(
        functools.partial(_merge_ffn_kernel, tm=tm), grid=(g, s // tm),
        in_specs=[tok(D_MODEL), tok(D_A), tok(D_VB), tok(D_MODEL), tok(D_MODEL), per_group]
                 + [_resident(w.shape) for w in weights],
        out_specs=(tok(D_MODEL), per_group),
        out_shape=(jax.ShapeDtypeStruct((g, s, D_MODEL), F32),
                   jax.ShapeDtypeStruct((g, CONV_W - 1, D_FF), F32)),
        scratch_shapes=[pltpu.VMEM((_G0 + tm, D_FF), F32)],
        compiler_params=pltpu.CompilerParams(dimension_semantics=("arbitrary", "arbitrary"),
                                             vmem_limit_bytes=VMEM_LIMIT_V7X),
        name="merge_ffn")(x, ya, yb, ga, gb, state, *weights)


def _rope_tables(pos, lane0, scale):
    half = DH_ROPE // 2
    inv = ROPE_BASE ** (-jnp.arange(half, dtype=F32) / half)
    ang = pos.astype(F32)[:, None] * inv[None, :]
    c, s = jnp.cos(ang), jnp.sin(ang)
    n = pos.shape[0]
    rest = jnp.zeros((n, HEAD_PAD - lane0 - DH_ROPE), F32)
    ct = jnp.concatenate([jnp.ones((n, lane0), F32), c, c, rest], axis=1) * scale
    st = jnp.concatenate([jnp.zeros((n, lane0), F32), -s, s, rest], axis=1) * scale
    return ct, st


def _band_bias(table, tq, nk):
    qpos = (nk - tq) + jnp.arange(tq)[None, :]
    kpos = jnp.arange(nk)[:, None]
    bias = table[:, jnp.clip(qpos - kpos, -MAX_REL, MAX_REL) + MAX_REL].astype(F32)
    qc, kc = qpos // CHUNK, kpos // CHUNK
    valid = (kc >= qc - LEFT_CHUNKS) & (kc <= qc)
    return jnp.where(valid[None], bias, NEG)


def _chunk_mask(tk, tq):
    kc = jnp.arange(tk)[:, None] // CHUNK
    qc = jnp.arange(tq)[None, :] // CHUNK
    return jnp.where(kc <= qc, 0.0, NEG).astype(F32)


def _prep_weights(w_in, w_uk, w_uv):
    o = np.cumsum((0,) + IN_SIZES)
    col = lambda j: w_in[:, o[j]:o[j + 1]]
    d = w_in.shape[0]
    qn = col(3).reshape(d, H_B, DH_NOPE)
    qr = col(4).reshape(d, H_B, DH_ROPE)
    w_qf = jnp.concatenate([qn, qr, jnp.zeros((d, H_B, HEAD_PAD - DH_NOPE - DH_ROPE), F32)], axis=2).reshape(d, D_QF)
    w_kr = jnp.concatenate([col(6), jnp.zeros((d, HEAD_PAD - DH_ROPE), F32)], axis=1)
    wm = jnp.concatenate([col(0) * A_SCALE, col(1), w_qf, col(5), w_kr, col(7), col(8)], axis=1).astype(BF16)
    wva = col(2).astype(BF16)
    wk = jnp.concatenate([w_uk, jnp.zeros((D_C, H_B, HEAD_PAD - DH_NOPE), F32)], axis=2).reshape(D_C, D_QF).astype(BF16)
    place = jnp.concatenate([jnp.zeros((DH_ROPE, DH_NOPE), F32), jnp.eye(DH_ROPE, dtype=F32),
                             jnp.zeros((DH_ROPE, HEAD_PAD - DH_NOPE - DH_ROPE), F32)], axis=1)
    ek = jnp.tile(place, (1, H_B)).astype(BF16)
    wvt = w_uv.reshape(D_C, D_VB).T.astype(BF16)
    return wm, wva.T, wva, wk, ek, wvt


def kernel(x_prompt, x_sample, cache_a_k, cache_a_v, cache_mla_ckv, cache_mla_krope, state_ffn_conv, norm_mix_pre,
           norm_mix_post, w_in, rel_bias_table, kv_norm, w_uk, w_uv, w_branch_a, w_branch_b, w_out, norm_ffn_pre,
           norm_ffn_post, w_ffn_gate, w_ffn_up, conv_w, conv_b, w_ffn_down):
    assert w_in.shape[0] == 1, "single layer"
    b, s, _ = x_prompt.shape
    db, t, _ = x_sample.shape
    past = cache_mla_ckv.shape[2]
    wcache = cache_a_k.shape[2]
    keep = min(A_WINDOW, s)
    tm1 = 512
    assert keep == tm1 and db * t == tm1 and wcache == A_WINDOW

    wm, wvat, wva, wk, ek, wvt = _prep_weights(w_in[0], w_uk[0], w_uv[0])
    row = lambda v: v.reshape(1, -1)
    proj_w = (row(norm_mix_pre[0]), wm, wvat, wva, row(kv_norm[0]))
    ffn_w = (w_branch_a[0].astype(BF16), w_branch_b[0].astype(BF16), w_out[0].astype(BF16), row(norm_mix_post[0]),
             row(norm_ffn_pre[0]), row(norm_ffn_post[0]), w_ffn_gate[0].astype(BF16), w_ffn_up[0].astype(BF16),
             conv_w[0], row(conv_b[0]), w_ffn_down[0].astype(BF16))
    table = rel_bias_table[0]

    pos = jnp.arange(s, dtype=jnp.int32)
    tabs = _rope_tables(pos, DH_NOPE, MLA_SCALE) + _rope_tables(pos, 0, 1.0)
    qa, ka, vat, kaf, vaf, qf, ckv, kr, ga, gb = _in_proj(x_prompt, *proj_w, *tabs, tm=tm1)
    kf, vt = _kv_up(ckv, kr, wk, ek, wvt, tm=1024, tk=256)
    ya = _band_prompt(qa, ka, vat, _band_bias(table, 128, A_WINDOW + 128), tm=tm1, tq=128)
    yb = _mla_attn(qf, kf, vt, _chunk_mask(256, 256), tq=256, tk=256, causal=True)
    y_prompt, conv_p = _merge_ffn(x_prompt, ya, yb, ga, gb, jnp.zeros((b, CONV_W - 1, D_FF), F32), *ffn_w, tm=256)

    pos_s = jnp.tile(past + jnp.arange(t, dtype=jnp.int32), db)
    tabs_s = _rope_tables(pos_s, DH_NOPE, MLA_SCALE) + _rope_tables(pos_s, 0, 1.0)
    qa2, ka2, vat2, kaf2, vaf2, qf2, ckv2, kr2, ga2, gb2 = _in_proj(x_sample.reshape(1, db * t, D_MODEL), *proj_w,
                                                                    *tabs_s, tm=tm1)
    per_seq = lambda v: v.reshape(db, t, v.shape[-1])
    ckv2, kr2 = per_seq(ckv2[0]), per_seq(kr2[0])
    new_k = jnp.concatenate([cache_a_k[0].reshape(db, wcache, D_A), per_seq(kaf2[0])], axis=1)
    new_v = jnp.concatenate([cache_a_v[0].reshape(db, wcache, D_A), per_seq(vaf2[0])], axis=1)
    ya2 = _band_sample(per_seq(qa2[0]), new_k.astype(BF16), jnp.swapaxes(new_v, 1, 2).astype(BF16),
                       _band_bias(table, t, wcache + t))
    c_all = jnp.concatenate([cache_mla_ckv[0], ckv2], axis=1)
    kr_all = jnp.concatenate([cache_mla_krope[0], kr2], axis=1)
    kf2, vt2 = _kv_up(c_all, kr_all, wk, ek, wvt, tm=past + t, tk=past + t)
    yb2 = _mla_attn(per_seq(qf2[0]), kf2, vt2, jnp.zeros((8, 128), F32), tq=t, tk=past + t, causal=False)
    y_sample, conv_s = _merge_ffn(x_sample, ya2, yb2, per_seq(ga2[0]), per_seq(gb2[0]), state_ffn_conv[0], *ffn_w,
                                  tm=t)

    heads = lambda v: v.reshape(1, v.shape[0], v.shape[1], H_A, DH_A)
    return (y_prompt, y_sample,
            heads(kaf), heads(vaf), ckv[None], kr[None], conv_p[None],
            heads(new_k[:, -wcache:]), heads(new_v[:, -wcache:]), ckv2[None], kr2[None], conv_s[None])
```

```python
import functools

import jax
import jax.numpy as jnp
import numpy as np
from jax import lax
from jax.experimental import pallas as pl
from jax.experimental.pallas import tpu as pltpu

D_MODEL = 1024
CHUNK = 64
LEFT_CHUNKS = 8
A_WINDOW = LEFT_CHUNKS * CHUNK
H_A = 8
DH_A = 64
MAX_REL = 256
H_B = 8
DH_NOPE = 64
DH_ROPE = 32
DV_B = 64
D_C = 256
D_FF = 2816
CONV_W = 3
ROPE_BASE = 10000.0
EPS = 1e-6
A_SCALE = DH_A ** -0.5
MLA_SCALE = (DH_NOPE + DH_ROPE) ** -0.5
IN_SIZES = (H_A * DH_A, H_A * DH_A, H_A * DH_A, H_B * DH_NOPE, H_B * DH_ROPE, D_C, DH_ROPE, D_MODEL, D_MODEL)

HEAD_PAD = 128
D_A = H_A * DH_A
D_VB = H_B * DV_B
D_QF = H_B * HEAD_PAD
NEG = -1e30
LOG2E = 1.4426950408889634
_L_ROWS = 16
VMEM_LIMIT_V7X = 56 * 1024 * 1024

F32 = jnp.float32
BF16 = jnp.bfloat16
_NT = (((1,), (1,)), ((), ()))


def _resident(shape):
    nd = len(shape)
    return pl.BlockSpec(shape, lambda *_: (0,) * nd, pipeline_mode=pl.Buffered(1))


def _rms(x, w):
    return x * lax.rsqrt(jnp.mean(x * x, axis=-1, keepdims=True) + EPS) * w


def _rope_lanes(x, c, s):
    lane = lax.broadcasted_iota(jnp.int32, x.shape, 1)
    partner = jnp.where(lane % 32 < 16, pltpu.roll(x, 128 - 16, 1), pltpu.roll(x, 16, 1))
    return x * c + partner * s


_C_QA, _C_KA, _C_QF, _C_CKV, _C_KR, _C_GA, _C_GB, _C_END = 0, 512, 1024, 2048, 2304, 2432, 3456, 4480


def _in_proj_kernel(x_ref, nw_ref, wm_ref, wvt_ref, wva_ref, kvn_ref, cq_ref, sq_ref, ck_ref, sk_ref,
                    qa_ref, ka_ref, vat_ref, kaf_ref, vaf_ref, qf_ref, ckv_ref, kr_ref, ga_ref, gb_ref):
    i = pl.program_id(1)
    xn = _rms(x_ref[0], nw_ref[...]).astype(BF16)

    def proj(lo, hi):
        return jnp.dot(xn, wm_ref[:, lo:hi], preferred_element_type=F32)

    qa_ref[0] = proj(_C_QA, _C_KA).astype(BF16)
    ka = proj(_C_KA, _C_QF)
    ka_ref[0] = ka.astype(BF16)
    kaf_ref[0] = ka
    vat_ref[0, 0] = lax.dot_general(wvt_ref[...], xn, _NT, preferred_element_type=F32).astype(BF16)

    @pl.when(i == pl.num_programs(1) - 1)
    def _():
        vaf_ref[0] = jnp.dot(xn, wva_ref[...], preferred_element_type=F32)

    cq, sq = cq_ref[...], sq_ref[...]
    for h in range(H_B):
        lo = _C_QF + h * HEAD_PAD
        qf_ref[0, :, h * HEAD_PAD:(h + 1) * HEAD_PAD] = _rope_lanes(proj(lo, lo + HEAD_PAD), cq, sq).astype(BF16)

    ckv_ref[0] = _rms(proj(_C_CKV, _C_KR), kvn_ref[...])
    kr_ref[0] = _rope_lanes(proj(_C_KR, _C_GA), ck_ref[...], sk_ref[...])[:, :DH_ROPE]
    ga_ref[0] = proj(_C_GA, _C_GB).astype(BF16)
    gb_ref[0] = proj(_C_GB, _C_END).astype(BF16)


def _in_proj(x, nw, wm, wvt, wva, kvn, cq, sq, ck, sk, *, tm):
    g, s, _ = x.shape
    n = s // tm
    tok = lambda w: pl.BlockSpec((1, tm, w), lambda a, b: (a, b, 0))
    tab = pl.BlockSpec((tm, HEAD_PAD), lambda a, b: (b, 0))
    tail = pl.BlockSpec((1, tm, D_A), lambda a, b: (a, 0, 0))
    out_shape = (
        jax.ShapeDtypeStruct((g, s, D_A), BF16),
        jax.ShapeDtypeStruct((g, s, D_A), BF16),
        jax.ShapeDtypeStruct((g, n, D_A, tm), BF16),
        jax.ShapeDtypeStruct((g, tm, D_A), F32),
        jax.ShapeDtypeStruct((g, tm, D_A), F32),
        jax.ShapeDtypeStruct((g, s, D_QF), BF16),
        jax.ShapeDtypeStruct((g, s, D_C), F32),
        jax.ShapeDtypeStruct((g, s, DH_ROPE), F32),
        jax.ShapeDtypeStruct((g, s, D_MODEL), BF16),
        jax.ShapeDtypeStruct((g, s, D_MODEL), BF16),
    )
    out_specs = (tok(D_A), tok(D_A), pl.BlockSpec((1, 1, D_A, tm), lambda a, b: (a, b, 0, 0)), tail, tail,
                 tok(D_QF), tok(D_C), tok(DH_ROPE), tok(D_MODEL), tok(D_MODEL))
    in_specs = [tok(D_MODEL), _resident(nw.shape), _resident(wm.shape), _resident(wvt.shape), _resident(wva.shape),
                _resident(kvn.shape), tab, tab, tab, tab]
    return pl.pallas_call(
        _in_proj_kernel, grid=(g, n), in_specs=in_specs, out_specs=out_specs, out_shape=out_shape,
        compiler_params=pltpu.CompilerParams(dimension_semantics=("arbitrary", "arbitrary"),
                                             vmem_limit_bytes=VMEM_LIMIT_V7X),
        name="in_proj")(x, nw, wm, wvt, wva, kvn, cq, sq, ck, sk)


def _kv_up_kernel(ckv_ref, kr_ref, wk_ref, ek_ref, wvt_ref, kf_ref, vt_ref, *, tk):
    c = ckv_ref[0].astype(BF16)
    kf = jnp.dot(c, wk_ref[...], preferred_element_type=F32)
    kf = kf + jnp.dot(kr_ref[0].astype(BF16), ek_ref[...], preferred_element_type=F32)
    kf_ref[0] = kf.astype(BF16)
    for j in range(c.shape[0] // tk):
        vt_ref[0, j] = lax.dot_general(wvt_ref[...], c[j * tk:(j + 1) * tk], _NT,
                                       preferred_element_type=F32).astype(BF16)


def _kv_up(ckv, kr, wk, ek, wvt, *, tm, tk):
    g, s, _ = ckv.shape
    return pl.pallas_call(
        functools.partial(_kv_up_kernel, tk=tk), grid=(g, s // tm),
        in_specs=[pl.BlockSpec((1, tm, D_C), lambda a, b: (a, b, 0)),
                  pl.BlockSpec((1, tm, DH_ROPE), lambda a, b: (a, b, 0)),
                  _resident(wk.shape), _resident(ek.shape), _resident(wvt.shape)],
        out_specs=(pl.BlockSpec((1, tm, D_QF), lambda a, b: (a, b, 0)),
                   pl.BlockSpec((1, tm // tk, D_VB, tk), lambda a, b: (a, b, 0, 0))),
        out_shape=(jax.ShapeDtypeStruct((g, s, D_QF), BF16),
                   jax.ShapeDtypeStruct((g, s // tk, D_VB, tk), BF16)),
        compiler_params=pltpu.CompilerParams(dimension_semantics=("arbitrary", "arbitrary"),
                                             vmem_limit_bytes=VMEM_LIMIT_V7X),
        name="kv_up")(ckv, kr, wk, ek, wvt)


def _band_heads(q, k, vt, bias_ref, pen):
    tq = q.shape[0]
    lane = lax.broadcasted_iota(jnp.int32, (tq, 2 * DH_A), 1)
    outs = []
    for h in range(H_A):
        pair = slice((h // 2) * 2 * DH_A, (h // 2 + 1) * 2 * DH_A)
        qm = jnp.where((lane >= DH_A) == (h % 2 == 1), q[:, pair], jnp.zeros((), BF16))
        s = lax.dot_general(k[:, pair], qm, _NT, preferred_element_type=F32)
        s = s + bias_ref[h]
        if pen is not None:
            s = s + pen
        m = jnp.max(s, axis=0, keepdims=True)
        p = jnp.exp(s - m)
        l = jnp.sum(p, axis=0, keepdims=True)
        o = jnp.dot(vt[h * DH_A:(h + 1) * DH_A], p.astype(BF16), preferred_element_type=F32)
        outs.append(o / l)
    return jnp.concatenate(outs, axis=0).T


def _band_prompt_kernel(q_ref, kp_ref, kc_ref, vp_ref, vc_ref, bias_ref, o_ref, *, tm, tq):
    i = pl.program_id(1)
    nk = A_WINDOW + tq
    row = lax.broadcasted_iota(jnp.int32, (nk, tq), 0)
    for sub in range(tm // tq):
        lo = sub * tq
        k = jnp.concatenate([kp_ref[0, lo:, :], kc_ref[0, :lo + tq, :]], axis=0)
        vt = jnp.concatenate([vp_ref[0, 0, :, lo:], vc_ref[0, 0, :, :lo + tq]], axis=1)
        pen = jnp.where(row < jnp.where(i == 0, tm - lo, 0), NEG, 0.0).astype(F32)
        o_ref[0, lo:lo + tq, :] = _band_heads(q_ref[0, lo:lo + tq, :], k, vt, bias_ref, pen).astype(BF16)


def _band_prompt(qa, ka, vat, bias, *, tm, tq):
    g, s, _ = qa.shape
    prev = lambda a, b: (a, jnp.maximum(b - 1, 0), 0)
    prev4 = lambda a, b: (a, jnp.maximum(b - 1, 0), 0, 0)
    return pl.pallas_call(
        functools.partial(_band_prompt_kernel, tm=tm, tq=tq), grid=(g, s // tm),
        in_specs=[pl.BlockSpec((1, tm, D_A), lambda a, b: (a, b, 0)),
                  pl.BlockSpec((1, tm, D_A), prev), pl.BlockSpec((1, tm, D_A), lambda a, b: (a, b, 0)),
                  pl.BlockSpec((1, 1, D_A, tm), prev4), pl.BlockSpec((1, 1, D_A, tm), lambda a, b: (a, b, 0, 0)),
                  _resident(bias.shape)],
        out_specs=pl.BlockSpec((1, tm, D_A), lambda a, b: (a, b, 0)),
        out_shape=jax.ShapeDtypeStruct((g, s, D_A), BF16),
        compiler_params=pltpu.CompilerParams(dimension_semantics=("arbitrary", "arbitrary"),
                                             vmem_limit_bytes=VMEM_LIMIT_V7X),
        name="band_prompt")(qa, ka, ka, vat, vat, bias)


def _band_sample_kernel(q_ref, k_ref, vt_ref, bias_ref, o_ref):
    o_ref[0] = _band_heads(q_ref[0], k_ref[0], vt_ref[0], bias_ref, None).astype(BF16)


def _band_sample(qa, k_all, vt_all, bias):
    g, t, _ = qa.shape
    nk = k_all.shape[1]
    return pl.pallas_call(
        _band_sample_kernel, grid=(g,),
        in_specs=[pl.BlockSpec((1, t, D_A), lambda a: (a, 0, 0)),
                  pl.BlockSpec((1, nk, D_A), lambda a: (a, 0, 0)),
                  pl.BlockSpec((1, D_A, nk), lambda a: (a, 0, 0)),
                  _resident(bias.shape)],
        out_specs=pl.BlockSpec((1, t, D_A), lambda a: (a, 0, 0)),
        out_shape=jax.ShapeDtypeStruct((g, t, D_A), BF16),
        compiler_params=pltpu.CompilerParams(dimension_semantics=("arbitrary",),
                                             vmem_limit_bytes=VMEM_LIMIT_V7X),
        name="band_sample")(qa, k_all, vt_all, bias)


def _mla_kernel(q_ref, k_ref, vt_ref, mask_ref, o_ref, s_scr, *, tq, tk, causal):
    nq = q_ref.shape[1] // tq
    nk = k_ref.shape[1] // tk
    ones = jnp.ones((_L_ROWS, tk), BF16)

    def q_tile(i, _):
        qrow = pl.multiple_of(i * tq, tq)
        qs = [q_ref[0, pl.ds(qrow, tq), hl * HEAD_PAD:(hl + 1) * HEAD_PAD] for hl in range(2)]

        def scores(kk):
            krow = pl.multiple_of(kk * tk, tk)
            return [lax.dot_general(k_ref[0, pl.ds(krow, tk), hl * HEAD_PAD:(hl + 1) * HEAD_PAD], qs[hl], _NT,
                                    preferred_element_type=F32) for hl in range(2)]

        def stash(ss):
            for hl in range(2):
                s_scr[hl] = ss[hl]
            return tuple(jnp.max(s, axis=0, keepdims=True) for s in ss)

        def consume(kk, carry, tile_max, masked):
            stats = []
            for hl in range(2):
                if masked:
                    s = s_scr[hl] + mask_ref[...]
                    m_new = jnp.maximum(carry[2 * hl], jnp.max(s, axis=0, keepdims=True))
                else:
                    s = s_scr[hl]
                    m_new = jnp.maximum(carry[2 * hl], tile_max[hl])
                stats.append((m_new, jnp.exp2(carry[2 * hl] - m_new), jnp.exp2(s - m_new).astype(BF16)))
            new = []
            for hl in range(2):
                m_new, alpha, p = stats[hl]
                v = jnp.concatenate([vt_ref[0, kk, hl * DV_B:(hl + 1) * DV_B, :], ones], axis=0)
                new += [m_new, carry[2 * hl + 1] * alpha + jnp.dot(v, p, preferred_element_type=F32)]
            return tuple(new)

        def body(t, carry):
            nxt = scores(t + 1)
            state = consume(t, carry[:4], carry[4:], False)
            return state + stash(nxt)

        last = i if causal else nk - 1
        init = (jnp.full((1, tq), NEG, F32), jnp.zeros((DV_B + _L_ROWS, tq), F32)) * 2 + stash(scores(0))
        carry = lax.fori_loop(0, last, body, init)
        carry = consume(last, carry[:4], carry[4:], causal)
        o = jnp.concatenate([carry[2 * hl + 1][:DV_B] / carry[2 * hl + 1][DV_B:DV_B + 1] for hl in range(2)], axis=0)
        o_ref[0, pl.ds(qrow, tq), :] = o.T.astype(BF16)
        return 0

    lax.fori_loop(0, nq, q_tile, 0)


def _mla_attn(qf, kf, vt, mask, *, tq, tk, causal):
    g, sq, _ = qf.shape
    sk = kf.shape[1]
    return pl.pallas_call(
        functools.partial(_mla_kernel, tq=tq, tk=tk, causal=causal), grid=(g, H_B // 2),
        in_specs=[pl.BlockSpec((1, sq, 2 * HEAD_PAD), lambda a, b: (a, 0, b)),
                  pl.BlockSpec((1, sk, 2 * HEAD_PAD), lambda a, b: (a, 0, b)),
                  pl.BlockSpec((1, sk // tk, 2 * DV_B, tk), lambda a, b: (a, 0, b, 0)),
                  _resident(mask.shape)],
        out_specs=pl.BlockSpec((1, sq, 2 * DV_B), lambda a, b: (a, 0, b)),
        out_shape=jax.ShapeDtypeStruct((g, sq, D_VB), BF16),
        scratch_shapes=[pltpu.VMEM((2, tk, tq), F32)],
        compiler_params=pltpu.CompilerParams(dimension_semantics=("arbitrary", "arbitrary"),
                                             vmem_limit_bytes=VMEM_LIMIT_V7X),
        name="mla_attn")(qf, kf, vt, mask)


_G0 = 8


def _merge_ffn_kernel(x_ref, ya_ref, yb_ref, ga_ref, gb_ref, st_ref, wa_ref, wb_ref, wo_ref, n1_ref, n2_ref, n3_ref,
                      wg_ref, wu_ref, cw_ref, cb_ref, wd_ref, y_ref, cs_ref, gbuf, *, tm):
    i = pl.program_id(1)
    za = jnp.dot(ya_ref[0], wa_ref[...], preferred_element_type=F32)
    zb = jnp.dot(yb_ref[0], wb_ref[...], preferred_element_type=F32)
    mix = jax.nn.sigmoid(ga_ref[0].astype(F32)) * za + jax.nn.sigmoid(gb_ref[0].astype(F32)) * zb
    mo = jnp.dot(mix.astype(BF16), wo_ref[...], preferred_element_type=F32)
    x1 = x_ref[0] + _rms(mo, n1_ref[...])
    xn = _rms(x1, n2_ref[...]).astype(BF16)

    @pl.when(i == 0)
    def _():
        gbuf[_G0 - 2:_G0, :] = st_ref[0]

    @pl.when(i > 0)
    def _():
        gbuf[_G0 - 2:_G0, :] = gbuf[_G0 + tm - 2:_G0 + tm, :]

    gbuf[_G0:_G0 + tm, :] = jnp.dot(xn, wg_ref[...], preferred_element_type=F32)
    cs_ref[0] = gbuf[_G0 + tm - 2:_G0 + tm, :]
    u = jnp.dot(xn, wu_ref[...], preferred_element_type=F32)
    c = cb_ref[...] + cw_ref[0:1, :] * gbuf[_G0 - 2:_G0 - 2 + tm, :]
    c = c + cw_ref[1:2, :] * gbuf[_G0 - 1:_G0 - 1 + tm, :]
    c = c + cw_ref[2:3, :] * gbuf[_G0:_G0 + tm, :]
    hid = (jax.nn.gelu(c, approximate=True) * u).astype(BF16)
    f = jnp.dot(hid, wd_ref[...], preferred_element_type=F32)
    y_ref[0] = x1 + _rms(f, n3_ref[...])


def _merge_ffn(x, ya, yb, ga, gb, state, wa, wb, wo, n1, n2, n3, wg, wu, cw, cb, wd, *, tm):
    g, s, _ = x.shape
    tok = lambda w: pl.BlockSpec((1, tm, w), lambda a, b: (a, b, 0))
    per_group = pl.BlockSpec((1, CONV_W - 1, D_FF), lambda a, b: (a, 0, 0))
    weights = (wa, wb, wo, n1, n2, n3, wg, wu, cw, cb, wd)
    return pl.pallas_call(
        functools.partial(_merge_ffn_kernel, tm=tm), grid=(g, s // tm),
        in_specs=[tok(D_MODEL), tok(D_A), tok(D_VB), tok(D_MODEL), tok(D_MODEL), per_group]
                 + [_resident(w.shape) for w in weights],
        out_specs=(tok(D_MODEL), per_group),
        out_shape=(jax.ShapeDtypeStruct((g, s, D_MODEL), F32),
                   jax.ShapeDtypeStruct((g, CONV_W - 1, D_FF), F32)),
        scratch_shapes=[pltpu.VMEM((_G0 + tm, D_FF), F32)],
        compiler_params=pltpu.CompilerParams(dimension_semantics=("arbitrary", "arbitrary"),
                                             vmem_limit_bytes=VMEM_LIMIT_V7X),
        name="merge_ffn")(x, ya, yb, ga, gb, state, *weights)


def _rope_tables(pos, lane0, scale):
    half = DH_ROPE // 2
    inv = ROPE_BASE ** (-jnp.arange(half, dtype=F32) / half)
    ang = pos.astype(F32)[:, None] * inv[None, :]
    c, s = jnp.cos(ang), jnp.sin(ang)
    n = pos.shape[0]
    rest = jnp.zeros((n, HEAD_PAD - lane0 - DH_ROPE), F32)
    ct = jnp.concatenate([jnp.ones((n, lane0), F32), c, c, rest], axis=1) * scale
    st = jnp.concatenate([jnp.zeros((n, lane0), F32), -s, s, rest], axis=1) * scale
    return ct, st


def _band_bias(table, tq, nk):
    qpos = (nk - tq) + jnp.arange(tq)[None, :]
    kpos = jnp.arange(nk)[:, None]
    w = nk + tq
    d_lo, d_hi = 1 - tq, w - tq
    assert -MAX_REL <= d_lo and d_hi >= MAX_REL
    h = table.shape[0]
    ext = jnp.concatenate([table[:, d_lo + MAX_REL:], jnp.broadcast_to(table[:, -1:], (h, d_hi - MAX_REL))], axis=1)
    ext = jnp.roll(ext, -(nk - 1), axis=1)
    bias = jnp.tile(ext, (1, nk))[:, :nk * (w - 1)].reshape(h, nk, w - 1)[:, :, :tq].astype(F32)
    qc, kc = qpos // CHUNK, kpos // CHUNK
    valid = (kc >= qc - LEFT_CHUNKS) & (kc <= qc)
    return jnp.where(valid[None], bias, NEG)


def _chunk_mask(tk, tq):
    kc = jnp.arange(tk)[:, None] // CHUNK
    qc = jnp.arange(tq)[None, :] // CHUNK
    return jnp.where(kc <= qc, 0.0, NEG).astype(F32)


def _prep_weights(w_in, w_uk, w_uv):
    o = np.cumsum((0,) + IN_SIZES)
    col = lambda j: w_in[:, o[j]:o[j + 1]]
    d = w_in.shape[0]
    qn = col(3).reshape(d, H_B, DH_NOPE)
    qr = col(4).reshape(d, H_B, DH_ROPE)
    w_qf = jnp.concatenate([qn, qr, jnp.zeros((d, H_B, HEAD_PAD - DH_NOPE - DH_ROPE), F32)], axis=2).reshape(d, D_QF)
    w_kr = jnp.concatenate([col(6), jnp.zeros((d, HEAD_PAD - DH_ROPE), F32)], axis=1)
    wm = jnp.concatenate([col(0) * A_SCALE, col(1), w_qf, col(5), w_kr, col(7), col(8)], axis=1).astype(BF16)
    wva = col(2).astype(BF16)
    wk = jnp.concatenate([w_uk, jnp.zeros((D_C, H_B, HEAD_PAD - DH_NOPE), F32)], axis=2).reshape(D_C, D_QF).astype(BF16)
    place = jnp.concatenate([jnp.zeros((DH_ROPE, DH_NOPE), F32), jnp.eye(DH_ROPE, dtype=F32),
                             jnp.zeros((DH_ROPE, HEAD_PAD - DH_NOPE - DH_ROPE), F32)], axis=1)
    ek = jnp.tile(place, (1, H_B)).astype(BF16)
    wvt = w_uv.reshape(D_C, D_VB).T.astype(BF16)
    return wm, wva.T, wva, wk, ek, wvt


def kernel(x_prompt, x_sample, cache_a_k, cache_a_v, cache_mla_ckv, cache_mla_krope, state_ffn_conv, norm_mix_pre,
           norm_mix_post, w_in, rel_bias_table, kv_norm, w_uk, w_uv, w_branch_a, w_branch_b, w_out, norm_ffn_pre,
           norm_ffn_post, w_ffn_gate, w_ffn_up, conv_w, conv_b, w_ffn_down):
    assert w_in.shape[0] == 1, "single layer"
    b, s, _ = x_prompt.shape
    db, t, _ = x_sample.shape
    past = cache_mla_ckv.shape[2]
    wcache = cache_a_k.shape[2]
    keep = min(A_WINDOW, s)
    tm1 = 512
    assert keep == tm1 and db * t == tm1 and wcache == A_WINDOW

    wm, wvat, wva, wk, ek, wvt = _prep_weights(w_in[0], w_uk[0], w_uv[0])
    row = lambda v: v.reshape(1, -1)
    proj_w = (row(norm_mix_pre[0]), wm, wvat, wva, row(kv_norm[0]))
    ffn_w = (w_branch_a[0].astype(BF16), w_branch_b[0].astype(BF16), w_out[0].astype(BF16), row(norm_mix_post[0]),
             row(norm_ffn_pre[0]), row(norm_ffn_post[0]), w_ffn_gate[0].astype(BF16), w_ffn_up[0].astype(BF16),
             conv_w[0], row(conv_b[0]), w_ffn_down[0].astype(BF16))
    table = rel_bias_table[0]

    pos = jnp.arange(s, dtype=jnp.int32)
    tabs = _rope_tables(pos, DH_NOPE, MLA_SCALE * LOG2E) + _rope_tables(pos, 0, 1.0)
    qa, ka, vat, kaf, vaf, qf, ckv, kr, ga, gb = _in_proj(x_prompt, *proj_w, *tabs, tm=tm1)
    t_mla = 512
    kf, vt = _kv_up(ckv, kr, wk, ek, wvt, tm=1024, tk=t_mla)
    ya = _band_prompt(qa, ka, vat, _band_bias(table, 128, A_WINDOW + 128), tm=tm1, tq=128)
    yb = _mla_attn(qf, kf, vt, _chunk_mask(t_mla, t_mla), tq=t_mla, tk=t_mla, causal=True)
    y_prompt, conv_p = _merge_ffn(x_prompt, ya, yb, ga, gb, jnp.zeros((b, CONV_W - 1, D_FF), F32), *ffn_w, tm=256)

    pos_s = jnp.tile(past + jnp.arange(t, dtype=jnp.int32), db)
    tabs_s = _rope_tables(pos_s, DH_NOPE, MLA_SCALE * LOG2E) + _rope_tables(pos_s, 0, 1.0)
    qa2, ka2, vat2, kaf2, vaf2, qf2, ckv2, kr2, ga2, gb2 = _in_proj(x_sample.reshape(1, db * t, D_MODEL), *proj_w,
                                                                    *tabs_s, tm=tm1)
    per_seq = lambda v: v.reshape(db, t, v.shape[-1])
    ckv2, kr2 = per_seq(ckv2[0]), per_seq(kr2[0])
    new_k = jnp.concatenate([cache_a_k[0].reshape(db, wcache, D_A), per_seq(kaf2[0])], axis=1)
    new_v = jnp.concatenate([cache_a_v[0].reshape(db, wcache, D_A), per_seq(vaf2[0])], axis=1)
    ya2 = _band_sample(per_seq(qa2[0]), new_k.astype(BF16), jnp.swapaxes(new_v, 1, 2).astype(BF16),
                       _band_bias(table, t, wcache + t))
    c_all = jnp.concatenate([cache_mla_ckv[0], ckv2], axis=1)
    kr_all = jnp.concatenate([cache_mla_krope[0], kr2], axis=1)
    kf2, vt2 = _kv_up(c_all, kr_all, wk, ek, wvt, tm=past + t, tk=past + t)
    yb2 = _mla_attn(per_seq(qf2[0]), kf2, vt2, jnp.zeros((8, 128), F32), tq=t, tk=past + t, causal=False)
    y_sample, conv_s = _merge_ffn(x_sample, ya2, yb2, per_seq(ga2[0]), per_seq(gb2[0]), state_ffn_conv[0], *ffn_w,
                                  tm=t)

    heads = lambda v: v.reshape(1, v.shape[0], v.shape[1], H_A, DH_A)
    return (y_prompt, y_sample,
            heads(kaf), heads(vaf), ckv[None], kr[None], conv_p[None],
            heads(new_k[:, -wcache:]), heads(new_v[:, -wcache:]), ckv2[None], kr2[None], conv_s[None])
```

```python
import functools

import jax
import jax.numpy as jnp
import numpy as np
from jax import lax
from jax.experimental import pallas as pl
from jax.experimental.pallas import tpu as pltpu

D_MODEL = 1024
CHUNK = 64
LEFT_CHUNKS = 8
A_WINDOW = LEFT_CHUNKS * CHUNK
H_A = 8
DH_A = 64
MAX_REL = 256
H_B = 8
DH_NOPE = 64
DH_ROPE = 32
DV_B = 64
D_C = 256
D_FF = 2816
CONV_W = 3
ROPE_BASE = 10000.0
EPS = 1e-6
A_SCALE = DH_A ** -0.5
MLA_SCALE = (DH_NOPE + DH_ROPE) ** -0.5
IN_SIZES = (H_A * DH_A, H_A * DH_A, H_A * DH_A, H_B * DH_NOPE, H_B * DH_ROPE, D_C, DH_ROPE, D_MODEL, D_MODEL)

HEAD_PAD = 128
D_A = H_A * DH_A
D_VB = H_B * DV_B
D_QF = H_B * HEAD_PAD
PAIR_W = 2 * DH_A
N_PAIRS = H_A // 2
NEG = -1e30
LOG2E = 1.4426950408889634
_L_ROWS = 16
VMEM_LIMIT_V7X = 56 * 1024 * 1024

F32 = jnp.float32
BF16 = jnp.bfloat16
_NT = (((1,), (1,)), ((), ()))


def _resident(shape):
    nd = len(shape)
    return pl.BlockSpec(shape, lambda *_: (0,) * nd, pipeline_mode=pl.Buffered(1))


def _rms(x, w):
    return x * lax.rsqrt(jnp.mean(x * x, axis=-1, keepdims=True) + EPS) * w


def _rope_lanes(x, c, s):
    lane = lax.broadcasted_iota(jnp.int32, x.shape, 1)
    partner = jnp.where(lane % 32 < 16, pltpu.roll(x, 128 - 16, 1), pltpu.roll(x, 16, 1))
    return x * c + partner * s


_C_QA, _C_KA, _C_QF, _C_CKV, _C_KR, _C_GA, _C_GB, _C_END = 0, 512, 1024, 2048, 2304, 2432, 3456, 4480


def _store_pairs(ref, v):
    for g in range(N_PAIRS):
        ref[0, g] = v[:, g * PAIR_W:(g + 1) * PAIR_W]


def _in_proj_kernel(x_ref, nw_ref, wm_ref, wvt_ref, wva_ref, kvn_ref, cq_ref, sq_ref, ck_ref, sk_ref,
                    qa_ref, ka_ref, vat_ref, kaf_ref, vaf_ref, qf_ref, ckv_ref, kr_ref, ga_ref, gb_ref, *, n):
    i = pl.program_id(1)
    xn = _rms(x_ref[0], nw_ref[...]).astype(BF16)

    def proj(lo, hi):
        return jnp.dot(xn, wm_ref[:, lo:hi], preferred_element_type=F32)

    _store_pairs(qa_ref, proj(_C_QA, _C_KA).astype(BF16))
    ka = proj(_C_KA, _C_QF)
    _store_pairs(ka_ref, ka.astype(BF16))
    kaf_ref[0] = ka
    vat_ref[0, 0] = lax.dot_general(wvt_ref[...], xn, _NT, preferred_element_type=F32).astype(BF16)

    def va_tail():
        vaf_ref[0] = jnp.dot(xn, wva_ref[...], preferred_element_type=F32)

    if n == 1:
        va_tail()
    else:
        pl.when(i == n - 1)(va_tail)

    cq, sq = cq_ref[...], sq_ref[...]
    for h in range(H_B):
        lo = _C_QF + h * HEAD_PAD
        qf_ref[0, :, h * HEAD_PAD:(h + 1) * HEAD_PAD] = _rope_lanes(proj(lo, lo + HEAD_PAD), cq, sq).astype(BF16)

    ckv_ref[0] = _rms(proj(_C_CKV, _C_KR), kvn_ref[...])
    kr_ref[0] = _rope_lanes(proj(_C_KR, _C_GA), ck_ref[...], sk_ref[...])[:, :DH_ROPE]
    ga_ref[0] = proj(_C_GA, _C_GB).astype(BF16)
    gb_ref[0] = proj(_C_GB, _C_END).astype(BF16)


def _in_proj(x, nw, wm, wvt, wva, kvn, cq, sq, ck, sk, *, tm):
    g, s, _ = x.shape
    n = s // tm
    tok = lambda w: pl.BlockSpec((1, tm, w), lambda a, b: (a, b, 0))
    tab = pl.BlockSpec((tm, HEAD_PAD), lambda a, b: (b, 0))
    tail = pl.BlockSpec((1, tm, D_A), lambda a, b: (a, 0, 0))
    out_shape = (
        jax.ShapeDtypeStruct((g, N_PAIRS, s, PAIR_W), BF16),
        jax.ShapeDtypeStruct((g, N_PAIRS, s, PAIR_W), BF16),
        jax.ShapeDtypeStruct((g, n, D_A, tm), BF16),
        jax.ShapeDtypeStruct((g, tm, D_A), F32),
        jax.ShapeDtypeStruct((g, tm, D_A), F32),
        jax.ShapeDtypeStruct((g, s, D_QF), BF16),
        jax.ShapeDtypeStruct((g, s, D_C), F32),
        jax.ShapeDtypeStruct((g, s, DH_ROPE), F32),
        jax.ShapeDtypeStruct((g, s, D_MODEL), BF16),
        jax.ShapeDtypeStruct((g, s, D_MODEL), BF16),
    )
    pairs = pl.BlockSpec((1, N_PAIRS, tm, PAIR_W), lambda a, b: (a, 0, b, 0))
    out_specs = (pairs, pairs, pl.BlockSpec((1, 1, D_A, tm), lambda a, b: (a, b, 0, 0)), tail, tail,
                 tok(D_QF), tok(D_C), tok(DH_ROPE), tok(D_MODEL), tok(D_MODEL))
    in_specs = [tok(D_MODEL), _resident(nw.shape), _resident(wm.shape), _resident(wvt.shape), _resident(wva.shape),
                _resident(kvn.shape), tab, tab, tab, tab]
    return pl.pallas_call(
        functools.partial(_in_proj_kernel, n=n), grid=(g, n), in_specs=in_specs, out_specs=out_specs, out_shape=out_shape,
        compiler_params=pltpu.CompilerParams(dimension_semantics=("arbitrary", "arbitrary"),
                                             vmem_limit_bytes=VMEM_LIMIT_V7X),
        name="in_proj")(x, nw, wm, wvt, wva, kvn, cq, sq, ck, sk)


def _kv_up_kernel(ckv_ref, kr_ref, wk_ref, ek_ref, wvt_ref, kf_ref, vt_ref, *, tk):
    c = ckv_ref[0].astype(BF16)
    kf = jnp.dot(c, wk_ref[...], preferred_element_type=F32)
    kf = kf + jnp.dot(kr_ref[0].astype(BF16), ek_ref[...], preferred_element_type=F32)
    kf_ref[0] = kf.astype(BF16)
    for j in range(c.shape[0] // tk):
        vt_ref[0, j] = lax.dot_general(wvt_ref[...], c[j * tk:(j + 1) * tk], _NT,
                                       preferred_element_type=F32).astype(BF16)


def _kv_up(ckv, kr, wk, ek, wvt, *, tm, tk):
    g, s, _ = ckv.shape
    return pl.pallas_call(
        functools.partial(_kv_up_kernel, tk=tk), grid=(g, s // tm),
        in_specs=[pl.BlockSpec((1, tm, D_C), lambda a, b: (a, b, 0)),
                  pl.BlockSpec((1, tm, DH_ROPE), lambda a, b: (a, b, 0)),
                  _resident(wk.shape), _resident(ek.shape), _resident(wvt.shape)],
        out_specs=(pl.BlockSpec((1, tm, D_QF), lambda a, b: (a, b, 0)),
                   pl.BlockSpec((1, tm // tk, D_VB, tk), lambda a, b: (a, b, 0, 0))),
        out_shape=(jax.ShapeDtypeStruct((g, s, D_QF), BF16),
                   jax.ShapeDtypeStruct((g, s // tk, D_VB, tk), BF16)),
        compiler_params=pltpu.CompilerParams(dimension_semantics=("arbitrary", "arbitrary"),
                                             vmem_limit_bytes=VMEM_LIMIT_V7X),
        name="kv_up")(ckv, kr, wk, ek, wvt)


def _band_units(units, bias_ref, n_pairs=H_A // 2):
    def scores(u, g):
        q, k_of, _, _ = units[u]
        lanes = slice(g * 2 * DH_A, (g + 1) * 2 * DH_A)
        lane = lax.broadcasted_iota(jnp.int32, (q.shape[0], 2 * DH_A), 1)
        k2 = k_of(lanes)
        return [lax.dot_general(k2, jnp.where((lane >= DH_A) == (hl == 1), q[:, lanes], jnp.zeros((), BF16)), _NT,
                                preferred_element_type=F32) for hl in range(2)]

    steps = [(u, g) for u in range(len(units)) for g in range(n_pairs)]
    outs = [[] for _ in units]
    pend = scores(*steps[0])
    for j, (u, g) in enumerate(steps):
        nxt = scores(*steps[j + 1]) if j + 1 < len(steps) else None
        _, _, v_of, b0 = units[u]
        for hl in range(2):
            h = 2 * g + hl
            nk = pend[hl].shape[0]
            s = pend[hl] + bias_ref[h, b0:b0 + nk, :]
            p = jnp.exp2(s - jnp.max(s, axis=0, keepdims=True)).astype(BF16)
            v = jnp.concatenate([v_of(slice(h * DH_A, (h + 1) * DH_A)), jnp.ones((_L_ROWS, nk), BF16)], axis=0)
            o = jnp.dot(v, p, preferred_element_type=F32)
            outs[u].append(o[:DH_A] / o[DH_A:DH_A + 1])
        pend = nxt
    return [jnp.concatenate(o, axis=0).T for o in outs]


def _band_prompt_kernel(q_ref, kp_ref, kc_ref, vp_ref, vc_ref, bias_ref, o_ref, s_scr, *, tm, tq):
    subs = [(lo, lo + tq) for lo in range(0, tm, tq)]
    lane = lax.broadcasted_iota(jnp.int32, (tq, PAIR_W), 1)

    def tile(first):
        def window(lo, hi):
            return (hi, tm - lo) if first else (tm - lo + hi, 0)

        def scores(g, j):
            lo, hi = subs[j]
            nk, b0 = window(lo, hi)
            q2 = q_ref[0, g, lo:hi, :]
            qcat = jnp.concatenate([jnp.where(lane < DH_A, q2, jnp.zeros((), BF16)),
                                    jnp.where(lane >= DH_A, q2, jnp.zeros((), BF16))], axis=0)
            k2 = kc_ref[0, g, :hi, :]
            if not first:
                k2 = jnp.concatenate([kp_ref[0, g, lo:, :], k2], axis=0)
            bias = jnp.concatenate([bias_ref[2 * g + hl, b0:b0 + nk, :] for hl in range(2)], axis=1)
            return lax.dot_general(k2, qcat, _NT, preferred_element_type=F32) + bias

        def stash(j, s):
            s_scr[j, :s.shape[0], :] = s
            return jnp.max(s, axis=0, keepdims=True)

        def consume(g, j, col_max):
            lo, hi = subs[j]
            nk, _ = window(lo, hi)
            outs = []
            for hl in range(2):
                cols = slice(hl * tq, (hl + 1) * tq)
                p = jnp.exp2(s_scr[j, :nk, cols] - col_max[:, cols]).astype(BF16)
                rows = pl.ds(g * PAIR_W + hl * DH_A, DH_A)
                v = vc_ref[0, 0, rows, :hi]
                if not first:
                    v = jnp.concatenate([vp_ref[0, 0, rows, lo:], v], axis=1)
                v = jnp.concatenate([v, jnp.ones((_L_ROWS, nk), BF16)], axis=0)
                o = jnp.dot(v, p, preferred_element_type=F32)
                outs.append(o[:DH_A] / o[DH_A:DH_A + 1])
            o_ref[0, g, lo:hi, :] = jnp.concatenate(outs, axis=0).T.astype(BF16)

        def body(g, col_max):
            new_max = []
            for j in range(len(subs)):
                nxt = scores(g + 1, j)
                consume(g, j, col_max[j])
                new_max.append(stash(j, nxt))
            return tuple(new_max)

        col_max = tuple(stash(j, scores(0, j)) for j in range(len(subs)))
        for g in range(N_PAIRS - 1):
            col_max = body(g, col_max)
        for j in range(len(subs)):
            consume(N_PAIRS - 1, j, col_max[j])

    pl.when(pl.program_id(1) == 0)(functools.partial(tile, True))
    pl.when(pl.program_id(1) > 0)(functools.partial(tile, False))


def _band_prompt(qa, ka, vat, bias, *, tm, tq):
    g, _, s, _ = qa.shape
    cur = lambda a, b: (a, 0, b, 0)
    prev = lambda a, b: (a, 0, jnp.maximum(b - 1, 0), 0)
    prev_v = lambda a, b: (a, jnp.maximum(b - 1, 0), 0, 0)
    pairs = lambda idx: pl.BlockSpec((1, N_PAIRS, tm, PAIR_W), idx)
    return pl.pallas_call(
        functools.partial(_band_prompt_kernel, tm=tm, tq=tq), grid=(g, s // tm),
        in_specs=[pairs(cur), pairs(prev), pairs(cur),
                  pl.BlockSpec((1, 1, D_A, tm), prev_v), pl.BlockSpec((1, 1, D_A, tm), lambda a, b: (a, b, 0, 0)),
                  _resident(bias.shape)],
        out_specs=pairs(cur),
        out_shape=jax.ShapeDtypeStruct((g, N_PAIRS, s, PAIR_W), BF16),
        scratch_shapes=[pltpu.VMEM((tm // tq, tm + tq, 2 * tq), F32)],
        compiler_params=pltpu.CompilerParams(dimension_semantics=("arbitrary", "arbitrary"),
                                             vmem_limit_bytes=VMEM_LIMIT_V7X),
        name="band_prompt")(qa, ka, ka, vat, vat, bias)


def _band_sample_kernel(q_ref, k_ref, vt_ref, bias_ref, o_ref):
    q = jnp.concatenate([q_ref[0, g] for g in range(N_PAIRS)], axis=1)
    unit = (q, lambda lanes: k_ref[0, :, lanes], lambda rows: vt_ref[0, rows, :], 0)
    _store_pairs(o_ref, _band_units([unit], bias_ref)[0].astype(BF16))


def _band_sample(qa, k_all, vt_all, bias):
    g, _, t, _ = qa.shape
    nk = k_all.shape[1]
    pairs = pl.BlockSpec((1, N_PAIRS, t, PAIR_W), lambda a: (a, 0, 0, 0))
    return pl.pallas_call(
        _band_sample_kernel, grid=(g,),
        in_specs=[pairs,
                  pl.BlockSpec((1, nk, D_A), lambda a: (a, 0, 0)),
                  pl.BlockSpec((1, D_A, nk), lambda a: (a, 0, 0)),
                  _resident(bias.shape)],
        out_specs=pairs,
        out_shape=jax.ShapeDtypeStruct((g, N_PAIRS, t, PAIR_W), BF16),
        compiler_params=pltpu.CompilerParams(dimension_semantics=("arbitrary",),
                                             vmem_limit_bytes=VMEM_LIMIT_V7X),
        name="band_sample")(qa, k_all, vt_all, bias)


def _mla_kernel(q_ref, k_ref, vt_ref, mask_ref, o_ref, s_scr, *, tq, tk, causal):
    nq = q_ref.shape[1] // tq
    nk = k_ref.shape[1] // tk
    ones = jnp.ones((_L_ROWS, tk), BF16)

    def q_tile(i, _):
        qrow = pl.multiple_of(i * tq, tq)
        qs = [q_ref[0, pl.ds(qrow, tq), hl * HEAD_PAD:(hl + 1) * HEAD_PAD] for hl in range(2)]

        def scores(kk, hl):
            krow = pl.multiple_of(kk * tk, tk)
            return lax.dot_general(k_ref[0, pl.ds(krow, tk), hl * HEAD_PAD:(hl + 1) * HEAD_PAD], qs[hl], _NT,
                                   preferred_element_type=F32)

        def stash(hl, s):
            s_scr[hl] = s
            return jnp.max(s, axis=0, keepdims=True)

        def consume(kk, hl, m, acc, tile_max, masked):
            if masked:
                s = s_scr[hl] + mask_ref[...]
                m_new = jnp.maximum(m, jnp.max(s, axis=0, keepdims=True))
            else:
                s = s_scr[hl]
                m_new = jnp.maximum(m, tile_max)
            p = jnp.exp2(s - m_new).astype(BF16)
            v = jnp.concatenate([vt_ref[0, kk, hl * DV_B:(hl + 1) * DV_B, :], ones], axis=0)
            return m_new, acc * jnp.exp2(m - m_new) + jnp.dot(v, p, preferred_element_type=F32)

        def body(t, carry):
            out = []
            for hl in range(2):
                m, acc, tile_max = carry[3 * hl:3 * hl + 3]
                nxt = scores(t + 1, hl)
                out += list(consume(t, hl, m, acc, tile_max, False)) + [stash(hl, nxt)]
            return tuple(out)

        last = i if causal else nk - 1
        init = ()
        for hl in range(2):
            init += (jnp.full((1, tq), NEG, F32), jnp.zeros((DV_B + _L_ROWS, tq), F32), stash(hl, scores(0, hl)))
        pairs_of_tiles = lax.shift_right_logical(last, 1) if causal else last // 2
        carry = lax.fori_loop(0, pairs_of_tiles, lambda u, c: body(2 * u + 1, body(2 * u, c)), init)
        carry = lax.fori_loop(0, last & 1, lambda _, c: body(last - 1, c), carry)
        accs =[consume(last, hl, *carry[3 * hl:3 * hl + 3], causal)[1] for hl in range(2)]
        o = jnp.concatenate([acc[:DV_B] / acc[DV_B:DV_B + 1] for acc in accs], axis=0)
        o_ref[0, pl.ds(qrow, tq), :] = o.T.astype(BF16)
        return 0

    lax.fori_loop(0, nq, q_tile, 0)


def _mla_attn(qf, kf, vt, mask, *, tq, tk, causal):
    g, sq, _ = qf.shape
    sk = kf.shape[1]
    return pl.pallas_call(
        functools.partial(_mla_kernel, tq=tq, tk=tk, causal=causal), grid=(g, H_B // 2),
        in_specs=[pl.BlockSpec((1, sq, 2 * HEAD_PAD), lambda a, b: (a, 0, b)),
                  pl.BlockSpec((1, sk, 2 * HEAD_PAD), lambda a, b: (a, 0, b)),
                  pl.BlockSpec((1, sk // tk, 2 * DV_B, tk), lambda a, b: (a, 0, b, 0)),
                  _resident(mask.shape)],
        out_specs=pl.BlockSpec((1, sq, 2 * DV_B), lambda a, b: (a, 0, b)),
        out_shape=jax.ShapeDtypeStruct((g, sq, D_VB), BF16),
        scratch_shapes=[pltpu.VMEM((2, tk, tq), F32)],
        compiler_params=pltpu.CompilerParams(dimension_semantics=("arbitrary", "arbitrary"),
                                             vmem_limit_bytes=VMEM_LIMIT_V7X),
        name="mla_attn")(qf, kf, vt, mask)


_G0 = 8


def _merge_ffn_kernel(x_ref, ya_ref, yb_ref, ga_ref, gb_ref, st_ref, wa_ref, wb_ref, wo_ref, n1_ref, n2_ref, n3_ref,
                      wg_ref, wu_ref, cw_ref, cb_ref, wd_ref, y_ref, cs_ref, gbuf, *, tm):
    i = pl.program_id(1)
    ya = jnp.concatenate([ya_ref[0, g] for g in range(N_PAIRS)], axis=1)
    za = jnp.dot(ya, wa_ref[...], preferred_element_type=F32)
    zb = jnp.dot(yb_ref[0], wb_ref[...], preferred_element_type=F32)
    mix = jax.nn.sigmoid(ga_ref[0].astype(F32)) * za + jax.nn.sigmoid(gb_ref[0].astype(F32)) * zb
    mo = jnp.dot(mix.astype(BF16), wo_ref[...], preferred_element_type=F32)
    x1 = x_ref[0] + _rms(mo, n1_ref[...])
    xn = _rms(x1, n2_ref[...]).astype(BF16)

    @pl.when(i == 0)
    def _():
        gbuf[_G0 - 2:_G0, :] = st_ref[0]

    @pl.when(i > 0)
    def _():
        gbuf[_G0 - 2:_G0, :] = gbuf[_G0 + tm - 2:_G0 + tm, :]

    gbuf[_G0:_G0 + tm, :] = jnp.dot(xn, wg_ref[...], preferred_element_type=F32)
    cs_ref[0] = gbuf[_G0 + tm - 2:_G0 + tm, :]
    u = jnp.dot(xn, wu_ref[...], preferred_element_type=F32)
    c = cb_ref[...] + cw_ref[0:1, :] * gbuf[_G0 - 2:_G0 - 2 + tm, :]
    c = c + cw_ref[1:2, :] * gbuf[_G0 - 1:_G0 - 1 + tm, :]
    c = c + cw_ref[2:3, :] * gbuf[_G0:_G0 + tm, :]
    hid = (jax.nn.gelu(c, approximate=True) * u).astype(BF16)
    f = jnp.dot(hid, wd_ref[...], preferred_element_type=F32)
    y_ref[0] = x1 + _rms(f, n3_ref[...])


def _merge_ffn(x, ya, yb, ga, gb, state, wa, wb, wo, n1, n2, n3, wg, wu, cw, cb, wd, *, tm):
    g, s, _ = x.shape
    tok = lambda w: pl.BlockSpec((1, tm, w), lambda a, b: (a, b, 0))
    per_group = pl.BlockSpec((1, CONV_W - 1, D_FF), lambda a, b: (a, 0, 0))
    weights = (wa, wb, wo, n1, n2, n3, wg, wu, cw, cb, wd)
    return pl.pallas_call(
        functools.partial(_merge_ffn_kernel, tm=tm), grid=(g, s // tm),
        in_specs=[tok(D_MODEL), pl.BlockSpec((1, N_PAIRS, tm, PAIR_W), lambda a, b: (a, 0, b, 0)), tok(D_VB),
                  tok(D_MODEL), tok(D_MODEL), per_group]
                 + [_resident(w.shape) for w in weights],
        out_specs=(tok(D_MODEL), per_group),
        out_shape=(jax.ShapeDtypeStruct((g, s, D_MODEL), F32),
                   jax.ShapeDtypeStruct((g, CONV_W - 1, D_FF), F32)),
        scratch_shapes=[pltpu.VMEM((_G0 + tm, D_FF), F32)],
        compiler_params=pltpu.CompilerParams(dimension_semantics=("arbitrary", "arbitrary"),
                                             vmem_limit_bytes=VMEM_LIMIT_V7X),
        name="merge_ffn")(x, ya, yb, ga, gb, state, *weights)


def _rope_tables(pos, lane0, scale):
    half = DH_ROPE // 2
    inv = ROPE_BASE ** (-jnp.arange(half, dtype=F32) / half)
    ang = pos.astype(F32)[:, None] * inv[None, :]
    c, s = jnp.cos(ang), jnp.sin(ang)
    n = pos.shape[0]
    rest = jnp.zeros((n, HEAD_PAD - lane0 - DH_ROPE), F32)
    ct = jnp.concatenate([jnp.ones((n, lane0), F32), c, c, rest], axis=1) * scale
    st = jnp.concatenate([jnp.zeros((n, lane0), F32), -s, s, rest], axis=1) * scale
    return ct, st


def _band_bias(table, tq, nk):
    qpos = (nk - tq) + jnp.arange(tq)[None, :]
    kpos = jnp.arange(nk)[:, None]
    w = nk + tq
    d_lo, d_hi = 1 - tq, w - tq
    assert -MAX_REL <= d_lo and d_hi >= MAX_REL
    h = table.shape[0]
    ext = jnp.concatenate([table[:, d_lo + MAX_REL:], jnp.broadcast_to(table[:, -1:], (h, d_hi - MAX_REL))], axis=1)
    ext = jnp.roll(ext, -(nk - 1), axis=1)
    bias = jnp.tile(ext, (1, nk))[:, :nk * (w - 1)].reshape(h, nk, w - 1)[:, :, :tq].astype(F32)
    qc, kc = qpos // CHUNK, kpos // CHUNK
    valid = (kc >= qc - LEFT_CHUNKS) & (kc <= qc)
    return jnp.where(valid[None], bias * LOG2E, NEG)


def _chunk_mask(tk, tq):
    kc = jnp.arange(tk)[:, None] // CHUNK
    qc = jnp.arange(tq)[None, :] // CHUNK
    return jnp.where(kc <= qc, 0.0, NEG).astype(F32)


def _prep_weights(w_in, w_uk, w_uv):
    o = np.cumsum((0,) + IN_SIZES)
    col = lambda j: w_in[:, o[j]:o[j + 1]]
    d = w_in.shape[0]
    qn = col(3).reshape(d, H_B, DH_NOPE)
    qr = col(4).reshape(d, H_B, DH_ROPE)
    w_qf = jnp.concatenate([qn, qr, jnp.zeros((d, H_B, HEAD_PAD - DH_NOPE - DH_ROPE), F32)], axis=2).reshape(d, D_QF)
    w_kr = jnp.concatenate([col(6), jnp.zeros((d, HEAD_PAD - DH_ROPE), F32)], axis=1)
    wm = jnp.concatenate([col(0) * (A_SCALE * LOG2E), col(1), w_qf, col(5), w_kr, col(7), col(8)], axis=1).astype(BF16)
    wva = col(2).astype(BF16)
    wk = jnp.concatenate([w_uk, jnp.zeros((D_C, H_B, HEAD_PAD - DH_NOPE), F32)], axis=2).reshape(D_C, D_QF).astype(BF16)
    place = jnp.concatenate([jnp.zeros((DH_ROPE, DH_NOPE), F32), jnp.eye(DH_ROPE, dtype=F32),
                             jnp.zeros((DH_ROPE, HEAD_PAD - DH_NOPE - DH_ROPE), F32)], axis=1)
    ek = jnp.tile(place, (1, H_B)).astype(BF16)
    wvt = w_uv.reshape(D_C, D_VB).T.astype(BF16)
    return wm, wva.T, wva, wk, ek, wvt


def kernel(x_prompt, x_sample, cache_a_k, cache_a_v, cache_mla_ckv, cache_mla_krope, state_ffn_conv, norm_mix_pre,
           norm_mix_post, w_in, rel_bias_table, kv_norm, w_uk, w_uv, w_branch_a, w_branch_b, w_out, norm_ffn_pre,
           norm_ffn_post, w_ffn_gate, w_ffn_up, conv_w, conv_b, w_ffn_down):
    assert w_in.shape[0] == 1, "single layer"
    b, s, _ = x_prompt.shape
    db, t, _ = x_sample.shape
    past = cache_mla_ckv.shape[2]
    wcache = cache_a_k.shape[2]
    keep = min(A_WINDOW, s)
    tm1 = 512
    assert keep == tm1 and db * t == tm1 and wcache == A_WINDOW

    wm, wvat, wva, wk, ek, wvt = _prep_weights(w_in[0], w_uk[0], w_uv[0])
    row = lambda v: v.reshape(1, -1)
    proj_w = (row(norm_mix_pre[0]), wm, wvat, wva, row(kv_norm[0]))
    ffn_w = (w_branch_a[0].astype(BF16), w_branch_b[0].astype(BF16), w_out[0].astype(BF16), row(norm_mix_post[0]),
             row(norm_ffn_pre[0]), row(norm_ffn_post[0]), w_ffn_gate[0].astype(BF16), w_ffn_up[0].astype(BF16),
             conv_w[0], row(conv_b[0]), w_ffn_down[0].astype(BF16))
    table = rel_bias_table[0]

    pos = jnp.arange(s, dtype=jnp.int32)
    tabs = _rope_tables(pos, DH_NOPE, MLA_SCALE * LOG2E) + _rope_tables(pos, 0, 1.0)
    qa, ka, vat, kaf, vaf, qf, ckv, kr, ga, gb = _in_proj(x_prompt, *proj_w, *tabs, tm=tm1)
    t_mla = 512
    kf, vt = _kv_up(ckv, kr, wk, ek, wvt, tm=1024, tk=t_mla)
    ya = _band_prompt(qa, ka, vat, _band_bias(table, 256, A_WINDOW + 256), tm=tm1, tq=256)
    yb = _mla_attn(qf, kf, vt, _chunk_mask(t_mla, t_mla), tq=t_mla, tk=t_mla, causal=True)
    y_prompt, conv_p = _merge_ffn(x_prompt, ya, yb, ga, gb, jnp.zeros((b, CONV_W - 1, D_FF), F32), *ffn_w, tm=256)

    pos_s = jnp.tile(past + jnp.arange(t, dtype=jnp.int32), db)
    tabs_s = _rope_tables(pos_s, DH_NOPE, MLA_SCALE * LOG2E) + _rope_tables(pos_s, 0, 1.0)
    qa2, ka2, vat2, kaf2, vaf2, qf2, ckv2, kr2, ga2, gb2 = _in_proj(x_sample.reshape(1, db * t, D_MODEL), *proj_w,
                                                                    *tabs_s, tm=tm1)
    per_seq = lambda v: v.reshape(db, t, v.shape[-1])
    ckv2, kr2 = per_seq(ckv2[0]), per_seq(kr2[0])
    new_k = jnp.concatenate([cache_a_k[0].reshape(db, wcache, D_A), per_seq(kaf2[0])], axis=1)
    new_v = jnp.concatenate([cache_a_v[0].reshape(db, wcache, D_A), per_seq(vaf2[0])], axis=1)
    qa2 = jnp.swapaxes(qa2[0].reshape(N_PAIRS, db, t, PAIR_W), 0, 1)
    ya2 = _band_sample(qa2, new_k.astype(BF16), jnp.swapaxes(new_v, 1, 2).astype(BF16),
                       _band_bias(table, t, wcache + t))
    c_all = jnp.concatenate([cache_mla_ckv[0], ckv2], axis=1)
    kr_all = jnp.concatenate([cache_mla_krope[0], kr2], axis=1)
    kf2, vt2 = _kv_up(c_all, kr_all, wk, ek, wvt, tm=past + t, tk=past + t)
    yb2 = _mla_attn(per_seq(qf2[0]), kf2, vt2, jnp.zeros((8, 128), F32), tq=t, tk=past + t, causal=False)
    y_sample, conv_s = _merge_ffn(x_sample, ya2, yb2, per_seq(ga2[0]), per_seq(gb2[0]), state_ffn_conv[0], *ffn_w,
                                  tm=t)

    heads = lambda v: v.reshape(1, v.shape[0], v.shape[1], H_A, DH_A)
    return (y_prompt, y_sample,
            heads(kaf), heads(vaf), ckv[None], kr[None], conv_p[None],
            heads(new_k[:, -wcache:]), heads(new_v[:, -wcache:]), ckv2[None], kr2[None], conv_s[None])
```

```python
import functools

import jax
import jax.numpy as jnp
import numpy as np
from jax import lax
from jax.experimental import pallas as pl
from jax.experimental.pallas import tpu as pltpu

D_MODEL = 1024
CHUNK = 64
LEFT_CHUNKS = 8
A_WINDOW = LEFT_CHUNKS * CHUNK
H_A = 8
DH_A = 64
MAX_REL = 256
H_B = 8
DH_NOPE = 64
DH_ROPE = 32
DV_B = 64
D_C = 256
D_FF = 2816
CONV_W = 3
ROPE_BASE = 10000.0
EPS = 1e-6
A_SCALE = DH_A ** -0.5
MLA_SCALE = (DH_NOPE + DH_ROPE) ** -0.5
IN_SIZES = (H_A * DH_A, H_A * DH_A, H_A * DH_A, H_B * DH_NOPE, H_B * DH_ROPE, D_C, DH_ROPE, D_MODEL, D_MODEL)

HEAD_PAD = 128
D_A = H_A * DH_A
D_VB = H_B * DV_B
D_QF = H_B * HEAD_PAD
PAIR_W = 2 * DH_A
N_PAIRS = H_A // 2
NEG = -1e30
LOG2E = 1.4426950408889634
_L_ROWS = 16
VMEM_LIMIT_V7X = 56 * 1024 * 1024

F32 = jnp.float32
BF16 = jnp.bfloat16
_NT = (((1,), (1,)), ((), ()))


def _resident(shape):
    nd = len(shape)
    return pl.BlockSpec(shape, lambda *_: (0,) * nd, pipeline_mode=pl.Buffered(1))


def _rms(x, w):
    return x * lax.rsqrt(jnp.mean(x * x, axis=-1, keepdims=True) + EPS) * w


def _rope_lanes(x, c, s):
    lane = lax.broadcasted_iota(jnp.int32, x.shape, 1)
    partner = jnp.where(lane % 32 < 16, pltpu.roll(x, 128 - 16, 1), pltpu.roll(x, 16, 1))
    return x * c + partner * s


_C_QA, _C_KA, _C_QF, _C_CKV, _C_KR, _C_GA, _C_GB, _C_END = 0, 512, 1024, 2048, 2304, 2432, 3456, 4480


def _store_pairs(ref, v):
    for g in range(N_PAIRS):
        ref[0, g] = v[:, g * PAIR_W:(g + 1) * PAIR_W]


def _in_proj_kernel(x_ref, nw_ref, wm_ref, wvt_ref, wva_ref, kvn_ref, cq_ref, sq_ref, ck_ref, sk_ref,
                    qa_ref, ka_ref, vat_ref, kaf_ref, vaf_ref, qf_ref, ckv_ref, kr_ref, ga_ref, gb_ref, *, n):
    i = pl.program_id(1)
    xn = _rms(x_ref[0], nw_ref[...]).astype(BF16)

    def proj(lo, hi):
        return jnp.dot(xn, wm_ref[:, lo:hi], preferred_element_type=F32)

    _store_pairs(qa_ref, proj(_C_QA, _C_KA).astype(BF16))
    ka = proj(_C_KA, _C_QF)
    _store_pairs(ka_ref, ka.astype(BF16))
    kaf_ref[0] = ka
    vat_ref[0, 0] = lax.dot_general(wvt_ref[...], xn, _NT, preferred_element_type=F32).astype(BF16)

    def va_tail():
        vaf_ref[0] = jnp.dot(xn, wva_ref[...], preferred_element_type=F32)

    if n == 1:
        va_tail()
    else:
        pl.when(i == n - 1)(va_tail)

    cq, sq = cq_ref[...], sq_ref[...]
    for h in range(H_B):
        lo = _C_QF + h * HEAD_PAD
        qf_ref[0, :, h * HEAD_PAD:(h + 1) * HEAD_PAD] = _rope_lanes(proj(lo, lo + HEAD_PAD), cq, sq).astype(BF16)

    ckv_ref[0] = _rms(proj(_C_CKV, _C_KR), kvn_ref[...])
    kr_ref[0] = _rope_lanes(proj(_C_KR, _C_GA), ck_ref[...], sk_ref[...])[:, :DH_ROPE]
    ga_ref[0] = proj(_C_GA, _C_GB).astype(BF16)
    gb_ref[0] = proj(_C_GB, _C_END).astype(BF16)


def _in_proj(x, nw, wm, wvt, wva, kvn, cq, sq, ck, sk, *, tm):
    g, s, _ = x.shape
    n = s // tm
    tok = lambda w: pl.BlockSpec((1, tm, w), lambda a, b: (a, b, 0))
    tab = pl.BlockSpec((tm, HEAD_PAD), lambda a, b: (b, 0))
    tail = pl.BlockSpec((1, tm, D_A), lambda a, b: (a, 0, 0))
    out_shape = (
        jax.ShapeDtypeStruct((g, N_PAIRS, s, PAIR_W), BF16),
        jax.ShapeDtypeStruct((g, N_PAIRS, s, PAIR_W), BF16),
        jax.ShapeDtypeStruct((g, n, D_A, tm), BF16),
        jax.ShapeDtypeStruct((g, tm, D_A), F32),
        jax.ShapeDtypeStruct((g, tm, D_A), F32),
        jax.ShapeDtypeStruct((g, s, D_QF), BF16),
        jax.ShapeDtypeStruct((g, s, D_C), F32),
        jax.ShapeDtypeStruct((g, s, DH_ROPE), F32),
        jax.ShapeDtypeStruct((g, s, D_MODEL), BF16),
        jax.ShapeDtypeStruct((g, s, D_MODEL), BF16),
    )
    pairs = pl.BlockSpec((1, N_PAIRS, tm, PAIR_W), lambda a, b: (a, 0, b, 0))
    out_specs = (pairs, pairs, pl.BlockSpec((1, 1, D_A, tm), lambda a, b: (a, b, 0, 0)), tail, tail,
                 tok(D_QF), tok(D_C), tok(DH_ROPE), tok(D_MODEL), tok(D_MODEL))
    in_specs = [tok(D_MODEL), _resident(nw.shape), _resident(wm.shape), _resident(wvt.shape), _resident(wva.shape),
                _resident(kvn.shape), tab, tab, tab, tab]
    return pl.pallas_call(
        functools.partial(_in_proj_kernel, n=n), grid=(g, n), in_specs=in_specs, out_specs=out_specs, out_shape=out_shape,
        compiler_params=pltpu.CompilerParams(dimension_semantics=("arbitrary", "arbitrary"),
                                             vmem_limit_bytes=VMEM_LIMIT_V7X),
        name="in_proj")(x, nw, wm, wvt, wva, kvn, cq, sq, ck, sk)


def _kv_up_kernel(ckv_ref, kr_ref, wk_ref, ek_ref, wvt_ref, kf_ref, vt_ref, *, tk):
    c = ckv_ref[0].astype(BF16)
    kf = jnp.dot(c, wk_ref[...], preferred_element_type=F32)
    kf = kf + jnp.dot(kr_ref[0].astype(BF16), ek_ref[...], preferred_element_type=F32)
    kf_ref[0] = kf.astype(BF16)
    for j in range(c.shape[0] // tk):
        vt_ref[0, j] = lax.dot_general(wvt_ref[...], c[j * tk:(j + 1) * tk], _NT,
                                       preferred_element_type=F32).astype(BF16)


def _kv_up(ckv, kr, wk, ek, wvt, *, tm, tk):
    g, s, _ = ckv.shape
    return pl.pallas_call(
        functools.partial(_kv_up_kernel, tk=tk), grid=(g, s // tm),
        in_specs=[pl.BlockSpec((1, tm, D_C), lambda a, b: (a, b, 0)),
                  pl.BlockSpec((1, tm, DH_ROPE), lambda a, b: (a, b, 0)),
                  _resident(wk.shape), _resident(ek.shape), _resident(wvt.shape)],
        out_specs=(pl.BlockSpec((1, tm, D_QF), lambda a, b: (a, b, 0)),
                   pl.BlockSpec((1, tm // tk, D_VB, tk), lambda a, b: (a, b, 0, 0))),
        out_shape=(jax.ShapeDtypeStruct((g, s, D_QF), BF16),
                   jax.ShapeDtypeStruct((g, s // tk, D_VB, tk), BF16)),
        compiler_params=pltpu.CompilerParams(dimension_semantics=("arbitrary", "arbitrary"),
                                             vmem_limit_bytes=VMEM_LIMIT_V7X),
        name="kv_up")(ckv, kr, wk, ek, wvt)


def _band_units(units, bias_ref, n_pairs=H_A // 2):
    def scores(u, g):
        q, k_of, _, _ = units[u]
        lanes = slice(g * 2 * DH_A, (g + 1) * 2 * DH_A)
        lane = lax.broadcasted_iota(jnp.int32, (q.shape[0], 2 * DH_A), 1)
        k2 = k_of(lanes)
        return [lax.dot_general(k2, jnp.where((lane >= DH_A) == (hl == 1), q[:, lanes], jnp.zeros((), BF16)), _NT,
                                preferred_element_type=F32) for hl in range(2)]

    steps = [(u, g) for u in range(len(units)) for g in range(n_pairs)]
    outs = [[] for _ in units]
    pend = scores(*steps[0])
    for j, (u, g) in enumerate(steps):
        nxt = scores(*steps[j + 1]) if j + 1 < len(steps) else None
        _, _, v_of, b0 = units[u]
        for hl in range(2):
            h = 2 * g + hl
            nk = pend[hl].shape[0]
            s = pend[hl] + bias_ref[h, b0:b0 + nk, :]
            p = jnp.exp2(s - jnp.max(s, axis=0, keepdims=True)).astype(BF16)
            v = jnp.concatenate([v_of(slice(h * DH_A, (h + 1) * DH_A)), jnp.ones((_L_ROWS, nk), BF16)], axis=0)
            o = jnp.dot(v, p, preferred_element_type=F32)
            outs[u].append(o[:DH_A] / o[DH_A:DH_A + 1])
        pend = nxt
    return [jnp.concatenate(o, axis=0).T for o in outs]


def _band_prompt_kernel(q_ref, kp_ref, kc_ref, vp_ref, vc_ref, bias_ref, o_ref, s_scr, *, tm, tq):
    subs = [(lo, lo + tq) for lo in range(0, tm, tq)]
    lane = lax.broadcasted_iota(jnp.int32, (tq, PAIR_W), 1)

    def tile(first):
        def window(lo, hi):
            return (hi, tm - lo) if first else (tm - lo + hi, 0)

        def scores(g, j):
            lo, hi = subs[j]
            nk, b0 = window(lo, hi)
            q2 = q_ref[0, g, lo:hi, :]
            qcat = jnp.concatenate([jnp.where(lane < DH_A, q2, jnp.zeros((), BF16)),
                                    jnp.where(lane >= DH_A, q2, jnp.zeros((), BF16))], axis=0)
            k2 = kc_ref[0, g, :hi, :]
            if not first:
                k2 = jnp.concatenate([kp_ref[0, g, lo:, :], k2], axis=0)
            bias = jnp.concatenate([bias_ref[2 * g + hl, b0:b0 + nk, :] for hl in range(2)], axis=1)
            return lax.dot_general(k2, qcat, _NT, preferred_element_type=F32) + bias

        def stash(j, s):
            s_scr[j, :s.shape[0], :] = s
            return jnp.max(s, axis=0, keepdims=True)

        def consume(g, j, col_max):
            lo, hi = subs[j]
            nk, _ = window(lo, hi)
            outs = []
            for hl in range(2):
                cols = slice(hl * tq, (hl + 1) * tq)
                p = jnp.exp2(s_scr[j, :nk, cols] - col_max[:, cols]).astype(BF16)
                rows = pl.ds(g * PAIR_W + hl * DH_A, DH_A)
                v = vc_ref[0, 0, rows, :hi]
                if not first:
                    v = jnp.concatenate([vp_ref[0, 0, rows, lo:], v], axis=1)
                v = jnp.concatenate([v, jnp.ones((_L_ROWS, nk), BF16)], axis=0)
                o = jnp.dot(v, p, preferred_element_type=F32)
                outs.append(o[:DH_A] / o[DH_A:DH_A + 1])
            o_ref[0, g, lo:hi, :] = jnp.concatenate(outs, axis=0).T.astype(BF16)

        def body(g, col_max):
            new_max = []
            for j in range(len(subs)):
                nxt = scores(g + 1, j)
                consume(g, j, col_max[j])
                new_max.append(stash(j, nxt))
            return tuple(new_max)

        col_max = tuple(stash(j, scores(0, j)) for j in range(len(subs)))
        for g in range(N_PAIRS - 1):
            col_max = body(g, col_max)
        for j in range(len(subs)):
            consume(N_PAIRS - 1, j, col_max[j])

    pl.when(pl.program_id(1) == 0)(functools.partial(tile, True))
    pl.when(pl.program_id(1) > 0)(functools.partial(tile, False))


def _band_prompt(qa, ka, vat, bias, *, tm, tq):
    g, _, s, _ = qa.shape
    cur = lambda a, b: (a, 0, b, 0)
    prev = lambda a, b: (a, 0, jnp.maximum(b - 1, 0), 0)
    prev_v = lambda a, b: (a, jnp.maximum(b - 1, 0), 0, 0)
    pairs = lambda idx: pl.BlockSpec((1, N_PAIRS, tm, PAIR_W), idx)
    return pl.pallas_call(
        functools.partial(_band_prompt_kernel, tm=tm, tq=tq), grid=(g, s // tm),
        in_specs=[pairs(cur), pairs(prev), pairs(cur),
                  pl.BlockSpec((1, 1, D_A, tm), prev_v), pl.BlockSpec((1, 1, D_A, tm), lambda a, b: (a, b, 0, 0)),
                  _resident(bias.shape)],
        out_specs=pairs(cur),
        out_shape=jax.ShapeDtypeStruct((g, N_PAIRS, s, PAIR_W), BF16),
        scratch_shapes=[pltpu.VMEM((tm // tq, tm + tq, 2 * tq), F32)],
        compiler_params=pltpu.CompilerParams(dimension_semantics=("arbitrary", "arbitrary"),
                                             vmem_limit_bytes=VMEM_LIMIT_V7X),
        name="band_prompt")(qa, ka, ka, vat, vat, bias)


def _band_sample_kernel(q_ref, k_ref, vt_ref, bias_ref, o_ref):
    q = jnp.concatenate([q_ref[0, g] for g in range(N_PAIRS)], axis=1)
    unit = (q, lambda lanes: k_ref[0, :, lanes], lambda rows: vt_ref[0, rows, :], 0)
    _store_pairs(o_ref, _band_units([unit], bias_ref)[0].astype(BF16))


def _band_sample(qa, k_all, vt_all, bias):
    g, _, t, _ = qa.shape
    nk = k_all.shape[1]
    pairs = pl.BlockSpec((1, N_PAIRS, t, PAIR_W), lambda a: (a, 0, 0, 0))
    return pl.pallas_call(
        _band_sample_kernel, grid=(g,),
        in_specs=[pairs,
                  pl.BlockSpec((1, nk, D_A), lambda a: (a, 0, 0)),
                  pl.BlockSpec((1, D_A, nk), lambda a: (a, 0, 0)),
                  _resident(bias.shape)],
        out_specs=pairs,
        out_shape=jax.ShapeDtypeStruct((g, N_PAIRS, t, PAIR_W), BF16),
        compiler_params=pltpu.CompilerParams(dimension_semantics=("arbitrary",),
                                             vmem_limit_bytes=VMEM_LIMIT_V7X),
        name="band_sample")(qa, k_all, vt_all, bias)


def _mla_kernel(q_ref, k_ref, vt_ref, mask_ref, o_ref, s_scr, *, tq, tk, causal):
    nq = q_ref.shape[1] // tq
    nk = k_ref.shape[1] // tk
    ones = jnp.ones((_L_ROWS, tk), BF16)

    def q_tile(i, _):
        qrow = pl.multiple_of(i * tq, tq)
        qs = [q_ref[0, pl.ds(qrow, tq), hl * HEAD_PAD:(hl + 1) * HEAD_PAD] for hl in range(2)]

        def scores(kk, hl):
            krow = pl.multiple_of(kk * tk, tk)
            return lax.dot_general(k_ref[0, pl.ds(krow, tk), hl * HEAD_PAD:(hl + 1) * HEAD_PAD], qs[hl], _NT,
                                   preferred_element_type=F32)

        def stash(hl, s):
            s_scr[hl] = s
            return jnp.max(s, axis=0, keepdims=True)

        def consume(kk, hl, m, acc, tile_max, masked):
            if masked:
                s = s_scr[hl] + mask_ref[...]
                m_new = jnp.maximum(m, jnp.max(s, axis=0, keepdims=True))
            else:
                s = s_scr[hl]
                m_new = jnp.maximum(m, tile_max)
            p = jnp.exp2(s - m_new).astype(BF16)
            v = jnp.concatenate([vt_ref[0, kk, hl * DV_B:(hl + 1) * DV_B, :], ones], axis=0)
            return m_new, acc * jnp.exp2(m - m_new) + jnp.dot(v, p, preferred_element_type=F32)

        def body(t, carry):
            out = []
            for hl in range(2):
                m, acc, tile_max = carry[3 * hl:3 * hl + 3]
                nxt = scores(t + 1, hl)
                out += list(consume(t, hl, m, acc, tile_max, False)) + [stash(hl, nxt)]
            return tuple(out)

        last = i if causal else nk - 1
        init = ()
        for hl in range(2):
            init += (jnp.full((1, tq), NEG, F32), jnp.zeros((DV_B + _L_ROWS, tq), F32), stash(hl, scores(0, hl)))
        pairs_of_tiles = lax.shift_right_logical(last, 1) if causal else last // 2
        carry = lax.fori_loop(0, pairs_of_tiles, lambda u, c: body(2 * u + 1, body(2 * u, c)), init)
        carry = lax.fori_loop(0, last & 1, lambda _, c: body(last - 1, c), carry)
        accs =[consume(last, hl, *carry[3 * hl:3 * hl + 3], causal)[1] for hl in range(2)]
        o = jnp.concatenate([acc[:DV_B] / acc[DV_B:DV_B + 1] for acc in accs], axis=0)
        o_ref[0, pl.ds(qrow, tq), :] = o.T.astype(BF16)
        return 0

    lax.fori_loop(0, nq, q_tile, 0)


def _mla_attn(qf, kf, vt, mask, *, tq, tk, causal):
    g, sq, _ = qf.shape
    sk = kf.shape[1]
    return pl.pallas_call(
        functools.partial(_mla_kernel, tq=tq, tk=tk, causal=causal), grid=(g, H_B // 2),
        in_specs=[pl.BlockSpec((1, sq, 2 * HEAD_PAD), lambda a, b: (a, 0, b)),
                  pl.BlockSpec((1, sk, 2 * HEAD_PAD), lambda a, b: (a, 0, b)),
                  pl.BlockSpec((1, sk // tk, 2 * DV_B, tk), lambda a, b: (a, 0, b, 0)),
                  _resident(mask.shape)],
        out_specs=pl.BlockSpec((1, sq, 2 * DV_B), lambda a, b: (a, 0, b)),
        out_shape=jax.ShapeDtypeStruct((g, sq, D_VB), BF16),
        scratch_shapes=[pltpu.VMEM((2, tk, tq), F32)],
        compiler_params=pltpu.CompilerParams(dimension_semantics=("arbitrary", "arbitrary"),
                                             vmem_limit_bytes=VMEM_LIMIT_V7X),
        name="mla_attn")(qf, kf, vt, mask)


_G0 = 8


def _merge_ffn_kernel(x_ref, ya_ref, yb_ref, ga_ref, gb_ref, st_ref, wa_ref, wb_ref, wo_ref, n1_ref, n2_ref, n3_ref,
                      wg_ref, wu_ref, cw_ref, cb_ref, wd_ref, y_ref, cs_ref, gbuf, *, tm):
    i = pl.program_id(1)
    ya = jnp.concatenate([ya_ref[0, g] for g in range(N_PAIRS)], axis=1)
    za = jnp.dot(ya, wa_ref[...], preferred_element_type=F32)
    zb = jnp.dot(yb_ref[0], wb_ref[...], preferred_element_type=F32)
    mix = jax.nn.sigmoid(ga_ref[0].astype(F32)) * za + jax.nn.sigmoid(gb_ref[0].astype(F32)) * zb
    mo = jnp.dot(mix.astype(BF16), wo_ref[...], preferred_element_type=F32)
    x1 = x_ref[0] + _rms(mo, n1_ref[...])
    xn = _rms(x1, n2_ref[...]).astype(BF16)

    @pl.when(i == 0)
    def _():
        gbuf[_G0 - 2:_G0, :] = st_ref[0]

    @pl.when(i > 0)
    def _():
        gbuf[_G0 - 2:_G0, :] = gbuf[_G0 + tm - 2:_G0 + tm, :]

    gbuf[_G0:_G0 + tm, :] = jnp.dot(xn, wg_ref[...], preferred_element_type=F32)
    cs_ref[0] = gbuf[_G0 + tm - 2:_G0 + tm, :]
    u = jnp.dot(xn, wu_ref[...], preferred_element_type=F32)
    c = cb_ref[...] + cw_ref[0:1, :] * gbuf[_G0 - 2:_G0 - 2 + tm, :]
    c = c + cw_ref[1:2, :] * gbuf[_G0 - 1:_G0 - 1 + tm, :]
    c = c + cw_ref[2:3, :] * gbuf[_G0:_G0 + tm, :]
    hid = (jax.nn.gelu(c, approximate=True) * u).astype(BF16)
    f = jnp.dot(hid, wd_ref[...], preferred_element_type=F32)
    y_ref[0] = x1 + _rms(f, n3_ref[...])


def _merge_ffn(x, ya, yb, ga, gb, state, wa, wb, wo, n1, n2, n3, wg, wu, cw, cb, wd, *, tm):
    g, s, _ = x.shape
    tok = lambda w: pl.BlockSpec((1, tm, w), lambda a, b: (a, b, 0))
    per_group = pl.BlockSpec((1, CONV_W - 1, D_FF), lambda a, b: (a, 0, 0))
    weights = (wa, wb, wo, n1, n2, n3, wg, wu, cw, cb, wd)
    return pl.pallas_call(
        functools.partial(_merge_ffn_kernel, tm=tm), grid=(g, s // tm),
        in_specs=[tok(D_MODEL), pl.BlockSpec((1, N_PAIRS, tm, PAIR_W), lambda a, b: (a, 0, b, 0)), tok(D_VB),
                  tok(D_MODEL), tok(D_MODEL), per_group]
                 + [_resident(w.shape) for w in weights],
        out_specs=(tok(D_MODEL), per_group),
        out_shape=(jax.ShapeDtypeStruct((g, s, D_MODEL), F32),
                   jax.ShapeDtypeStruct((g, CONV_W - 1, D_FF), F32)),
        scratch_shapes=[pltpu.VMEM((_G0 + tm, D_FF), F32)],
        compiler_params=pltpu.CompilerParams(dimension_semantics=("arbitrary", "arbitrary"),
                                             vmem_limit_bytes=VMEM_LIMIT_V7X),
        name="merge_ffn")(x, ya, yb, ga, gb, state, *weights)


def _rope_tables(pos, lane0, scale):
    half = DH_ROPE // 2
    inv = ROPE_BASE ** (-jnp.arange(half, dtype=F32) / half)
    ang = pos.astype(F32)[:, None] * inv[None, :]
    c, s = jnp.cos(ang), jnp.sin(ang)
    n = pos.shape[0]
    rest = jnp.zeros((n, HEAD_PAD - lane0 - DH_ROPE), F32)
    ct = jnp.concatenate([jnp.ones((n, lane0), F32), c, c, rest], axis=1) * scale
    st = jnp.concatenate([jnp.zeros((n, lane0), F32), -s, s, rest], axis=1) * scale
    return ct, st


def _band_bias_kernel(ext_ref, o_ref, *, tq, nk):
    w = ext_ref.shape[-1]
    toeplitz = pltpu.roll(jnp.broadcast_to(ext_ref[0], (nk, w)), 0, 1, stride=1, stride_axis=0)[:, :tq]
    qc = (nk - tq + lax.broadcasted_iota(jnp.int32, (nk, tq), 1)) // CHUNK
    kc = lax.broadcasted_iota(jnp.int32, (nk, tq), 0) // CHUNK
    o_ref[0] = jnp.where((kc >= qc - LEFT_CHUNKS) & (kc <= qc), toeplitz, NEG)


def _band_bias(table, tq, nk):
    w = nk + tq
    d_lo, d_hi = 1 - tq, w - tq
    assert -MAX_REL <= d_lo and d_hi >= MAX_REL and w % 128 == 0
    h = table.shape[0]
    ext = jnp.concatenate([table[:, d_lo + MAX_REL:], jnp.broadcast_to(table[:, -1:], (h, d_hi - MAX_REL))], axis=1)
    ext = (jnp.roll(ext, -(nk - 1), axis=1) * LOG2E).reshape(h, 1, w)
    return pl.pallas_call(
        functools.partial(_band_bias_kernel, tq=tq, nk=nk), grid=(h,),
        in_specs=[pl.BlockSpec((1, 1, w), lambda a: (a, 0, 0))],
        out_specs=pl.BlockSpec((1, nk, tq), lambda a: (a, 0, 0)),
        out_shape=jax.ShapeDtypeStruct((h, nk, tq), F32),
        compiler_params=pltpu.CompilerParams(dimension_semantics=("arbitrary",)),
        name="band_bias")(ext)


def _chunk_mask(tk, tq):
    kc = jnp.arange(tk)[:, None] // CHUNK
    qc = jnp.arange(tq)[None, :] // CHUNK
    return jnp.where(kc <= qc, 0.0, NEG).astype(F32)


def _prep_weights(w_in, w_uk, w_uv):
    o = np.cumsum((0,) + IN_SIZES)
    col = lambda j: w_in[:, o[j]:o[j + 1]]
    d = w_in.shape[0]
    qn = col(3).reshape(d, H_B, DH_NOPE)
    qr = col(4).reshape(d, H_B, DH_ROPE)
    w_qf = jnp.concatenate([qn, qr, jnp.zeros((d, H_B, HEAD_PAD - DH_NOPE - DH_ROPE), F32)], axis=2).reshape(d, D_QF)
    w_kr = jnp.concatenate([col(6), jnp.zeros((d, HEAD_PAD - DH_ROPE), F32)], axis=1)
    wm = jnp.concatenate([col(0) * (A_SCALE * LOG2E), col(1), w_qf, col(5), w_kr, col(7), col(8)], axis=1).astype(BF16)
    wva = col(2).astype(BF16)
    wk = jnp.concatenate([w_uk, jnp.zeros((D_C, H_B, HEAD_PAD - DH_NOPE), F32)], axis=2).reshape(D_C, D_QF).astype(BF16)
    place = jnp.concatenate([jnp.zeros((DH_ROPE, DH_NOPE), F32), jnp.eye(DH_ROPE, dtype=F32),
                             jnp.zeros((DH_ROPE, HEAD_PAD - DH_NOPE - DH_ROPE), F32)], axis=1)
    ek = jnp.tile(place, (1, H_B)).astype(BF16)
    wvt = w_uv.reshape(D_C, D_VB).T.astype(BF16)
    return wm, wva.T, wva, wk, ek, wvt


def kernel(x_prompt, x_sample, cache_a_k, cache_a_v, cache_mla_ckv, cache_mla_krope, state_ffn_conv, norm_mix_pre,
           norm_mix_post, w_in, rel_bias_table, kv_norm, w_uk, w_uv, w_branch_a, w_branch_b, w_out, norm_ffn_pre,
           norm_ffn_post, w_ffn_gate, w_ffn_up, conv_w, conv_b, w_ffn_down):
    assert w_in.shape[0] == 1, "single layer"
    b, s, _ = x_prompt.shape
    db, t, _ = x_sample.shape
    past = cache_mla_ckv.shape[2]
    wcache = cache_a_k.shape[2]
    keep = min(A_WINDOW, s)
    tm1 = 512
    assert keep == tm1 and db * t == tm1 and wcache == A_WINDOW

    wm, wvat, wva, wk, ek, wvt = _prep_weights(w_in[0], w_uk[0], w_uv[0])
    row = lambda v: v.reshape(1, -1)
    proj_w = (row(norm_mix_pre[0]), wm, wvat, wva, row(kv_norm[0]))
    ffn_w = (w_branch_a[0].astype(BF16), w_branch_b[0].astype(BF16), w_out[0].astype(BF16), row(norm_mix_post[0]),
             row(norm_ffn_pre[0]), row(norm_ffn_post[0]), w_ffn_gate[0].astype(BF16), w_ffn_up[0].astype(BF16),
             conv_w[0], row(conv_b[0]), w_ffn_down[0].astype(BF16))
    table = rel_bias_table[0]

    pos = jnp.arange(s, dtype=jnp.int32)
    tabs = _rope_tables(pos, DH_NOPE, MLA_SCALE * LOG2E) + _rope_tables(pos, 0, 1.0)
    qa, ka, vat, kaf, vaf, qf, ckv, kr, ga, gb = _in_proj(x_prompt, *proj_w, *tabs, tm=tm1)
    t_mla = 512
    kf, vt = _kv_up(ckv, kr, wk, ek, wvt, tm=1024, tk=t_mla)
    ya = _band_prompt(qa, ka, vat, _band_bias(table, 256, A_WINDOW + 256), tm=tm1, tq=256)
    yb = _mla_attn(qf, kf, vt, _chunk_mask(t_mla, t_mla), tq=t_mla, tk=t_mla, causal=True)
    y_prompt, conv_p = _merge_ffn(x_prompt, ya, yb, ga, gb, jnp.zeros((b, CONV_W - 1, D_FF), F32), *ffn_w, tm=256)

    pos_s = jnp.tile(past + jnp.arange(t, dtype=jnp.int32), db)
    tabs_s = _rope_tables(pos_s, DH_NOPE, MLA_SCALE * LOG2E) + _rope_tables(pos_s, 0, 1.0)
    qa2, ka2, vat2, kaf2, vaf2, qf2, ckv2, kr2, ga2, gb2 = _in_proj(x_sample.reshape(1, db * t, D_MODEL), *proj_w,
                                                                    *tabs_s, tm=tm1)
    per_seq = lambda v: v.reshape(db, t, v.shape[-1])
    ckv2, kr2 = per_seq(ckv2[0]), per_seq(kr2[0])
    new_k = jnp.concatenate([cache_a_k[0].reshape(db, wcache, D_A), per_seq(kaf2[0])], axis=1)
    new_v = jnp.concatenate([cache_a_v[0].reshape(db, wcache, D_A), per_seq(vaf2[0])], axis=1)
    qa2 = jnp.swapaxes(qa2[0].reshape(N_PAIRS, db, t, PAIR_W), 0, 1)
    ya2 = _band_sample(qa2, new_k.astype(BF16), jnp.swapaxes(new_v, 1, 2).astype(BF16),
                       _band_bias(table, t, wcache + t))
    c_all = jnp.concatenate([cache_mla_ckv[0], ckv2], axis=1)
    kr_all = jnp.concatenate([cache_mla_krope[0], kr2], axis=1)
    kf2, vt2 = _kv_up(c_all, kr_all, wk, ek, wvt, tm=past + t, tk=past + t)
    yb2 = _mla_attn(per_seq(qf2[0]), kf2, vt2, jnp.zeros((8, 128), F32), tq=t, tk=past + t, causal=False)
    y_sample, conv_s = _merge_ffn(x_sample, ya2, yb2, per_seq(ga2[0]), per_seq(gb2[0]), state_ffn_conv[0], *ffn_w,
                                  tm=t)

    heads = lambda v: v.reshape(1, v.shape[0], v.shape[1], H_A, DH_A)
    return (y_prompt, y_sample,
            heads(kaf), heads(vaf), ckv[None], kr[None], conv_p[None],
            heads(new_k[:, -wcache:]), heads(new_v[:, -wcache:]), ckv2[None], kr2[None], conv_s[None])
```

```python
import functools

import jax
import jax.numpy as jnp
import numpy as np
from jax import lax
from jax.experimental import pallas as pl
from jax.experimental.pallas import tpu as pltpu

D_MODEL = 1024
CHUNK = 64
LEFT_CHUNKS = 8
A_WINDOW = LEFT_CHUNKS * CHUNK
H_A = 8
DH_A = 64
MAX_REL = 256
H_B = 8
DH_NOPE = 64
DH_ROPE = 32
DV_B = 64
D_C = 256
D_FF = 2816
CONV_W = 3
ROPE_BASE = 10000.0
EPS = 1e-6
A_SCALE = DH_A ** -0.5
MLA_SCALE = (DH_NOPE + DH_ROPE) ** -0.5
IN_SIZES = (H_A * DH_A, H_A * DH_A, H_A * DH_A, H_B * DH_NOPE, H_B * DH_ROPE, D_C, DH_ROPE, D_MODEL, D_MODEL)

HEAD_PAD = 128
D_A = H_A * DH_A
D_VB = H_B * DV_B
D_QF = H_B * HEAD_PAD
PAIR_W = 2 * DH_A
N_PAIRS = H_A // 2
NEG = -1e30
LOG2E = 1.4426950408889634
_L_ROWS = 16
VMEM_LIMIT_V7X = 56 * 1024 * 1024

F32 = jnp.float32
BF16 = jnp.bfloat16
_NT = (((1,), (1,)), ((), ()))


def _resident(shape):
    nd = len(shape)
    return pl.BlockSpec(shape, lambda *_: (0,) * nd, pipeline_mode=pl.Buffered(1))


def _rms(x, w):
    return x * lax.rsqrt(jnp.mean(x * x, axis=-1, keepdims=True) + EPS) * w


def _rope_lanes(x, c, s):
    lane = lax.broadcasted_iota(jnp.int32, x.shape, 1)
    partner = jnp.where(lane % 32 < 16, pltpu.roll(x, 128 - 16, 1), pltpu.roll(x, 16, 1))
    return x * c + partner * s


_C_QA, _C_KA, _C_QF, _C_CKV, _C_GA, _C_GB, _C_END = 0, 512, 1024, 2048, 2304, 3328, 4352
_KR_LANE = DH_NOPE + DH_ROPE


def _store_pairs(ref, v):
    for g in range(N_PAIRS):
        ref[0, g] = v[:, g * PAIR_W:(g + 1) * PAIR_W]


def _in_proj_kernel(x_ref, nw_ref, wm_ref, wvt_ref, wva_ref, kvn_ref, cq_ref, sq_ref, ck_ref, sk_ref,
                    qa_ref, ka_ref, vat_ref, kaf_ref, vaf_ref, qf_ref, ckv_ref, kr_ref, ga_ref, gb_ref, *, n):
    i = pl.program_id(1)
    xn = _rms(x_ref[0], nw_ref[...]).astype(BF16)

    def proj(lo, hi):
        return jnp.dot(xn, wm_ref[:, lo:hi], preferred_element_type=F32)

    _store_pairs(qa_ref, proj(_C_QA, _C_KA).astype(BF16))
    ka = proj(_C_KA, _C_QF)
    _store_pairs(ka_ref, ka.astype(BF16))
    kaf_ref[0] = ka
    vat_ref[0, 0] = lax.dot_general(wvt_ref[...], xn, _NT, preferred_element_type=F32).astype(BF16)

    def va_tail():
        vaf_ref[0] = jnp.dot(xn, wva_ref[...], preferred_element_type=F32)

    if n == 1:
        va_tail()
    else:
        pl.when(i == n - 1)(va_tail)

    cq, sq = cq_ref[...], sq_ref[...]
    qraw = proj(_C_QF, _C_CKV)
    for h in range(H_B):
        slab = slice(h * HEAD_PAD, (h + 1) * HEAD_PAD)
        qf_ref[0, :, slab] = _rope_lanes(qraw[:, slab], cq, sq).astype(BF16)
    kr = _rope_lanes(qraw[:, :HEAD_PAD], ck_ref[...], sk_ref[...])
    kr_ref[0] = pltpu.roll(kr, HEAD_PAD - _KR_LANE, 1)[:, :DH_ROPE]

    ckv_ref[0] = _rms(proj(_C_CKV, _C_GA), kvn_ref[...])
    ga_ref[0] = proj(_C_GA, _C_GB).astype(BF16)
    gb_ref[0] = proj(_C_GB, _C_END).astype(BF16)


def _in_proj(x, nw, wm, wvt, wva, kvn, cq, sq, ck, sk, *, tm):
    g, s, _ = x.shape
    n = s // tm
    tok = lambda w: pl.BlockSpec((1, tm, w), lambda a, b: (a, b, 0))
    tab = pl.BlockSpec((tm, HEAD_PAD), lambda a, b: (b, 0))
    tail = pl.BlockSpec((1, tm, D_A), lambda a, b: (a, 0, 0))
    out_shape = (
        jax.ShapeDtypeStruct((g, N_PAIRS, s, PAIR_W), BF16),
        jax.ShapeDtypeStruct((g, N_PAIRS, s, PAIR_W), BF16),
        jax.ShapeDtypeStruct((g, n, D_A, tm), BF16),
        jax.ShapeDtypeStruct((g, tm, D_A), F32),
        jax.ShapeDtypeStruct((g, tm, D_A), F32),
        jax.ShapeDtypeStruct((g, s, D_QF), BF16),
        jax.ShapeDtypeStruct((g, s, D_C), F32),
        jax.ShapeDtypeStruct((g, s, DH_ROPE), F32),
        jax.ShapeDtypeStruct((g, s, D_MODEL), BF16),
        jax.ShapeDtypeStruct((g, s, D_MODEL), BF16),
    )
    pairs = pl.BlockSpec((1, N_PAIRS, tm, PAIR_W), lambda a, b: (a, 0, b, 0))
    out_specs = (pairs, pairs, pl.BlockSpec((1, 1, D_A, tm), lambda a, b: (a, b, 0, 0)), tail, tail,
                 tok(D_QF), tok(D_C), tok(DH_ROPE), tok(D_MODEL), tok(D_MODEL))
    in_specs = [tok(D_MODEL), _resident(nw.shape), _resident(wm.shape), _resident(wvt.shape), _resident(wva.shape),
                _resident(kvn.shape), tab, tab, tab, tab]
    return pl.pallas_call(
        functools.partial(_in_proj_kernel, n=n), grid=(g, n), in_specs=in_specs, out_specs=out_specs, out_shape=out_shape,
        compiler_params=pltpu.CompilerParams(dimension_semantics=("arbitrary", "arbitrary"),
                                             vmem_limit_bytes=VMEM_LIMIT_V7X),
        name="in_proj")(x, nw, wm, wvt, wva, kvn, cq, sq, ck, sk)


def _kv_up_kernel(ckv_ref, kr_ref, wk_ref, ek_ref, wvt_ref, kf_ref, vt_ref, *, tk):
    c = ckv_ref[0].astype(BF16)
    kf = jnp.dot(c, wk_ref[...], preferred_element_type=F32)
    kf = kf + jnp.dot(kr_ref[0].astype(BF16), ek_ref[...], preferred_element_type=F32)
    kf_ref[0] = kf.astype(BF16)
    for j in range(c.shape[0] // tk):
        vt_ref[0, j] = lax.dot_general(wvt_ref[...], c[j * tk:(j + 1) * tk], _NT,
                                       preferred_element_type=F32).astype(BF16)


def _kv_up(ckv, kr, wk, ek, wvt, *, tm, tk):
    g, s, _ = ckv.shape
    return pl.pallas_call(
        functools.partial(_kv_up_kernel, tk=tk), grid=(g, s // tm),
        in_specs=[pl.BlockSpec((1, tm, D_C), lambda a, b: (a, b, 0)),
                  pl.BlockSpec((1, tm, DH_ROPE), lambda a, b: (a, b, 0)),
                  _resident(wk.shape), _resident(ek.shape), _resident(wvt.shape)],
        out_specs=(pl.BlockSpec((1, tm, D_QF), lambda a, b: (a, b, 0)),
                   pl.BlockSpec((1, tm // tk, D_VB, tk), lambda a, b: (a, b, 0, 0))),
        out_shape=(jax.ShapeDtypeStruct((g, s, D_QF), BF16),
                   jax.ShapeDtypeStruct((g, s // tk, D_VB, tk), BF16)),
        compiler_params=pltpu.CompilerParams(dimension_semantics=("arbitrary", "arbitrary"),
                                             vmem_limit_bytes=VMEM_LIMIT_V7X),
        name="kv_up")(ckv, kr, wk, ek, wvt)


def _band_units(units, bias_ref, n_pairs=H_A // 2):
    def scores(u, g):
        q, k_of, _, _ = units[u]
        lanes = slice(g * 2 * DH_A, (g + 1) * 2 * DH_A)
        lane = lax.broadcasted_iota(jnp.int32, (q.shape[0], 2 * DH_A), 1)
        k2 = k_of(lanes)
        return [lax.dot_general(k2, jnp.where((lane >= DH_A) == (hl == 1), q[:, lanes], jnp.zeros((), BF16)), _NT,
                                preferred_element_type=F32) for hl in range(2)]

    steps = [(u, g) for u in range(len(units)) for g in range(n_pairs)]
    outs = [[] for _ in units]
    pend = scores(*steps[0])
    for j, (u, g) in enumerate(steps):
        nxt = scores(*steps[j + 1]) if j + 1 < len(steps) else None
        _, _, v_of, b0 = units[u]
        for hl in range(2):
            h = 2 * g + hl
            nk = pend[hl].shape[0]
            s = pend[hl] + bias_ref[h, b0:b0 + nk, :]
            p = jnp.exp2(s - jnp.max(s, axis=0, keepdims=True)).astype(BF16)
            v = jnp.concatenate([v_of(slice(h * DH_A, (h + 1) * DH_A)), jnp.ones((_L_ROWS, nk), BF16)], axis=0)
            o = jnp.dot(v, p, preferred_element_type=F32)
            outs[u].append(o[:DH_A] / o[DH_A:DH_A + 1])
        pend = nxt
    return [jnp.concatenate(o, axis=0).T for o in outs]


def _band_prompt_kernel(q_ref, kp_ref, kc_ref, vp_ref, vc_ref, bias_ref, o_ref, s_scr, *, tm, tq):
    subs = [(lo, lo + tq) for lo in range(0, tm, tq)]
    lane = lax.broadcasted_iota(jnp.int32, (tq, PAIR_W), 1)

    def tile(first):
        def window(lo, hi):
            return (hi, tm - lo) if first else (tm - lo + hi, 0)

        def scores(g, j):
            lo, hi = subs[j]
            nk, b0 = window(lo, hi)
            q2 = q_ref[0, g, lo:hi, :]
            qcat = jnp.concatenate([jnp.where(lane < DH_A, q2, jnp.zeros((), BF16)),
                                    jnp.where(lane >= DH_A, q2, jnp.zeros((), BF16))], axis=0)
            k2 = kc_ref[0, g, :hi, :]
            if not first:
                k2 = jnp.concatenate([kp_ref[0, g, lo:, :], k2], axis=0)
            bias = jnp.concatenate([bias_ref[2 * g + hl, b0:b0 + nk, :] for hl in range(2)], axis=1)
            return lax.dot_general(k2, qcat, _NT, preferred_element_type=F32) + bias

        def stash(j, s):
            s_scr[j, :s.shape[0], :] = s
            return jnp.max(s, axis=0, keepdims=True)

        def consume(g, j, col_max):
            lo, hi = subs[j]
            nk, _ = window(lo, hi)
            outs = []
            for hl in range(2):
                cols = slice(hl * tq, (hl + 1) * tq)
                p = jnp.exp2(s_scr[j, :nk, cols] - col_max[:, cols]).astype(BF16)
                rows = pl.ds(g * PAIR_W + hl * DH_A, DH_A)
                v = vc_ref[0, 0, rows, :hi]
                if not first:
                    v = jnp.concatenate([vp_ref[0, 0, rows, lo:], v], axis=1)
                v = jnp.concatenate([v, jnp.ones((_L_ROWS, nk), BF16)], axis=0)
                o = jnp.dot(v, p, preferred_element_type=F32)
                outs.append(o[:DH_A] / o[DH_A:DH_A + 1])
            o_ref[0, g, lo:hi, :] = jnp.concatenate(outs, axis=0).T.astype(BF16)

        def body(g, col_max):
            new_max = []
            for j in range(len(subs)):
                nxt = scores(g + 1, j)
                consume(g, j, col_max[j])
                new_max.append(stash(j, nxt))
            return tuple(new_max)

        col_max = tuple(stash(j, scores(0, j)) for j in range(len(subs)))
        for g in range(N_PAIRS - 1):
            col_max = body(g, col_max)
        for j in range(len(subs)):
            consume(N_PAIRS - 1, j, col_max[j])

    pl.when(pl.program_id(1) == 0)(functools.partial(tile, True))
    pl.when(pl.program_id(1) > 0)(functools.partial(tile, False))


def _band_prompt(qa, ka, vat, bias, *, tm, tq):
    g, _, s, _ = qa.shape
    cur = lambda a, b: (a, 0, b, 0)
    prev = lambda a, b: (a, 0, jnp.maximum(b - 1, 0), 0)
    prev_v = lambda a, b: (a, jnp.maximum(b - 1, 0), 0, 0)
    pairs = lambda idx: pl.BlockSpec((1, N_PAIRS, tm, PAIR_W), idx)
    return pl.pallas_call(
        functools.partial(_band_prompt_kernel, tm=tm, tq=tq), grid=(g, s // tm),
        in_specs=[pairs(cur), pairs(prev), pairs(cur),
                  pl.BlockSpec((1, 1, D_A, tm), prev_v), pl.BlockSpec((1, 1, D_A, tm), lambda a, b: (a, b, 0, 0)),
                  _resident(bias.shape)],
        out_specs=pairs(cur),
        out_shape=jax.ShapeDtypeStruct((g, N_PAIRS, s, PAIR_W), BF16),
        scratch_shapes=[pltpu.VMEM((tm // tq, tm + tq, 2 * tq), F32)],
        compiler_params=pltpu.CompilerParams(dimension_semantics=("arbitrary", "arbitrary"),
                                             vmem_limit_bytes=VMEM_LIMIT_V7X),
        name="band_prompt")(qa, ka, ka, vat, vat, bias)


def _band_sample_kernel(q_ref, k_ref, vt_ref, bias_ref, o_ref):
    q = jnp.concatenate([q_ref[0, g] for g in range(N_PAIRS)], axis=1)
    unit = (q, lambda lanes: k_ref[0, :, lanes], lambda rows: vt_ref[0, rows, :], 0)
    _store_pairs(o_ref, _band_units([unit], bias_ref)[0].astype(BF16))


def _band_sample(qa, k_all, vt_all, bias):
    g, _, t, _ = qa.shape
    nk = k_all.shape[1]
    pairs = pl.BlockSpec((1, N_PAIRS, t, PAIR_W), lambda a: (a, 0, 0, 0))
    return pl.pallas_call(
        _band_sample_kernel, grid=(g,),
        in_specs=[pairs,
                  pl.BlockSpec((1, nk, D_A), lambda a: (a, 0, 0)),
                  pl.BlockSpec((1, D_A, nk), lambda a: (a, 0, 0)),
                  _resident(bias.shape)],
        out_specs=pairs,
        out_shape=jax.ShapeDtypeStruct((g, N_PAIRS, t, PAIR_W), BF16),
        compiler_params=pltpu.CompilerParams(dimension_semantics=("arbitrary",),
                                             vmem_limit_bytes=VMEM_LIMIT_V7X),
        name="band_sample")(qa, k_all, vt_all, bias)


def _mla_kernel(q_ref, k_ref, vt_ref, mask_ref, o_ref, s_scr, *, tq, tk, causal):
    nq = q_ref.shape[1] // tq
    nk = k_ref.shape[1] // tk
    ones = jnp.ones((_L_ROWS, tk), BF16)

    def q_rows(i):
        qrow = pl.multiple_of(i * tq, tq)
        return [q_ref[0, pl.ds(qrow, tq), hl * HEAD_PAD:(hl + 1) * HEAD_PAD] for hl in range(2)]

    def scores(qs, kk, hl):
        krow = pl.multiple_of(kk * tk, tk)
        return lax.dot_general(k_ref[0, pl.ds(krow, tk), hl * HEAD_PAD:(hl + 1) * HEAD_PAD], qs[hl], _NT,
                               preferred_element_type=F32)

    def stash(hl, s):
        s_scr[hl] = s
        return jnp.max(s, axis=0, keepdims=True)

    def consume(kk, hl, m, acc, tile_max, masked):
        if masked:
            s = s_scr[hl] + mask_ref[...]
            m_new = jnp.maximum(m, jnp.max(s, axis=0, keepdims=True))
        else:
            s = s_scr[hl]
            m_new = jnp.maximum(m, tile_max)
        p = jnp.exp2(s - m_new).astype(BF16)
        v = jnp.concatenate([vt_ref[0, kk, hl * DV_B:(hl + 1) * DV_B, :], ones], axis=0)
        return m_new, acc * jnp.exp2(m - m_new) + jnp.dot(v, p, preferred_element_type=F32)

    def q_tile(i, first_max):
        qs = q_rows(i)

        def body(t, carry):
            out = []
            for hl in range(2):
                m, acc, tile_max = carry[3 * hl:3 * hl + 3]
                nxt = scores(qs, t + 1, hl)
                out += list(consume(t, hl, m, acc, tile_max, False)) + [stash(hl, nxt)]
            return tuple(out)

        last = i if causal else nk - 1
        carry = ()
        for hl in range(2):
            carry += (jnp.full((1, tq), NEG, F32), jnp.zeros((DV_B + _L_ROWS, tq), F32), first_max[hl])
        done = 0
        for n in (4, 2, 1):
            left = last - done
            trips = left // n if isinstance(left, int) else lax.shift_right_logical(left, n.bit_length() - 1)

            def trip(u, c, n=n, done=done):
                for j in range(n):
                    c = body(done + n * u + j, c)
                return c

            carry = lax.fori_loop(0, trips, trip, carry)
            done = done + trips * n
        qs_next = q_rows(jnp.minimum(i + 1, nq - 1)) if nq > 1 else None
        accs, next_max = [], []
        for hl in range(2):
            nxt = scores(qs_next, 0, hl) if nq > 1 else None
            accs.append(consume(last, hl, *carry[3 * hl:3 * hl + 3], causal)[1])
            next_max.append(stash(hl, nxt) if nq > 1 else first_max[hl])
        o = jnp.concatenate([acc[:DV_B] / acc[DV_B:DV_B + 1] for acc in accs], axis=0)
        o_ref[0, pl.ds(pl.multiple_of(i * tq, tq), tq), :] = o.T.astype(BF16)
        return tuple(next_max)

    first = q_rows(0)
    lax.fori_loop(0, nq, q_tile, tuple(stash(hl, scores(first, 0, hl)) for hl in range(2)))


def _mla_attn(qf, kf, vt, mask, *, tq, tk, causal):
    g, sq, _ = qf.shape
    sk = kf.shape[1]
    return pl.pallas_call(
        functools.partial(_mla_kernel, tq=tq, tk=tk, causal=causal), grid=(g, H_B // 2),
        in_specs=[pl.BlockSpec((1, sq, 2 * HEAD_PAD), lambda a, b: (a, 0, b)),
                  pl.BlockSpec((1, sk, 2 * HEAD_PAD), lambda a, b: (a, 0, b)),
                  pl.BlockSpec((1, sk // tk, 2 * DV_B, tk), lambda a, b: (a, 0, b, 0)),
                  _resident(mask.shape)],
        out_specs=pl.BlockSpec((1, sq, 2 * DV_B), lambda a, b: (a, 0, b)),
        out_shape=jax.ShapeDtypeStruct((g, sq, D_VB), BF16),
        scratch_shapes=[pltpu.VMEM((2, tk, tq), F32)],
        compiler_params=pltpu.CompilerParams(dimension_semantics=("arbitrary", "arbitrary"),
                                             vmem_limit_bytes=VMEM_LIMIT_V7X),
        name="mla_attn")(qf, kf, vt, mask)


_G0 = 8


def _merge_ffn_kernel(x_ref, ya_ref, yb_ref, ga_ref, gb_ref, st_ref, wa_ref, wb_ref, wo_ref, n1_ref, n2_ref, n3_ref,
                      wg_ref, wu_ref, cw_ref, cb_ref, wd_ref, y_ref, cs_ref, gbuf, *, tm):
    i = pl.program_id(1)
    ya = jnp.concatenate([ya_ref[0, g] for g in range(N_PAIRS)], axis=1)
    za = jnp.dot(ya, wa_ref[...], preferred_element_type=F32)
    zb = jnp.dot(yb_ref[0], wb_ref[...], preferred_element_type=F32)
    mix = jax.nn.sigmoid(ga_ref[0].astype(F32)) * za + jax.nn.sigmoid(gb_ref[0].astype(F32)) * zb
    mo = jnp.dot(mix.astype(BF16), wo_ref[...], preferred_element_type=F32)
    x1 = x_ref[0] + _rms(mo, n1_ref[...])
    xn = _rms(x1, n2_ref[...]).astype(BF16)

    @pl.when(i == 0)
    def _():
        gbuf[_G0 - 2:_G0, :] = st_ref[0]

    @pl.when(i > 0)
    def _():
        gbuf[_G0 - 2:_G0, :] = gbuf[_G0 + tm - 2:_G0 + tm, :]

    gbuf[_G0:_G0 + tm, :] = jnp.dot(xn, wg_ref[...], preferred_element_type=F32)
    cs_ref[0] = gbuf[_G0 + tm - 2:_G0 + tm, :]
    u = jnp.dot(xn, wu_ref[...], preferred_element_type=F32)
    c = cb_ref[...] + cw_ref[0:1, :] * gbuf[_G0 - 2:_G0 - 2 + tm, :]
    c = c + cw_ref[1:2, :] * gbuf[_G0 - 1:_G0 - 1 + tm, :]
    c = c + cw_ref[2:3, :] * gbuf[_G0:_G0 + tm, :]
    hid = (jax.nn.gelu(c, approximate=True) * u).astype(BF16)
    f = jnp.dot(hid, wd_ref[...], preferred_element_type=F32)
    y_ref[0] = x1 + _rms(f, n3_ref[...])


def _merge_ffn(x, ya, yb, ga, gb, state, wa, wb, wo, n1, n2, n3, wg, wu, cw, cb, wd, *, tm):
    g, s, _ = x.shape
    tok = lambda w: pl.BlockSpec((1, tm, w), lambda a, b: (a, b, 0))
    per_group = pl.BlockSpec((1, CONV_W - 1, D_FF), lambda a, b: (a, 0, 0))
    weights = (wa, wb, wo, n1, n2, n3, wg, wu, cw, cb, wd)
    return pl.pallas_call(
        functools.partial(_merge_ffn_kernel, tm=tm), grid=(g, s // tm),
        in_specs=[tok(D_MODEL), pl.BlockSpec((1, N_PAIRS, tm, PAIR_W), lambda a, b: (a, 0, b, 0)), tok(D_VB),
                  tok(D_MODEL), tok(D_MODEL), per_group]
                 + [_resident(w.shape) for w in weights],
        out_specs=(tok(D_MODEL), per_group),
        out_shape=(jax.ShapeDtypeStruct((g, s, D_MODEL), F32),
                   jax.ShapeDtypeStruct((g, CONV_W - 1, D_FF), F32)),
        scratch_shapes=[pltpu.VMEM((_G0 + tm, D_FF), F32)],
        compiler_params=pltpu.CompilerParams(dimension_semantics=("arbitrary", "arbitrary"),
                                             vmem_limit_bytes=VMEM_LIMIT_V7X),
        name="merge_ffn")(x, ya, yb, ga, gb, state, *weights)


def _rope_tables(pos, lane0, scale):
    half = DH_ROPE // 2
    inv = ROPE_BASE ** (-np.arange(half, dtype=np.float64) / half)
    ang = np.asarray(pos, np.float64)[:, None] * inv[None, :]
    c, s = jnp.asarray(np.cos(ang), F32), jnp.asarray(np.sin(ang), F32)
    n = ang.shape[0]
    rest = jnp.zeros((n, HEAD_PAD - lane0 - DH_ROPE), F32)
    ct = jnp.concatenate([jnp.ones((n, lane0), F32), c, c, rest], axis=1) * scale
    st = jnp.concatenate([jnp.zeros((n, lane0), F32), -s, s, rest], axis=1) * scale
    return ct, st


def _band_bias_kernel(ext_ref, o_ref, *, tq, nk):
    w = ext_ref.shape[-1]
    toeplitz = pltpu.roll(jnp.broadcast_to(ext_ref[0], (nk, w)), 0, 1, stride=1, stride_axis=0)[:, :tq]
    qc = (nk - tq + lax.broadcasted_iota(jnp.int32, (nk, tq), 1)) // CHUNK
    kc = lax.broadcasted_iota(jnp.int32, (nk, tq), 0) // CHUNK
    o_ref[0] = jnp.where((kc >= qc - LEFT_CHUNKS) & (kc <= qc), toeplitz, NEG)


def _band_bias(table, tq, nk):
    w = nk + tq
    d_lo, d_hi = 1 - tq, w - tq
    assert -MAX_REL <= d_lo and d_hi >= MAX_REL and w % 128 == 0
    h = table.shape[0]
    ext = jnp.concatenate([table[:, d_lo + MAX_REL:], jnp.broadcast_to(table[:, -1:], (h, d_hi - MAX_REL))], axis=1)
    ext = (jnp.roll(ext, -(nk - 1), axis=1) * LOG2E).reshape(h, 1, w)
    return pl.pallas_call(
        functools.partial(_band_bias_kernel, tq=tq, nk=nk), grid=(h,),
        in_specs=[pl.BlockSpec((1, 1, w), lambda a: (a, 0, 0))],
        out_specs=pl.BlockSpec((1, nk, tq), lambda a: (a, 0, 0)),
        out_shape=jax.ShapeDtypeStruct((h, nk, tq), F32),
        compiler_params=pltpu.CompilerParams(dimension_semantics=("arbitrary",)),
        name="band_bias")(ext)


def _chunk_mask(tk, tq):
    kc = jnp.arange(tk)[:, None] // CHUNK
    qc = jnp.arange(tq)[None, :] // CHUNK
    return jnp.where(kc <= qc, 0.0, NEG).astype(F32)


def _prep_weights(w_in, w_uk, w_uv):
    o = np.cumsum((0,) + IN_SIZES)
    col = lambda j: w_in[:, o[j]:o[j + 1]]
    d = w_in.shape[0]
    qn = col(3).reshape(d, H_B, DH_NOPE)
    qr = col(4).reshape(d, H_B, DH_ROPE)
    pad = jnp.zeros((d, H_B, HEAD_PAD - DH_NOPE - DH_ROPE), F32).at[:, 0, :].set(col(6))
    w_qf = jnp.concatenate([qn, qr, pad], axis=2).reshape(d, D_QF)
    wm = jnp.concatenate([col(0) * (A_SCALE * LOG2E), col(1), w_qf, col(5), col(7), col(8)], axis=1).astype(BF16)
    wva = col(2).astype(BF16)
    wk = jnp.concatenate([w_uk, jnp.zeros((D_C, H_B, HEAD_PAD - DH_NOPE), F32)], axis=2).reshape(D_C, D_QF).astype(BF16)
    place = jnp.concatenate([jnp.zeros((DH_ROPE, DH_NOPE), F32), jnp.eye(DH_ROPE, dtype=F32),
                             jnp.zeros((DH_ROPE, HEAD_PAD - DH_NOPE - DH_ROPE), F32)], axis=1)
    ek = jnp.tile(place, (1, H_B)).astype(BF16)
    wvt = w_uv.reshape(D_C, D_VB).T.astype(BF16)
    return wm, wva.T, wva, wk, ek, wvt


def kernel(x_prompt, x_sample, cache_a_k, cache_a_v, cache_mla_ckv, cache_mla_krope, state_ffn_conv, norm_mix_pre,
           norm_mix_post, w_in, rel_bias_table, kv_norm, w_uk, w_uv, w_branch_a, w_branch_b, w_out, norm_ffn_pre,
           norm_ffn_post, w_ffn_gate, w_ffn_up, conv_w, conv_b, w_ffn_down):
    assert w_in.shape[0] == 1, "single layer"
    b, s, _ = x_prompt.shape
    db, t, _ = x_sample.shape
    past = cache_mla_ckv.shape[2]
    wcache = cache_a_k.shape[2]
    keep = min(A_WINDOW, s)
    tm1 = 512
    assert keep == tm1 and db * t == tm1 and wcache == A_WINDOW

    wm, wvat, wva, wk, ek, wvt = _prep_weights(w_in[0], w_uk[0], w_uv[0])
    row = lambda v: v.reshape(1, -1)
    proj_w = (row(norm_mix_pre[0]), wm, wvat, wva, row(kv_norm[0]))
    ffn_w = (w_branch_a[0].astype(BF16), w_branch_b[0].astype(BF16), w_out[0].astype(BF16), row(norm_mix_post[0]),
             row(norm_ffn_pre[0]), row(norm_ffn_post[0]), w_ffn_gate[0].astype(BF16), w_ffn_up[0].astype(BF16),
             conv_w[0], row(conv_b[0]), w_ffn_down[0].astype(BF16))
    table = rel_bias_table[0]

    pos = np.arange(s)
    tabs = _rope_tables(pos, DH_NOPE, MLA_SCALE * LOG2E) + _rope_tables(pos, _KR_LANE, 1.0)
    qa, ka, vat, kaf, vaf, qf, ckv, kr, ga, gb = _in_proj(x_prompt, *proj_w, *tabs, tm=tm1)
    t_mla = 512
    kf, vt = _kv_up(ckv, kr, wk, ek, wvt, tm=1024, tk=t_mla)
    ya = _band_prompt(qa, ka, vat, _band_bias(table, 256, A_WINDOW + 256), tm=tm1, tq=256)
    yb = _mla_attn(qf, kf, vt, _chunk_mask(t_mla, t_mla), tq=t_mla, tk=t_mla, causal=True)
    y_prompt, conv_p = _merge_ffn(x_prompt, ya, yb, ga, gb, jnp.zeros((b, CONV_W - 1, D_FF), F32), *ffn_w, tm=512)

    pos_s = np.tile(past + np.arange(t), db)
    tabs_s = _rope_tables(pos_s, DH_NOPE, MLA_SCALE * LOG2E) + _rope_tables(pos_s, _KR_LANE, 1.0)
    qa2, ka2, vat2, kaf2, vaf2, qf2, ckv2, kr2, ga2, gb2 = _in_proj(x_sample.reshape(1, db * t, D_MODEL), *proj_w,
                                                                    *tabs_s, tm=tm1)
    per_seq = lambda v: v.reshape(db, t, v.shape[-1])
    ckv2, kr2 = per_seq(ckv2[0]), per_seq(kr2[0])
    new_k = jnp.concatenate([cache_a_k[0].reshape(db, wcache, D_A), per_seq(kaf2[0])], axis=1)
    new_v = jnp.concatenate([cache_a_v[0].reshape(db, wcache, D_A), per_seq(vaf2[0])], axis=1)
    qa2 = jnp.swapaxes(qa2[0].reshape(N_PAIRS, db, t, PAIR_W), 0, 1)
    ya2 = _band_sample(qa2, new_k.astype(BF16), jnp.swapaxes(new_v, 1, 2).astype(BF16),
                       _band_bias(table, t, wcache + t))
    c_all = jnp.concatenate([cache_mla_ckv[0], ckv2], axis=1)
    kr_all = jnp.concatenate([cache_mla_krope[0], kr2], axis=1)
    kf2, vt2 = _kv_up(c_all, kr_all, wk, ek, wvt, tm=past + t, tk=past + t)
    yb2 = _mla_attn(per_seq(qf2[0]), kf2, vt2, jnp.zeros((8, 128), F32), tq=t, tk=past + t, causal=False)
    y_sample, conv_s = _merge_ffn(x_sample, ya2, yb2, per_seq(ga2[0]), per_seq(gb2[0]), state_ffn_conv[0], *ffn_w,
                                  tm=t)

    heads = lambda v: v.reshape(1, v.shape[0], v.shape[1], H_A, DH_A)
    return (y_prompt, y_sample,
            heads(kaf), heads(vaf), ckv[None], kr[None], conv_p[None],
            heads(new_k[:, -wcache:]), heads(new_v[:, -wcache:]), ckv2[None], kr2[None], conv_s[None])
```

```python
import functools

import jax
import jax.numpy as jnp
import numpy as np
from jax import lax
from jax.experimental import pallas as pl
from jax.experimental.pallas import tpu as pltpu

D_MODEL = 1024
CHUNK = 64
LEFT_CHUNKS = 8
A_WINDOW = LEFT_CHUNKS * CHUNK
H_A = 8
DH_A = 64
MAX_REL = 256
H_B = 8
DH_NOPE = 64
DH_ROPE = 32
DV_B = 64
D_C = 256
D_FF = 2816
CONV_W = 3
ROPE_BASE = 10000.0
EPS = 1e-6
A_SCALE = DH_A ** -0.5
MLA_SCALE = (DH_NOPE + DH_ROPE) ** -0.5
IN_SIZES = (H_A * DH_A, H_A * DH_A, H_A * DH_A, H_B * DH_NOPE, H_B * DH_ROPE, D_C, DH_ROPE, D_MODEL, D_MODEL)

HEAD_PAD = 128
D_A = H_A * DH_A
D_VB = H_B * DV_B
D_QF = H_B * HEAD_PAD
PAIR_W = 2 * DH_A
N_PAIRS = H_A // 2
NEG = -1e30
LOG2E = 1.4426950408889634
_L_ROWS = 16
VMEM_LIMIT_V7X = 56 * 1024 * 1024

F32 = jnp.float32
BF16 = jnp.bfloat16
_NT = (((1,), (1,)), ((), ()))


def _resident(shape):
    nd = len(shape)
    return pl.BlockSpec(shape, lambda *_: (0,) * nd, pipeline_mode=pl.Buffered(1))


def _rms(x, w):
    return x * lax.rsqrt(jnp.mean(x * x, axis=-1, keepdims=True) + EPS) * w


def _rope_lanes(x, c, s):
    lane = lax.broadcasted_iota(jnp.int32, x.shape, 1)
    partner = jnp.where(lane % 32 < 16, pltpu.roll(x, 128 - 16, 1), pltpu.roll(x, 16, 1))
    return x * c + partner * s


_C_QA, _C_KA, _C_QF, _C_CKV, _C_GA, _C_GB, _C_END = 0, 512, 1024, 2048, 2304, 3328, 4352
_KR_LANE = DH_NOPE + DH_ROPE


def _store_pairs(ref, v):
    for g in range(N_PAIRS):
        ref[0, g] = v[:, g * PAIR_W:(g + 1) * PAIR_W]


def _in_proj_kernel(x_ref, nw_ref, wm_ref, wvt_ref, wva_ref, kvn_ref, cq_ref, sq_ref, ck_ref, sk_ref,
                    qa_ref, ka_ref, vat_ref, kaf_ref, vaf_ref, qf_ref, ckv_ref, kr_ref, ga_ref, gb_ref, *, n):
    i = pl.program_id(1)
    xn = _rms(x_ref[0], nw_ref[...]).astype(BF16)

    def proj(lo, hi):
        return jnp.dot(xn, wm_ref[:, lo:hi], preferred_element_type=F32)

    _store_pairs(qa_ref, proj(_C_QA, _C_KA).astype(BF16))
    ka = proj(_C_KA, _C_QF)
    _store_pairs(ka_ref, ka.astype(BF16))
    kaf_ref[0] = ka
    vat_ref[0, 0] = lax.dot_general(wvt_ref[...], xn, _NT, preferred_element_type=F32).astype(BF16)

    def va_tail():
        vaf_ref[0] = jnp.dot(xn, wva_ref[...], preferred_element_type=F32)

    if n == 1:
        va_tail()
    else:
        pl.when(i == n - 1)(va_tail)

    cq, sq = cq_ref[...], sq_ref[...]
    qraw = proj(_C_QF, _C_CKV)
    for h in range(H_B):
        slab = slice(h * HEAD_PAD, (h + 1) * HEAD_PAD)
        qf_ref[0, :, slab] = _rope_lanes(qraw[:, slab], cq, sq).astype(BF16)
    kr = _rope_lanes(qraw[:, :HEAD_PAD], ck_ref[...], sk_ref[...])
    kr_ref[0] = pltpu.roll(kr, HEAD_PAD - _KR_LANE, 1)[:, :DH_ROPE]

    ckv_ref[0] = _rms(proj(_C_CKV, _C_GA), kvn_ref[...])
    ga_ref[0] = proj(_C_GA, _C_GB).astype(BF16)
    gb_ref[0] = proj(_C_GB, _C_END).astype(BF16)


def _in_proj(x, nw, wm, wvt, wva, kvn, cq, sq, ck, sk, *, tm):
    g, s, _ = x.shape
    n = s // tm
    tok = lambda w: pl.BlockSpec((1, tm, w), lambda a, b: (a, b, 0))
    tab = pl.BlockSpec((tm, HEAD_PAD), lambda a, b: (b, 0))
    tail = pl.BlockSpec((1, tm, D_A), lambda a, b: (a, 0, 0))
    out_shape = (
        jax.ShapeDtypeStruct((g, N_PAIRS, s, PAIR_W), BF16),
        jax.ShapeDtypeStruct((g, N_PAIRS, s, PAIR_W), BF16),
        jax.ShapeDtypeStruct((g, n, D_A, tm), BF16),
        jax.ShapeDtypeStruct((g, tm, D_A), F32),
        jax.ShapeDtypeStruct((g, tm, D_A), F32),
        jax.ShapeDtypeStruct((g, s, D_QF), BF16),
        jax.ShapeDtypeStruct((g, s, D_C), F32),
        jax.ShapeDtypeStruct((g, s, DH_ROPE), F32),
        jax.ShapeDtypeStruct((g, s, D_MODEL), BF16),
        jax.ShapeDtypeStruct((g, s, D_MODEL), BF16),
    )
    pairs = pl.BlockSpec((1, N_PAIRS, tm, PAIR_W), lambda a, b: (a, 0, b, 0))
    out_specs = (pairs, pairs, pl.BlockSpec((1, 1, D_A, tm), lambda a, b: (a, b, 0, 0)), tail, tail,
                 tok(D_QF), tok(D_C), tok(DH_ROPE), tok(D_MODEL), tok(D_MODEL))
    in_specs = [tok(D_MODEL), _resident(nw.shape), _resident(wm.shape), _resident(wvt.shape), _resident(wva.shape),
                _resident(kvn.shape), tab, tab, tab, tab]
    return pl.pallas_call(
        functools.partial(_in_proj_kernel, n=n), grid=(g, n), in_specs=in_specs, out_specs=out_specs, out_shape=out_shape,
        compiler_params=pltpu.CompilerParams(dimension_semantics=("arbitrary", "arbitrary"),
                                             vmem_limit_bytes=VMEM_LIMIT_V7X),
        name="in_proj")(x, nw, wm, wvt, wva, kvn, cq, sq, ck, sk)


def _kv_up_kernel(ckv_ref, kr_ref, wk_ref, ek_ref, wvt_ref, kf_ref, vt_ref, *, tk):
    c = ckv_ref[0].astype(BF16)
    kf = jnp.dot(c, wk_ref[...], preferred_element_type=F32)
    kf = kf + jnp.dot(kr_ref[0].astype(BF16), ek_ref[...], preferred_element_type=F32)
    kf_ref[0] = kf.astype(BF16)
    for j in range(c.shape[0] // tk):
        vt_ref[0, j] = lax.dot_general(wvt_ref[...], c[j * tk:(j + 1) * tk], _NT,
                                       preferred_element_type=F32).astype(BF16)


def _kv_up(ckv, kr, wk, ek, wvt, *, tm, tk):
    g, s, _ = ckv.shape
    return pl.pallas_call(
        functools.partial(_kv_up_kernel, tk=tk), grid=(g, s // tm),
        in_specs=[pl.BlockSpec((1, tm, D_C), lambda a, b: (a, b, 0)),
                  pl.BlockSpec((1, tm, DH_ROPE), lambda a, b: (a, b, 0)),
                  _resident(wk.shape), _resident(ek.shape), _resident(wvt.shape)],
        out_specs=(pl.BlockSpec((1, tm, D_QF), lambda a, b: (a, b, 0)),
                   pl.BlockSpec((1, tm // tk, D_VB, tk), lambda a, b: (a, b, 0, 0))),
        out_shape=(jax.ShapeDtypeStruct((g, s, D_QF), BF16),
                   jax.ShapeDtypeStruct((g, s // tk, D_VB, tk), BF16)),
        compiler_params=pltpu.CompilerParams(dimension_semantics=("arbitrary", "arbitrary"),
                                             vmem_limit_bytes=VMEM_LIMIT_V7X),
        name="kv_up")(ckv, kr, wk, ek, wvt)


def _band_units(units, bias_ref, n_pairs=H_A // 2):
    def scores(u, g):
        q, k_of, _, _ = units[u]
        lanes = slice(g * 2 * DH_A, (g + 1) * 2 * DH_A)
        lane = lax.broadcasted_iota(jnp.int32, (q.shape[0], 2 * DH_A), 1)
        k2 = k_of(lanes)
        return [lax.dot_general(k2, jnp.where((lane >= DH_A) == (hl == 1), q[:, lanes], jnp.zeros((), BF16)), _NT,
                                preferred_element_type=F32) for hl in range(2)]

    steps = [(u, g) for u in range(len(units)) for g in range(n_pairs)]
    outs = [[] for _ in units]
    pend = scores(*steps[0])
    for j, (u, g) in enumerate(steps):
        nxt = scores(*steps[j + 1]) if j + 1 < len(steps) else None
        _, _, v_of, b0 = units[u]
        for hl in range(2):
            h = 2 * g + hl
            nk = pend[hl].shape[0]
            s = pend[hl] + bias_ref[h, b0:b0 + nk, :]
            p = jnp.exp2(s - jnp.max(s, axis=0, keepdims=True)).astype(BF16)
            v = jnp.concatenate([v_of(slice(h * DH_A, (h + 1) * DH_A)), jnp.ones((_L_ROWS, nk), BF16)], axis=0)
            o = jnp.dot(v, p, preferred_element_type=F32)
            outs[u].append(o[:DH_A] / o[DH_A:DH_A + 1])
        pend = nxt
    return [jnp.concatenate(o, axis=0).T for o in outs]


def _band_prompt_kernel(q_ref, kp_ref, kc_ref, vp_ref, vc_ref, bias_ref, o_ref, s_scr, *, tm, tq):
    subs = [(lo, lo + tq) for lo in range(0, tm, tq)]
    lane = lax.broadcasted_iota(jnp.int32, (tq, PAIR_W), 1)

    def tile(first):
        def window(lo, hi):
            return (hi, tm - lo) if first else (tm - lo + hi, 0)

        def scores(g, j):
            lo, hi = subs[j]
            nk, b0 = window(lo, hi)
            q2 = q_ref[0, g, lo:hi, :]
            qcat = jnp.concatenate([jnp.where(lane < DH_A, q2, jnp.zeros((), BF16)),
                                    jnp.where(lane >= DH_A, q2, jnp.zeros((), BF16))], axis=0)
            k2 = kc_ref[0, g, :hi, :]
            if not first:
                k2 = jnp.concatenate([kp_ref[0, g, lo:, :], k2], axis=0)
            bias = jnp.concatenate([bias_ref[2 * g + hl, b0:b0 + nk, :] for hl in range(2)], axis=1)
            return lax.dot_general(k2, qcat, _NT, preferred_element_type=F32) + bias

        def stash(j, s):
            s_scr[j, :s.shape[0], :] = s
            return jnp.max(s, axis=0, keepdims=True)

        def consume(g, j, col_max):
            lo, hi = subs[j]
            nk, _ = window(lo, hi)
            outs = []
            for hl in range(2):
                cols = slice(hl * tq, (hl + 1) * tq)
                p = jnp.exp2(s_scr[j, :nk, cols] - col_max[:, cols]).astype(BF16)
                rows = pl.ds(g * PAIR_W + hl * DH_A, DH_A)
                v = vc_ref[0, 0, rows, :hi]
                if not first:
                    v = jnp.concatenate([vp_ref[0, 0, rows, lo:], v], axis=1)
                v = jnp.concatenate([v, jnp.ones((_L_ROWS, nk), BF16)], axis=0)
                o = jnp.dot(v, p, preferred_element_type=F32)
                outs.append(o[:DH_A] / o[DH_A:DH_A + 1])
            o_ref[0, g, lo:hi, :] = jnp.concatenate(outs, axis=0).T.astype(BF16)

        def body(g, col_max):
            new_max = []
            for j in range(len(subs)):
                nxt = scores(g + 1, j)
                consume(g, j, col_max[j])
                new_max.append(stash(j, nxt))
            return tuple(new_max)

        col_max = tuple(stash(j, scores(0, j)) for j in range(len(subs)))
        for g in range(N_PAIRS - 1):
            col_max = body(g, col_max)
        for j in range(len(subs)):
            consume(N_PAIRS - 1, j, col_max[j])

    pl.when(pl.program_id(1) == 0)(functools.partial(tile, True))
    pl.when(pl.program_id(1) > 0)(functools.partial(tile, False))


def _band_prompt(qa, ka, vat, bias, *, tm, tq):
    g, _, s, _ = qa.shape
    cur = lambda a, b: (a, 0, b, 0)
    prev = lambda a, b: (a, 0, jnp.maximum(b - 1, 0), 0)
    prev_v = lambda a, b: (a, jnp.maximum(b - 1, 0), 0, 0)
    pairs = lambda idx: pl.BlockSpec((1, N_PAIRS, tm, PAIR_W), idx)
    return pl.pallas_call(
        functools.partial(_band_prompt_kernel, tm=tm, tq=tq), grid=(g, s // tm),
        in_specs=[pairs(cur), pairs(prev), pairs(cur),
                  pl.BlockSpec((1, 1, D_A, tm), prev_v), pl.BlockSpec((1, 1, D_A, tm), lambda a, b: (a, b, 0, 0)),
                  _resident(bias.shape)],
        out_specs=pairs(cur),
        out_shape=jax.ShapeDtypeStruct((g, N_PAIRS, s, PAIR_W), BF16),
        scratch_shapes=[pltpu.VMEM((tm // tq, tm + tq, 2 * tq), F32)],
        compiler_params=pltpu.CompilerParams(dimension_semantics=("arbitrary", "arbitrary"),
                                             vmem_limit_bytes=VMEM_LIMIT_V7X),
        name="band_prompt")(qa, ka, ka, vat, vat, bias)


def _band_sample_kernel(q_ref, k_ref, vt_ref, bias_ref, o_ref):
    q = jnp.concatenate([q_ref[g, 0] for g in range(N_PAIRS)], axis=1)
    unit = (q, lambda lanes: k_ref[0, :, lanes], lambda rows: vt_ref[0, rows, :], 0)
    o = _band_units([unit], bias_ref)[0].astype(BF16)
    for g in range(N_PAIRS):
        o_ref[g, 0] = o[:, g * PAIR_W:(g + 1) * PAIR_W]


def _band_sample(qa, k_all, vt_all, bias):
    _, g, t, _ = qa.shape
    nk = k_all.shape[1]
    pairs = pl.BlockSpec((N_PAIRS, 1, t, PAIR_W), lambda a: (0, a, 0, 0))
    return pl.pallas_call(
        _band_sample_kernel, grid=(g,),
        in_specs=[pairs,
                  pl.BlockSpec((1, nk, D_A), lambda a: (a, 0, 0)),
                  pl.BlockSpec((1, D_A, nk), lambda a: (a, 0, 0)),
                  _resident(bias.shape)],
        out_specs=pairs,
        out_shape=jax.ShapeDtypeStruct((N_PAIRS, g, t, PAIR_W), BF16),
        compiler_params=pltpu.CompilerParams(dimension_semantics=("arbitrary",),
                                             vmem_limit_bytes=VMEM_LIMIT_V7X),
        name="band_sample")(qa, k_all, vt_all, bias)


def _mla_kernel(q_ref, k_ref, vt_ref, mask_ref, o_ref, s_scr, *, tq, tk, causal):
    nq = q_ref.shape[1] // tq
    nk = k_ref.shape[1] // tk
    ones = jnp.ones((_L_ROWS, tk), BF16)

    def q_rows(i):
        qrow = pl.multiple_of(i * tq, tq)
        return [q_ref[0, pl.ds(qrow, tq), hl * HEAD_PAD:(hl + 1) * HEAD_PAD] for hl in range(2)]

    def scores(qs, kk, hl):
        krow = pl.multiple_of(kk * tk, tk)
        return lax.dot_general(k_ref[0, pl.ds(krow, tk), hl * HEAD_PAD:(hl + 1) * HEAD_PAD], qs[hl], _NT,
                               preferred_element_type=F32)

    def stash(hl, s):
        s_scr[hl] = s
        return jnp.max(s, axis=0, keepdims=True)

    def consume(kk, hl, m, acc, tile_max, masked):
        if masked:
            s = s_scr[hl] + mask_ref[...]
            m_new = jnp.maximum(m, jnp.max(s, axis=0, keepdims=True))
        else:
            s = s_scr[hl]
            m_new = jnp.maximum(m, tile_max)
        p = jnp.exp2(s - m_new).astype(BF16)
        v = jnp.concatenate([vt_ref[0, kk, hl * DV_B:(hl + 1) * DV_B, :], ones], axis=0)
        return m_new, acc * jnp.exp2(m - m_new) + jnp.dot(v, p, preferred_element_type=F32)

    def q_tile(i, first_max):
        qs = q_rows(i)

        def body(t, carry):
            out = []
            for hl in range(2):
                m, acc, tile_max = carry[3 * hl:3 * hl + 3]
                nxt = scores(qs, t + 1, hl)
                out += list(consume(t, hl, m, acc, tile_max, False)) + [stash(hl, nxt)]
            return tuple(out)

        last = i if causal else nk - 1
        carry = ()
        for hl in range(2):
            carry += (jnp.full((1, tq), NEG, F32), jnp.zeros((DV_B + _L_ROWS, tq), F32), first_max[hl])
        done = 0
        for n in (4, 2, 1):
            left = last - done
            trips = left // n if isinstance(left, int) else lax.shift_right_logical(left, n.bit_length() - 1)

            def trip(u, c, n=n, done=done):
                for j in range(n):
                    c = body(done + n * u + j, c)
                return c

            carry = lax.fori_loop(0, trips, trip, carry)
            done = done + trips * n
        qs_next = q_rows(jnp.minimum(i + 1, nq - 1)) if nq > 1 else None
        accs, next_max = [], []
        for hl in range(2):
            nxt = scores(qs_next, 0, hl) if nq > 1 else None
            accs.append(consume(last, hl, *carry[3 * hl:3 * hl + 3], causal)[1])
            next_max.append(stash(hl, nxt) if nq > 1 else first_max[hl])
        o = jnp.concatenate([acc[:DV_B] / acc[DV_B:DV_B + 1] for acc in accs], axis=0)
        o_ref[0, pl.ds(pl.multiple_of(i * tq, tq), tq), :] = o.T.astype(BF16)
        return tuple(next_max)

    first = q_rows(0)
    lax.fori_loop(0, nq, q_tile, tuple(stash(hl, scores(first, 0, hl)) for hl in range(2)))


def _mla_attn(qf, kf, vt, mask, *, tq, tk, causal):
    g, sq, _ = qf.shape
    sk = kf.shape[1]
    return pl.pallas_call(
        functools.partial(_mla_kernel, tq=tq, tk=tk, causal=causal), grid=(g, H_B // 2),
        in_specs=[pl.BlockSpec((1, sq, 2 * HEAD_PAD), lambda a, b: (a, 0, b)),
                  pl.BlockSpec((1, sk, 2 * HEAD_PAD), lambda a, b: (a, 0, b)),
                  pl.BlockSpec((1, sk // tk, 2 * DV_B, tk), lambda a, b: (a, 0, b, 0)),
                  _resident(mask.shape)],
        out_specs=pl.BlockSpec((1, sq, 2 * DV_B), lambda a, b: (a, 0, b)),
        out_shape=jax.ShapeDtypeStruct((g, sq, D_VB), BF16),
        scratch_shapes=[pltpu.VMEM((2, tk, tq), F32)],
        compiler_params=pltpu.CompilerParams(dimension_semantics=("arbitrary", "arbitrary"),
                                             vmem_limit_bytes=VMEM_LIMIT_V7X),
        name="mla_attn")(qf, kf, vt, mask)


_G0 = 8


def _merge_ffn_kernel(x_ref, ya_ref, yb_ref, ga_ref, gb_ref, st_ref, wa_ref, wb_ref, wo_ref, n1_ref, n2_ref, n3_ref,
                      wg_ref, wu_ref, cw_ref, cb_ref, wd_ref, y_ref, cs_ref, gbuf, *, tm, nseq):
    i = pl.program_id(1)
    seg = tm // nseq
    ya = jnp.concatenate([ya_ref[0, g] for g in range(N_PAIRS)], axis=1)
    za = jnp.dot(ya, wa_ref[...], preferred_element_type=F32)
    zb = jnp.dot(yb_ref[0], wb_ref[...], preferred_element_type=F32)
    mix = jax.nn.sigmoid(ga_ref[0].astype(F32)) * za + jax.nn.sigmoid(gb_ref[0].astype(F32)) * zb
    mo = jnp.dot(mix.astype(BF16), wo_ref[...], preferred_element_type=F32)
    x1 = x_ref[0] + _rms(mo, n1_ref[...])
    xn = _rms(x1, n2_ref[...]).astype(BF16)

    starts = [_G0 + j * (seg + _G0) for j in range(nseq)]
    if nseq == 1:
        @pl.when(i == 0)
        def _():
            gbuf[_G0 - 2:_G0, :] = st_ref[0]

        @pl.when(i > 0)
        def _():
            gbuf[_G0 - 2:_G0, :] = gbuf[_G0 + tm - 2:_G0 + tm, :]
    else:
        for j, r0 in enumerate(starts):
            gbuf[r0 - 2:r0, :] = st_ref[j]

    gate = jnp.dot(xn, wg_ref[...], preferred_element_type=F32)
    for j, r0 in enumerate(starts):
        gbuf[r0:r0 + seg, :] = gate[j * seg:(j + 1) * seg]
        cs_ref[j] = gbuf[r0 + seg - 2:r0 + seg, :]
    u = jnp.dot(xn, wu_ref[...], preferred_element_type=F32)
    c = jnp.concatenate([cw_ref[0:1, :] * gbuf[r0 - 2:r0 - 2 + seg, :] + cw_ref[1:2, :] * gbuf[r0 - 1:r0 - 1 + seg, :]
                         + cw_ref[2:3, :] * gbuf[r0:r0 + seg, :] for r0 in starts], axis=0) + cb_ref[...]
    hid = (jax.nn.gelu(c, approximate=True) * u).astype(BF16)
    f = jnp.dot(hid, wd_ref[...], preferred_element_type=F32)
    y_ref[0] = x1 + _rms(f, n3_ref[...])


def _merge_ffn(x, ya, yb, ga, gb, state, wa, wb, wo, n1, n2, n3, wg, wu, cw, cb, wd, *, tm, nseq=1):
    g, s, _ = x.shape
    assert nseq == 1 or s == tm
    tok = lambda w: pl.BlockSpec((1, tm, w), lambda a, b: (a, b, 0))
    per_group = pl.BlockSpec((nseq, CONV_W - 1, D_FF), lambda a, b: (a, 0, 0))
    weights = (wa, wb, wo, n1, n2, n3, wg, wu, cw, cb, wd)
    return pl.pallas_call(
        functools.partial(_merge_ffn_kernel, tm=tm, nseq=nseq), grid=(g, s // tm),
        in_specs=[tok(D_MODEL), pl.BlockSpec((1, N_PAIRS, tm, PAIR_W), lambda a, b: (a, 0, b, 0)), tok(D_VB),
                  tok(D_MODEL), tok(D_MODEL), per_group]
                 + [_resident(w.shape) for w in weights],
        out_specs=(tok(D_MODEL), per_group),
        out_shape=(jax.ShapeDtypeStruct((g, s, D_MODEL), F32),
                   jax.ShapeDtypeStruct((g * nseq, CONV_W - 1, D_FF), F32)),
        scratch_shapes=[pltpu.VMEM((tm + nseq * _G0, D_FF), F32)],
        compiler_params=pltpu.CompilerParams(dimension_semantics=("arbitrary", "arbitrary"),
                                             vmem_limit_bytes=VMEM_LIMIT_V7X),
        name="merge_ffn")(x, ya, yb, ga, gb, state, *weights)


def _rope_tables(pos, lane0, scale):
    half = DH_ROPE // 2
    inv = ROPE_BASE ** (-np.arange(half, dtype=np.float64) / half)
    ang = np.asarray(pos, np.float64)[:, None] * inv[None, :]
    c, s = jnp.asarray(np.cos(ang), F32), jnp.asarray(np.sin(ang), F32)
    n = ang.shape[0]
    rest = jnp.zeros((n, HEAD_PAD - lane0 - DH_ROPE), F32)
    ct = jnp.concatenate([jnp.ones((n, lane0), F32), c, c, rest], axis=1) * scale
    st = jnp.concatenate([jnp.zeros((n, lane0), F32), -s, s, rest], axis=1) * scale
    return ct, st


def _band_bias_kernel(ext_ref, o_ref, *, tq, nk):
    w = ext_ref.shape[-1]
    toeplitz = pltpu.roll(jnp.broadcast_to(ext_ref[0], (nk, w)), 0, 1, stride=1, stride_axis=0)[:, :tq]
    qc = (nk - tq + lax.broadcasted_iota(jnp.int32, (nk, tq), 1)) // CHUNK
    kc = lax.broadcasted_iota(jnp.int32, (nk, tq), 0) // CHUNK
    o_ref[0] = jnp.where((kc >= qc - LEFT_CHUNKS) & (kc <= qc), toeplitz, NEG)


def _band_bias(table, tq, nk):
    w = nk + tq
    d_lo, d_hi = 1 - tq, w - tq
    assert -MAX_REL <= d_lo and d_hi >= MAX_REL and w % 128 == 0
    h = table.shape[0]
    ext = jnp.concatenate([table[:, d_lo + MAX_REL:], jnp.broadcast_to(table[:, -1:], (h, d_hi - MAX_REL))], axis=1)
    ext = (jnp.roll(ext, -(nk - 1), axis=1) * LOG2E).reshape(h, 1, w)
    return pl.pallas_call(
        functools.partial(_band_bias_kernel, tq=tq, nk=nk), grid=(h,),
        in_specs=[pl.BlockSpec((1, 1, w), lambda a: (a, 0, 0))],
        out_specs=pl.BlockSpec((1, nk, tq), lambda a: (a, 0, 0)),
        out_shape=jax.ShapeDtypeStruct((h, nk, tq), F32),
        compiler_params=pltpu.CompilerParams(dimension_semantics=("arbitrary",)),
        name="band_bias")(ext)


def _chunk_mask(tk, tq):
    kc = jnp.arange(tk)[:, None] // CHUNK
    qc = jnp.arange(tq)[None, :] // CHUNK
    return jnp.where(kc <= qc, 0.0, NEG).astype(F32)


def _prep_weights(w_in, w_uk, w_uv):
    o = np.cumsum((0,) + IN_SIZES)
    col = lambda j: w_in[:, o[j]:o[j + 1]]
    d = w_in.shape[0]
    qn = col(3).reshape(d, H_B, DH_NOPE)
    qr = col(4).reshape(d, H_B, DH_ROPE)
    pad = jnp.zeros((d, H_B, HEAD_PAD - DH_NOPE - DH_ROPE), F32).at[:, 0, :].set(col(6))
    w_qf = jnp.concatenate([qn, qr, pad], axis=2).reshape(d, D_QF)
    wm = jnp.concatenate([col(0) * (A_SCALE * LOG2E), col(1), w_qf, col(5), col(7), col(8)], axis=1).astype(BF16)
    wva = col(2).astype(BF16)
    wk = jnp.concatenate([w_uk, jnp.zeros((D_C, H_B, HEAD_PAD - DH_NOPE), F32)], axis=2).reshape(D_C, D_QF).astype(BF16)
    place = jnp.concatenate([jnp.zeros((DH_ROPE, DH_NOPE), F32), jnp.eye(DH_ROPE, dtype=F32),
                             jnp.zeros((DH_ROPE, HEAD_PAD - DH_NOPE - DH_ROPE), F32)], axis=1)
    ek = jnp.tile(place, (1, H_B)).astype(BF16)
    wvt = w_uv.reshape(D_C, D_VB).T.astype(BF16)
    return wm, wva.T, wva, wk, ek, wvt


def kernel(x_prompt, x_sample, cache_a_k, cache_a_v, cache_mla_ckv, cache_mla_krope, state_ffn_conv, norm_mix_pre,
           norm_mix_post, w_in, rel_bias_table, kv_norm, w_uk, w_uv, w_branch_a, w_branch_b, w_out, norm_ffn_pre,
           norm_ffn_post, w_ffn_gate, w_ffn_up, conv_w, conv_b, w_ffn_down):
    assert w_in.shape[0] == 1, "single layer"
    b, s, _ = x_prompt.shape
    db, t, _ = x_sample.shape
    past = cache_mla_ckv.shape[2]
    wcache = cache_a_k.shape[2]
    keep = min(A_WINDOW, s)
    tm1 = 512
    assert keep == tm1 and db * t == tm1 and wcache == A_WINDOW

    wm, wvat, wva, wk, ek, wvt = _prep_weights(w_in[0], w_uk[0], w_uv[0])
    row = lambda v: v.reshape(1, -1)
    proj_w = (row(norm_mix_pre[0]), wm, wvat, wva, row(kv_norm[0]))
    ffn_w = (w_branch_a[0].astype(BF16), w_branch_b[0].astype(BF16), w_out[0].astype(BF16), row(norm_mix_post[0]),
             row(norm_ffn_pre[0]), row(norm_ffn_post[0]), w_ffn_gate[0].astype(BF16), w_ffn_up[0].astype(BF16),
             conv_w[0], row(conv_b[0]), w_ffn_down[0].astype(BF16))
    table = rel_bias_table[0]

    pos = np.arange(s)
    tabs = _rope_tables(pos, DH_NOPE, MLA_SCALE * LOG2E) + _rope_tables(pos, _KR_LANE, 1.0)
    qa, ka, vat, kaf, vaf, qf, ckv, kr, ga, gb = _in_proj(x_prompt, *proj_w, *tabs, tm=tm1)
    t_mla = 512
    kf, vt = _kv_up(ckv, kr, wk, ek, wvt, tm=1024, tk=t_mla)
    ya = _band_prompt(qa, ka, vat, _band_bias(table, 256, A_WINDOW + 256), tm=tm1, tq=256)
    yb = _mla_attn(qf, kf, vt, _chunk_mask(t_mla, t_mla), tq=t_mla, tk=t_mla, causal=True)
    y_prompt, conv_p = _merge_ffn(x_prompt, ya, yb, ga, gb, jnp.zeros((b, CONV_W - 1, D_FF), F32), *ffn_w, tm=512)

    pos_s = np.tile(past + np.arange(t), db)
    tabs_s = _rope_tables(pos_s, DH_NOPE, MLA_SCALE * LOG2E) + _rope_tables(pos_s, _KR_LANE, 1.0)
    qa2, ka2, vat2, kaf2, vaf2, qf2, ckv2, kr2, ga2, gb2 = _in_proj(x_sample.reshape(1, db * t, D_MODEL), *proj_w,
                                                                    *tabs_s, tm=tm1)
    per_seq = lambda v: v.reshape(db, t, v.shape[-1])
    ckv2, kr2 = per_seq(ckv2[0]), per_seq(kr2[0])
    new_k = jnp.concatenate([cache_a_k[0].reshape(db, wcache, D_A), per_seq(kaf2[0])], axis=1)
    new_v = jnp.concatenate([cache_a_v[0].reshape(db, wcache, D_A), per_seq(vaf2[0])], axis=1)
    qa2 = qa2[0].reshape(N_PAIRS, db, t, PAIR_W)
    ya2 = _band_sample(qa2, new_k.astype(BF16), jnp.swapaxes(new_v, 1, 2).astype(BF16),
                       _band_bias(table, t, wcache + t))
    c_all = jnp.concatenate([cache_mla_ckv[0], ckv2], axis=1)
    kr_all = jnp.concatenate([cache_mla_krope[0], kr2], axis=1)
    kf2, vt2 = _kv_up(c_all, kr_all, wk, ek, wvt, tm=past + t, tk=past + t)
    yb2 = _mla_attn(per_seq(qf2[0]), kf2, vt2, jnp.zeros((8, 128), F32), tq=t, tk=past + t, causal=False)
    y_sample, conv_s = _merge_ffn(x_sample.reshape(1, db * t, D_MODEL), ya2.reshape(1, N_PAIRS, db * t, PAIR_W),
                                  yb2.reshape(1, db * t, D_VB), ga2, gb2, state_ffn_conv[0], *ffn_w,
                                  tm=db * t, nseq=db)
    y_sample = y_sample.reshape(db, t, D_MODEL)

    heads = lambda v: v.reshape(1, v.shape[0], v.shape[1], H_A, DH_A)
    return (y_prompt, y_sample,
            heads(kaf), heads(vaf), ckv[None], kr[None], conv_p[None],
            heads(new_k[:, -wcache:]), heads(new_v[:, -wcache:]), ckv2[None], kr2[None], conv_s[None])
```

```python
import functools

import jax
import jax.numpy as jnp
import numpy as np
from jax import lax
from jax.experimental import pallas as pl
from jax.experimental.pallas import tpu as pltpu

D_MODEL = 1024
CHUNK = 64
LEFT_CHUNKS = 8
A_WINDOW = LEFT_CHUNKS * CHUNK
H_A = 8
DH_A = 64
MAX_REL = 256
H_B = 8
DH_NOPE = 64
DH_ROPE = 32
DV_B = 64
D_C = 256
D_FF = 2816
CONV_W = 3
ROPE_BASE = 10000.0
EPS = 1e-6
A_SCALE = DH_A ** -0.5
MLA_SCALE = (DH_NOPE + DH_ROPE) ** -0.5
IN_SIZES = (H_A * DH_A, H_A * DH_A, H_A * DH_A, H_B * DH_NOPE, H_B * DH_ROPE, D_C, DH_ROPE, D_MODEL, D_MODEL)

HEAD_PAD = 128
D_A = H_A * DH_A
D_VB = H_B * DV_B
D_QF = H_B * HEAD_PAD
PAIR_W = 2 * DH_A
N_PAIRS = H_A // 2
NEG = -1e30
LOG2E = 1.4426950408889634
_L_ROWS = 16
VMEM_LIMIT_V7X = 56 * 1024 * 1024

F32 = jnp.float32
BF16 = jnp.bfloat16
_NT = (((1,), (1,)), ((), ()))


def _resident(shape):
    nd = len(shape)
    return pl.BlockSpec(shape, lambda *_: (0,) * nd, pipeline_mode=pl.Buffered(1))


def _rms(x, w):
    return x * lax.rsqrt(jnp.mean(x * x, axis=-1, keepdims=True) + EPS) * w


def _rope_lanes(x, c, s):
    lane = lax.broadcasted_iota(jnp.int32, x.shape, 1)
    partner = jnp.where(lane % 32 < 16, pltpu.roll(x, 128 - 16, 1), pltpu.roll(x, 16, 1))
    return x * c + partner * s


_C_QA, _C_KA, _C_QF, _C_CKV, _C_GA, _C_GB, _C_END = 0, 512, 1024, 2048, 2304, 3328, 4352
_KR_LANE = DH_NOPE + DH_ROPE


def _store_pairs(ref, v):
    for g in range(N_PAIRS):
        ref[0, g] = v[:, g * PAIR_W:(g + 1) * PAIR_W]


def _in_proj_kernel(x_ref, nw_ref, wm_ref, wvt_ref, wva_ref, kvn_ref, cq_ref, sq_ref, ck_ref, sk_ref,
                    qa_ref, ka_ref, vat_ref, kaf_ref, vaf_ref, qf_ref, ckv_ref, kr_ref, ga_ref, gb_ref, *, n):
    i = pl.program_id(1)
    xn = _rms(x_ref[0], nw_ref[...]).astype(BF16)

    def proj(lo, hi):
        return jnp.dot(xn, wm_ref[:, lo:hi], preferred_element_type=F32)

    _store_pairs(qa_ref, proj(_C_QA, _C_KA).astype(BF16))
    ka = proj(_C_KA, _C_QF)
    _store_pairs(ka_ref, ka.astype(BF16))
    kaf_ref[0] = ka
    vat_ref[0, 0] = lax.dot_general(wvt_ref[...], xn, _NT, preferred_element_type=F32).astype(BF16)

    def va_tail():
        vaf_ref[0] = jnp.dot(xn, wva_ref[...], preferred_element_type=F32)

    if n == 1:
        va_tail()
    else:
        pl.when(i == n - 1)(va_tail)

    cq, sq = cq_ref[...], sq_ref[...]
    qraw = proj(_C_QF, _C_CKV)
    for h in range(H_B):
        slab = slice(h * HEAD_PAD, (h + 1) * HEAD_PAD)
        qf_ref[0, :, slab] = _rope_lanes(qraw[:, slab], cq, sq).astype(BF16)
    kr = _rope_lanes(qraw[:, :HEAD_PAD], ck_ref[...], sk_ref[...])
    kr_ref[0] = pltpu.roll(kr, HEAD_PAD - _KR_LANE, 1)[:, :DH_ROPE]

    ckv_ref[0] = _rms(proj(_C_CKV, _C_GA), kvn_ref[...])
    ga_ref[0] = proj(_C_GA, _C_GB).astype(BF16)
    gb_ref[0] = proj(_C_GB, _C_END).astype(BF16)


def _in_proj(x, nw, wm, wvt, wva, kvn, cq, sq, ck, sk, *, tm):
    g, s, _ = x.shape
    n = s // tm
    tok = lambda w: pl.BlockSpec((1, tm, w), lambda a, b: (a, b, 0))
    tab = pl.BlockSpec((tm, HEAD_PAD), lambda a, b: (b, 0))
    tail = pl.BlockSpec((1, tm, D_A), lambda a, b: (a, 0, 0))
    out_shape = (
        jax.ShapeDtypeStruct((g, N_PAIRS, s, PAIR_W), BF16),
        jax.ShapeDtypeStruct((g, N_PAIRS, s, PAIR_W), BF16),
        jax.ShapeDtypeStruct((g, n, D_A, tm), BF16),
        jax.ShapeDtypeStruct((g, tm, D_A), F32),
        jax.ShapeDtypeStruct((g, tm, D_A), F32),
        jax.ShapeDtypeStruct((g, s, D_QF), BF16),
        jax.ShapeDtypeStruct((g, s, D_C), F32),
        jax.ShapeDtypeStruct((g, s, DH_ROPE), F32),
        jax.ShapeDtypeStruct((g, s, D_MODEL), BF16),
        jax.ShapeDtypeStruct((g, s, D_MODEL), BF16),
    )
    pairs = pl.BlockSpec((1, N_PAIRS, tm, PAIR_W), lambda a, b: (a, 0, b, 0))
    out_specs = (pairs, pairs, pl.BlockSpec((1, 1, D_A, tm), lambda a, b: (a, b, 0, 0)), tail, tail,
                 tok(D_QF), tok(D_C), tok(DH_ROPE), tok(D_MODEL), tok(D_MODEL))
    in_specs = [tok(D_MODEL), _resident(nw.shape), _resident(wm.shape), _resident(wvt.shape), _resident(wva.shape),
                _resident(kvn.shape), tab, tab, tab, tab]
    return pl.pallas_call(
        functools.partial(_in_proj_kernel, n=n), grid=(g, n), in_specs=in_specs, out_specs=out_specs, out_shape=out_shape,
        compiler_params=pltpu.CompilerParams(dimension_semantics=("arbitrary", "arbitrary"),
                                             vmem_limit_bytes=VMEM_LIMIT_V7X),
        name="in_proj")(x, nw, wm, wvt, wva, kvn, cq, sq, ck, sk)


def _kv_up_kernel(ckv_ref, kr_ref, wk_ref, ek_ref, wvt_ref, kf_ref, vt_ref, *, tk):
    c = ckv_ref[0].astype(BF16)
    kf = jnp.dot(c, wk_ref[...], preferred_element_type=F32)
    kf = kf + jnp.dot(kr_ref[0].astype(BF16), ek_ref[...], preferred_element_type=F32)
    kf_ref[0] = kf.astype(BF16)
    for j in range(c.shape[0] // tk):
        vt_ref[0, j] = lax.dot_general(wvt_ref[...], c[j * tk:(j + 1) * tk], _NT,
                                       preferred_element_type=F32).astype(BF16)


def _kv_up(ckv, kr, wk, ek, wvt, *, tm, tk):
    g, s, _ = ckv.shape
    return pl.pallas_call(
        functools.partial(_kv_up_kernel, tk=tk), grid=(g, s // tm),
        in_specs=[pl.BlockSpec((1, tm, D_C), lambda a, b: (a, b, 0)),
                  pl.BlockSpec((1, tm, DH_ROPE), lambda a, b: (a, b, 0)),
                  _resident(wk.shape), _resident(ek.shape), _resident(wvt.shape)],
        out_specs=(pl.BlockSpec((1, tm, D_QF), lambda a, b: (a, b, 0)),
                   pl.BlockSpec((1, tm // tk, D_VB, tk), lambda a, b: (a, b, 0, 0))),
        out_shape=(jax.ShapeDtypeStruct((g, s, D_QF), BF16),
                   jax.ShapeDtypeStruct((g, s // tk, D_VB, tk), BF16)),
        compiler_params=pltpu.CompilerParams(dimension_semantics=("arbitrary", "arbitrary"),
                                             vmem_limit_bytes=VMEM_LIMIT_V7X),
        name="kv_up")(ckv, kr, wk, ek, wvt)


def _band_units(units, bias_ref, n_pairs=H_A // 2):
    def scores(u, g):
        q, k_of, _, _ = units[u]
        lanes = slice(g * 2 * DH_A, (g + 1) * 2 * DH_A)
        lane = lax.broadcasted_iota(jnp.int32, (q.shape[0], 2 * DH_A), 1)
        k2 = k_of(lanes)
        return [lax.dot_general(k2, jnp.where((lane >= DH_A) == (hl == 1), q[:, lanes], jnp.zeros((), BF16)), _NT,
                                preferred_element_type=F32) for hl in range(2)]

    steps = [(u, g) for u in range(len(units)) for g in range(n_pairs)]
    outs = [[] for _ in units]
    pend = scores(*steps[0])
    for j, (u, g) in enumerate(steps):
        nxt = scores(*steps[j + 1]) if j + 1 < len(steps) else None
        _, _, v_of, b0 = units[u]
        for hl in range(2):
            h = 2 * g + hl
            nk = pend[hl].shape[0]
            s = pend[hl] + bias_ref[h, b0:b0 + nk, :]
            p = jnp.exp2(s - jnp.max(s, axis=0, keepdims=True)).astype(BF16)
            v = jnp.concatenate([v_of(slice(h * DH_A, (h + 1) * DH_A)), jnp.ones((_L_ROWS, nk), BF16)], axis=0)
            o = jnp.dot(v, p, preferred_element_type=F32)
            outs[u].append(o[:DH_A] / o[DH_A:DH_A + 1])
        pend = nxt
    return [jnp.concatenate(o, axis=0).T for o in outs]


def _band_prompt_kernel(q_ref, kp_ref, kc_ref, vp_ref, vc_ref, bias_ref, o_ref, s_scr, *, tm, tq):
    subs = [(lo, lo + tq) for lo in range(0, tm, tq)]
    lane = lax.broadcasted_iota(jnp.int32, (tq, PAIR_W), 1)

    def tile(first):
        def window(lo, hi):
            return (hi, tm - lo) if first else (tm - lo + hi, 0)

        def scores(g, j):
            lo, hi = subs[j]
            nk, b0 = window(lo, hi)
            q2 = q_ref[0, g, lo:hi, :]
            qcat = jnp.concatenate([jnp.where(lane < DH_A, q2, jnp.zeros((), BF16)),
                                    jnp.where(lane >= DH_A, q2, jnp.zeros((), BF16))], axis=0)
            k2 = kc_ref[0, g, :hi, :]
            if not first:
                k2 = jnp.concatenate([kp_ref[0, g, lo:, :], k2], axis=0)
            bias = jnp.concatenate([bias_ref[2 * g + hl, b0:b0 + nk, :] for hl in range(2)], axis=1)
            return lax.dot_general(k2, qcat, _NT, preferred_element_type=F32) + bias

        def stash(j, s):
            s_scr[j, :s.shape[0], :] = s
            return jnp.max(s, axis=0, keepdims=True)

        def consume(g, j, col_max):
            lo, hi = subs[j]
            nk, _ = window(lo, hi)
            outs = []
            for hl in range(2):
                cols = slice(hl * tq, (hl + 1) * tq)
                p = jnp.exp2(s_scr[j, :nk, cols] - col_max[:, cols]).astype(BF16)
                rows = pl.ds(g * PAIR_W + hl * DH_A, DH_A)
                v = vc_ref[0, 0, rows, :hi]
                if not first:
                    v = jnp.concatenate([vp_ref[0, 0, rows, lo:], v], axis=1)
                v = jnp.concatenate([v, jnp.ones((_L_ROWS, nk), BF16)], axis=0)
                o = jnp.dot(v, p, preferred_element_type=F32)
                outs.append(o[:DH_A] / o[DH_A:DH_A + 1])
            o_ref[0, g, lo:hi, :] = jnp.concatenate(outs, axis=0).T.astype(BF16)

        def body(g, col_max):
            new_max = []
            for j in range(len(subs)):
                nxt = scores(g + 1, j)
                consume(g, j, col_max[j])
                new_max.append(stash(j, nxt))
            return tuple(new_max)

        col_max = tuple(stash(j, scores(0, j)) for j in range(len(subs)))
        for g in range(N_PAIRS - 1):
            col_max = body(g, col_max)
        for j in range(len(subs)):
            consume(N_PAIRS - 1, j, col_max[j])

    pl.when(pl.program_id(1) == 0)(functools.partial(tile, True))
    pl.when(pl.program_id(1) > 0)(functools.partial(tile, False))


def _band_prompt(qa, ka, vat, bias, *, tm, tq):
    g, _, s, _ = qa.shape
    cur = lambda a, b: (a, 0, b, 0)
    prev = lambda a, b: (a, 0, jnp.maximum(b - 1, 0), 0)
    prev_v = lambda a, b: (a, jnp.maximum(b - 1, 0), 0, 0)
    pairs = lambda idx: pl.BlockSpec((1, N_PAIRS, tm, PAIR_W), idx)
    return pl.pallas_call(
        functools.partial(_band_prompt_kernel, tm=tm, tq=tq), grid=(g, s // tm),
        in_specs=[pairs(cur), pairs(prev), pairs(cur),
                  pl.BlockSpec((1, 1, D_A, tm), prev_v), pl.BlockSpec((1, 1, D_A, tm), lambda a, b: (a, b, 0, 0)),
                  _resident(bias.shape)],
        out_specs=pairs(cur),
        out_shape=jax.ShapeDtypeStruct((g, N_PAIRS, s, PAIR_W), BF16),
        scratch_shapes=[pltpu.VMEM((tm // tq, tm + tq, 2 * tq), F32)],
        compiler_params=pltpu.CompilerParams(dimension_semantics=("arbitrary", "arbitrary"),
                                             vmem_limit_bytes=VMEM_LIMIT_V7X),
        name="band_prompt")(qa, ka, ka, vat, vat, bias)


def _band_sample_kernel(q_ref, k_ref, vt_ref, bias_ref, o_ref):
    q = jnp.concatenate([q_ref[g, 0] for g in range(N_PAIRS)], axis=1)
    unit = (q, lambda lanes: k_ref[0, :, lanes], lambda rows: vt_ref[0, rows, :], 0)
    o = _band_units([unit], bias_ref)[0].astype(BF16)
    for g in range(N_PAIRS):
        o_ref[g, 0] = o[:, g * PAIR_W:(g + 1) * PAIR_W]


def _band_sample(qa, k_all, vt_all, bias):
    _, g, t, _ = qa.shape
    nk = k_all.shape[1]
    pairs = pl.BlockSpec((N_PAIRS, 1, t, PAIR_W), lambda a: (0, a, 0, 0))
    return pl.pallas_call(
        _band_sample_kernel, grid=(g,),
        in_specs=[pairs,
                  pl.BlockSpec((1, nk, D_A), lambda a: (a, 0, 0)),
                  pl.BlockSpec((1, D_A, nk), lambda a: (a, 0, 0)),
                  _resident(bias.shape)],
        out_specs=pairs,
        out_shape=jax.ShapeDtypeStruct((N_PAIRS, g, t, PAIR_W), BF16),
        compiler_params=pltpu.CompilerParams(dimension_semantics=("arbitrary",),
                                             vmem_limit_bytes=VMEM_LIMIT_V7X),
        name="band_sample")(qa, k_all, vt_all, bias)


def _mla_kernel(q_ref, k_ref, vt_ref, mask_ref, o_ref, s_scr, *, tq, tk, causal):
    nq = q_ref.shape[1] // tq
    nk = k_ref.shape[1] // tk
    ones = jnp.ones((_L_ROWS, tk), BF16)

    def q_rows(i):
        qrow = pl.multiple_of(i * tq, tq)
        return [q_ref[0, pl.ds(qrow, tq), hl * HEAD_PAD:(hl + 1) * HEAD_PAD] for hl in range(2)]

    def scores(qs, kk, hl):
        krow = pl.multiple_of(kk * tk, tk)
        return lax.dot_general(k_ref[0, pl.ds(krow, tk), hl * HEAD_PAD:(hl + 1) * HEAD_PAD], qs[hl], _NT,
                               preferred_element_type=F32)

    def stash(hl, s):
        s_scr[hl] = s
        return jnp.max(s, axis=0, keepdims=True)

    def consume(kk, hl, m, acc, tile_max, mask_j):
        if mask_j is not None:
            s = s_scr[hl] + mask_ref[mask_j]
            m_new = jnp.maximum(m, jnp.max(s, axis=0, keepdims=True))
        else:
            s = s_scr[hl]
            m_new = jnp.maximum(m, tile_max)
        p = jnp.exp2(s - m_new).astype(BF16)
        v = jnp.concatenate([vt_ref[0, kk, hl * DV_B:(hl + 1) * DV_B, :], ones], axis=0)
        return m_new, acc * jnp.exp2(m - m_new) + jnp.dot(v, p, preferred_element_type=F32)

    def q_tile(i, first_max):
        qs = q_rows(i)

        def body(t, carry):
            out = []
            for hl in range(2):
                m, acc, tile_max = carry[3 * hl:3 * hl + 3]
                nxt = scores(qs, t + 1, hl)
                out += list(consume(t, hl, m, acc, tile_max, None)) + [stash(hl, nxt)]
            return tuple(out)

        r = tq // tk
        n_full = r * i if causal else nk - 1
        carry = ()
        for hl in range(2):
            carry += (jnp.full((1, tq), NEG, F32), jnp.zeros((DV_B + _L_ROWS, tq), F32), first_max[hl])
        done = 0
        for n in (4, 2, 1):
            left = n_full - done
            trips = left // n if isinstance(left, int) else lax.shift_right_logical(left, n.bit_length() - 1)

            def trip(u, c, n=n, done=done):
                for j in range(n):
                    c = body(done + n * u + j, c)
                return c

            carry = lax.fori_loop(0, trips, trip, carry)
            done = done + trips * n
        tail = [(r * i + j, j) for j in range(r)] if causal else [(nk - 1, None)]
        qs_next = q_rows(jnp.minimum(i + 1, nq - 1)) if nq > 1 else None
        state = [list(carry[3 * hl:3 * hl + 3]) for hl in range(2)]
        for idx, (tile, mask_j) in enumerate(tail):
            for hl in range(2):
                if idx + 1 < len(tail):
                    nxt = scores(qs, tile + 1, hl)
                else:
                    nxt = scores(qs_next, 0, hl) if nq > 1 else None
                m, acc = consume(tile, hl, *state[hl], mask_j)
                state[hl] = [m, acc, stash(hl, nxt) if nxt is not None else state[hl][2]]
        accs = [state[hl][1] for hl in range(2)]
        next_max = [state[hl][2] for hl in range(2)]
        o = jnp.concatenate([acc[:DV_B] / acc[DV_B:DV_B + 1] for acc in accs], axis=0)
        o_ref[0, pl.ds(pl.multiple_of(i * tq, tq), tq), :] = o.T.astype(BF16)
        return tuple(next_max)

    first = q_rows(0)
    lax.fori_loop(0, nq, q_tile, tuple(stash(hl, scores(first, 0, hl)) for hl in range(2)))


def _mla_attn(qf, kf, vt, mask, *, tq, tk, causal):
    g, sq, _ = qf.shape
    sk = kf.shape[1]
    return pl.pallas_call(
        functools.partial(_mla_kernel, tq=tq, tk=tk, causal=causal), grid=(g, H_B // 2),
        in_specs=[pl.BlockSpec((1, sq, 2 * HEAD_PAD), lambda a, b: (a, 0, b)),
                  pl.BlockSpec((1, sk, 2 * HEAD_PAD), lambda a, b: (a, 0, b)),
                  pl.BlockSpec((1, sk // tk, 2 * DV_B, tk), lambda a, b: (a, 0, b, 0)),
                  _resident(mask.shape)],
        out_specs=pl.BlockSpec((1, sq, 2 * DV_B), lambda a, b: (a, 0, b)),
        out_shape=jax.ShapeDtypeStruct((g, sq, D_VB), BF16),
        scratch_shapes=[pltpu.VMEM((2, tk, tq), F32)],
        compiler_params=pltpu.CompilerParams(dimension_semantics=("arbitrary", "arbitrary"),
                                             vmem_limit_bytes=VMEM_LIMIT_V7X),
        name="mla_attn")(qf, kf, vt, mask)


_G0 = 8


def _merge_ffn_kernel(x_ref, ya_ref, yb_ref, ga_ref, gb_ref, st_ref, wa_ref, wb_ref, wo_ref, n1_ref, n2_ref, n3_ref,
                      wg_ref, wu_ref, cw_ref, cb_ref, wd_ref, y_ref, cs_ref, gbuf, *, tm, nseq):
    i = pl.program_id(1)
    seg = tm // nseq
    ya = jnp.concatenate([ya_ref[0, g] for g in range(N_PAIRS)], axis=1)
    za = jnp.dot(ya, wa_ref[...], preferred_element_type=F32)
    zb = jnp.dot(yb_ref[0], wb_ref[...], preferred_element_type=F32)
    mix = jax.nn.sigmoid(ga_ref[0].astype(F32)) * za + jax.nn.sigmoid(gb_ref[0].astype(F32)) * zb
    mo = jnp.dot(mix.astype(BF16), wo_ref[...], preferred_element_type=F32)
    x1 = x_ref[0] + _rms(mo, n1_ref[...])
    xn = _rms(x1, n2_ref[...]).astype(BF16)

    starts = [_G0 + j * (seg + _G0) for j in range(nseq)]
    if nseq == 1:
        @pl.when(i == 0)
        def _():
            gbuf[_G0 - 2:_G0, :] = st_ref[0]

        @pl.when(i > 0)
        def _():
            gbuf[_G0 - 2:_G0, :] = gbuf[_G0 + tm - 2:_G0 + tm, :]
    else:
        for j, r0 in enumerate(starts):
            gbuf[r0 - 2:r0, :] = st_ref[j]

    gate = jnp.dot(xn, wg_ref[...], preferred_element_type=F32)
    for j, r0 in enumerate(starts):
        gbuf[r0:r0 + seg, :] = gate[j * seg:(j + 1) * seg]
        cs_ref[j] = gbuf[r0 + seg - 2:r0 + seg, :]
    u = jnp.dot(xn, wu_ref[...], preferred_element_type=F32)
    c = jnp.concatenate([cw_ref[0:1, :] * gbuf[r0 - 2:r0 - 2 + seg, :] + cw_ref[1:2, :] * gbuf[r0 - 1:r0 - 1 + seg, :]
                         + cw_ref[2:3, :] * gbuf[r0:r0 + seg, :] for r0 in starts], axis=0) + cb_ref[...]
    hid = (jax.nn.gelu(c, approximate=True) * u).astype(BF16)
    f = jnp.dot(hid, wd_ref[...], preferred_element_type=F32)
    y_ref[0] = x1 + _rms(f, n3_ref[...])


def _merge_ffn(x, ya, yb, ga, gb, state, wa, wb, wo, n1, n2, n3, wg, wu, cw, cb, wd, *, tm, nseq=1):
    g, s, _ = x.shape
    assert nseq == 1 or s == tm
    tok = lambda w: pl.BlockSpec((1, tm, w), lambda a, b: (a, b, 0))
    per_group = pl.BlockSpec((nseq, CONV_W - 1, D_FF), lambda a, b: (a, 0, 0))
    weights = (wa, wb, wo, n1, n2, n3, wg, wu, cw, cb, wd)
    return pl.pallas_call(
        functools.partial(_merge_ffn_kernel, tm=tm, nseq=nseq), grid=(g, s // tm),
        in_specs=[tok(D_MODEL), pl.BlockSpec((1, N_PAIRS, tm, PAIR_W), lambda a, b: (a, 0, b, 0)), tok(D_VB),
                  tok(D_MODEL), tok(D_MODEL), per_group]
                 + [_resident(w.shape) for w in weights],
        out_specs=(tok(D_MODEL), per_group),
        out_shape=(jax.ShapeDtypeStruct((g, s, D_MODEL), F32),
                   jax.ShapeDtypeStruct((g * nseq, CONV_W - 1, D_FF), F32)),
        scratch_shapes=[pltpu.VMEM((tm + nseq * _G0, D_FF), F32)],
        compiler_params=pltpu.CompilerParams(dimension_semantics=("arbitrary", "arbitrary"),
                                             vmem_limit_bytes=VMEM_LIMIT_V7X),
        name="merge_ffn")(x, ya, yb, ga, gb, state, *weights)


def _rope_tables(pos, lane0, scale):
    half = DH_ROPE // 2
    inv = ROPE_BASE ** (-np.arange(half, dtype=np.float64) / half)
    ang = np.asarray(pos, np.float64)[:, None] * inv[None, :]
    c, s = jnp.asarray(np.cos(ang), F32), jnp.asarray(np.sin(ang), F32)
    n = ang.shape[0]
    rest = jnp.zeros((n, HEAD_PAD - lane0 - DH_ROPE), F32)
    ct = jnp.concatenate([jnp.ones((n, lane0), F32), c, c, rest], axis=1) * scale
    st = jnp.concatenate([jnp.zeros((n, lane0), F32), -s, s, rest], axis=1) * scale
    return ct, st


def _band_bias_kernel(ext_ref, o_ref, *, tq, nk):
    w = ext_ref.shape[-1]
    toeplitz = pltpu.roll(jnp.broadcast_to(ext_ref[0], (nk, w)), 0, 1, stride=1, stride_axis=0)[:, :tq]
    qc = (nk - tq + lax.broadcasted_iota(jnp.int32, (nk, tq), 1)) // CHUNK
    kc = lax.broadcasted_iota(jnp.int32, (nk, tq), 0) // CHUNK
    o_ref[0] = jnp.where((kc >= qc - LEFT_CHUNKS) & (kc <= qc), toeplitz, NEG)


def _band_bias(table, tq, nk):
    w = nk + tq
    d_lo, d_hi = 1 - tq, w - tq
    assert -MAX_REL <= d_lo and d_hi >= MAX_REL and w % 128 == 0
    h = table.shape[0]
    ext = jnp.concatenate([table[:, d_lo + MAX_REL:], jnp.broadcast_to(table[:, -1:], (h, d_hi - MAX_REL))], axis=1)
    ext = (jnp.roll(ext, -(nk - 1), axis=1) * LOG2E).reshape(h, 1, w)
    return pl.pallas_call(
        functools.partial(_band_bias_kernel, tq=tq, nk=nk), grid=(h,),
        in_specs=[pl.BlockSpec((1, 1, w), lambda a: (a, 0, 0))],
        out_specs=pl.BlockSpec((1, nk, tq), lambda a: (a, 0, 0)),
        out_shape=jax.ShapeDtypeStruct((h, nk, tq), F32),
        compiler_params=pltpu.CompilerParams(dimension_semantics=("arbitrary",)),
        name="band_bias")(ext)


def _chunk_masks(tk, tq):
    kc = jnp.arange(tq)[:, None] // CHUNK
    qc = jnp.arange(tq)[None, :] // CHUNK
    return jnp.where(kc <= qc, 0.0, NEG).astype(F32).reshape(tq // tk, tk, tq)


def _prep_weights(w_in, w_uk, w_uv):
    o = np.cumsum((0,) + IN_SIZES)
    col = lambda j: w_in[:, o[j]:o[j + 1]]
    d = w_in.shape[0]
    qn = col(3).reshape(d, H_B, DH_NOPE)
    qr = col(4).reshape(d, H_B, DH_ROPE)
    pad = jnp.zeros((d, H_B, HEAD_PAD - DH_NOPE - DH_ROPE), F32).at[:, 0, :].set(col(6))
    w_qf = jnp.concatenate([qn, qr, pad], axis=2).reshape(d, D_QF)
    wm = jnp.concatenate([col(0) * (A_SCALE * LOG2E), col(1), w_qf, col(5), col(7), col(8)], axis=1).astype(BF16)
    wva = col(2).astype(BF16)
    wk = jnp.concatenate([w_uk, jnp.zeros((D_C, H_B, HEAD_PAD - DH_NOPE), F32)], axis=2).reshape(D_C, D_QF).astype(BF16)
    place = jnp.concatenate([jnp.zeros((DH_ROPE, DH_NOPE), F32), jnp.eye(DH_ROPE, dtype=F32),
                             jnp.zeros((DH_ROPE, HEAD_PAD - DH_NOPE - DH_ROPE), F32)], axis=1)
    ek = jnp.tile(place, (1, H_B)).astype(BF16)
    wvt = w_uv.reshape(D_C, D_VB).T.astype(BF16)
    return wm, wva.T, wva, wk, ek, wvt


def kernel(x_prompt, x_sample, cache_a_k, cache_a_v, cache_mla_ckv, cache_mla_krope, state_ffn_conv, norm_mix_pre,
           norm_mix_post, w_in, rel_bias_table, kv_norm, w_uk, w_uv, w_branch_a, w_branch_b, w_out, norm_ffn_pre,
           norm_ffn_post, w_ffn_gate, w_ffn_up, conv_w, conv_b, w_ffn_down):
    assert w_in.shape[0] == 1, "single layer"
    b, s, _ = x_prompt.shape
    db, t, _ = x_sample.shape
    past = cache_mla_ckv.shape[2]
    wcache = cache_a_k.shape[2]
    keep = min(A_WINDOW, s)
    tm1 = 512
    assert keep == tm1 and db * t == tm1 and wcache == A_WINDOW

    wm, wvat, wva, wk, ek, wvt = _prep_weights(w_in[0], w_uk[0], w_uv[0])
    row = lambda v: v.reshape(1, -1)
    proj_w = (row(norm_mix_pre[0]), wm, wvat, wva, row(kv_norm[0]))
    ffn_w = (w_branch_a[0].astype(BF16), w_branch_b[0].astype(BF16), w_out[0].astype(BF16), row(norm_mix_post[0]),
             row(norm_ffn_pre[0]), row(norm_ffn_post[0]), w_ffn_gate[0].astype(BF16), w_ffn_up[0].astype(BF16),
             conv_w[0], row(conv_b[0]), w_ffn_down[0].astype(BF16))
    table = rel_bias_table[0]

    pos = np.arange(s)
    tabs = _rope_tables(pos, DH_NOPE, MLA_SCALE * LOG2E) + _rope_tables(pos, _KR_LANE, 1.0)
    qa, ka, vat, kaf, vaf, qf, ckv, kr, ga, gb = _in_proj(x_prompt, *proj_w, *tabs, tm=tm1)
    t_mla = 512
    kf, vt = _kv_up(ckv, kr, wk, ek, wvt, tm=1024, tk=t_mla)
    ya = _band_prompt(qa, ka, vat, _band_bias(table, 256, A_WINDOW + 256), tm=tm1, tq=256)
    yb = _mla_attn(qf, kf, vt, _chunk_masks(t_mla, 2 * t_mla), tq=2 * t_mla, tk=t_mla, causal=True)
    y_prompt, conv_p = _merge_ffn(x_prompt, ya, yb, ga, gb, jnp.zeros((b, CONV_W - 1, D_FF), F32), *ffn_w, tm=512)

    pos_s = np.tile(past + np.arange(t), db)
    tabs_s = _rope_tables(pos_s, DH_NOPE, MLA_SCALE * LOG2E) + _rope_tables(pos_s, _KR_LANE, 1.0)
    qa2, ka2, vat2, kaf2, vaf2, qf2, ckv2, kr2, ga2, gb2 = _in_proj(x_sample.reshape(1, db * t, D_MODEL), *proj_w,
                                                                    *tabs_s, tm=tm1)
    per_seq = lambda v: v.reshape(db, t, v.shape[-1])
    ckv2, kr2 = per_seq(ckv2[0]), per_seq(kr2[0])
    new_k = jnp.concatenate([cache_a_k[0].reshape(db, wcache, D_A), per_seq(kaf2[0])], axis=1)
    new_v = jnp.concatenate([cache_a_v[0].reshape(db, wcache, D_A), per_seq(vaf2[0])], axis=1)
    qa2 = qa2[0].reshape(N_PAIRS, db, t, PAIR_W)
    ya2 = _band_sample(qa2, new_k.astype(BF16), jnp.swapaxes(new_v, 1, 2).astype(BF16),
                       _band_bias(table, t, wcache + t))
    c_all = jnp.concatenate([cache_mla_ckv[0], ckv2], axis=1)
    kr_all = jnp.concatenate([cache_mla_krope[0], kr2], axis=1)
    kf2, vt2 = _kv_up(c_all, kr_all, wk, ek, wvt, tm=past + t, tk=past + t)
    yb2 = _mla_attn(per_seq(qf2[0]), kf2, vt2, jnp.zeros((1, 8, 128), F32), tq=t, tk=past + t, causal=False)
    y_sample, conv_s = _merge_ffn(x_sample.reshape(1, db * t, D_MODEL), ya2.reshape(1, N_PAIRS, db * t, PAIR_W),
                                  yb2.reshape(1, db * t, D_VB), ga2, gb2, state_ffn_conv[0], *ffn_w,
                                  tm=db * t, nseq=db)
    y_sample = y_sample.reshape(db, t, D_MODEL)

    heads = lambda v: v.reshape(1, v.shape[0], v.shape[1], H_A, DH_A)
    return (y_prompt, y_sample,
            heads(kaf), heads(vaf), ckv[None], kr[None], conv_p[None],
            heads(new_k[:, -wcache:]), heads(new_v[:, -wcache:]), ckv2[None], kr2[None], conv_s[None])
```

```python
import functools

import jax
import jax.numpy as jnp
import numpy as np
from jax import lax
from jax.experimental import pallas as pl
from jax.experimental.pallas import tpu as pltpu

D_MODEL = 1024
CHUNK = 64
LEFT_CHUNKS = 8
A_WINDOW = LEFT_CHUNKS * CHUNK
H_A = 8
DH_A = 64
MAX_REL = 256
H_B = 8
DH_NOPE = 64
DH_ROPE = 32
DV_B = 64
D_C = 256
D_FF = 2816
CONV_W = 3
ROPE_BASE = 10000.0
EPS = 1e-6
A_SCALE = DH_A ** -0.5
MLA_SCALE = (DH_NOPE + DH_ROPE) ** -0.5
IN_SIZES = (H_A * DH_A, H_A * DH_A, H_A * DH_A, H_B * DH_NOPE, H_B * DH_ROPE, D_C, DH_ROPE, D_MODEL, D_MODEL)

HEAD_PAD = 128
D_A = H_A * DH_A
D_VB = H_B * DV_B
D_QF = H_B * HEAD_PAD
PAIR_W = 2 * DH_A
N_PAIRS = H_A // 2
NEG = -1e30
LOG2E = 1.4426950408889634
_L_ROWS = 16
VMEM_LIMIT_V7X = 56 * 1024 * 1024

F32 = jnp.float32
BF16 = jnp.bfloat16
_NT = (((1,), (1,)), ((), ()))


def _resident(shape):
    nd = len(shape)
    return pl.BlockSpec(shape, lambda *_: (0,) * nd, pipeline_mode=pl.Buffered(1))


def _rms(x, w):
    return x * lax.rsqrt(jnp.mean(x * x, axis=-1, keepdims=True) + EPS) * w


_C_QA, _C_KA, _C_CKV, _C_GA, _C_GB, _C_END = 0, 512, 1024, 1280, 2304, 3328
_KR_ROW = DH_NOPE + DH_ROPE
_HALF = DH_ROPE // 2


def _store_pairs(ref, v):
    for g in range(N_PAIRS):
        ref[0, g] = v[:, g * PAIR_W:(g + 1) * PAIR_W]


def _rope_rows(x, c, s):
    x1, x2 = x[:_HALF], x[_HALF:]
    return [x1 * c - x2 * s, x1 * s + x2 * c]


def _in_proj_kernel(x_ref, nw_ref, wm_ref, wqt_ref, wvt_ref, wva_ref, kvn_ref, cq_ref, sq_ref, ck_ref, sk_ref,
                    qa_ref, ka_ref, vat_ref, kaf_ref, vaf_ref, qf_ref, ckv_ref, kr_ref, ga_ref, gb_ref, *, n):
    i = pl.program_id(1)
    xn = _rms(x_ref[0], nw_ref[...]).astype(BF16)

    def proj(lo, hi):
        return jnp.dot(xn, wm_ref[:, lo:hi], preferred_element_type=F32)

    _store_pairs(qa_ref, proj(_C_QA, _C_KA).astype(BF16))
    ka = proj(_C_KA, _C_CKV)
    _store_pairs(ka_ref, ka.astype(BF16))
    kaf_ref[0] = ka
    vat_ref[0, 0] = lax.dot_general(wvt_ref[...], xn, _NT, preferred_element_type=F32).astype(BF16)

    def va_tail():
        vaf_ref[0] = jnp.dot(xn, wva_ref[...], preferred_element_type=F32)

    if n == 1:
        va_tail()
    else:
        pl.when(i == n - 1)(va_tail)

    qt = lax.dot_general(wqt_ref[...], xn, _NT, preferred_element_type=F32)
    cq, sq = cq_ref[...], sq_ref[...]
    pad = jnp.zeros((HEAD_PAD - _KR_ROW, qt.shape[1]), F32)
    for h in range(H_B):
        slab = qt[h * HEAD_PAD:(h + 1) * HEAD_PAD]
        rows = [slab[:DH_NOPE] * (MLA_SCALE * LOG2E)] + _rope_rows(slab[DH_NOPE:_KR_ROW], cq, sq) + [pad]
        qf_ref[0, 0, h * HEAD_PAD:(h + 1) * HEAD_PAD, :] = jnp.concatenate(rows, axis=0).astype(BF16)
    kr_t = _rope_rows(qt[_KR_ROW:HEAD_PAD], ck_ref[...], sk_ref[...]) + [jnp.zeros((HEAD_PAD - DH_ROPE, qt.shape[1]), F32)]
    kr_ref[0] = jnp.concatenate(kr_t, axis=0).T[:, :DH_ROPE]

    ckv_ref[0] = _rms(proj(_C_CKV, _C_GA), kvn_ref[...])
    ga_ref[0] = proj(_C_GA, _C_GB).astype(BF16)
    gb_ref[0] = proj(_C_GB, _C_END).astype(BF16)


def _in_proj(x, nw, wm, wqt, wvt, wva, kvn, cq, sq, ck, sk, *, tm):
    g, s, _ = x.shape
    n = s // tm
    tok = lambda w: pl.BlockSpec((1, tm, w), lambda a, b: (a, b, 0))
    tab = pl.BlockSpec((_HALF, tm), lambda a, b: (0, b))
    tail = pl.BlockSpec((1, tm, D_A), lambda a, b: (a, 0, 0))
    out_shape = (
        jax.ShapeDtypeStruct((g, N_PAIRS, s, PAIR_W), BF16),
        jax.ShapeDtypeStruct((g, N_PAIRS, s, PAIR_W), BF16),
        jax.ShapeDtypeStruct((g, n, D_A, tm), BF16),
        jax.ShapeDtypeStruct((g, tm, D_A), F32),
        jax.ShapeDtypeStruct((g, tm, D_A), F32),
        jax.ShapeDtypeStruct((g, n, D_QF, tm), BF16),
        jax.ShapeDtypeStruct((g, s, D_C), F32),
        jax.ShapeDtypeStruct((g, s, DH_ROPE), F32),
        jax.ShapeDtypeStruct((g, s, D_MODEL), BF16),
        jax.ShapeDtypeStruct((g, s, D_MODEL), BF16),
    )
    pairs = pl.BlockSpec((1, N_PAIRS, tm, PAIR_W), lambda a, b: (a, 0, b, 0))
    out_specs = (pairs, pairs, pl.BlockSpec((1, 1, D_A, tm), lambda a, b: (a, b, 0, 0)), tail, tail,
                 pl.BlockSpec((1, 1, D_QF, tm), lambda a, b: (a, b, 0, 0)), tok(D_C), tok(DH_ROPE), tok(D_MODEL),
                 tok(D_MODEL))
    in_specs = [tok(D_MODEL), _resident(nw.shape), _resident(wm.shape), _resident(wqt.shape), _resident(wvt.shape),
                _resident(wva.shape), _resident(kvn.shape), tab, tab, tab, tab]
    return pl.pallas_call(
        functools.partial(_in_proj_kernel, n=n), grid=(g, n), in_specs=in_specs, out_specs=out_specs, out_shape=out_shape,
        compiler_params=pltpu.CompilerParams(dimension_semantics=("arbitrary", "arbitrary"),
                                             vmem_limit_bytes=VMEM_LIMIT_V7X),
        name="in_proj")(x, nw, wm, wqt, wvt, wva, kvn, cq, sq, ck, sk)


def _kv_up_kernel(ckv_ref, kr_ref, wk_ref, ek_ref, wvt_ref, kf_ref, vt_ref, *, tk):
    c = ckv_ref[0].astype(BF16)
    kf = jnp.dot(c, wk_ref[...], preferred_element_type=F32)
    kf = kf + jnp.dot(kr_ref[0].astype(BF16), ek_ref[...], preferred_element_type=F32)
    kf_ref[0] = kf.astype(BF16)
    for j in range(c.shape[0] // tk):
        vt_ref[0, j] = lax.dot_general(wvt_ref[...], c[j * tk:(j + 1) * tk], _NT,
                                       preferred_element_type=F32).astype(BF16)


def _kv_up(ckv, kr, wk, ek, wvt, *, tm, tk):
    g, s, _ = ckv.shape
    return pl.pallas_call(
        functools.partial(_kv_up_kernel, tk=tk), grid=(g, s // tm),
        in_specs=[pl.BlockSpec((1, tm, D_C), lambda a, b: (a, b, 0)),
                  pl.BlockSpec((1, tm, DH_ROPE), lambda a, b: (a, b, 0)),
                  _resident(wk.shape), _resident(ek.shape), _resident(wvt.shape)],
        out_specs=(pl.BlockSpec((1, tm, D_QF), lambda a, b: (a, b, 0)),
                   pl.BlockSpec((1, tm // tk, D_VB, tk), lambda a, b: (a, b, 0, 0))),
        out_shape=(jax.ShapeDtypeStruct((g, s, D_QF), BF16),
                   jax.ShapeDtypeStruct((g, s // tk, D_VB, tk), BF16)),
        compiler_params=pltpu.CompilerParams(dimension_semantics=("arbitrary", "arbitrary"),
                                             vmem_limit_bytes=VMEM_LIMIT_V7X),
        name="kv_up")(ckv, kr, wk, ek, wvt)


def _band_units(units, bias_ref, n_pairs=H_A // 2):
    def scores(u, g):
        q, k_of, _, _ = units[u]
        lanes = slice(g * 2 * DH_A, (g + 1) * 2 * DH_A)
        lane = lax.broadcasted_iota(jnp.int32, (q.shape[0], 2 * DH_A), 1)
        k2 = k_of(lanes)
        return [lax.dot_general(k2, jnp.where((lane >= DH_A) == (hl == 1), q[:, lanes], jnp.zeros((), BF16)), _NT,
                                preferred_element_type=F32) for hl in range(2)]

    steps = [(u, g) for u in range(len(units)) for g in range(n_pairs)]
    outs = [[] for _ in units]
    pend = scores(*steps[0])
    for j, (u, g) in enumerate(steps):
        nxt = scores(*steps[j + 1]) if j + 1 < len(steps) else None
        _, _, v_of, b0 = units[u]
        for hl in range(2):
            h = 2 * g + hl
            nk = pend[hl].shape[0]
            s = pend[hl] + bias_ref[h, b0:b0 + nk, :]
            p = jnp.exp2(s - jnp.max(s, axis=0, keepdims=True)).astype(BF16)
            v = jnp.concatenate([v_of(slice(h * DH_A, (h + 1) * DH_A)), jnp.ones((_L_ROWS, nk), BF16)], axis=0)
            o = jnp.dot(v, p, preferred_element_type=F32)
            outs[u].append(o[:DH_A] / o[DH_A:DH_A + 1])
        pend = nxt
    return [jnp.concatenate(o, axis=0).T for o in outs]


def _band_prompt_kernel(q_ref, kp_ref, kc_ref, vp_ref, vc_ref, bias_ref, o_ref, s_scr, *, tm, tq):
    subs = [(lo, lo + tq) for lo in range(0, tm, tq)]
    lane = lax.broadcasted_iota(jnp.int32, (tq, PAIR_W), 1)

    def tile(first):
        def window(lo, hi):
            return (hi, tm - lo) if first else (tm - lo + hi, 0)

        def scores(g, j):
            lo, hi = subs[j]
            nk, b0 = window(lo, hi)
            q2 = q_ref[0, g, lo:hi, :]
            qcat = jnp.concatenate([jnp.where(lane < DH_A, q2, jnp.zeros((), BF16)),
                                    jnp.where(lane >= DH_A, q2, jnp.zeros((), BF16))], axis=0)
            k2 = kc_ref[0, g, :hi, :]
            if not first:
                k2 = jnp.concatenate([kp_ref[0, g, lo:, :], k2], axis=0)
            bias = jnp.concatenate([bias_ref[2 * g + hl, b0:b0 + nk, :] for hl in range(2)], axis=1)
            return lax.dot_general(k2, qcat, _NT, preferred_element_type=F32) + bias

        def stash(j, s):
            s_scr[j, :s.shape[0], :] = s
            return jnp.max(s, axis=0, keepdims=True)

        def consume(g, j, col_max):
            lo, hi = subs[j]
            nk, _ = window(lo, hi)
            outs = []
            for hl in range(2):
                cols = slice(hl * tq, (hl + 1) * tq)
                p = jnp.exp2(s_scr[j, :nk, cols] - col_max[:, cols]).astype(BF16)
                rows = pl.ds(g * PAIR_W + hl * DH_A, DH_A)
                v = vc_ref[0, 0, rows, :hi]
                if not first:
                    v = jnp.concatenate([vp_ref[0, 0, rows, lo:], v], axis=1)
                v = jnp.concatenate([v, jnp.ones((_L_ROWS, nk), BF16)], axis=0)
                o = jnp.dot(v, p, preferred_element_type=F32)
                outs.append(o[:DH_A] / o[DH_A:DH_A + 1])
            o_ref[0, g, lo:hi, :] = jnp.concatenate(outs, axis=0).T.astype(BF16)

        def body(g, col_max):
            new_max = []
            for j in range(len(subs)):
                nxt = scores(g + 1, j)
                consume(g, j, col_max[j])
                new_max.append(stash(j, nxt))
            return tuple(new_max)

        col_max = tuple(stash(j, scores(0, j)) for j in range(len(subs)))
        for g in range(N_PAIRS - 1):
            col_max = body(g, col_max)
        for j in range(len(subs)):
            consume(N_PAIRS - 1, j, col_max[j])

    pl.when(pl.program_id(1) == 0)(functools.partial(tile, True))
    pl.when(pl.program_id(1) > 0)(functools.partial(tile, False))


def _band_prompt(qa, ka, vat, bias, *, tm, tq):
    g, _, s, _ = qa.shape
    cur = lambda a, b: (a, 0, b, 0)
    prev = lambda a, b: (a, 0, jnp.maximum(b - 1, 0), 0)
    prev_v = lambda a, b: (a, jnp.maximum(b - 1, 0), 0, 0)
    pairs = lambda idx: pl.BlockSpec((1, N_PAIRS, tm, PAIR_W), idx)
    return pl.pallas_call(
        functools.partial(_band_prompt_kernel, tm=tm, tq=tq), grid=(g, s // tm),
        in_specs=[pairs(cur), pairs(prev), pairs(cur),
                  pl.BlockSpec((1, 1, D_A, tm), prev_v), pl.BlockSpec((1, 1, D_A, tm), lambda a, b: (a, b, 0, 0)),
                  _resident(bias.shape)],
        out_specs=pairs(cur),
        out_shape=jax.ShapeDtypeStruct((g, N_PAIRS, s, PAIR_W), BF16),
        scratch_shapes=[pltpu.VMEM((tm // tq, tm + tq, 2 * tq), F32)],
        compiler_params=pltpu.CompilerParams(dimension_semantics=("arbitrary", "arbitrary"),
                                             vmem_limit_bytes=VMEM_LIMIT_V7X),
        name="band_prompt")(qa, ka, ka, vat, vat, bias)


def _band_sample_kernel(q_ref, k_ref, vt_ref, bias_ref, o_ref):
    q = jnp.concatenate([q_ref[g, 0] for g in range(N_PAIRS)], axis=1)
    unit = (q, lambda lanes: k_ref[0, :, lanes], lambda rows: vt_ref[0, rows, :], 0)
    o = _band_units([unit], bias_ref)[0].astype(BF16)
    for g in range(N_PAIRS):
        o_ref[g, 0] = o[:, g * PAIR_W:(g + 1) * PAIR_W]


def _band_sample(qa, k_all, vt_all, bias):
    _, g, t, _ = qa.shape
    nk = k_all.shape[1]
    pairs = pl.BlockSpec((N_PAIRS, 1, t, PAIR_W), lambda a: (0, a, 0, 0))
    return pl.pallas_call(
        _band_sample_kernel, grid=(g,),
        in_specs=[pairs,
                  pl.BlockSpec((1, nk, D_A), lambda a: (a, 0, 0)),
                  pl.BlockSpec((1, D_A, nk), lambda a: (a, 0, 0)),
                  _resident(bias.shape)],
        out_specs=pairs,
        out_shape=jax.ShapeDtypeStruct((N_PAIRS, g, t, PAIR_W), BF16),
        compiler_params=pltpu.CompilerParams(dimension_semantics=("arbitrary",),
                                             vmem_limit_bytes=VMEM_LIMIT_V7X),
        name="band_sample")(qa, k_all, vt_all, bias)


def _mla_kernel(q_ref, k_ref, vt_ref, mask_ref, o_ref, s_scr, *, tq, tk, causal):
    nq = q_ref.shape[1]
    nk = k_ref.shape[1] // tk
    ones = jnp.ones((_L_ROWS, tk), BF16)

    def q_rows(i):
        return [q_ref[0, i, hl * HEAD_PAD:(hl + 1) * HEAD_PAD, :] for hl in range(2)]

    def scores(qs, kk, hl):
        krow = pl.multiple_of(kk * tk, tk)
        return jnp.dot(k_ref[0, pl.ds(krow, tk), hl * HEAD_PAD:(hl + 1) * HEAD_PAD], qs[hl],
                       preferred_element_type=F32)

    def stash(hl, s):
        s_scr[hl] = s
        return jnp.max(s, axis=0, keepdims=True)

    def consume(kk, hl, m, acc, tile_max, mask_j):
        if mask_j is not None:
            s = s_scr[hl] + mask_ref[mask_j]
            m_new = jnp.maximum(m, jnp.max(s, axis=0, keepdims=True))
        else:
            s = s_scr[hl]
            m_new = jnp.maximum(m, tile_max)
        p = jnp.exp2(s - m_new).astype(BF16)
        v = jnp.concatenate([vt_ref[0, kk, hl * DV_B:(hl + 1) * DV_B, :], ones], axis=0)
        return m_new, acc * jnp.exp2(m - m_new) + jnp.dot(v, p, preferred_element_type=F32)

    def q_tile(i, first_max):
        qs = q_rows(i)

        def body(t, carry):
            out = []
            for hl in range(2):
                m, acc, tile_max = carry[3 * hl:3 * hl + 3]
                nxt = scores(qs, t + 1, hl)
                out += list(consume(t, hl, m, acc, tile_max, None)) + [stash(hl, nxt)]
            return tuple(out)

        r = tq // tk
        n_full = r * i if causal else nk - 1
        carry = ()
        for hl in range(2):
            carry += (jnp.full((1, tq), NEG, F32), jnp.zeros((DV_B + _L_ROWS, tq), F32), first_max[hl])
        done = 0
        for n in (4, 2, 1):
            left = n_full - done
            trips = left // n if isinstance(left, int) else lax.shift_right_logical(left, n.bit_length() - 1)

            def trip(u, c, n=n, done=done):
                for j in range(n):
                    c = body(done + n * u + j, c)
                return c

            carry = lax.fori_loop(0, trips, trip, carry)
            done = done + trips * n
        tail = [(r * i + j, j) for j in range(r)] if causal else [(nk - 1, None)]
        qs_next = q_rows(jnp.minimum(i + 1, nq - 1)) if nq > 1 else None
        state = [list(carry[3 * hl:3 * hl + 3]) for hl in range(2)]
        for idx, (tile, mask_j) in enumerate(tail):
            for hl in range(2):
                if idx + 1 < len(tail):
                    nxt = scores(qs, tile + 1, hl)
                else:
                    nxt = scores(qs_next, 0, hl) if nq > 1 else None
                m, acc = consume(tile, hl, *state[hl], mask_j)
                state[hl] = [m, acc, stash(hl, nxt) if nxt is not None else state[hl][2]]
        accs = [state[hl][1] for hl in range(2)]
        next_max = [state[hl][2] for hl in range(2)]
        o = jnp.concatenate([acc[:DV_B] / acc[DV_B:DV_B + 1] for acc in accs], axis=0)
        o_ref[0, pl.ds(pl.multiple_of(i * tq, tq), tq), :] = o.T.astype(BF16)
        return tuple(next_max)

    first = q_rows(0)
    lax.fori_loop(0, nq, q_tile, tuple(stash(hl, scores(first, 0, hl)) for hl in range(2)))


def _mla_attn(qf, kf, vt, mask, *, tq, tk, causal):
    g, nq, _, _ = qf.shape
    sq, sk = nq * tq, kf.shape[1]
    return pl.pallas_call(
        functools.partial(_mla_kernel, tq=tq, tk=tk, causal=causal), grid=(g, H_B // 2),
        in_specs=[pl.BlockSpec((1, nq, 2 * HEAD_PAD, tq), lambda a, b: (a, 0, b, 0)),
                  pl.BlockSpec((1, sk, 2 * HEAD_PAD), lambda a, b: (a, 0, b)),
                  pl.BlockSpec((1, sk // tk, 2 * DV_B, tk), lambda a, b: (a, 0, b, 0)),
                  _resident(mask.shape)],
        out_specs=pl.BlockSpec((1, sq, 2 * DV_B), lambda a, b: (a, 0, b)),
        out_shape=jax.ShapeDtypeStruct((g, sq, D_VB), BF16),
        scratch_shapes=[pltpu.VMEM((2, tk, tq), F32)],
        compiler_params=pltpu.CompilerParams(dimension_semantics=("arbitrary", "arbitrary"),
                                             vmem_limit_bytes=VMEM_LIMIT_V7X),
        name="mla_attn")(qf, kf, vt, mask)


_G0 = 8


def _merge_ffn_kernel(x_ref, ya_ref, yb_ref, ga_ref, gb_ref, st_ref, wa_ref, wb_ref, wo_ref, n1_ref, n2_ref, n3_ref,
                      wg_ref, wu_ref, cw_ref, cb_ref, wd_ref, y_ref, cs_ref, gbuf, *, tm, nseq):
    i = pl.program_id(1)
    seg = tm // nseq
    ya = jnp.concatenate([ya_ref[0, g] for g in range(N_PAIRS)], axis=1)
    za = jnp.dot(ya, wa_ref[...], preferred_element_type=F32)
    zb = jnp.dot(yb_ref[0], wb_ref[...], preferred_element_type=F32)
    mix = jax.nn.sigmoid(ga_ref[0].astype(F32)) * za + jax.nn.sigmoid(gb_ref[0].astype(F32)) * zb
    mo = jnp.dot(mix.astype(BF16), wo_ref[...], preferred_element_type=F32)
    x1 = x_ref[0] + _rms(mo, n1_ref[...])
    xn = _rms(x1, n2_ref[...]).astype(BF16)

    starts = [_G0 + j * (seg + _G0) for j in range(nseq)]
    if nseq == 1:
        @pl.when(i == 0)
        def _():
            gbuf[_G0 - 2:_G0, :] = st_ref[0]

        @pl.when(i > 0)
        def _():
            gbuf[_G0 - 2:_G0, :] = gbuf[_G0 + tm - 2:_G0 + tm, :]
    else:
        for j, r0 in enumerate(starts):
            gbuf[r0 - 2:r0, :] = st_ref[j]

    gate = jnp.dot(xn, wg_ref[...], preferred_element_type=F32)
    for j, r0 in enumerate(starts):
        gbuf[r0:r0 + seg, :] = gate[j * seg:(j + 1) * seg]
        cs_ref[j] = gbuf[r0 + seg - 2:r0 + seg, :]
    u = jnp.dot(xn, wu_ref[...], preferred_element_type=F32)
    c = jnp.concatenate([cw_ref[0:1, :] * gbuf[r0 - 2:r0 - 2 + seg, :] + cw_ref[1:2, :] * gbuf[r0 - 1:r0 - 1 + seg, :]
                         + cw_ref[2:3, :] * gbuf[r0:r0 + seg, :] for r0 in starts], axis=0) + cb_ref[...]
    hid = (jax.nn.gelu(c, approximate=True) * u).astype(BF16)
    f = jnp.dot(hid, wd_ref[...], preferred_element_type=F32)
    y_ref[0] = x1 + _rms(f, n3_ref[...])


def _merge_ffn(x, ya, yb, ga, gb, state, wa, wb, wo, n1, n2, n3, wg, wu, cw, cb, wd, *, tm, nseq=1):
    g, s, _ = x.shape
    assert nseq == 1 or s == tm
    tok = lambda w: pl.BlockSpec((1, tm, w), lambda a, b: (a, b, 0))
    per_group = pl.BlockSpec((nseq, CONV_W - 1, D_FF), lambda a, b: (a, 0, 0))
    weights = (wa, wb, wo, n1, n2, n3, wg, wu, cw, cb, wd)
    return pl.pallas_call(
        functools.partial(_merge_ffn_kernel, tm=tm, nseq=nseq), grid=(g, s // tm),
        in_specs=[tok(D_MODEL), pl.BlockSpec((1, N_PAIRS, tm, PAIR_W), lambda a, b: (a, 0, b, 0)), tok(D_VB),
                  tok(D_MODEL), tok(D_MODEL), per_group]
                 + [_resident(w.shape) for w in weights],
        out_specs=(tok(D_MODEL), per_group),
        out_shape=(jax.ShapeDtypeStruct((g, s, D_MODEL), F32),
                   jax.ShapeDtypeStruct((g * nseq, CONV_W - 1, D_FF), F32)),
        scratch_shapes=[pltpu.VMEM((tm + nseq * _G0, D_FF), F32)],
        compiler_params=pltpu.CompilerParams(dimension_semantics=("arbitrary", "arbitrary"),
                                             vmem_limit_bytes=VMEM_LIMIT_V7X),
        name="merge_ffn")(x, ya, yb, ga, gb, state, *weights)


def _rope_tables(pos, scale):
    inv = ROPE_BASE ** (-np.arange(_HALF, dtype=np.float64) / _HALF)
    ang = inv[:, None] * np.asarray(pos, np.float64)[None, :]
    return jnp.asarray(np.cos(ang) * scale, F32), jnp.asarray(np.sin(ang) * scale, F32)


def _band_bias_kernel(ext_ref, o_ref, *, tq, nk):
    w = ext_ref.shape[-1]
    toeplitz = pltpu.roll(jnp.broadcast_to(ext_ref[0], (nk, w)), 0, 1, stride=1, stride_axis=0)[:, :tq]
    qc = (nk - tq + lax.broadcasted_iota(jnp.int32, (nk, tq), 1)) // CHUNK
    kc = lax.broadcasted_iota(jnp.int32, (nk, tq), 0) // CHUNK
    o_ref[0] = jnp.where((kc >= qc - LEFT_CHUNKS) & (kc <= qc), toeplitz, NEG)


def _band_bias(table, tq, nk):
    w = nk + tq
    d_lo, d_hi = 1 - tq, w - tq
    assert -MAX_REL <= d_lo and d_hi >= MAX_REL and w % 128 == 0
    h = table.shape[0]
    ext = jnp.concatenate([table[:, d_lo + MAX_REL:], jnp.broadcast_to(table[:, -1:], (h, d_hi - MAX_REL))], axis=1)
    ext = (jnp.roll(ext, -(nk - 1), axis=1) * LOG2E).reshape(h, 1, w)
    return pl.pallas_call(
        functools.partial(_band_bias_kernel, tq=tq, nk=nk), grid=(h,),
        in_specs=[pl.BlockSpec((1, 1, w), lambda a: (a, 0, 0))],
        out_specs=pl.BlockSpec((1, nk, tq), lambda a: (a, 0, 0)),
        out_shape=jax.ShapeDtypeStruct((h, nk, tq), F32),
        compiler_params=pltpu.CompilerParams(dimension_semantics=("arbitrary",)),
        name="band_bias")(ext)


def _chunk_masks(tk, tq):
    kc = jnp.arange(tq)[:, None] // CHUNK
    qc = jnp.arange(tq)[None, :] // CHUNK
    return jnp.where(kc <= qc, 0.0, NEG).astype(F32).reshape(tq // tk, tk, tq)


def _prep_weights(w_in, w_uk, w_uv):
    o = np.cumsum((0,) + IN_SIZES)
    col = lambda j: w_in[:, o[j]:o[j + 1]]
    d = w_in.shape[0]
    qn = col(3).reshape(d, H_B, DH_NOPE)
    qr = col(4).reshape(d, H_B, DH_ROPE)
    pad = jnp.zeros((d, H_B, HEAD_PAD - DH_NOPE - DH_ROPE), F32).at[:, 0, :].set(col(6))
    w_qf = jnp.concatenate([qn, qr, pad], axis=2).reshape(d, D_QF)
    wm = jnp.concatenate([col(0) * (A_SCALE * LOG2E), col(1), col(5), col(7), col(8)], axis=1).astype(BF16)
    wva = col(2).astype(BF16)
    wk = jnp.concatenate([w_uk, jnp.zeros((D_C, H_B, HEAD_PAD - DH_NOPE), F32)], axis=2).reshape(D_C, D_QF).astype(BF16)
    place = jnp.concatenate([jnp.zeros((DH_ROPE, DH_NOPE), F32), jnp.eye(DH_ROPE, dtype=F32),
                             jnp.zeros((DH_ROPE, HEAD_PAD - DH_NOPE - DH_ROPE), F32)], axis=1)
    ek = jnp.tile(place, (1, H_B)).astype(BF16)
    wvt = w_uv.reshape(D_C, D_VB).T.astype(BF16)
    return wm, w_qf.T.astype(BF16), wva.T, wva, wk, ek, wvt


def kernel(x_prompt, x_sample, cache_a_k, cache_a_v, cache_mla_ckv, cache_mla_krope, state_ffn_conv, norm_mix_pre,
           norm_mix_post, w_in, rel_bias_table, kv_norm, w_uk, w_uv, w_branch_a, w_branch_b, w_out, norm_ffn_pre,
           norm_ffn_post, w_ffn_gate, w_ffn_up, conv_w, conv_b, w_ffn_down):
    assert w_in.shape[0] == 1, "single layer"
    b, s, _ = x_prompt.shape
    db, t, _ = x_sample.shape
    past = cache_mla_ckv.shape[2]
    wcache = cache_a_k.shape[2]
    keep = min(A_WINDOW, s)
    tm1 = 512
    assert keep == tm1 and db * t == tm1 and wcache == A_WINDOW

    wm, wqt, wvat, wva, wk, ek, wvt = _prep_weights(w_in[0], w_uk[0], w_uv[0])
    row = lambda v: v.reshape(1, -1)
    proj_w = (row(norm_mix_pre[0]), wm, wqt, wvat, wva, row(kv_norm[0]))
    ffn_w = (w_branch_a[0].astype(BF16), w_branch_b[0].astype(BF16), w_out[0].astype(BF16), row(norm_mix_post[0]),
             row(norm_ffn_pre[0]), row(norm_ffn_post[0]), w_ffn_gate[0].astype(BF16), w_ffn_up[0].astype(BF16),
             conv_w[0], row(conv_b[0]), w_ffn_down[0].astype(BF16))
    table = rel_bias_table[0]

    pos = np.arange(s)
    tabs = _rope_tables(pos, MLA_SCALE * LOG2E) + _rope_tables(pos, 1.0)
    qa, ka, vat, kaf, vaf, qf, ckv, kr, ga, gb = _in_proj(x_prompt, *proj_w, *tabs, tm=tm1)
    t_mla = 512
    kf, vt = _kv_up(ckv, kr, wk, ek, wvt, tm=1024, tk=t_mla)
    ya = _band_prompt(qa, ka, vat, _band_bias(table, 256, A_WINDOW + 256), tm=tm1, tq=256)
    yb = _mla_attn(qf, kf, vt, _chunk_masks(t_mla, t_mla), tq=t_mla, tk=t_mla, causal=True)
    y_prompt, conv_p = _merge_ffn(x_prompt, ya, yb, ga, gb, jnp.zeros((b, CONV_W - 1, D_FF), F32), *ffn_w, tm=512)

    pos_s = np.tile(past + np.arange(t), db)
    tabs_s = _rope_tables(pos_s, MLA_SCALE * LOG2E) + _rope_tables(pos_s, 1.0)
    qa2, ka2, vat2, kaf2, vaf2, qf2, ckv2, kr2, ga2, gb2 = _in_proj(x_sample.reshape(1, db * t, D_MODEL), *proj_w,
                                                                    *tabs_s, tm=tm1)
    per_seq = lambda v: v.reshape(db, t, v.shape[-1])
    ckv2, kr2 = per_seq(ckv2[0]), per_seq(kr2[0])
    new_k = jnp.concatenate([cache_a_k[0].reshape(db, wcache, D_A), per_seq(kaf2[0])], axis=1)
    new_v = jnp.concatenate([cache_a_v[0].reshape(db, wcache, D_A), per_seq(vaf2[0])], axis=1)
    qa2 = qa2[0].reshape(N_PAIRS, db, t, PAIR_W)
    ya2 = _band_sample(qa2, new_k.astype(BF16), jnp.swapaxes(new_v, 1, 2).astype(BF16),
                       _band_bias(table, t, wcache + t))
    c_all = jnp.concatenate([cache_mla_ckv[0], ckv2], axis=1)
    kr_all = jnp.concatenate([cache_mla_krope[0], kr2], axis=1)
    kf2, vt2 = _kv_up(c_all, kr_all, wk, ek, wvt, tm=past + t, tk=past + t)
    qf2 = jnp.transpose(qf2[0, 0].reshape(D_QF, db, t), (1, 0, 2))[:, None]
    yb2 = _mla_attn(qf2, kf2, vt2, jnp.zeros((1, 8, 128), F32), tq=t, tk=past + t, causal=False)
    y_sample, conv_s = _merge_ffn(x_sample.reshape(1, db * t, D_MODEL), ya2.reshape(1, N_PAIRS, db * t, PAIR_W),
                                  yb2.reshape(1, db * t, D_VB), ga2, gb2, state_ffn_conv[0], *ffn_w,
                                  tm=db * t, nseq=db)
    y_sample = y_sample.reshape(db, t, D_MODEL)

    heads = lambda v: v.reshape(1, v.shape[0], v.shape[1], H_A, DH_A)
    return (y_prompt, y_sample,
            heads(kaf), heads(vaf), ckv[None], kr[None], conv_p[None],
            heads(new_k[:, -wcache:]), heads(new_v[:, -wcache:]), ckv2[None], kr2[None], conv_s[None])
```

```python
import functools

import jax
import jax.numpy as jnp
import numpy as np
from jax import lax
from jax.experimental import pallas as pl
from jax.experimental.pallas import tpu as pltpu

D_MODEL = 1024
CHUNK = 64
LEFT_CHUNKS = 8
A_WINDOW = LEFT_CHUNKS * CHUNK
H_A = 8
DH_A = 64
MAX_REL = 256
H_B = 8
DH_NOPE = 64
DH_ROPE = 32
DV_B = 64
D_C = 256
D_FF = 2816
CONV_W = 3
ROPE_BASE = 10000.0
EPS = 1e-6
A_SCALE = DH_A ** -0.5
MLA_SCALE = (DH_NOPE + DH_ROPE) ** -0.5
IN_SIZES = (H_A * DH_A, H_A * DH_A, H_A * DH_A, H_B * DH_NOPE, H_B * DH_ROPE, D_C, DH_ROPE, D_MODEL, D_MODEL)

HEAD_PAD = 128
D_A = H_A * DH_A
D_VB = H_B * DV_B
D_QF = H_B * HEAD_PAD
PAIR_W = 2 * DH_A
N_PAIRS = H_A // 2
NEG = -1e30
LOG2E = 1.4426950408889634
_L_ROWS = 16
VMEM_LIMIT_V7X = 56 * 1024 * 1024

F32 = jnp.float32
BF16 = jnp.bfloat16
_NT = (((1,), (1,)), ((), ()))


def _resident(shape):
    nd = len(shape)
    return pl.BlockSpec(shape, lambda *_: (0,) * nd, pipeline_mode=pl.Buffered(1))


def _rms(x, w):
    return x * lax.rsqrt(jnp.mean(x * x, axis=-1, keepdims=True) + EPS) * w


_C_KA, _C_CKV, _C_GA, _C_GB, _C_END = 0, 512, 768, 1792, 2816
_KR_ROW = DH_NOPE + DH_ROPE
_HALF = DH_ROPE // 2


def _store_pairs(ref, v):
    for g in range(N_PAIRS):
        ref[0, g] = v[:, g * PAIR_W:(g + 1) * PAIR_W]


def _rope_rows(x, c, s):
    x1, x2 = x[:_HALF], x[_HALF:]
    return [x1 * c - x2 * s, x1 * s + x2 * c]


def _in_proj_kernel(x_ref, nw_ref, wm_ref, wqt_ref, wvt_ref, wva_ref, kvn_ref, cq_ref, sq_ref, ck_ref, sk_ref,
                    qa_ref, ka_ref, vat_ref, kaf_ref, vaf_ref, qf_ref, ckv_ref, kr_ref, ga_ref, gb_ref, *, n):
    i = pl.program_id(1)
    xn = _rms(x_ref[0], nw_ref[...]).astype(BF16)

    def proj(lo, hi):
        return jnp.dot(xn, wm_ref[:, lo:hi], preferred_element_type=F32)

    ka = proj(_C_KA, _C_CKV)
    _store_pairs(ka_ref, ka.astype(BF16))
    kaf_ref[0] = ka
    vat_ref[0, 0] = lax.dot_general(wvt_ref[...], xn, _NT, preferred_element_type=F32).astype(BF16)

    def va_tail():
        vaf_ref[0] = jnp.dot(xn, wva_ref[...], preferred_element_type=F32)

    if n == 1:
        va_tail()
    else:
        pl.when(i == n - 1)(va_tail)

    qt = lax.dot_general(wqt_ref[...], xn, _NT, preferred_element_type=F32)
    for g in range(N_PAIRS):
        qa_ref[0, g] = qt[D_QF + g * PAIR_W:D_QF + (g + 1) * PAIR_W].astype(BF16)
    cq, sq = cq_ref[...], sq_ref[...]
    pad = jnp.zeros((HEAD_PAD - _KR_ROW, qt.shape[1]), F32)
    for h in range(H_B):
        slab = qt[h * HEAD_PAD:(h + 1) * HEAD_PAD]
        rows = [slab[:DH_NOPE] * (MLA_SCALE * LOG2E)] + _rope_rows(slab[DH_NOPE:_KR_ROW], cq, sq) + [pad]
        qf_ref[0, 0, h * HEAD_PAD:(h + 1) * HEAD_PAD, :] = jnp.concatenate(rows, axis=0).astype(BF16)
    kr_t = _rope_rows(qt[_KR_ROW:HEAD_PAD], ck_ref[...], sk_ref[...]) + [jnp.zeros((HEAD_PAD - DH_ROPE, qt.shape[1]), F32)]
    kr_ref[0] = jnp.concatenate(kr_t, axis=0).T[:, :DH_ROPE]

    ckv_ref[0] = _rms(proj(_C_CKV, _C_GA), kvn_ref[...])
    ga_ref[0] = proj(_C_GA, _C_GB).astype(BF16)
    gb_ref[0] = proj(_C_GB, _C_END).astype(BF16)


def _in_proj(x, nw, wm, wqt, wvt, wva, kvn, cq, sq, ck, sk, *, tm):
    g, s, _ = x.shape
    n = s // tm
    tok = lambda w: pl.BlockSpec((1, tm, w), lambda a, b: (a, b, 0))
    tab = pl.BlockSpec((_HALF, tm), lambda a, b: (0, b))
    tail = pl.BlockSpec((1, tm, D_A), lambda a, b: (a, 0, 0))
    out_shape = (
        jax.ShapeDtypeStruct((g, N_PAIRS, PAIR_W, s), BF16),
        jax.ShapeDtypeStruct((g, N_PAIRS, s, PAIR_W), BF16),
        jax.ShapeDtypeStruct((g, n, D_A, tm), BF16),
        jax.ShapeDtypeStruct((g, tm, D_A), F32),
        jax.ShapeDtypeStruct((g, tm, D_A), F32),
        jax.ShapeDtypeStruct((g, n, D_QF, tm), BF16),
        jax.ShapeDtypeStruct((g, s, D_C), F32),
        jax.ShapeDtypeStruct((g, s, DH_ROPE), F32),
        jax.ShapeDtypeStruct((g, s, D_MODEL), BF16),
        jax.ShapeDtypeStruct((g, s, D_MODEL), BF16),
    )
    pairs = pl.BlockSpec((1, N_PAIRS, tm, PAIR_W), lambda a, b: (a, 0, b, 0))
    pairs_t = pl.BlockSpec((1, N_PAIRS, PAIR_W, tm), lambda a, b: (a, 0, 0, b))
    out_specs = (pairs_t, pairs, pl.BlockSpec((1, 1, D_A, tm), lambda a, b: (a, b, 0, 0)), tail, tail,
                 pl.BlockSpec((1, 1, D_QF, tm), lambda a, b: (a, b, 0, 0)), tok(D_C), tok(DH_ROPE), tok(D_MODEL),
                 tok(D_MODEL))
    in_specs = [tok(D_MODEL), _resident(nw.shape), _resident(wm.shape), _resident(wqt.shape), _resident(wvt.shape),
                _resident(wva.shape), _resident(kvn.shape), tab, tab, tab, tab]
    return pl.pallas_call(
        functools.partial(_in_proj_kernel, n=n), grid=(g, n), in_specs=in_specs, out_specs=out_specs, out_shape=out_shape,
        compiler_params=pltpu.CompilerParams(dimension_semantics=("arbitrary", "arbitrary"),
                                             vmem_limit_bytes=VMEM_LIMIT_V7X),
        name="in_proj")(x, nw, wm, wqt, wvt, wva, kvn, cq, sq, ck, sk)


def _kv_up_kernel(ckv_ref, kr_ref, wk_ref, ek_ref, wvt_ref, kf_ref, vt_ref, *, tk):
    c = ckv_ref[0].astype(BF16)
    kf = jnp.dot(c, wk_ref[...], preferred_element_type=F32)
    kf = kf + jnp.dot(kr_ref[0].astype(BF16), ek_ref[...], preferred_element_type=F32)
    kf_ref[0] = kf.astype(BF16)
    for j in range(c.shape[0] // tk):
        vt_ref[0, j] = lax.dot_general(wvt_ref[...], c[j * tk:(j + 1) * tk], _NT,
                                       preferred_element_type=F32).astype(BF16)


def _kv_up(ckv, kr, wk, ek, wvt, *, tm, tk):
    g, s, _ = ckv.shape
    return pl.pallas_call(
        functools.partial(_kv_up_kernel, tk=tk), grid=(g, s // tm),
        in_specs=[pl.BlockSpec((1, tm, D_C), lambda a, b: (a, b, 0)),
                  pl.BlockSpec((1, tm, DH_ROPE), lambda a, b: (a, b, 0)),
                  _resident(wk.shape), _resident(ek.shape), _resident(wvt.shape)],
        out_specs=(pl.BlockSpec((1, tm, D_QF), lambda a, b: (a, b, 0)),
                   pl.BlockSpec((1, tm // tk, D_VB, tk), lambda a, b: (a, b, 0, 0))),
        out_shape=(jax.ShapeDtypeStruct((g, s, D_QF), BF16),
                   jax.ShapeDtypeStruct((g, s // tk, D_VB, tk), BF16)),
        compiler_params=pltpu.CompilerParams(dimension_semantics=("arbitrary", "arbitrary"),
                                             vmem_limit_bytes=VMEM_LIMIT_V7X),
        name="kv_up")(ckv, kr, wk, ek, wvt)


def _band_units(units, bias_ref, n_pairs=H_A // 2):
    def scores(u, g):
        q, k_of, _, _ = units[u]
        lanes = slice(g * 2 * DH_A, (g + 1) * 2 * DH_A)
        lane = lax.broadcasted_iota(jnp.int32, (q.shape[0], 2 * DH_A), 1)
        k2 = k_of(lanes)
        return [lax.dot_general(k2, jnp.where((lane >= DH_A) == (hl == 1), q[:, lanes], jnp.zeros((), BF16)), _NT,
                                preferred_element_type=F32) for hl in range(2)]

    steps = [(u, g) for u in range(len(units)) for g in range(n_pairs)]
    outs = [[] for _ in units]
    pend = scores(*steps[0])
    for j, (u, g) in enumerate(steps):
        nxt = scores(*steps[j + 1]) if j + 1 < len(steps) else None
        _, _, v_of, b0 = units[u]
        for hl in range(2):
            h = 2 * g + hl
            nk = pend[hl].shape[0]
            s = pend[hl] + bias_ref[h, b0:b0 + nk, :]
            p = jnp.exp2(s - jnp.max(s, axis=0, keepdims=True)).astype(BF16)
            v = jnp.concatenate([v_of(slice(h * DH_A, (h + 1) * DH_A)), jnp.ones((_L_ROWS, nk), BF16)], axis=0)
            o = jnp.dot(v, p, preferred_element_type=F32)
            outs[u].append(o[:DH_A] / o[DH_A:DH_A + 1])
        pend = nxt
    return [jnp.concatenate(o, axis=0).T for o in outs]


def _band_prompt_kernel(q_ref, kp_ref, kc_ref, vp_ref, vc_ref, bias_ref, o_ref, s_scr, *, tm, tq):
    subs = [(lo, lo + tq) for lo in range(0, tm, tq)]
    row = lax.broadcasted_iota(jnp.int32, (PAIR_W, tq), 0)

    def tile(first):
        def window(lo, hi):
            return (hi, tm - lo) if first else (tm - lo + hi, 0)

        def scores(g, j):
            lo, hi = subs[j]
            nk, b0 = window(lo, hi)
            q2 = q_ref[0, g, :, lo:hi]
            qcat = jnp.concatenate([jnp.where(row < DH_A, q2, jnp.zeros((), BF16)),
                                    jnp.where(row >= DH_A, q2, jnp.zeros((), BF16))], axis=1)
            k2 = kc_ref[0, g, :hi, :]
            if not first:
                k2 = jnp.concatenate([kp_ref[0, g, lo:, :], k2], axis=0)
            bias = jnp.concatenate([bias_ref[2 * g + hl, b0:b0 + nk, :] for hl in range(2)], axis=1)
            return jnp.dot(k2, qcat, preferred_element_type=F32) + bias

        def stash(j, s):
            s_scr[j, :s.shape[0], :] = s
            return jnp.max(s, axis=0, keepdims=True)

        def consume(g, j, col_max):
            lo, hi = subs[j]
            nk, _ = window(lo, hi)
            outs = []
            for hl in range(2):
                cols = slice(hl * tq, (hl + 1) * tq)
                p = jnp.exp2(s_scr[j, :nk, cols] - col_max[:, cols]).astype(BF16)
                rows = pl.ds(g * PAIR_W + hl * DH_A, DH_A)
                v = vc_ref[0, 0, rows, :hi]
                if not first:
                    v = jnp.concatenate([vp_ref[0, 0, rows, lo:], v], axis=1)
                v = jnp.concatenate([v, jnp.ones((_L_ROWS, nk), BF16)], axis=0)
                o = jnp.dot(v, p, preferred_element_type=F32)
                outs.append(o[:DH_A] / o[DH_A:DH_A + 1])
            o_ref[0, g, lo:hi, :] = jnp.concatenate(outs, axis=0).T.astype(BF16)

        def body(g, col_max):
            new_max = []
            for j in range(len(subs)):
                nxt = scores(g + 1, j)
                consume(g, j, col_max[j])
                new_max.append(stash(j, nxt))
            return tuple(new_max)

        col_max = tuple(stash(j, scores(0, j)) for j in range(len(subs)))
        for g in range(N_PAIRS - 1):
            col_max = body(g, col_max)
        for j in range(len(subs)):
            consume(N_PAIRS - 1, j, col_max[j])

    pl.when(pl.program_id(1) == 0)(functools.partial(tile, True))
    pl.when(pl.program_id(1) > 0)(functools.partial(tile, False))


def _band_prompt(qa, ka, vat, bias, *, tm, tq):
    g, _, _, s = qa.shape
    cur = lambda a, b: (a, 0, b, 0)
    prev = lambda a, b: (a, 0, jnp.maximum(b - 1, 0), 0)
    prev_v = lambda a, b: (a, jnp.maximum(b - 1, 0), 0, 0)
    pairs = lambda idx: pl.BlockSpec((1, N_PAIRS, tm, PAIR_W), idx)
    return pl.pallas_call(
        functools.partial(_band_prompt_kernel, tm=tm, tq=tq), grid=(g, s // tm),
        in_specs=[pl.BlockSpec((1, N_PAIRS, PAIR_W, tm), lambda a, b: (a, 0, 0, b)), pairs(prev), pairs(cur),
                  pl.BlockSpec((1, 1, D_A, tm), prev_v), pl.BlockSpec((1, 1, D_A, tm), lambda a, b: (a, b, 0, 0)),
                  _resident(bias.shape)],
        out_specs=pairs(cur),
        out_shape=jax.ShapeDtypeStruct((g, N_PAIRS, s, PAIR_W), BF16),
        scratch_shapes=[pltpu.VMEM((tm // tq, tm + tq, 2 * tq), F32)],
        compiler_params=pltpu.CompilerParams(dimension_semantics=("arbitrary", "arbitrary"),
                                             vmem_limit_bytes=VMEM_LIMIT_V7X),
        name="band_prompt")(qa, ka, ka, vat, vat, bias)


def _band_sample_kernel(q_ref, k_ref, vt_ref, bias_ref, o_ref):
    q = jnp.concatenate([q_ref[g, 0] for g in range(N_PAIRS)], axis=1)
    unit = (q, lambda lanes: k_ref[0, :, lanes], lambda rows: vt_ref[0, rows, :], 0)
    o = _band_units([unit], bias_ref)[0].astype(BF16)
    for g in range(N_PAIRS):
        o_ref[g, 0] = o[:, g * PAIR_W:(g + 1) * PAIR_W]


def _band_sample(qa, k_all, vt_all, bias):
    _, g, t, _ = qa.shape
    nk = k_all.shape[1]
    pairs = pl.BlockSpec((N_PAIRS, 1, t, PAIR_W), lambda a: (0, a, 0, 0))
    return pl.pallas_call(
        _band_sample_kernel, grid=(g,),
        in_specs=[pairs,
                  pl.BlockSpec((1, nk, D_A), lambda a: (a, 0, 0)),
                  pl.BlockSpec((1, D_A, nk), lambda a: (a, 0, 0)),
                  _resident(bias.shape)],
        out_specs=pairs,
        out_shape=jax.ShapeDtypeStruct((N_PAIRS, g, t, PAIR_W), BF16),
        compiler_params=pltpu.CompilerParams(dimension_semantics=("arbitrary",),
                                             vmem_limit_bytes=VMEM_LIMIT_V7X),
        name="band_sample")(qa, k_all, vt_all, bias)


def _mla_kernel(q_ref, k_ref, vt_ref, mask_ref, o_ref, s_scr, *, tq, tk, causal):
    nq = q_ref.shape[1]
    nk = k_ref.shape[1] // tk
    ones = jnp.ones((_L_ROWS, tk), BF16)

    def q_rows(i):
        return [q_ref[0, i, hl * HEAD_PAD:(hl + 1) * HEAD_PAD, :] for hl in range(2)]

    def scores(qs, kk, hl):
        krow = pl.multiple_of(kk * tk, tk)
        return jnp.dot(k_ref[0, pl.ds(krow, tk), hl * HEAD_PAD:(hl + 1) * HEAD_PAD], qs[hl],
                       preferred_element_type=F32)

    def stash(hl, s):
        s_scr[hl] = s
        return jnp.max(s, axis=0, keepdims=True)

    def consume(kk, hl, m, acc, tile_max, mask_j):
        if mask_j is not None:
            s = s_scr[hl] + mask_ref[mask_j]
            m_new = jnp.maximum(m, jnp.max(s, axis=0, keepdims=True))
        else:
            s = s_scr[hl]
            m_new = jnp.maximum(m, tile_max)
        p = jnp.exp2(s - m_new).astype(BF16)
        v = jnp.concatenate([vt_ref[0, kk, hl * DV_B:(hl + 1) * DV_B, :], ones], axis=0)
        return m_new, acc * jnp.exp2(m - m_new) + jnp.dot(v, p, preferred_element_type=F32)

    def q_tile(i, first_max):
        qs = q_rows(i)

        def body(t, carry):
            out = []
            for hl in range(2):
                m, acc, tile_max = carry[3 * hl:3 * hl + 3]
                nxt = scores(qs, t + 1, hl)
                out += list(consume(t, hl, m, acc, tile_max, None)) + [stash(hl, nxt)]
            return tuple(out)

        r = tq // tk
        n_full = r * i if causal else nk - 1
        carry = ()
        for hl in range(2):
            carry += (jnp.full((1, tq), NEG, F32), jnp.zeros((DV_B + _L_ROWS, tq), F32), first_max[hl])
        done = 0
        for n in (4, 2, 1):
            left = n_full - done
            trips = left // n if isinstance(left, int) else lax.shift_right_logical(left, n.bit_length() - 1)

            def trip(u, c, n=n, done=done):
                for j in range(n):
                    c = body(done + n * u + j, c)
                return c

            carry = lax.fori_loop(0, trips, trip, carry)
            done = done + trips * n
        tail = [(r * i + j, j) for j in range(r)] if causal else [(nk - 1, None)]
        qs_next = q_rows(jnp.minimum(i + 1, nq - 1)) if nq > 1 else None
        state = [list(carry[3 * hl:3 * hl + 3]) for hl in range(2)]
        for idx, (tile, mask_j) in enumerate(tail):
            for hl in range(2):
                if idx + 1 < len(tail):
                    nxt = scores(qs, tile + 1, hl)
                else:
                    nxt = scores(qs_next, 0, hl) if nq > 1 else None
                m, acc = consume(tile, hl, *state[hl], mask_j)
                state[hl] = [m, acc, stash(hl, nxt) if nxt is not None else state[hl][2]]
        accs = [state[hl][1] for hl in range(2)]
        next_max = [state[hl][2] for hl in range(2)]
        o = jnp.concatenate([acc[:DV_B] / acc[DV_B:DV_B + 1] for acc in accs], axis=0)
        o_ref[0, pl.ds(pl.multiple_of(i * tq, tq), tq), :] = o.T.astype(BF16)
        return tuple(next_max)

    first = q_rows(0)
    lax.fori_loop(0, nq, q_tile, tuple(stash(hl, scores(first, 0, hl)) for hl in range(2)))


def _mla_attn(qf, kf, vt, mask, *, tq, tk, causal):
    g, nq, _, _ = qf.shape
    sq, sk = nq * tq, kf.shape[1]
    return pl.pallas_call(
        functools.partial(_mla_kernel, tq=tq, tk=tk, causal=causal), grid=(g, H_B // 2),
        in_specs=[pl.BlockSpec((1, nq, 2 * HEAD_PAD, tq), lambda a, b: (a, 0, b, 0)),
                  pl.BlockSpec((1, sk, 2 * HEAD_PAD), lambda a, b: (a, 0, b)),
                  pl.BlockSpec((1, sk // tk, 2 * DV_B, tk), lambda a, b: (a, 0, b, 0)),
                  _resident(mask.shape)],
        out_specs=pl.BlockSpec((1, sq, 2 * DV_B), lambda a, b: (a, 0, b)),
        out_shape=jax.ShapeDtypeStruct((g, sq, D_VB), BF16),
        scratch_shapes=[pltpu.VMEM((2, tk, tq), F32)],
        compiler_params=pltpu.CompilerParams(dimension_semantics=("arbitrary", "arbitrary"),
                                             vmem_limit_bytes=VMEM_LIMIT_V7X),
        name="mla_attn")(qf, kf, vt, mask)


_G0 = 8


def _merge_ffn_kernel(x_ref, ya_ref, yb_ref, ga_ref, gb_ref, st_ref, wa_ref, wb_ref, wo_ref, n1_ref, n2_ref, n3_ref,
                      wg_ref, wu_ref, cw_ref, cb_ref, wd_ref, y_ref, cs_ref, gbuf, *, tm, nseq):
    i = pl.program_id(1)
    seg = tm // nseq
    ya = jnp.concatenate([ya_ref[0, g] for g in range(N_PAIRS)], axis=1)
    za = jnp.dot(ya, wa_ref[...], preferred_element_type=F32)
    zb = jnp.dot(yb_ref[0], wb_ref[...], preferred_element_type=F32)
    mix = jax.nn.sigmoid(ga_ref[0].astype(F32)) * za + jax.nn.sigmoid(gb_ref[0].astype(F32)) * zb
    mo = jnp.dot(mix.astype(BF16), wo_ref[...], preferred_element_type=F32)
    x1 = x_ref[0] + _rms(mo, n1_ref[...])
    xn = _rms(x1, n2_ref[...]).astype(BF16)

    starts = [_G0 + j * (seg + _G0) for j in range(nseq)]
    if nseq == 1:
        @pl.when(i == 0)
        def _():
            gbuf[_G0 - 2:_G0, :] = st_ref[0]

        @pl.when(i > 0)
        def _():
            gbuf[_G0 - 2:_G0, :] = gbuf[_G0 + tm - 2:_G0 + tm, :]
    else:
        for j, r0 in enumerate(starts):
            gbuf[r0 - 2:r0, :] = st_ref[j]

    gate = jnp.dot(xn, wg_ref[...], preferred_element_type=F32)
    for j, r0 in enumerate(starts):
        gbuf[r0:r0 + seg, :] = gate[j * seg:(j + 1) * seg]
        cs_ref[j] = gbuf[r0 + seg - 2:r0 + seg, :]
    u = jnp.dot(xn, wu_ref[...], preferred_element_type=F32)
    c = jnp.concatenate([cw_ref[0:1, :] * gbuf[r0 - 2:r0 - 2 + seg, :] + cw_ref[1:2, :] * gbuf[r0 - 1:r0 - 1 + seg, :]
                         + cw_ref[2:3, :] * gbuf[r0:r0 + seg, :] for r0 in starts], axis=0) + cb_ref[...]
    hid = (jax.nn.gelu(c, approximate=True) * u).astype(BF16)
    f = jnp.dot(hid, wd_ref[...], preferred_element_type=F32)
    y_ref[0] = x1 + _rms(f, n3_ref[...])


def _merge_ffn(x, ya, yb, ga, gb, state, wa, wb, wo, n1, n2, n3, wg, wu, cw, cb, wd, *, tm, nseq=1):
    g, s, _ = x.shape
    assert nseq == 1 or s == tm
    tok = lambda w: pl.BlockSpec((1, tm, w), lambda a, b: (a, b, 0))
    per_group = pl.BlockSpec((nseq, CONV_W - 1, D_FF), lambda a, b: (a, 0, 0))
    weights = (wa, wb, wo, n1, n2, n3, wg, wu, cw, cb, wd)
    return pl.pallas_call(
        functools.partial(_merge_ffn_kernel, tm=tm, nseq=nseq), grid=(g, s // tm),
        in_specs=[tok(D_MODEL), pl.BlockSpec((1, N_PAIRS, tm, PAIR_W), lambda a, b: (a, 0, b, 0)), tok(D_VB),
                  tok(D_MODEL), tok(D_MODEL), per_group]
                 + [_resident(w.shape) for w in weights],
        out_specs=(tok(D_MODEL), per_group),
        out_shape=(jax.ShapeDtypeStruct((g, s, D_MODEL), F32),
                   jax.ShapeDtypeStruct((g * nseq, CONV_W - 1, D_FF), F32)),
        scratch_shapes=[pltpu.VMEM((tm + nseq * _G0, D_FF), F32)],
        compiler_params=pltpu.CompilerParams(dimension_semantics=("arbitrary", "arbitrary"),
                                             vmem_limit_bytes=VMEM_LIMIT_V7X),
        name="merge_ffn")(x, ya, yb, ga, gb, state, *weights)


def _rope_tables(pos, scale):
    inv = ROPE_BASE ** (-np.arange(_HALF, dtype=np.float64) / _HALF)
    ang = inv[:, None] * np.asarray(pos, np.float64)[None, :]
    return jnp.asarray(np.cos(ang) * scale, F32), jnp.asarray(np.sin(ang) * scale, F32)


def _band_bias_kernel(ext_ref, o_ref, *, tq, nk):
    w = ext_ref.shape[-1]
    toeplitz = pltpu.roll(jnp.broadcast_to(ext_ref[0], (nk, w)), 0, 1, stride=1, stride_axis=0)[:, :tq]
    qc = (nk - tq + lax.broadcasted_iota(jnp.int32, (nk, tq), 1)) // CHUNK
    kc = lax.broadcasted_iota(jnp.int32, (nk, tq), 0) // CHUNK
    o_ref[0] = jnp.where((kc >= qc - LEFT_CHUNKS) & (kc <= qc), toeplitz, NEG)


def _band_bias(table, tq, nk):
    w = nk + tq
    d_lo, d_hi = 1 - tq, w - tq
    assert -MAX_REL <= d_lo and d_hi >= MAX_REL and w % 128 == 0
    h = table.shape[0]
    ext = jnp.concatenate([table[:, d_lo + MAX_REL:], jnp.broadcast_to(table[:, -1:], (h, d_hi - MAX_REL))], axis=1)
    ext = (jnp.roll(ext, -(nk - 1), axis=1) * LOG2E).reshape(h, 1, w)
    return pl.pallas_call(
        functools.partial(_band_bias_kernel, tq=tq, nk=nk), grid=(h,),
        in_specs=[pl.BlockSpec((1, 1, w), lambda a: (a, 0, 0))],
        out_specs=pl.BlockSpec((1, nk, tq), lambda a: (a, 0, 0)),
        out_shape=jax.ShapeDtypeStruct((h, nk, tq), F32),
        compiler_params=pltpu.CompilerParams(dimension_semantics=("arbitrary",)),
        name="band_bias")(ext)


def _chunk_masks(tk, tq):
    kc = jnp.arange(tq)[:, None] // CHUNK
    qc = jnp.arange(tq)[None, :] // CHUNK
    return jnp.where(kc <= qc, 0.0, NEG).astype(F32).reshape(tq // tk, tk, tq)


def _scaled_cast_kernel(w_ref, s_ref, o_ref):
    o_ref[...] = (w_ref[...] * s_ref[...]).astype(BF16)


def _scaled_cast(w, col_scale, *, rows=128):
    d, n = w.shape
    return pl.pallas_call(
        _scaled_cast_kernel, grid=(d // rows,),
        in_specs=[pl.BlockSpec((rows, n), lambda a: (a, 0)), _resident(col_scale.shape)],
        out_specs=pl.BlockSpec((rows, n), lambda a: (a, 0)),
        out_shape=jax.ShapeDtypeStruct((d, n), BF16),
        compiler_params=pltpu.CompilerParams(dimension_semantics=("arbitrary",)),
        name="cast_w_in")(w, col_scale)


def _prep_weights(w_in, w_uk, w_uv):
    o = np.cumsum((0,) + IN_SIZES)
    d = w_in.shape[0]
    col_scale = np.ones((1, o[-1]), np.float32)
    col_scale[:, o[0]:o[1]] = A_SCALE * LOG2E
    w_bf = _scaled_cast(w_in, jnp.asarray(col_scale))
    col = lambda j: w_bf[:, o[j]:o[j + 1]]
    qn = col(3).reshape(d, H_B, DH_NOPE)
    qr = col(4).reshape(d, H_B, DH_ROPE)
    pad = jnp.zeros((d, H_B, HEAD_PAD - DH_NOPE - DH_ROPE), BF16).at[:, 0, :].set(col(6))
    w_qf = jnp.concatenate([qn, qr, pad], axis=2).reshape(d, D_QF)
    wm = jnp.concatenate([col(1), col(5), col(7), col(8)], axis=1)
    wva = col(2)
    wk = jnp.concatenate([w_uk, jnp.zeros((D_C, H_B, HEAD_PAD - DH_NOPE), F32)], axis=2).reshape(D_C, D_QF).astype(BF16)
    place = jnp.concatenate([jnp.zeros((DH_ROPE, DH_NOPE), F32), jnp.eye(DH_ROPE, dtype=F32),
                             jnp.zeros((DH_ROPE, HEAD_PAD - DH_NOPE - DH_ROPE), F32)], axis=1)
    ek = jnp.tile(place, (1, H_B)).astype(BF16)
    wvt = w_uv.reshape(D_C, D_VB).T.astype(BF16)
    return wm, jnp.concatenate([w_qf, col(0)], axis=1).T, wva.T, wva, wk, ek, wvt


def kernel(x_prompt, x_sample, cache_a_k, cache_a_v, cache_mla_ckv, cache_mla_krope, state_ffn_conv, norm_mix_pre,
           norm_mix_post, w_in, rel_bias_table, kv_norm, w_uk, w_uv, w_branch_a, w_branch_b, w_out, norm_ffn_pre,
           norm_ffn_post, w_ffn_gate, w_ffn_up, conv_w, conv_b, w_ffn_down):
    assert w_in.shape[0] == 1, "single layer"
    b, s, _ = x_prompt.shape
    db, t, _ = x_sample.shape
    past = cache_mla_ckv.shape[2]
    wcache = cache_a_k.shape[2]
    keep = min(A_WINDOW, s)
    tm1 = 512
    assert keep == tm1 and db * t == tm1 and wcache == A_WINDOW

    wm, wqt, wvat, wva, wk, ek, wvt = _prep_weights(w_in[0], w_uk[0], w_uv[0])
    row = lambda v: v.reshape(1, -1)
    proj_w = (row(norm_mix_pre[0]), wm, wqt, wvat, wva, row(kv_norm[0]))
    ffn_w = (w_branch_a[0].astype(BF16), w_branch_b[0].astype(BF16), w_out[0].astype(BF16), row(norm_mix_post[0]),
             row(norm_ffn_pre[0]), row(norm_ffn_post[0]), w_ffn_gate[0].astype(BF16), w_ffn_up[0].astype(BF16),
             conv_w[0], row(conv_b[0]), w_ffn_down[0].astype(BF16))
    table = rel_bias_table[0]

    pos = np.arange(s)
    tabs = _rope_tables(pos, MLA_SCALE * LOG2E) + _rope_tables(pos, 1.0)
    qa, ka, vat, kaf, vaf, qf, ckv, kr, ga, gb = _in_proj(x_prompt, *proj_w, *tabs, tm=tm1)
    t_mla = 512
    kf, vt = _kv_up(ckv, kr, wk, ek, wvt, tm=1024, tk=t_mla)
    ya = _band_prompt(qa, ka, vat, _band_bias(table, 256, A_WINDOW + 256), tm=tm1, tq=256)
    yb = _mla_attn(qf, kf, vt, _chunk_masks(t_mla, t_mla), tq=t_mla, tk=t_mla, causal=True)
    y_prompt, conv_p = _merge_ffn(x_prompt, ya, yb, ga, gb, jnp.zeros((b, CONV_W - 1, D_FF), F32), *ffn_w, tm=512)

    pos_s = np.tile(past + np.arange(t), db)
    tabs_s = _rope_tables(pos_s, MLA_SCALE * LOG2E) + _rope_tables(pos_s, 1.0)
    qa2, ka2, vat2, kaf2, vaf2, qf2, ckv2, kr2, ga2, gb2 = _in_proj(x_sample.reshape(1, db * t, D_MODEL), *proj_w,
                                                                    *tabs_s, tm=tm1)
    per_seq = lambda v: v.reshape(db, t, v.shape[-1])
    ckv2, kr2 = per_seq(ckv2[0]), per_seq(kr2[0])
    new_k = jnp.concatenate([cache_a_k[0].reshape(db, wcache, D_A), per_seq(kaf2[0])], axis=1)
    new_v = jnp.concatenate([cache_a_v[0].reshape(db, wcache, D_A), per_seq(vaf2[0])], axis=1)
    qa2 = jnp.transpose(qa2[0].reshape(N_PAIRS, PAIR_W, db, t), (0, 2, 3, 1))
    ya2 = _band_sample(qa2, new_k.astype(BF16), jnp.swapaxes(new_v, 1, 2).astype(BF16),
                       _band_bias(table, t, wcache + t))
    c_all = jnp.concatenate([cache_mla_ckv[0], ckv2], axis=1)
    kr_all = jnp.concatenate([cache_mla_krope[0], kr2], axis=1)
    kf2, vt2 = _kv_up(c_all, kr_all, wk, ek, wvt, tm=past + t, tk=past + t)
    qf2 = jnp.transpose(qf2[0, 0].reshape(D_QF, db, t), (1, 0, 2))[:, None]
    yb2 = _mla_attn(qf2, kf2, vt2, jnp.zeros((1, 8, 128), F32), tq=t, tk=past + t, causal=False)
    y_sample, conv_s = _merge_ffn(x_sample.reshape(1, db * t, D_MODEL), ya2.reshape(1, N_PAIRS, db * t, PAIR_W),
                                  yb2.reshape(1, db * t, D_VB), ga2, gb2, state_ffn_conv[0], *ffn_w,
                                  tm=db * t, nseq=db)
    y_sample = y_sample.reshape(db, t, D_MODEL)

    heads = lambda v: v.reshape(1, v.shape[0], v.shape[1], H_A, DH_A)
    return (y_prompt, y_sample,
            heads(kaf), heads(vaf), ckv[None], kr[None], conv_p[None],
            heads(new_k[:, -wcache:]), heads(new_v[:, -wcache:]), ckv2[None], kr2[None], conv_s[None])
```

```python
import functools

import jax
import jax.numpy as jnp
import numpy as np
from jax import lax
from jax.experimental import pallas as pl
from jax.experimental.pallas import tpu as pltpu

D_MODEL = 1024
CHUNK = 64
LEFT_CHUNKS = 8
A_WINDOW = LEFT_CHUNKS * CHUNK
H_A = 8
DH_A = 64
MAX_REL = 256
H_B = 8
DH_NOPE = 64
DH_ROPE = 32
DV_B = 64
D_C = 256
D_FF = 2816
CONV_W = 3
ROPE_BASE = 10000.0
EPS = 1e-6
A_SCALE = DH_A ** -0.5
MLA_SCALE = (DH_NOPE + DH_ROPE) ** -0.5
IN_SIZES = (H_A * DH_A, H_A * DH_A, H_A * DH_A, H_B * DH_NOPE, H_B * DH_ROPE, D_C, DH_ROPE, D_MODEL, D_MODEL)

HEAD_PAD = 128
D_A = H_A * DH_A
D_VB = H_B * DV_B
D_QF = H_B * HEAD_PAD
PAIR_W = 2 * DH_A
N_PAIRS = H_A // 2
NEG = -1e30
LOG2E = 1.4426950408889634
_L_ROWS = 16
VMEM_LIMIT_V7X = 56 * 1024 * 1024

F32 = jnp.float32
BF16 = jnp.bfloat16
_NT = (((1,), (1,)), ((), ()))


def _resident(shape):
    nd = len(shape)
    return pl.BlockSpec(shape, lambda *_: (0,) * nd, pipeline_mode=pl.Buffered(1))


def _rms(x, w):
    return x * lax.rsqrt(jnp.mean(x * x, axis=-1, keepdims=True) + EPS) * w


_C_KA, _C_CKV, _C_GA, _C_GB, _C_END = 0, 512, 768, 1792, 2816
_KR_ROW = DH_NOPE + DH_ROPE
_HALF = DH_ROPE // 2


def _store_pairs(ref, v):
    for g in range(N_PAIRS):
        ref[0, g] = v[:, g * PAIR_W:(g + 1) * PAIR_W]


def _rope_rows(x, c, s):
    x1, x2 = x[:_HALF], x[_HALF:]
    return [x1 * c - x2 * s, x1 * s + x2 * c]


def _in_proj_kernel(x_ref, nw_ref, wm_ref, wqt_ref, wvt_ref, wva_ref, kvn_ref, cq_ref, sq_ref, ck_ref, sk_ref,
                    qa_ref, ka_ref, vat_ref, kaf_ref, vaf_ref, qf_ref, ckv_ref, kr_ref, ga_ref, gb_ref, *, n):
    i = pl.program_id(1)
    xn = _rms(x_ref[0], nw_ref[...]).astype(BF16)

    def proj(lo, hi):
        return jnp.dot(xn, wm_ref[:, lo:hi], preferred_element_type=F32)

    ka = proj(_C_KA, _C_CKV)
    _store_pairs(ka_ref, ka.astype(BF16))
    kaf_ref[0] = ka
    vat_ref[0, 0] = lax.dot_general(wvt_ref[...], xn, _NT, preferred_element_type=F32).astype(BF16)

    def va_tail():
        vaf_ref[0] = jnp.dot(xn, wva_ref[...], preferred_element_type=F32)

    if n == 1:
        va_tail()
    else:
        pl.when(i == n - 1)(va_tail)

    qt = lax.dot_general(wqt_ref[...], xn, _NT, preferred_element_type=F32)
    for g in range(N_PAIRS):
        qa_ref[0, g] = qt[D_QF + g * PAIR_W:D_QF + (g + 1) * PAIR_W].astype(BF16)
    cq, sq = cq_ref[...], sq_ref[...]
    pad = jnp.zeros((HEAD_PAD - _KR_ROW, qt.shape[1]), F32)
    for h in range(H_B):
        slab = qt[h * HEAD_PAD:(h + 1) * HEAD_PAD]
        rows = [slab[:DH_NOPE] * (MLA_SCALE * LOG2E)] + _rope_rows(slab[DH_NOPE:_KR_ROW], cq, sq) + [pad]
        qf_ref[0, 0, h * HEAD_PAD:(h + 1) * HEAD_PAD, :] = jnp.concatenate(rows, axis=0).astype(BF16)
    kr_t = _rope_rows(qt[_KR_ROW:HEAD_PAD], ck_ref[...], sk_ref[...]) + [jnp.zeros((HEAD_PAD - DH_ROPE, qt.shape[1]), F32)]
    kr_ref[0] = jnp.concatenate(kr_t, axis=0).T[:, :DH_ROPE]

    ckv_ref[0] = _rms(proj(_C_CKV, _C_GA), kvn_ref[...])
    ga_ref[0] = proj(_C_GA, _C_GB).astype(BF16)
    gb_ref[0] = proj(_C_GB, _C_END).astype(BF16)


def _in_proj(x, nw, wm, wqt, wvt, wva, kvn, cq, sq, ck, sk, *, tm):
    g, s, _ = x.shape
    n = s // tm
    tok = lambda w: pl.BlockSpec((1, tm, w), lambda a, b: (a, b, 0))
    tab = pl.BlockSpec((_HALF, tm), lambda a, b: (0, b))
    tail = pl.BlockSpec((1, tm, D_A), lambda a, b: (a, 0, 0))
    out_shape = (
        jax.ShapeDtypeStruct((g, N_PAIRS, PAIR_W, s), BF16),
        jax.ShapeDtypeStruct((g, N_PAIRS, s, PAIR_W), BF16),
        jax.ShapeDtypeStruct((g, n, D_A, tm), BF16),
        jax.ShapeDtypeStruct((g, tm, D_A), F32),
        jax.ShapeDtypeStruct((g, tm, D_A), F32),
        jax.ShapeDtypeStruct((g, n, D_QF, tm), BF16),
        jax.ShapeDtypeStruct((g, s, D_C), F32),
        jax.ShapeDtypeStruct((g, s, DH_ROPE), F32),
        jax.ShapeDtypeStruct((g, s, D_MODEL), BF16),
        jax.ShapeDtypeStruct((g, s, D_MODEL), BF16),
    )
    pairs = pl.BlockSpec((1, N_PAIRS, tm, PAIR_W), lambda a, b: (a, 0, b, 0))
    pairs_t = pl.BlockSpec((1, N_PAIRS, PAIR_W, tm), lambda a, b: (a, 0, 0, b))
    out_specs = (pairs_t, pairs, pl.BlockSpec((1, 1, D_A, tm), lambda a, b: (a, b, 0, 0)), tail, tail,
                 pl.BlockSpec((1, 1, D_QF, tm), lambda a, b: (a, b, 0, 0)), tok(D_C), tok(DH_ROPE), tok(D_MODEL),
                 tok(D_MODEL))
    in_specs = [tok(D_MODEL), _resident(nw.shape), _resident(wm.shape), _resident(wqt.shape), _resident(wvt.shape),
                _resident(wva.shape), _resident(kvn.shape), tab, tab, tab, tab]
    return pl.pallas_call(
        functools.partial(_in_proj_kernel, n=n), grid=(g, n), in_specs=in_specs, out_specs=out_specs, out_shape=out_shape,
        compiler_params=pltpu.CompilerParams(dimension_semantics=("arbitrary", "arbitrary"),
                                             vmem_limit_bytes=VMEM_LIMIT_V7X),
        name="in_proj")(x, nw, wm, wqt, wvt, wva, kvn, cq, sq, ck, sk)


def _kv_up_kernel(ckv_ref, kr_ref, wk_ref, ek_ref, wvt_ref, kf_ref, vt_ref, *, tk):
    c = ckv_ref[0].astype(BF16)
    kf = jnp.dot(c, wk_ref[...], preferred_element_type=F32)
    kf = kf + jnp.dot(kr_ref[0].astype(BF16), ek_ref[...], preferred_element_type=F32)
    kf_ref[0] = kf.astype(BF16)
    for j in range(c.shape[0] // tk):
        vt_ref[0, j] = lax.dot_general(wvt_ref[...], c[j * tk:(j + 1) * tk], _NT,
                                       preferred_element_type=F32).astype(BF16)


def _kv_up(ckv, kr, wk, ek, wvt, *, tm, tk):
    g, s, _ = ckv.shape
    return pl.pallas_call(
        functools.partial(_kv_up_kernel, tk=tk), grid=(g, s // tm),
        in_specs=[pl.BlockSpec((1, tm, D_C), lambda a, b: (a, b, 0)),
                  pl.BlockSpec((1, tm, DH_ROPE), lambda a, b: (a, b, 0)),
                  _resident(wk.shape), _resident(ek.shape), _resident(wvt.shape)],
        out_specs=(pl.BlockSpec((1, tm, D_QF), lambda a, b: (a, b, 0)),
                   pl.BlockSpec((1, tm // tk, D_VB, tk), lambda a, b: (a, b, 0, 0))),
        out_shape=(jax.ShapeDtypeStruct((g, s, D_QF), BF16),
                   jax.ShapeDtypeStruct((g, s // tk, D_VB, tk), BF16)),
        compiler_params=pltpu.CompilerParams(dimension_semantics=("arbitrary", "arbitrary"),
                                             vmem_limit_bytes=VMEM_LIMIT_V7X),
        name="kv_up")(ckv, kr, wk, ek, wvt)


def _band_units(units, bias_ref, n_pairs=H_A // 2):
    def scores(u, g):
        q, k_of, _, _ = units[u]
        lanes = slice(g * 2 * DH_A, (g + 1) * 2 * DH_A)
        lane = lax.broadcasted_iota(jnp.int32, (q.shape[0], 2 * DH_A), 1)
        k2 = k_of(lanes)
        return [lax.dot_general(k2, jnp.where((lane >= DH_A) == (hl == 1), q[:, lanes], jnp.zeros((), BF16)), _NT,
                                preferred_element_type=F32) for hl in range(2)]

    steps = [(u, g) for u in range(len(units)) for g in range(n_pairs)]
    outs = [[] for _ in units]
    pend = scores(*steps[0])
    for j, (u, g) in enumerate(steps):
        nxt = scores(*steps[j + 1]) if j + 1 < len(steps) else None
        _, _, v_of, b0 = units[u]
        for hl in range(2):
            h = 2 * g + hl
            nk = pend[hl].shape[0]
            s = pend[hl] + bias_ref[h, b0:b0 + nk, :]
            p = jnp.exp2(s - jnp.max(s, axis=0, keepdims=True)).astype(BF16)
            v = jnp.concatenate([v_of(slice(h * DH_A, (h + 1) * DH_A)), jnp.ones((_L_ROWS, nk), BF16)], axis=0)
            o = jnp.dot(v, p, preferred_element_type=F32)
            outs[u].append(o[:DH_A] / o[DH_A:DH_A + 1])
        pend = nxt
    return [jnp.concatenate(o, axis=0).T for o in outs]


def _band_prompt_kernel(q_ref, kp_ref, kc_ref, vp_ref, vc_ref, bias_ref, o_ref, s_scr, *, tm, tq):
    subs = [(lo, lo + tq) for lo in range(0, tm, tq)]
    row = lax.broadcasted_iota(jnp.int32, (PAIR_W, tq), 0)

    def tile(first):
        def window(lo, hi):
            return (hi, tm - lo) if first else (tm - lo + hi, 0)

        def scores(g, j):
            lo, hi = subs[j]
            nk, b0 = window(lo, hi)
            q2 = q_ref[0, g, :, lo:hi]
            qcat = jnp.concatenate([jnp.where(row < DH_A, q2, jnp.zeros((), BF16)),
                                    jnp.where(row >= DH_A, q2, jnp.zeros((), BF16))], axis=1)
            k2 = kc_ref[0, g, :hi, :]
            if not first:
                k2 = jnp.concatenate([kp_ref[0, g, lo:, :], k2], axis=0)
            bias = jnp.concatenate([bias_ref[2 * g + hl, b0:b0 + nk, :] for hl in range(2)], axis=1)
            return jnp.dot(k2, qcat, preferred_element_type=F32) + bias

        def stash(j, s):
            s_scr[j, :s.shape[0], :] = s
            return jnp.max(s, axis=0, keepdims=True)

        def consume(g, j, col_max):
            lo, hi = subs[j]
            nk, _ = window(lo, hi)
            outs = []
            for hl in range(2):
                cols = slice(hl * tq, (hl + 1) * tq)
                p = jnp.exp2(s_scr[j, :nk, cols] - col_max[:, cols]).astype(BF16)
                rows = pl.ds(g * PAIR_W + hl * DH_A, DH_A)
                v = vc_ref[0, 0, rows, :hi]
                if not first:
                    v = jnp.concatenate([vp_ref[0, 0, rows, lo:], v], axis=1)
                v = jnp.concatenate([v, jnp.ones((_L_ROWS, nk), BF16)], axis=0)
                o = jnp.dot(v, p, preferred_element_type=F32)
                outs.append(o[:DH_A] / o[DH_A:DH_A + 1])
            o_ref[0, g, lo:hi, :] = jnp.concatenate(outs, axis=0).T.astype(BF16)

        def body(g, col_max):
            new_max = []
            for j in range(len(subs)):
                nxt = scores(g + 1, j)
                consume(g, j, col_max[j])
                new_max.append(stash(j, nxt))
            return tuple(new_max)

        col_max = tuple(stash(j, scores(0, j)) for j in range(len(subs)))
        for g in range(N_PAIRS - 1):
            col_max = body(g, col_max)
        for j in range(len(subs)):
            consume(N_PAIRS - 1, j, col_max[j])

    pl.when(pl.program_id(1) == 0)(functools.partial(tile, True))
    pl.when(pl.program_id(1) > 0)(functools.partial(tile, False))


def _band_prompt(qa, ka, vat, bias, *, tm, tq):
    g, _, _, s = qa.shape
    cur = lambda a, b: (a, 0, b, 0)
    prev = lambda a, b: (a, 0, jnp.maximum(b - 1, 0), 0)
    prev_v = lambda a, b: (a, jnp.maximum(b - 1, 0), 0, 0)
    pairs = lambda idx: pl.BlockSpec((1, N_PAIRS, tm, PAIR_W), idx)
    return pl.pallas_call(
        functools.partial(_band_prompt_kernel, tm=tm, tq=tq), grid=(g, s // tm),
        in_specs=[pl.BlockSpec((1, N_PAIRS, PAIR_W, tm), lambda a, b: (a, 0, 0, b)), pairs(prev), pairs(cur),
                  pl.BlockSpec((1, 1, D_A, tm), prev_v), pl.BlockSpec((1, 1, D_A, tm), lambda a, b: (a, b, 0, 0)),
                  _resident(bias.shape)],
        out_specs=pairs(cur),
        out_shape=jax.ShapeDtypeStruct((g, N_PAIRS, s, PAIR_W), BF16),
        scratch_shapes=[pltpu.VMEM((tm // tq, tm + tq, 2 * tq), F32)],
        compiler_params=pltpu.CompilerParams(dimension_semantics=("arbitrary", "arbitrary"),
                                             vmem_limit_bytes=VMEM_LIMIT_V7X),
        name="band_prompt")(qa, ka, ka, vat, vat, bias)


def _band_sample_kernel(q_ref, k_ref, vt_ref, bias_ref, o_ref):
    q = jnp.concatenate([q_ref[g, 0] for g in range(N_PAIRS)], axis=1)
    unit = (q, lambda lanes: k_ref[0, :, lanes], lambda rows: vt_ref[0, rows, :], 0)
    o = _band_units([unit], bias_ref)[0].astype(BF16)
    for g in range(N_PAIRS):
        o_ref[g, 0] = o[:, g * PAIR_W:(g + 1) * PAIR_W]


def _band_sample(qa, k_all, vt_all, bias):
    _, g, t, _ = qa.shape
    nk = k_all.shape[1]
    pairs = pl.BlockSpec((N_PAIRS, 1, t, PAIR_W), lambda a: (0, a, 0, 0))
    return pl.pallas_call(
        _band_sample_kernel, grid=(g,),
        in_specs=[pairs,
                  pl.BlockSpec((1, nk, D_A), lambda a: (a, 0, 0)),
                  pl.BlockSpec((1, D_A, nk), lambda a: (a, 0, 0)),
                  _resident(bias.shape)],
        out_specs=pairs,
        out_shape=jax.ShapeDtypeStruct((N_PAIRS, g, t, PAIR_W), BF16),
        compiler_params=pltpu.CompilerParams(dimension_semantics=("arbitrary",),
                                             vmem_limit_bytes=VMEM_LIMIT_V7X),
        name="band_sample")(qa, k_all, vt_all, bias)


def _mla_kernel(q_ref, k_ref, vt_ref, mask_ref, o_ref, s_scr, *, tq, tk, causal):
    nq = q_ref.shape[1]
    nk = k_ref.shape[1] // tk
    ones = jnp.ones((_L_ROWS, tk), BF16)

    def q_rows(i):
        return [q_ref[0, i, hl * HEAD_PAD:(hl + 1) * HEAD_PAD, :] for hl in range(2)]

    def scores(qs, kk, hl):
        krow = pl.multiple_of(kk * tk, tk)
        return jnp.dot(k_ref[0, pl.ds(krow, tk), hl * HEAD_PAD:(hl + 1) * HEAD_PAD], qs[hl],
                       preferred_element_type=F32)

    def stash(hl, s):
        s_scr[hl] = s
        return jnp.max(s, axis=0, keepdims=True)

    def consume(kk, hl, m, acc, tile_max, mask_j):
        if mask_j is not None:
            s = s_scr[hl] + mask_ref[mask_j]
            m_new = jnp.maximum(m, jnp.max(s, axis=0, keepdims=True))
        else:
            s = s_scr[hl]
            m_new = jnp.maximum(m, tile_max)
        p = jnp.exp2(s - m_new).astype(BF16)
        v = jnp.concatenate([vt_ref[0, kk, hl * DV_B:(hl + 1) * DV_B, :], ones], axis=0)
        return m_new, acc * jnp.exp2(m - m_new) + jnp.dot(v, p, preferred_element_type=F32)

    def q_tile(i, first_max):
        qs = q_rows(i)

        def body(t, carry):
            out = []
            for hl in range(2):
                m, acc, tile_max = carry[3 * hl:3 * hl + 3]
                nxt = scores(qs, t + 1, hl)
                out += list(consume(t, hl, m, acc, tile_max, None)) + [stash(hl, nxt)]
            return tuple(out)

        r = tq // tk
        n_full = r * i if causal else nk - 1
        carry = ()
        for hl in range(2):
            carry += (jnp.full((1, tq), NEG, F32), jnp.zeros((DV_B + _L_ROWS, tq), F32), first_max[hl])
        done = 0
        for n in (4, 2, 1):
            left = n_full - done
            trips = left // n if isinstance(left, int) else lax.shift_right_logical(left, n.bit_length() - 1)

            def trip(u, c, n=n, done=done):
                for j in range(n):
                    c = body(done + n * u + j, c)
                return c

            carry = lax.fori_loop(0, trips, trip, carry)
            done = done + trips * n
        tail = [(r * i + j, j) for j in range(r)] if causal else [(nk - 1, None)]
        qs_next = q_rows(jnp.minimum(i + 1, nq - 1)) if nq > 1 else None
        state = [list(carry[3 * hl:3 * hl + 3]) for hl in range(2)]
        for idx, (tile, mask_j) in enumerate(tail):
            for hl in range(2):
                if idx + 1 < len(tail):
                    nxt = scores(qs, tile + 1, hl)
                else:
                    nxt = scores(qs_next, 0, hl) if nq > 1 else None
                m, acc = consume(tile, hl, *state[hl], mask_j)
                state[hl] = [m, acc, stash(hl, nxt) if nxt is not None else state[hl][2]]
        accs = [state[hl][1] for hl in range(2)]
        next_max = [state[hl][2] for hl in range(2)]
        o = jnp.concatenate([acc[:DV_B] / acc[DV_B:DV_B + 1] for acc in accs], axis=0)
        o_ref[0, pl.ds(pl.multiple_of(i * tq, tq), tq), :] = o.T.astype(BF16)
        return tuple(next_max)

    first = q_rows(0)
    lax.fori_loop(0, nq, q_tile, tuple(stash(hl, scores(first, 0, hl)) for hl in range(2)))


def _mla_attn(qf, kf, vt, mask, *, tq, tk, causal):
    g, nq, _, _ = qf.shape
    sq, sk = nq * tq, kf.shape[1]
    return pl.pallas_call(
        functools.partial(_mla_kernel, tq=tq, tk=tk, causal=causal), grid=(g, H_B // 2),
        in_specs=[pl.BlockSpec((1, nq, 2 * HEAD_PAD, tq), lambda a, b: (a, 0, b, 0)),
                  pl.BlockSpec((1, sk, 2 * HEAD_PAD), lambda a, b: (a, 0, b)),
                  pl.BlockSpec((1, sk // tk, 2 * DV_B, tk), lambda a, b: (a, 0, b, 0)),
                  _resident(mask.shape)],
        out_specs=pl.BlockSpec((1, sq, 2 * DV_B), lambda a, b: (a, 0, b)),
        out_shape=jax.ShapeDtypeStruct((g, sq, D_VB), BF16),
        scratch_shapes=[pltpu.VMEM((2, tk, tq), F32)],
        compiler_params=pltpu.CompilerParams(dimension_semantics=("arbitrary", "arbitrary"),
                                             vmem_limit_bytes=VMEM_LIMIT_V7X),
        name="mla_attn")(qf, kf, vt, mask)


_G0 = 8


def _merge_ffn_kernel(x_ref, ya_ref, yb_ref, ga_ref, gb_ref, st_ref, wa_ref, wb_ref, wo_ref, n1_ref, n2_ref, n3_ref,
                      wg_ref, wu_ref, cw_ref, cb_ref, wd_ref, y_ref, cs_ref, gbuf, *, tm, nseq):
    i = pl.program_id(1)
    seg = tm // nseq
    ya = jnp.concatenate([ya_ref[0, g] for g in range(N_PAIRS)], axis=1)
    za = jnp.dot(ya, wa_ref[...], preferred_element_type=F32)
    zb = jnp.dot(yb_ref[0], wb_ref[...], preferred_element_type=F32)
    mix = jax.nn.sigmoid(ga_ref[0].astype(F32)) * za + jax.nn.sigmoid(gb_ref[0].astype(F32)) * zb
    mo = jnp.dot(mix.astype(BF16), wo_ref[...], preferred_element_type=F32)
    x1 = x_ref[0] + _rms(mo, n1_ref[...])
    xn = _rms(x1, n2_ref[...]).astype(BF16)

    starts = [_G0 + j * (seg + _G0) for j in range(nseq)]
    if nseq == 1:
        @pl.when(i == 0)
        def _():
            gbuf[_G0 - 2:_G0, :] = st_ref[0]

        @pl.when(i > 0)
        def _():
            gbuf[_G0 - 2:_G0, :] = gbuf[_G0 + tm - 2:_G0 + tm, :]
    else:
        for j, r0 in enumerate(starts):
            gbuf[r0 - 2:r0, :] = st_ref[j]

    gate = jnp.dot(xn, wg_ref[...], preferred_element_type=F32)
    for j, r0 in enumerate(starts):
        gbuf[r0:r0 + seg, :] = gate[j * seg:(j + 1) * seg]
        cs_ref[j] = gbuf[r0 + seg - 2:r0 + seg, :]
    u = jnp.dot(xn, wu_ref[...], preferred_element_type=F32)
    c = jnp.concatenate([cw_ref[0:1, :] * gbuf[r0 - 2:r0 - 2 + seg, :] + cw_ref[1:2, :] * gbuf[r0 - 1:r0 - 1 + seg, :]
                         + cw_ref[2:3, :] * gbuf[r0:r0 + seg, :] for r0 in starts], axis=0) + cb_ref[...]
    hid = (jax.nn.gelu(c, approximate=True) * u).astype(BF16)
    f = jnp.dot(hid, wd_ref[...], preferred_element_type=F32)
    y_ref[0] = x1 + _rms(f, n3_ref[...])


def _merge_ffn(x, ya, yb, ga, gb, state, wa, wb, wo, n1, n2, n3, wg, wu, cw, cb, wd, *, tm, nseq=1):
    g, s, _ = x.shape
    assert nseq == 1 or s == tm
    tok = lambda w: pl.BlockSpec((1, tm, w), lambda a, b: (a, b, 0))
    per_group = pl.BlockSpec((nseq, CONV_W - 1, D_FF), lambda a, b: (a, 0, 0))
    weights = (wa, wb, wo, n1, n2, n3, wg, wu, cw, cb, wd)
    return pl.pallas_call(
        functools.partial(_merge_ffn_kernel, tm=tm, nseq=nseq), grid=(g, s // tm),
        in_specs=[tok(D_MODEL), pl.BlockSpec((1, N_PAIRS, tm, PAIR_W), lambda a, b: (a, 0, b, 0)), tok(D_VB),
                  tok(D_MODEL), tok(D_MODEL), per_group]
                 + [_resident(w.shape) for w in weights],
        out_specs=(tok(D_MODEL), per_group),
        out_shape=(jax.ShapeDtypeStruct((g, s, D_MODEL), F32),
                   jax.ShapeDtypeStruct((g * nseq, CONV_W - 1, D_FF), F32)),
        scratch_shapes=[pltpu.VMEM((tm + nseq * _G0, D_FF), F32)],
        compiler_params=pltpu.CompilerParams(dimension_semantics=("arbitrary", "arbitrary"),
                                             vmem_limit_bytes=VMEM_LIMIT_V7X),
        name="merge_ffn")(x, ya, yb, ga, gb, state, *weights)


def _rope_tables(pos, scale):
    inv = ROPE_BASE ** (-np.arange(_HALF, dtype=np.float64) / _HALF)
    ang = inv[:, None] * np.asarray(pos, np.float64)[None, :]
    return jnp.asarray(np.cos(ang) * scale, F32), jnp.asarray(np.sin(ang) * scale, F32)


def _band_bias_kernel(ext_ref, o_ref, *, tq, nk):
    w = ext_ref.shape[-1]
    toeplitz = pltpu.roll(jnp.broadcast_to(ext_ref[0], (nk, w)), 0, 1, stride=1, stride_axis=0)[:, :tq]
    qc = (nk - tq + lax.broadcasted_iota(jnp.int32, (nk, tq), 1)) // CHUNK
    kc = lax.broadcasted_iota(jnp.int32, (nk, tq), 0) // CHUNK
    o_ref[0] = jnp.where((kc >= qc - LEFT_CHUNKS) & (kc <= qc), toeplitz, NEG)


def _band_bias(table, tq, nk):
    w = nk + tq
    d_lo, d_hi = 1 - tq, w - tq
    assert -MAX_REL <= d_lo and d_hi >= MAX_REL and w % 128 == 0
    h = table.shape[0]
    ext = jnp.concatenate([table[:, d_lo + MAX_REL:], jnp.broadcast_to(table[:, -1:], (h, d_hi - MAX_REL))], axis=1)
    ext = (jnp.roll(ext, -(nk - 1), axis=1) * LOG2E).reshape(h, 1, w)
    return pl.pallas_call(
        functools.partial(_band_bias_kernel, tq=tq, nk=nk), grid=(h,),
        in_specs=[pl.BlockSpec((1, 1, w), lambda a: (a, 0, 0))],
        out_specs=pl.BlockSpec((1, nk, tq), lambda a: (a, 0, 0)),
        out_shape=jax.ShapeDtypeStruct((h, nk, tq), F32),
        compiler_params=pltpu.CompilerParams(dimension_semantics=("arbitrary",)),
        name="band_bias")(ext)


def _chunk_masks(tk, tq):
    kc = jnp.arange(tq)[:, None] // CHUNK
    qc = jnp.arange(tq)[None, :] // CHUNK
    return jnp.where(kc <= qc, 0.0, NEG).astype(F32).reshape(tq // tk, tk, tq)


def _scaled_cast_kernel(w_ref, o_ref, *, rows, n_scaled, scale):
    r = pl.program_id(0) * rows + lax.broadcasted_iota(jnp.int32, (rows, 1), 0)
    o_ref[...] = (w_ref[0] * jnp.where(r < n_scaled, scale, 1.0)).astype(BF16)


def _scaled_cast(w, n_scaled, scale, *, rows):
    _, d, n = w.shape
    return pl.pallas_call(
        functools.partial(_scaled_cast_kernel, rows=rows, n_scaled=n_scaled, scale=scale), grid=(d // rows,),
        in_specs=[pl.BlockSpec((1, rows, n), lambda a: (0, a, 0))],
        out_specs=pl.BlockSpec((rows, n), lambda a: (a, 0)),
        out_shape=jax.ShapeDtypeStruct((d, n), BF16),
        compiler_params=pltpu.CompilerParams(dimension_semantics=("arbitrary",)),
        name="cast_w_in")(w)


def _prep_weights(w_in, w_uk, w_uv):
    o = np.cumsum((0,) + IN_SIZES)
    d = w_in.shape[1]
    w_t = _scaled_cast(jnp.swapaxes(w_in, 1, 2), int(o[1]), A_SCALE * LOG2E, rows=464)
    row = lambda j: w_t[o[j]:o[j + 1]]
    qn = row(3).reshape(H_B, DH_NOPE, d)
    qr = row(4).reshape(H_B, DH_ROPE, d)
    pad = jnp.zeros((H_B, HEAD_PAD - DH_NOPE - DH_ROPE, d), BF16).at[0].set(row(6))
    wqt = jnp.concatenate([jnp.concatenate([qn, qr, pad], axis=1).reshape(D_QF, d), row(0)], axis=0)
    wm = jnp.concatenate([row(1), row(5), row(7), row(8)], axis=0).T
    wvat = row(2)
    wk = jnp.concatenate([w_uk, jnp.zeros((D_C, H_B, HEAD_PAD - DH_NOPE), F32)], axis=2).reshape(D_C, D_QF).astype(BF16)
    place = jnp.concatenate([jnp.zeros((DH_ROPE, DH_NOPE), F32), jnp.eye(DH_ROPE, dtype=F32),
                             jnp.zeros((DH_ROPE, HEAD_PAD - DH_NOPE - DH_ROPE), F32)], axis=1)
    ek = jnp.tile(place, (1, H_B)).astype(BF16)
    wvt = w_uv.reshape(D_C, D_VB).T.astype(BF16)
    return wm, wqt, wvat, wvat.T, wk, ek, wvt


def kernel(x_prompt, x_sample, cache_a_k, cache_a_v, cache_mla_ckv, cache_mla_krope, state_ffn_conv, norm_mix_pre,
           norm_mix_post, w_in, rel_bias_table, kv_norm, w_uk, w_uv, w_branch_a, w_branch_b, w_out, norm_ffn_pre,
           norm_ffn_post, w_ffn_gate, w_ffn_up, conv_w, conv_b, w_ffn_down):
    assert w_in.shape[0] == 1, "single layer"
    b, s, _ = x_prompt.shape
    db, t, _ = x_sample.shape
    past = cache_mla_ckv.shape[2]
    wcache = cache_a_k.shape[2]
    keep = min(A_WINDOW, s)
    tm1 = 512
    assert keep == tm1 and db * t == tm1 and wcache == A_WINDOW

    wm, wqt, wvat, wva, wk, ek, wvt = _prep_weights(w_in, w_uk[0], w_uv[0])
    row = lambda v: v.reshape(1, -1)
    proj_w = (row(norm_mix_pre[0]), wm, wqt, wvat, wva, row(kv_norm[0]))
    ffn_w = (w_branch_a[0].astype(BF16), w_branch_b[0].astype(BF16), w_out[0].astype(BF16), row(norm_mix_post[0]),
             row(norm_ffn_pre[0]), row(norm_ffn_post[0]), w_ffn_gate[0].astype(BF16), w_ffn_up[0].astype(BF16),
             conv_w[0], row(conv_b[0]), w_ffn_down[0].astype(BF16))
    table = rel_bias_table[0]

    pos = np.arange(s)
    tabs = _rope_tables(pos, MLA_SCALE * LOG2E) + _rope_tables(pos, 1.0)
    qa, ka, vat, kaf, vaf, qf, ckv, kr, ga, gb = _in_proj(x_prompt, *proj_w, *tabs, tm=tm1)
    t_mla = 512
    kf, vt = _kv_up(ckv, kr, wk, ek, wvt, tm=1024, tk=t_mla)
    ya = _band_prompt(qa, ka, vat, _band_bias(table, 256, A_WINDOW + 256), tm=tm1, tq=256)
    yb = _mla_attn(qf, kf, vt, _chunk_masks(t_mla, t_mla), tq=t_mla, tk=t_mla, causal=True)
    y_prompt, conv_p = _merge_ffn(x_prompt, ya, yb, ga, gb, jnp.zeros((b, CONV_W - 1, D_FF), F32), *ffn_w, tm=512)

    pos_s = np.tile(past + np.arange(t), db)
    tabs_s = _rope_tables(pos_s, MLA_SCALE * LOG2E) + _rope_tables(pos_s, 1.0)
    qa2, ka2, vat2, kaf2, vaf2, qf2, ckv2, kr2, ga2, gb2 = _in_proj(x_sample.reshape(1, db * t, D_MODEL), *proj_w,
                                                                    *tabs_s, tm=tm1)
    per_seq = lambda v: v.reshape(db, t, v.shape[-1])
    ckv2, kr2 = per_seq(ckv2[0]), per_seq(kr2[0])
    new_k = jnp.concatenate([cache_a_k[0].reshape(db, wcache, D_A), per_seq(kaf2[0])], axis=1)
    new_v = jnp.concatenate([cache_a_v[0].reshape(db, wcache, D_A), per_seq(vaf2[0])], axis=1)
    qa2 = jnp.transpose(qa2[0].reshape(N_PAIRS, PAIR_W, db, t), (0, 2, 3, 1))
    ya2 = _band_sample(qa2, new_k.astype(BF16), jnp.swapaxes(new_v, 1, 2).astype(BF16),
                       _band_bias(table, t, wcache + t))
    c_all = jnp.concatenate([cache_mla_ckv[0], ckv2], axis=1)
    kr_all = jnp.concatenate([cache_mla_krope[0], kr2], axis=1)
    kf2, vt2 = _kv_up(c_all, kr_all, wk, ek, wvt, tm=past + t, tk=past + t)
    qf2 = jnp.transpose(qf2[0, 0].reshape(D_QF, db, t), (1, 0, 2))[:, None]
    yb2 = _mla_attn(qf2, kf2, vt2, jnp.zeros((1, 8, 128), F32), tq=t, tk=past + t, causal=False)
    y_sample, conv_s = _merge_ffn(x_sample.reshape(1, db * t, D_MODEL), ya2.reshape(1, N_PAIRS, db * t, PAIR_W),
                                  yb2.reshape(1, db * t, D_VB), ga2, gb2, state_ffn_conv[0], *ffn_w,
                                  tm=db * t, nseq=db)
    y_sample = y_sample.reshape(db, t, D_MODEL)

    heads = lambda v: v.reshape(1, v.shape[0], v.shape[1], H_A, DH_A)
    return (y_prompt, y_sample,
            heads(kaf), heads(vaf), ckv[None], kr[None], conv_p[None],
            heads(new_k[:, -wcache:]), heads(new_v[:, -wcache:]), ckv2[None], kr2[None], conv_s[None])
```

```python
import functools

import jax
import jax.numpy as jnp
import numpy as np
from jax import lax
from jax.experimental import pallas as pl
from jax.experimental.pallas import tpu as pltpu

D_MODEL = 1024
CHUNK = 64
LEFT_CHUNKS = 8
A_WINDOW = LEFT_CHUNKS * CHUNK
H_A = 8
DH_A = 64
MAX_REL = 256
H_B = 8
DH_NOPE = 64
DH_ROPE = 32
DV_B = 64
D_C = 256
D_FF = 2816
CONV_W = 3
ROPE_BASE = 10000.0
EPS = 1e-6
A_SCALE = DH_A ** -0.5
MLA_SCALE = (DH_NOPE + DH_ROPE) ** -0.5
IN_SIZES = (H_A * DH_A, H_A * DH_A, H_A * DH_A, H_B * DH_NOPE, H_B * DH_ROPE, D_C, DH_ROPE, D_MODEL, D_MODEL)

HEAD_PAD = 128
D_A = H_A * DH_A
D_VB = H_B * DV_B
D_QF = H_B * HEAD_PAD
PAIR_W = 2 * DH_A
N_PAIRS = H_A // 2
NEG = -1e30
LOG2E = 1.4426950408889634
_L_ROWS = 16
VMEM_LIMIT_V7X = 56 * 1024 * 1024

F32 = jnp.float32
BF16 = jnp.bfloat16
_NT = (((1,), (1,)), ((), ()))


def _resident(shape):
    nd = len(shape)
    return pl.BlockSpec(shape, lambda *_: (0,) * nd, pipeline_mode=pl.Buffered(1))


def _rms(x, w):
    return x * lax.rsqrt(jnp.mean(x * x, axis=-1, keepdims=True) + EPS) * w


_C_KA, _C_CKV, _C_GA, _C_GB, _C_END = 0, 512, 768, 1792, 2816
_KR_ROW = DH_NOPE + DH_ROPE
_HALF = DH_ROPE // 2


def _store_pairs(ref, v):
    for g in range(N_PAIRS):
        ref[0, g] = v[:, g * PAIR_W:(g + 1) * PAIR_W]


def _rope_rows(x, c, s):
    x1, x2 = x[:_HALF], x[_HALF:]
    return [x1 * c - x2 * s, x1 * s + x2 * c]


def _in_proj_kernel(x_ref, nw_ref, wm_ref, wqt_ref, wvt_ref, wva_ref, kvn_ref, wuk_ref, wuvt_ref, cq_ref, sq_ref,
                    ck_ref, sk_ref, qa_ref, ka_ref, vat_ref, kaf_ref, vaf_ref, qf_ref, ckv_ref, kr_ref, ga_ref, gb_ref,
                    kf_ref, vt_ref, *, n):
    i = pl.program_id(1)
    xn = _rms(x_ref[0], nw_ref[...]).astype(BF16)

    def proj(lo, hi):
        return jnp.dot(xn, wm_ref[:, lo:hi], preferred_element_type=F32)

    ka = proj(_C_KA, _C_CKV)
    _store_pairs(ka_ref, ka.astype(BF16))
    kaf_ref[0] = ka
    vat_ref[0, 0] = lax.dot_general(wvt_ref[...], xn, _NT, preferred_element_type=F32).astype(BF16)

    def va_tail():
        vaf_ref[0] = jnp.dot(xn, wva_ref[...], preferred_element_type=F32)

    if n == 1:
        va_tail()
    else:
        pl.when(i == n - 1)(va_tail)

    qt = lax.dot_general(wqt_ref[...], xn, _NT, preferred_element_type=F32)
    for g in range(N_PAIRS):
        qa_ref[0, g] = qt[D_QF + g * PAIR_W:D_QF + (g + 1) * PAIR_W].astype(BF16)
    cq, sq = cq_ref[...], sq_ref[...]
    pad = jnp.zeros((HEAD_PAD - _KR_ROW, qt.shape[1]), F32)
    for h in range(H_B):
        slab = qt[h * HEAD_PAD:(h + 1) * HEAD_PAD]
        rows = [slab[:DH_NOPE] * (MLA_SCALE * LOG2E)] + _rope_rows(slab[DH_NOPE:_KR_ROW], cq, sq) + [pad]
        qf_ref[0, 0, h * HEAD_PAD:(h + 1) * HEAD_PAD, :] = jnp.concatenate(rows, axis=0).astype(BF16)
    kr_t = _rope_rows(qt[_KR_ROW:HEAD_PAD], ck_ref[...], sk_ref[...])
    zero_rows = lambda r: jnp.zeros((r, qt.shape[1]), F32)
    kr_ref[0] = jnp.concatenate(kr_t + [zero_rows(HEAD_PAD - DH_ROPE)], axis=0).T[:, :DH_ROPE]

    ckv = _rms(proj(_C_CKV, _C_GA), kvn_ref[...])
    ckv_ref[0] = ckv
    c_bf = ckv.astype(BF16)
    k_nope = jnp.dot(c_bf, wuk_ref[...], preferred_element_type=F32)
    k_rope = jnp.concatenate([zero_rows(DH_NOPE)] + kr_t + [zero_rows(HEAD_PAD - _KR_ROW)], axis=0).T
    for h in range(H_B):
        slab = slice(h * HEAD_PAD, (h + 1) * HEAD_PAD)
        kf_ref[0, :, slab] = (k_nope[:, slab] + k_rope).astype(BF16)
    vt_ref[0, 0] = lax.dot_general(wuvt_ref[...], c_bf, _NT, preferred_element_type=F32).astype(BF16)
    ga_ref[0] = proj(_C_GA, _C_GB).astype(BF16)
    gb_ref[0] = proj(_C_GB, _C_END).astype(BF16)


def _in_proj(x, nw, wm, wqt, wvt, wva, kvn, wuk, wuvt, cq, sq, ck, sk, *, tm):
    g, s, _ = x.shape
    n = s // tm
    tok = lambda w: pl.BlockSpec((1, tm, w), lambda a, b: (a, b, 0))
    tab = pl.BlockSpec((_HALF, tm), lambda a, b: (0, b))
    tail = pl.BlockSpec((1, tm, D_A), lambda a, b: (a, 0, 0))
    out_shape = (
        jax.ShapeDtypeStruct((g, N_PAIRS, PAIR_W, s), BF16),
        jax.ShapeDtypeStruct((g, N_PAIRS, s, PAIR_W), BF16),
        jax.ShapeDtypeStruct((g, n, D_A, tm), BF16),
        jax.ShapeDtypeStruct((g, tm, D_A), F32),
        jax.ShapeDtypeStruct((g, tm, D_A), F32),
        jax.ShapeDtypeStruct((g, n, D_QF, tm), BF16),
        jax.ShapeDtypeStruct((g, s, D_C), F32),
        jax.ShapeDtypeStruct((g, s, DH_ROPE), F32),
        jax.ShapeDtypeStruct((g, s, D_MODEL), BF16),
        jax.ShapeDtypeStruct((g, s, D_MODEL), BF16),
        jax.ShapeDtypeStruct((g, s, D_QF), BF16),
        jax.ShapeDtypeStruct((g, n, D_VB, tm), BF16),
    )
    pairs = pl.BlockSpec((1, N_PAIRS, tm, PAIR_W), lambda a, b: (a, 0, b, 0))
    pairs_t = pl.BlockSpec((1, N_PAIRS, PAIR_W, tm), lambda a, b: (a, 0, 0, b))
    out_specs = (pairs_t, pairs, pl.BlockSpec((1, 1, D_A, tm), lambda a, b: (a, b, 0, 0)), tail, tail,
                 pl.BlockSpec((1, 1, D_QF, tm), lambda a, b: (a, b, 0, 0)), tok(D_C), tok(DH_ROPE), tok(D_MODEL),
                 tok(D_MODEL), tok(D_QF), pl.BlockSpec((1, 1, D_VB, tm), lambda a, b: (a, b, 0, 0)))
    in_specs = [tok(D_MODEL), _resident(nw.shape), _resident(wm.shape), _resident(wqt.shape), _resident(wvt.shape),
                _resident(wva.shape), _resident(kvn.shape), _resident(wuk.shape), _resident(wuvt.shape), tab, tab, tab,
                tab]
    return pl.pallas_call(
        functools.partial(_in_proj_kernel, n=n), grid=(g, n), in_specs=in_specs, out_specs=out_specs, out_shape=out_shape,
        compiler_params=pltpu.CompilerParams(dimension_semantics=("arbitrary", "arbitrary"),
                                             vmem_limit_bytes=VMEM_LIMIT_V7X),
        name="in_proj")(x, nw, wm, wqt, wvt, wva, kvn, wuk, wuvt, cq, sq, ck, sk)


def _kv_up_kernel(ckv_ref, kr_ref, wk_ref, ek_ref, wvt_ref, kf_ref, vt_ref, *, tk):
    c = ckv_ref[0].astype(BF16)
    kf = jnp.dot(c, wk_ref[...], preferred_element_type=F32)
    kf = kf + jnp.dot(kr_ref[0].astype(BF16), ek_ref[...], preferred_element_type=F32)
    kf_ref[0] = kf.astype(BF16)
    for j in range(c.shape[0] // tk):
        vt_ref[0, j] = lax.dot_general(wvt_ref[...], c[j * tk:(j + 1) * tk], _NT,
                                       preferred_element_type=F32).astype(BF16)


def _kv_up(ckv, kr, wk, ek, wvt, *, tm, tk):
    g, s, _ = ckv.shape
    return pl.pallas_call(
        functools.partial(_kv_up_kernel, tk=tk), grid=(g, s // tm),
        in_specs=[pl.BlockSpec((1, tm, D_C), lambda a, b: (a, b, 0)),
                  pl.BlockSpec((1, tm, DH_ROPE), lambda a, b: (a, b, 0)),
                  _resident(wk.shape), _resident(ek.shape), _resident(wvt.shape)],
        out_specs=(pl.BlockSpec((1, tm, D_QF), lambda a, b: (a, b, 0)),
                   pl.BlockSpec((1, tm // tk, D_VB, tk), lambda a, b: (a, b, 0, 0))),
        out_shape=(jax.ShapeDtypeStruct((g, s, D_QF), BF16),
                   jax.ShapeDtypeStruct((g, s // tk, D_VB, tk), BF16)),
        compiler_params=pltpu.CompilerParams(dimension_semantics=("arbitrary", "arbitrary"),
                                             vmem_limit_bytes=VMEM_LIMIT_V7X),
        name="kv_up")(ckv, kr, wk, ek, wvt)


def _band_units(units, bias_ref, n_pairs=H_A // 2):
    def scores(u, g):
        q, k_of, _, _ = units[u]
        lanes = slice(g * 2 * DH_A, (g + 1) * 2 * DH_A)
        lane = lax.broadcasted_iota(jnp.int32, (q.shape[0], 2 * DH_A), 1)
        k2 = k_of(lanes)
        return [lax.dot_general(k2, jnp.where((lane >= DH_A) == (hl == 1), q[:, lanes], jnp.zeros((), BF16)), _NT,
                                preferred_element_type=F32) for hl in range(2)]

    steps = [(u, g) for u in range(len(units)) for g in range(n_pairs)]
    outs = [[] for _ in units]
    pend = scores(*steps[0])
    for j, (u, g) in enumerate(steps):
        nxt = scores(*steps[j + 1]) if j + 1 < len(steps) else None
        _, _, v_of, b0 = units[u]
        for hl in range(2):
            h = 2 * g + hl
            nk = pend[hl].shape[0]
            s = pend[hl] + bias_ref[h, b0:b0 + nk, :]
            p = jnp.exp2(s - jnp.max(s, axis=0, keepdims=True)).astype(BF16)
            v = jnp.concatenate([v_of(slice(h * DH_A, (h + 1) * DH_A)), jnp.ones((_L_ROWS, nk), BF16)], axis=0)
            o = jnp.dot(v, p, preferred_element_type=F32)
            outs[u].append(o[:DH_A] / o[DH_A:DH_A + 1])
        pend = nxt
    return [jnp.concatenate(o, axis=0).T for o in outs]


def _band_prompt_kernel(q_ref, kp_ref, kc_ref, vp_ref, vc_ref, bias_ref, o_ref, s_scr, *, tm, tq):
    subs = [(lo, lo + tq) for lo in range(0, tm, tq)]
    row = lax.broadcasted_iota(jnp.int32, (PAIR_W, tq), 0)

    def tile(first):
        def window(lo, hi):
            return (hi, tm - lo) if first else (tm - lo + hi, 0)

        def scores(g, j):
            lo, hi = subs[j]
            nk, b0 = window(lo, hi)
            q2 = q_ref[0, g, :, lo:hi]
            qcat = jnp.concatenate([jnp.where(row < DH_A, q2, jnp.zeros((), BF16)),
                                    jnp.where(row >= DH_A, q2, jnp.zeros((), BF16))], axis=1)
            k2 = kc_ref[0, g, :hi, :]
            if not first:
                k2 = jnp.concatenate([kp_ref[0, g, lo:, :], k2], axis=0)
            bias = jnp.concatenate([bias_ref[2 * g + hl, b0:b0 + nk, :] for hl in range(2)], axis=1)
            return jnp.dot(k2, qcat, preferred_element_type=F32) + bias

        def stash(j, s):
            s_scr[j, :s.shape[0], :] = s
            return jnp.max(s, axis=0, keepdims=True)

        def consume(g, j, col_max):
            lo, hi = subs[j]
            nk, _ = window(lo, hi)
            outs = []
            for hl in range(2):
                cols = slice(hl * tq, (hl + 1) * tq)
                p = jnp.exp2(s_scr[j, :nk, cols] - col_max[:, cols]).astype(BF16)
                rows = pl.ds(g * PAIR_W + hl * DH_A, DH_A)
                v = vc_ref[0, 0, rows, :hi]
                if not first:
                    v = jnp.concatenate([vp_ref[0, 0, rows, lo:], v], axis=1)
                v = jnp.concatenate([v, jnp.ones((_L_ROWS, nk), BF16)], axis=0)
                o = jnp.dot(v, p, preferred_element_type=F32)
                outs.append(o[:DH_A] / o[DH_A:DH_A + 1])
            o_ref[0, g, lo:hi, :] = jnp.concatenate(outs, axis=0).T.astype(BF16)

        def body(g, col_max):
            new_max = []
            for j in range(len(subs)):
                nxt = scores(g + 1, j)
                consume(g, j, col_max[j])
                new_max.append(stash(j, nxt))
            return tuple(new_max)

        col_max = tuple(stash(j, scores(0, j)) for j in range(len(subs)))
        for g in range(N_PAIRS - 1):
            col_max = body(g, col_max)
        for j in range(len(subs)):
            consume(N_PAIRS - 1, j, col_max[j])

    pl.when(pl.program_id(1) == 0)(functools.partial(tile, True))
    pl.when(pl.program_id(1) > 0)(functools.partial(tile, False))


def _band_prompt(qa, ka, vat, bias, *, tm, tq):
    g, _, _, s = qa.shape
    cur = lambda a, b: (a, 0, b, 0)
    prev = lambda a, b: (a, 0, jnp.maximum(b - 1, 0), 0)
    prev_v = lambda a, b: (a, jnp.maximum(b - 1, 0), 0, 0)
    pairs = lambda idx: pl.BlockSpec((1, N_PAIRS, tm, PAIR_W), idx)
    return pl.pallas_call(
        functools.partial(_band_prompt_kernel, tm=tm, tq=tq), grid=(g, s // tm),
        in_specs=[pl.BlockSpec((1, N_PAIRS, PAIR_W, tm), lambda a, b: (a, 0, 0, b)), pairs(prev), pairs(cur),
                  pl.BlockSpec((1, 1, D_A, tm), prev_v), pl.BlockSpec((1, 1, D_A, tm), lambda a, b: (a, b, 0, 0)),
                  _resident(bias.shape)],
        out_specs=pairs(cur),
        out_shape=jax.ShapeDtypeStruct((g, N_PAIRS, s, PAIR_W), BF16),
        scratch_shapes=[pltpu.VMEM((tm // tq, tm + tq, 2 * tq), F32)],
        compiler_params=pltpu.CompilerParams(dimension_semantics=("arbitrary", "arbitrary"),
                                             vmem_limit_bytes=VMEM_LIMIT_V7X),
        name="band_prompt")(qa, ka, ka, vat, vat, bias)


def _band_sample_kernel(q_ref, k_ref, vt_ref, bias_ref, o_ref):
    q = jnp.concatenate([q_ref[g, 0] for g in range(N_PAIRS)], axis=1)
    unit = (q, lambda lanes: k_ref[0, :, lanes], lambda rows: vt_ref[0, rows, :], 0)
    o = _band_units([unit], bias_ref)[0].astype(BF16)
    for g in range(N_PAIRS):
        o_ref[g, 0] = o[:, g * PAIR_W:(g + 1) * PAIR_W]


def _band_sample(qa, k_all, vt_all, bias):
    _, g, t, _ = qa.shape
    nk = k_all.shape[1]
    pairs = pl.BlockSpec((N_PAIRS, 1, t, PAIR_W), lambda a: (0, a, 0, 0))
    return pl.pallas_call(
        _band_sample_kernel, grid=(g,),
        in_specs=[pairs,
                  pl.BlockSpec((1, nk, D_A), lambda a: (a, 0, 0)),
                  pl.BlockSpec((1, D_A, nk), lambda a: (a, 0, 0)),
                  _resident(bias.shape)],
        out_specs=pairs,
        out_shape=jax.ShapeDtypeStruct((N_PAIRS, g, t, PAIR_W), BF16),
        compiler_params=pltpu.CompilerParams(dimension_semantics=("arbitrary",),
                                             vmem_limit_bytes=VMEM_LIMIT_V7X),
        name="band_sample")(qa, k_all, vt_all, bias)


def _mla_kernel(q_ref, k_ref, vt_ref, mask_ref, o_ref, s_scr, *, tq, tk, causal):
    nq = q_ref.shape[1]
    nk = k_ref.shape[1] // tk
    ones = jnp.ones((_L_ROWS, tk), BF16)

    def q_rows(i):
        return [q_ref[0, i, hl * HEAD_PAD:(hl + 1) * HEAD_PAD, :] for hl in range(2)]

    def scores(qs, kk, hl):
        krow = pl.multiple_of(kk * tk, tk)
        return jnp.dot(k_ref[0, pl.ds(krow, tk), hl * HEAD_PAD:(hl + 1) * HEAD_PAD], qs[hl],
                       preferred_element_type=F32)

    def stash(hl, s):
        s_scr[hl] = s
        return jnp.max(s, axis=0, keepdims=True)

    def consume(kk, hl, m, acc, tile_max, mask_j):
        if mask_j is not None:
            s = s_scr[hl] + mask_ref[mask_j]
            m_new = jnp.maximum(m, jnp.max(s, axis=0, keepdims=True))
        else:
            s = s_scr[hl]
            m_new = jnp.maximum(m, tile_max)
        p = jnp.exp2(s - m_new).astype(BF16)
        v = jnp.concatenate([vt_ref[0, kk, hl * DV_B:(hl + 1) * DV_B, :], ones], axis=0)
        return m_new, acc * jnp.exp2(m - m_new) + jnp.dot(v, p, preferred_element_type=F32)

    def q_tile(i, first_max):
        qs = q_rows(i)

        def body(t, carry):
            out = []
            for hl in range(2):
                m, acc, tile_max = carry[3 * hl:3 * hl + 3]
                nxt = scores(qs, t + 1, hl)
                out += list(consume(t, hl, m, acc, tile_max, None)) + [stash(hl, nxt)]
            return tuple(out)

        r = tq // tk
        n_full = r * i if causal else nk - 1
        carry = ()
        for hl in range(2):
            carry += (jnp.full((1, tq), NEG, F32), jnp.zeros((DV_B + _L_ROWS, tq), F32), first_max[hl])
        done = 0
        for n in (4, 2, 1):
            left = n_full - done
            trips = left // n if isinstance(left, int) else lax.shift_right_logical(left, n.bit_length() - 1)

            def trip(u, c, n=n, done=done):
                for j in range(n):
                    c = body(done + n * u + j, c)
                return c

            carry = lax.fori_loop(0, trips, trip, carry)
            done = done + trips * n
        tail = [(r * i + j, j) for j in range(r)] if causal else [(nk - 1, None)]
        qs_next = q_rows(jnp.minimum(i + 1, nq - 1)) if nq > 1 else None
        state = [list(carry[3 * hl:3 * hl + 3]) for hl in range(2)]
        for idx, (tile, mask_j) in enumerate(tail):
            for hl in range(2):
                if idx + 1 < len(tail):
                    nxt = scores(qs, tile + 1, hl)
                else:
                    nxt = scores(qs_next, 0, hl) if nq > 1 else None
                m, acc = consume(tile, hl, *state[hl], mask_j)
                state[hl] = [m, acc, stash(hl, nxt) if nxt is not None else state[hl][2]]
        accs = [state[hl][1] for hl in range(2)]
        next_max = [state[hl][2] for hl in range(2)]
        o = jnp.concatenate([acc[:DV_B] / acc[DV_B:DV_B + 1] for acc in accs], axis=0)
        o_ref[0, pl.ds(pl.multiple_of(i * tq, tq), tq), :] = o.T.astype(BF16)
        return tuple(next_max)

    first = q_rows(0)
    lax.fori_loop(0, nq, q_tile, tuple(stash(hl, scores(first, 0, hl)) for hl in range(2)))


def _mla_attn(qf, kf, vt, mask, *, tq, tk, causal):
    g, nq, _, _ = qf.shape
    sq, sk = nq * tq, kf.shape[1]
    return pl.pallas_call(
        functools.partial(_mla_kernel, tq=tq, tk=tk, causal=causal), grid=(g, H_B // 2),
        in_specs=[pl.BlockSpec((1, nq, 2 * HEAD_PAD, tq), lambda a, b: (a, 0, b, 0)),
                  pl.BlockSpec((1, sk, 2 * HEAD_PAD), lambda a, b: (a, 0, b)),
                  pl.BlockSpec((1, sk // tk, 2 * DV_B, tk), lambda a, b: (a, 0, b, 0)),
                  _resident(mask.shape)],
        out_specs=pl.BlockSpec((1, sq, 2 * DV_B), lambda a, b: (a, 0, b)),
        out_shape=jax.ShapeDtypeStruct((g, sq, D_VB), BF16),
        scratch_shapes=[pltpu.VMEM((2, tk, tq), F32)],
        compiler_params=pltpu.CompilerParams(dimension_semantics=("arbitrary", "arbitrary"),
                                             vmem_limit_bytes=VMEM_LIMIT_V7X),
        name="mla_attn")(qf, kf, vt, mask)


_G0 = 8


def _merge_ffn_kernel(x_ref, ya_ref, yb_ref, ga_ref, gb_ref, st_ref, wa_ref, wb_ref, wo_ref, n1_ref, n2_ref, n3_ref,
                      wg_ref, wu_ref, cw_ref, cb_ref, wd_ref, y_ref, cs_ref, gbuf, *, tm, nseq):
    i = pl.program_id(1)
    seg = tm // nseq
    ya = jnp.concatenate([ya_ref[0, g] for g in range(N_PAIRS)], axis=1)
    za = jnp.dot(ya, wa_ref[...], preferred_element_type=F32)
    zb = jnp.dot(yb_ref[0], wb_ref[...], preferred_element_type=F32)
    mix = jax.nn.sigmoid(ga_ref[0].astype(F32)) * za + jax.nn.sigmoid(gb_ref[0].astype(F32)) * zb
    mo = jnp.dot(mix.astype(BF16), wo_ref[...], preferred_element_type=F32)
    x1 = x_ref[0] + _rms(mo, n1_ref[...])
    xn = _rms(x1, n2_ref[...]).astype(BF16)

    starts = [_G0 + j * (seg + _G0) for j in range(nseq)]
    if nseq == 1:
        @pl.when(i == 0)
        def _():
            gbuf[_G0 - 2:_G0, :] = st_ref[0]

        @pl.when(i > 0)
        def _():
            gbuf[_G0 - 2:_G0, :] = gbuf[_G0 + tm - 2:_G0 + tm, :]
    else:
        for j, r0 in enumerate(starts):
            gbuf[r0 - 2:r0, :] = st_ref[j]

    gate = jnp.dot(xn, wg_ref[...], preferred_element_type=F32)
    for j, r0 in enumerate(starts):
        gbuf[r0:r0 + seg, :] = gate[j * seg:(j + 1) * seg]
        cs_ref[j] = gbuf[r0 + seg - 2:r0 + seg, :]
    u = jnp.dot(xn, wu_ref[...], preferred_element_type=F32)
    c = jnp.concatenate([cw_ref[0:1, :] * gbuf[r0 - 2:r0 - 2 + seg, :] + cw_ref[1:2, :] * gbuf[r0 - 1:r0 - 1 + seg, :]
                         + cw_ref[2:3, :] * gbuf[r0:r0 + seg, :] for r0 in starts], axis=0) + cb_ref[...]
    hid = (jax.nn.gelu(c, approximate=True) * u).astype(BF16)
    f = jnp.dot(hid, wd_ref[...], preferred_element_type=F32)
    y_ref[0] = x1 + _rms(f, n3_ref[...])


def _merge_ffn(x, ya, yb, ga, gb, state, wa, wb, wo, n1, n2, n3, wg, wu, cw, cb, wd, *, tm, nseq=1):
    g, s, _ = x.shape
    assert nseq == 1 or s == tm
    tok = lambda w: pl.BlockSpec((1, tm, w), lambda a, b: (a, b, 0))
    per_group = pl.BlockSpec((nseq, CONV_W - 1, D_FF), lambda a, b: (a, 0, 0))
    weights = (wa, wb, wo, n1, n2, n3, wg, wu, cw, cb, wd)
    return pl.pallas_call(
        functools.partial(_merge_ffn_kernel, tm=tm, nseq=nseq), grid=(g, s // tm),
        in_specs=[tok(D_MODEL), pl.BlockSpec((1, N_PAIRS, tm, PAIR_W), lambda a, b: (a, 0, b, 0)), tok(D_VB),
                  tok(D_MODEL), tok(D_MODEL), per_group]
                 + [_resident(w.shape) for w in weights],
        out_specs=(tok(D_MODEL), per_group),
        out_shape=(jax.ShapeDtypeStruct((g, s, D_MODEL), F32),
                   jax.ShapeDtypeStruct((g * nseq, CONV_W - 1, D_FF), F32)),
        scratch_shapes=[pltpu.VMEM((tm + nseq * _G0, D_FF), F32)],
        compiler_params=pltpu.CompilerParams(dimension_semantics=("arbitrary", "arbitrary"),
                                             vmem_limit_bytes=VMEM_LIMIT_V7X),
        name="merge_ffn")(x, ya, yb, ga, gb, state, *weights)


def _rope_tables(pos, scale):
    inv = ROPE_BASE ** (-np.arange(_HALF, dtype=np.float64) / _HALF)
    ang = inv[:, None] * np.asarray(pos, np.float64)[None, :]
    return jnp.asarray(np.cos(ang) * scale, F32), jnp.asarray(np.sin(ang) * scale, F32)


def _band_bias_kernel(ext_ref, o_ref, *, tq, nk):
    w = ext_ref.shape[-1]
    toeplitz = pltpu.roll(jnp.broadcast_to(ext_ref[0], (nk, w)), 0, 1, stride=1, stride_axis=0)[:, :tq]
    qc = (nk - tq + lax.broadcasted_iota(jnp.int32, (nk, tq), 1)) // CHUNK
    kc = lax.broadcasted_iota(jnp.int32, (nk, tq), 0) // CHUNK
    o_ref[0] = jnp.where((kc >= qc - LEFT_CHUNKS) & (kc <= qc), toeplitz, NEG)


def _band_bias(table, tq, nk):
    w = nk + tq
    d_lo, d_hi = 1 - tq, w - tq
    assert -MAX_REL <= d_lo and d_hi >= MAX_REL and w % 128 == 0
    h = table.shape[0]
    ext = jnp.concatenate([table[:, d_lo + MAX_REL:], jnp.broadcast_to(table[:, -1:], (h, d_hi - MAX_REL))], axis=1)
    ext = (jnp.roll(ext, -(nk - 1), axis=1) * LOG2E).reshape(h, 1, w)
    return pl.pallas_call(
        functools.partial(_band_bias_kernel, tq=tq, nk=nk), grid=(h,),
        in_specs=[pl.BlockSpec((1, 1, w), lambda a: (a, 0, 0))],
        out_specs=pl.BlockSpec((1, nk, tq), lambda a: (a, 0, 0)),
        out_shape=jax.ShapeDtypeStruct((h, nk, tq), F32),
        compiler_params=pltpu.CompilerParams(dimension_semantics=("arbitrary",)),
        name="band_bias")(ext)


def _chunk_masks(tk, tq):
    kc = jnp.arange(tq)[:, None] // CHUNK
    qc = jnp.arange(tq)[None, :] // CHUNK
    return jnp.where(kc <= qc, 0.0, NEG).astype(F32).reshape(tq // tk, tk, tq)


def _scaled_cast_kernel(w_ref, o_ref, *, rows, n_scaled, scale):
    r = pl.program_id(0) * rows + lax.broadcasted_iota(jnp.int32, (rows, 1), 0)
    o_ref[...] = (w_ref[0] * jnp.where(r < n_scaled, scale, 1.0)).astype(BF16)


def _scaled_cast(w, n_scaled, scale, *, rows):
    _, d, n = w.shape
    return pl.pallas_call(
        functools.partial(_scaled_cast_kernel, rows=rows, n_scaled=n_scaled, scale=scale), grid=(d // rows,),
        in_specs=[pl.BlockSpec((1, rows, n), lambda a: (0, a, 0))],
        out_specs=pl.BlockSpec((rows, n), lambda a: (a, 0)),
        out_shape=jax.ShapeDtypeStruct((d, n), BF16),
        compiler_params=pltpu.CompilerParams(dimension_semantics=("arbitrary",)),
        name="cast_w_in")(w)


def _prep_weights(w_in, w_uk, w_uv):
    o = np.cumsum((0,) + IN_SIZES)
    d = w_in.shape[1]
    w_t = _scaled_cast(jnp.swapaxes(w_in, 1, 2), int(o[1]), A_SCALE * LOG2E, rows=464)
    row = lambda j: w_t[o[j]:o[j + 1]]
    qn = row(3).reshape(H_B, DH_NOPE, d)
    qr = row(4).reshape(H_B, DH_ROPE, d)
    pad = jnp.zeros((H_B, HEAD_PAD - DH_NOPE - DH_ROPE, d), BF16).at[0].set(row(6))
    wqt = jnp.concatenate([jnp.concatenate([qn, qr, pad], axis=1).reshape(D_QF, d), row(0)], axis=0)
    wm = jnp.concatenate([row(1), row(5), row(7), row(8)], axis=0).T
    wvat = row(2)
    wk = jnp.concatenate([w_uk, jnp.zeros((D_C, H_B, HEAD_PAD - DH_NOPE), F32)], axis=2).reshape(D_C, D_QF).astype(BF16)
    place = jnp.concatenate([jnp.zeros((DH_ROPE, DH_NOPE), F32), jnp.eye(DH_ROPE, dtype=F32),
                             jnp.zeros((DH_ROPE, HEAD_PAD - DH_NOPE - DH_ROPE), F32)], axis=1)
    ek = jnp.tile(place, (1, H_B)).astype(BF16)
    wvt = w_uv.reshape(D_C, D_VB).T.astype(BF16)
    return wm, wqt, wvat, wvat.T, wk, ek, wvt


def kernel(x_prompt, x_sample, cache_a_k, cache_a_v, cache_mla_ckv, cache_mla_krope, state_ffn_conv, norm_mix_pre,
           norm_mix_post, w_in, rel_bias_table, kv_norm, w_uk, w_uv, w_branch_a, w_branch_b, w_out, norm_ffn_pre,
           norm_ffn_post, w_ffn_gate, w_ffn_up, conv_w, conv_b, w_ffn_down):
    assert w_in.shape[0] == 1, "single layer"
    b, s, _ = x_prompt.shape
    db, t, _ = x_sample.shape
    past = cache_mla_ckv.shape[2]
    wcache = cache_a_k.shape[2]
    keep = min(A_WINDOW, s)
    tm1 = 512
    assert keep == tm1 and db * t == tm1 and wcache == A_WINDOW

    wm, wqt, wvat, wva, wk, ek, wvt = _prep_weights(w_in, w_uk[0], w_uv[0])
    row = lambda v: v.reshape(1, -1)
    proj_w = (row(norm_mix_pre[0]), wm, wqt, wvat, wva, row(kv_norm[0]), wk, wvt)
    ffn_w = (w_branch_a[0].astype(BF16), w_branch_b[0].astype(BF16), w_out[0].astype(BF16), row(norm_mix_post[0]),
             row(norm_ffn_pre[0]), row(norm_ffn_post[0]), w_ffn_gate[0].astype(BF16), w_ffn_up[0].astype(BF16),
             conv_w[0], row(conv_b[0]), w_ffn_down[0].astype(BF16))
    table = rel_bias_table[0]

    pos = np.arange(s)
    tabs = _rope_tables(pos, MLA_SCALE * LOG2E) + _rope_tables(pos, 1.0)
    t_mla = tm1
    qa, ka, vat, kaf, vaf, qf, ckv, kr, ga, gb, kf, vt = _in_proj(x_prompt, *proj_w, *tabs, tm=tm1)
    ya = _band_prompt(qa, ka, vat, _band_bias(table, 256, A_WINDOW + 256), tm=tm1, tq=256)
    yb = _mla_attn(qf, kf, vt, _chunk_masks(t_mla, t_mla), tq=t_mla, tk=t_mla, causal=True)
    y_prompt, conv_p = _merge_ffn(x_prompt, ya, yb, ga, gb, jnp.zeros((b, CONV_W - 1, D_FF), F32), *ffn_w, tm=512)

    pos_s = np.tile(past + np.arange(t), db)
    tabs_s = _rope_tables(pos_s, MLA_SCALE * LOG2E) + _rope_tables(pos_s, 1.0)
    qa2, ka2, vat2, kaf2, vaf2, qf2, ckv2, kr2, ga2, gb2, _, _ = _in_proj(x_sample.reshape(1, db * t, D_MODEL), *proj_w,
                                                                    *tabs_s, tm=tm1)
    per_seq = lambda v: v.reshape(db, t, v.shape[-1])
    ckv2, kr2 = per_seq(ckv2[0]), per_seq(kr2[0])
    new_k = jnp.concatenate([cache_a_k[0].reshape(db, wcache, D_A), per_seq(kaf2[0])], axis=1)
    new_v = jnp.concatenate([cache_a_v[0].reshape(db, wcache, D_A), per_seq(vaf2[0])], axis=1)
    qa2 = jnp.transpose(qa2[0].reshape(N_PAIRS, PAIR_W, db, t), (0, 2, 3, 1))
    ya2 = _band_sample(qa2, new_k.astype(BF16), jnp.swapaxes(new_v, 1, 2).astype(BF16),
                       _band_bias(table, t, wcache + t))
    c_all = jnp.concatenate([cache_mla_ckv[0], ckv2], axis=1)
    kr_all = jnp.concatenate([cache_mla_krope[0], kr2], axis=1)
    kf2, vt2 = _kv_up(c_all, kr_all, wk, ek, wvt, tm=past + t, tk=past + t)
    qf2 = jnp.transpose(qf2[0, 0].reshape(D_QF, db, t), (1, 0, 2))[:, None]
    yb2 = _mla_attn(qf2, kf2, vt2, jnp.zeros((1, 8, 128), F32), tq=t, tk=past + t, causal=False)
    y_sample, conv_s = _merge_ffn(x_sample.reshape(1, db * t, D_MODEL), ya2.reshape(1, N_PAIRS, db * t, PAIR_W),
                                  yb2.reshape(1, db * t, D_VB), ga2, gb2, state_ffn_conv[0], *ffn_w,
                                  tm=db * t, nseq=db)
    y_sample = y_sample.reshape(db, t, D_MODEL)

    heads = lambda v: v.reshape(1, v.shape[0], v.shape[1], H_A, DH_A)
    return (y_prompt, y_sample,
            heads(kaf), heads(vaf), ckv[None], kr[None], conv_p[None],
            heads(new_k[:, -wcache:]), heads(new_v[:, -wcache:]), ckv2[None], kr2[None], conv_s[None])
```

```python
import functools

import jax
import jax.numpy as jnp
import numpy as np
from jax import lax
from jax.experimental import pallas as pl
from jax.experimental.pallas import tpu as pltpu

D_MODEL = 1024
CHUNK = 64
LEFT_CHUNKS = 8
A_WINDOW = LEFT_CHUNKS * CHUNK
H_A = 8
DH_A = 64
MAX_REL = 256
H_B = 8
DH_NOPE = 64
DH_ROPE = 32
DV_B = 64
D_C = 256
D_FF = 2816
CONV_W = 3
ROPE_BASE = 10000.0
EPS = 1e-6
A_SCALE = DH_A ** -0.5
MLA_SCALE = (DH_NOPE + DH_ROPE) ** -0.5
IN_SIZES = (H_A * DH_A, H_A * DH_A, H_A * DH_A, H_B * DH_NOPE, H_B * DH_ROPE, D_C, DH_ROPE, D_MODEL, D_MODEL)

HEAD_PAD = 128
D_A = H_A * DH_A
D_VB = H_B * DV_B
D_QF = H_B * HEAD_PAD
PAIR_W = 2 * DH_A
N_PAIRS = H_A // 2
NEG = -1e30
LOG2E = 1.4426950408889634
_L_ROWS = 16
VMEM_LIMIT_V7X = 56 * 1024 * 1024

F32 = jnp.float32
BF16 = jnp.bfloat16
_NT = (((1,), (1,)), ((), ()))


def _resident(shape):
    nd = len(shape)
    return pl.BlockSpec(shape, lambda *_: (0,) * nd, pipeline_mode=pl.Buffered(1))


def _rms(x, w):
    return x * lax.rsqrt(jnp.mean(x * x, axis=-1, keepdims=True) + EPS) * w


_C_KA, _C_CKV, _C_GA, _C_GB, _C_END = 0, 512, 768, 1792, 2816
_KR_ROW = DH_NOPE + DH_ROPE
_HALF = DH_ROPE // 2


def _store_pairs(ref, v):
    for g in range(N_PAIRS):
        ref[0, g] = v[:, g * PAIR_W:(g + 1) * PAIR_W]


def _rope_rows(x, c, s):
    x1, x2 = x[:_HALF], x[_HALF:]
    return [x1 * c - x2 * s, x1 * s + x2 * c]


def _in_proj_kernel(x_ref, nw_ref, wm_ref, wqt_ref, wvt_ref, wva_ref, kvn_ref, wuk_ref, wuvt_ref, cq_ref, sq_ref,
                    ck_ref, sk_ref, qa_ref, ka_ref, vat_ref, kaf_ref, vaf_ref, qf_ref, ckv_ref, kr_ref, ga_ref, gb_ref,
                    kf_ref, vt_ref, *, n):
    i = pl.program_id(1)
    xn = _rms(x_ref[0], nw_ref[...]).astype(BF16)

    def proj(lo, hi):
        return jnp.dot(xn, wm_ref[:, lo:hi], preferred_element_type=F32)

    ka = proj(_C_KA, _C_CKV)
    _store_pairs(ka_ref, ka.astype(BF16))
    kaf_ref[0] = ka
    vat_ref[0, 0] = lax.dot_general(wvt_ref[...], xn, _NT, preferred_element_type=F32).astype(BF16)

    def va_tail():
        vaf_ref[0] = jnp.dot(xn, wva_ref[...], preferred_element_type=F32)

    if n == 1:
        va_tail()
    else:
        pl.when(i == n - 1)(va_tail)

    qt = lax.dot_general(wqt_ref[...], xn, _NT, preferred_element_type=F32)
    for g in range(N_PAIRS):
        qa_ref[0, g] = qt[D_QF + g * PAIR_W:D_QF + (g + 1) * PAIR_W].astype(BF16)
    cq, sq = cq_ref[...], sq_ref[...]
    pad = jnp.zeros((HEAD_PAD - _KR_ROW, qt.shape[1]), F32)
    for h in range(H_B):
        slab = qt[h * HEAD_PAD:(h + 1) * HEAD_PAD]
        rows = [slab[:DH_NOPE] * (MLA_SCALE * LOG2E)] + _rope_rows(slab[DH_NOPE:_KR_ROW], cq, sq) + [pad]
        qf_ref[0, 0, h * HEAD_PAD:(h + 1) * HEAD_PAD, :] = jnp.concatenate(rows, axis=0).astype(BF16)
    kr_t = _rope_rows(qt[_KR_ROW:HEAD_PAD], ck_ref[...], sk_ref[...])
    zero_rows = lambda r: jnp.zeros((r, qt.shape[1]), F32)
    kr_ref[0] = jnp.concatenate(kr_t + [zero_rows(HEAD_PAD - DH_ROPE)], axis=0).T[:, :DH_ROPE]

    ckv = _rms(proj(_C_CKV, _C_GA), kvn_ref[...])
    ckv_ref[0] = ckv
    c_bf = ckv.astype(BF16)
    k_nope = jnp.dot(c_bf, wuk_ref[...], preferred_element_type=F32)
    k_rope = jnp.concatenate([zero_rows(DH_NOPE)] + kr_t + [zero_rows(HEAD_PAD - _KR_ROW)], axis=0).T
    for h in range(H_B):
        slab = slice(h * HEAD_PAD, (h + 1) * HEAD_PAD)
        kf_ref[0, :, slab] = (k_nope[:, slab] + k_rope).astype(BF16)
    vt_ref[0, 0] = lax.dot_general(wuvt_ref[...], c_bf, _NT, preferred_element_type=F32).astype(BF16)
    ga_ref[0] = proj(_C_GA, _C_GB).astype(BF16)
    gb_ref[0] = proj(_C_GB, _C_END).astype(BF16)


def _in_proj(x, nw, wm, wqt, wvt, wva, kvn, wuk, wuvt, cq, sq, ck, sk, *, tm):
    g, s, _ = x.shape
    n = s // tm
    tok = lambda w: pl.BlockSpec((1, tm, w), lambda a, b: (a, b, 0))
    tab = pl.BlockSpec((_HALF, tm), lambda a, b: (0, b))
    tail = pl.BlockSpec((1, tm, D_A), lambda a, b: (a, 0, 0))
    out_shape = (
        jax.ShapeDtypeStruct((g, N_PAIRS, PAIR_W, s), BF16),
        jax.ShapeDtypeStruct((g, N_PAIRS, s, PAIR_W), BF16),
        jax.ShapeDtypeStruct((g, n, D_A, tm), BF16),
        jax.ShapeDtypeStruct((g, tm, D_A), F32),
        jax.ShapeDtypeStruct((g, tm, D_A), F32),
        jax.ShapeDtypeStruct((g, n, D_QF, tm), BF16),
        jax.ShapeDtypeStruct((g, s, D_C), F32),
        jax.ShapeDtypeStruct((g, s, DH_ROPE), F32),
        jax.ShapeDtypeStruct((g, s, D_MODEL), BF16),
        jax.ShapeDtypeStruct((g, s, D_MODEL), BF16),
        jax.ShapeDtypeStruct((g, s, D_QF), BF16),
        jax.ShapeDtypeStruct((g, n, D_VB, tm), BF16),
    )
    pairs = pl.BlockSpec((1, N_PAIRS, tm, PAIR_W), lambda a, b: (a, 0, b, 0))
    pairs_t = pl.BlockSpec((1, N_PAIRS, PAIR_W, tm), lambda a, b: (a, 0, 0, b))
    out_specs = (pairs_t, pairs, pl.BlockSpec((1, 1, D_A, tm), lambda a, b: (a, b, 0, 0)), tail, tail,
                 pl.BlockSpec((1, 1, D_QF, tm), lambda a, b: (a, b, 0, 0)), tok(D_C), tok(DH_ROPE), tok(D_MODEL),
                 tok(D_MODEL), tok(D_QF), pl.BlockSpec((1, 1, D_VB, tm), lambda a, b: (a, b, 0, 0)))
    in_specs = [tok(D_MODEL), _resident(nw.shape), _resident(wm.shape), _resident(wqt.shape), _resident(wvt.shape),
                _resident(wva.shape), _resident(kvn.shape), _resident(wuk.shape), _resident(wuvt.shape), tab, tab, tab,
                tab]
    return pl.pallas_call(
        functools.partial(_in_proj_kernel, n=n), grid=(g, n), in_specs=in_specs, out_specs=out_specs, out_shape=out_shape,
        compiler_params=pltpu.CompilerParams(dimension_semantics=("arbitrary", "arbitrary"),
                                             vmem_limit_bytes=VMEM_LIMIT_V7X),
        name="in_proj")(x, nw, wm, wqt, wvt, wva, kvn, wuk, wuvt, cq, sq, ck, sk)


def _kv_up_kernel(ckv_ref, kr_ref, wk_ref, ek_ref, wvt_ref, kf_ref, vt_ref, *, tk):
    c = ckv_ref[0].astype(BF16)
    kf = jnp.dot(c, wk_ref[...], preferred_element_type=F32)
    kf = kf + jnp.dot(kr_ref[0].astype(BF16), ek_ref[...], preferred_element_type=F32)
    kf_ref[0] = kf.astype(BF16)
    for j in range(c.shape[0] // tk):
        vt_ref[0, j] = lax.dot_general(wvt_ref[...], c[j * tk:(j + 1) * tk], _NT,
                                       preferred_element_type=F32).astype(BF16)


def _kv_up(ckv, kr, wk, ek, wvt, *, tm, tk):
    g, s, _ = ckv.shape
    return pl.pallas_call(
        functools.partial(_kv_up_kernel, tk=tk), grid=(g, s // tm),
        in_specs=[pl.BlockSpec((1, tm, D_C), lambda a, b: (a, b, 0)),
                  pl.BlockSpec((1, tm, DH_ROPE), lambda a, b: (a, b, 0)),
                  _resident(wk.shape), _resident(ek.shape), _resident(wvt.shape)],
        out_specs=(pl.BlockSpec((1, tm, D_QF), lambda a, b: (a, b, 0)),
                   pl.BlockSpec((1, tm // tk, D_VB, tk), lambda a, b: (a, b, 0, 0))),
        out_shape=(jax.ShapeDtypeStruct((g, s, D_QF), BF16),
                   jax.ShapeDtypeStruct((g, s // tk, D_VB, tk), BF16)),
        compiler_params=pltpu.CompilerParams(dimension_semantics=("arbitrary", "arbitrary"),
                                             vmem_limit_bytes=VMEM_LIMIT_V7X),
        name="kv_up")(ckv, kr, wk, ek, wvt)


def _band_units(units, bias_ref, n_pairs=H_A // 2):
    def scores(u, g):
        q, k_of, _, _ = units[u]
        lanes = slice(g * 2 * DH_A, (g + 1) * 2 * DH_A)
        lane = lax.broadcasted_iota(jnp.int32, (q.shape[0], 2 * DH_A), 1)
        k2 = k_of(lanes)
        return [lax.dot_general(k2, jnp.where((lane >= DH_A) == (hl == 1), q[:, lanes], jnp.zeros((), BF16)), _NT,
                                preferred_element_type=F32) for hl in range(2)]

    steps = [(u, g) for u in range(len(units)) for g in range(n_pairs)]
    outs = [[] for _ in units]
    pend = scores(*steps[0])
    for j, (u, g) in enumerate(steps):
        nxt = scores(*steps[j + 1]) if j + 1 < len(steps) else None
        _, _, v_of, b0 = units[u]
        for hl in range(2):
            h = 2 * g + hl
            nk = pend[hl].shape[0]
            s = pend[hl] + bias_ref[h, b0:b0 + nk, :]
            p = jnp.exp2(s - jnp.max(s, axis=0, keepdims=True)).astype(BF16)
            v = jnp.concatenate([v_of(slice(h * DH_A, (h + 1) * DH_A)), jnp.ones((_L_ROWS, nk), BF16)], axis=0)
            o = jnp.dot(v, p, preferred_element_type=F32)
            outs[u].append(o[:DH_A] / o[DH_A:DH_A + 1])
        pend = nxt
    return [jnp.concatenate(o, axis=0).T for o in outs]


def _band_prompt_kernel(q_ref, kp_ref, kc_ref, vp_ref, vc_ref, bias_ref, o_ref, s_scr, *, tm, tq):
    subs = [(lo, lo + tq) for lo in range(0, tm, tq)]
    row = lax.broadcasted_iota(jnp.int32, (PAIR_W, tq), 0)

    def tile(first):
        def window(lo, hi):
            return (hi, tm - lo) if first else (tm - lo + hi, 0)

        def scores(g, j):
            lo, hi = subs[j]
            nk, b0 = window(lo, hi)
            q2 = q_ref[0, g, :, lo:hi]
            qcat = jnp.concatenate([jnp.where(row < DH_A, q2, jnp.zeros((), BF16)),
                                    jnp.where(row >= DH_A, q2, jnp.zeros((), BF16))], axis=1)
            k2 = kc_ref[0, g, :hi, :]
            if not first:
                k2 = jnp.concatenate([kp_ref[0, g, lo:, :], k2], axis=0)
            bias = jnp.concatenate([bias_ref[2 * g + hl, b0:b0 + nk, :] for hl in range(2)], axis=1)
            return jnp.dot(k2, qcat, preferred_element_type=F32) + bias

        def stash(j, s):
            s_scr[j, :s.shape[0], :] = s
            return jnp.max(s, axis=0, keepdims=True)

        def consume(g, j, col_max):
            lo, hi = subs[j]
            nk, _ = window(lo, hi)
            outs = []
            for hl in range(2):
                cols = slice(hl * tq, (hl + 1) * tq)
                p = jnp.exp2(s_scr[j, :nk, cols] - col_max[:, cols]).astype(BF16)
                rows = pl.ds(g * PAIR_W + hl * DH_A, DH_A)
                v = vc_ref[0, 0, rows, :hi]
                if not first:
                    v = jnp.concatenate([vp_ref[0, 0, rows, lo:], v], axis=1)
                v = jnp.concatenate([v, jnp.ones((_L_ROWS, nk), BF16)], axis=0)
                o = jnp.dot(v, p, preferred_element_type=F32)
                outs.append(o[:DH_A] / o[DH_A:DH_A + 1])
            o_ref[0, g, lo:hi, :] = jnp.concatenate(outs, axis=0).T.astype(BF16)

        def body(g, col_max):
            new_max = []
            for j in range(len(subs)):
                nxt = scores(g + 1, j)
                consume(g, j, col_max[j])
                new_max.append(stash(j, nxt))
            return tuple(new_max)

        col_max = tuple(stash(j, scores(0, j)) for j in range(len(subs)))
        for g in range(N_PAIRS - 1):
            col_max = body(g, col_max)
        for j in range(len(subs)):
            consume(N_PAIRS - 1, j, col_max[j])

    pl.when(pl.program_id(1) == 0)(functools.partial(tile, True))
    pl.when(pl.program_id(1) > 0)(functools.partial(tile, False))


def _band_prompt(qa, ka, vat, bias, *, tm, tq):
    g, _, _, s = qa.shape
    cur = lambda a, b: (a, 0, b, 0)
    prev = lambda a, b: (a, 0, jnp.maximum(b - 1, 0), 0)
    prev_v = lambda a, b: (a, jnp.maximum(b - 1, 0), 0, 0)
    pairs = lambda idx: pl.BlockSpec((1, N_PAIRS, tm, PAIR_W), idx)
    return pl.pallas_call(
        functools.partial(_band_prompt_kernel, tm=tm, tq=tq), grid=(g, s // tm),
        in_specs=[pl.BlockSpec((1, N_PAIRS, PAIR_W, tm), lambda a, b: (a, 0, 0, b)), pairs(prev), pairs(cur),
                  pl.BlockSpec((1, 1, D_A, tm), prev_v), pl.BlockSpec((1, 1, D_A, tm), lambda a, b: (a, b, 0, 0)),
                  _resident(bias.shape)],
        out_specs=pairs(cur),
        out_shape=jax.ShapeDtypeStruct((g, N_PAIRS, s, PAIR_W), BF16),
        scratch_shapes=[pltpu.VMEM((tm // tq, tm + tq, 2 * tq), F32)],
        compiler_params=pltpu.CompilerParams(dimension_semantics=("arbitrary", "arbitrary"),
                                             vmem_limit_bytes=VMEM_LIMIT_V7X),
        name="band_prompt")(qa, ka, ka, vat, vat, bias)


def _band_sample_kernel(q_ref, k_ref, vt_ref, bias_ref, o_ref):
    q = jnp.concatenate([q_ref[g, 0] for g in range(N_PAIRS)], axis=1)
    unit = (q, lambda lanes: k_ref[0, :, lanes], lambda rows: vt_ref[0, rows, :], 0)
    o = _band_units([unit], bias_ref)[0].astype(BF16)
    for g in range(N_PAIRS):
        o_ref[g, 0] = o[:, g * PAIR_W:(g + 1) * PAIR_W]


def _band_sample(qa, k_all, vt_all, bias):
    _, g, t, _ = qa.shape
    nk = k_all.shape[1]
    pairs = pl.BlockSpec((N_PAIRS, 1, t, PAIR_W), lambda a: (0, a, 0, 0))
    return pl.pallas_call(
        _band_sample_kernel, grid=(g,),
        in_specs=[pairs,
                  pl.BlockSpec((1, nk, D_A), lambda a: (a, 0, 0)),
                  pl.BlockSpec((1, D_A, nk), lambda a: (a, 0, 0)),
                  _resident(bias.shape)],
        out_specs=pairs,
        out_shape=jax.ShapeDtypeStruct((N_PAIRS, g, t, PAIR_W), BF16),
        compiler_params=pltpu.CompilerParams(dimension_semantics=("arbitrary",),
                                             vmem_limit_bytes=VMEM_LIMIT_V7X),
        name="band_sample")(qa, k_all, vt_all, bias)


def _mla_kernel(q_ref, k_ref, vt_ref, mask_ref, o_ref, s_scr, *, tq, tk, causal, pairs):
    nq = q_ref.shape[1]
    nk = k_ref.shape[1] // tk
    ones = jnp.ones((_L_ROWS, tk), BF16)

    def q_rows(hp, i):
        return [q_ref[0, i, (2 * hp + hl) * HEAD_PAD:(2 * hp + hl + 1) * HEAD_PAD, :] for hl in range(2)]

    def scores(hp, qs, kk, hl):
        krow = pl.multiple_of(kk * tk, tk)
        return jnp.dot(k_ref[0, pl.ds(krow, tk), (2 * hp + hl) * HEAD_PAD:(2 * hp + hl + 1) * HEAD_PAD], qs[hl],
                       preferred_element_type=F32)

    def stash(hl, s):
        s_scr[hl] = s
        return jnp.max(s, axis=0, keepdims=True)

    def consume(hp, kk, hl, m, acc, tile_max, mask_j):
        if mask_j is not None:
            s = s_scr[hl] + mask_ref[mask_j]
            m_new = jnp.maximum(m, jnp.max(s, axis=0, keepdims=True))
        else:
            s = s_scr[hl]
            m_new = jnp.maximum(m, tile_max)
        p = jnp.exp2(s - m_new).astype(BF16)
        v = jnp.concatenate([vt_ref[0, kk, (2 * hp + hl) * DV_B:(2 * hp + hl + 1) * DV_B, :], ones], axis=0)
        return m_new, acc * jnp.exp2(m - m_new) + jnp.dot(v, p, preferred_element_type=F32)

    def q_tile(hp, i, first_max):
        qs = q_rows(hp, i)

        def body(t, carry):
            out = []
            for hl in range(2):
                m, acc, tile_max = carry[3 * hl:3 * hl + 3]
                nxt = scores(hp, qs, t + 1, hl)
                out += list(consume(hp, t, hl, m, acc, tile_max, None)) + [stash(hl, nxt)]
            return tuple(out)

        r = tq // tk
        n_full = r * i if causal else nk - 1
        carry = ()
        for hl in range(2):
            carry += (jnp.full((1, tq), NEG, F32), jnp.zeros((DV_B + _L_ROWS, tq), F32), first_max[hl])
        done = 0
        for n in (4, 2, 1):
            left = n_full - done
            trips = left // n if isinstance(left, int) else lax.shift_right_logical(left, n.bit_length() - 1)

            def trip(u, c, n=n, done=done):
                for j in range(n):
                    c = body(done + n * u + j, c)
                return c

            carry = lax.fori_loop(0, trips, trip, carry)
            done = done + trips * n
        tail = [(r * i + j, j) for j in range(r)] if causal else [(nk - 1, None)]
        qs_next = q_rows(hp, jnp.minimum(i + 1, nq - 1)) if nq > 1 else None
        state = [list(carry[3 * hl:3 * hl + 3]) for hl in range(2)]
        for idx, (tile, mask_j) in enumerate(tail):
            for hl in range(2):
                if idx + 1 < len(tail):
                    nxt = scores(hp, qs, tile + 1, hl)
                else:
                    nxt = scores(hp, qs_next, 0, hl) if nq > 1 else None
                m, acc = consume(hp, tile, hl, *state[hl], mask_j)
                state[hl] = [m, acc, stash(hl, nxt) if nxt is not None else state[hl][2]]
        accs = [state[hl][1] for hl in range(2)]
        next_max = [state[hl][2] for hl in range(2)]
        o = jnp.concatenate([acc[:DV_B] / acc[DV_B:DV_B + 1] for acc in accs], axis=0)
        o_ref[0, pl.ds(pl.multiple_of(i * tq, tq), tq), hp * 2 * DV_B:(hp + 1) * 2 * DV_B] = o.T.astype(BF16)
        return tuple(next_max)

    for hp in range(pairs):
        first = q_rows(hp, 0)
        lax.fori_loop(0, nq, functools.partial(q_tile, hp),
                      tuple(stash(hl, scores(hp, first, 0, hl)) for hl in range(2)))


def _mla_attn(qf, kf, vt, mask, *, tq, tk, causal, pairs=1):
    g, nq, _, _ = qf.shape
    sq, sk = nq * tq, kf.shape[1]
    return pl.pallas_call(
        functools.partial(_mla_kernel, tq=tq, tk=tk, causal=causal, pairs=pairs), grid=(g, H_B // 2 // pairs),
        in_specs=[pl.BlockSpec((1, nq, 2 * pairs * HEAD_PAD, tq), lambda a, b: (a, 0, b, 0)),
                  pl.BlockSpec((1, sk, 2 * pairs * HEAD_PAD), lambda a, b: (a, 0, b)),
                  pl.BlockSpec((1, sk // tk, 2 * pairs * DV_B, tk), lambda a, b: (a, 0, b, 0)),
                  _resident(mask.shape)],
        out_specs=pl.BlockSpec((1, sq, 2 * pairs * DV_B), lambda a, b: (a, 0, b)),
        out_shape=jax.ShapeDtypeStruct((g, sq, D_VB), BF16),
        scratch_shapes=[pltpu.VMEM((2, tk, tq), F32)],
        compiler_params=pltpu.CompilerParams(dimension_semantics=("arbitrary", "arbitrary"),
                                             vmem_limit_bytes=VMEM_LIMIT_V7X),
        name="mla_attn")(qf, kf, vt, mask)


_G0 = 8


def _merge_ffn_kernel(x_ref, ya_ref, yb_ref, ga_ref, gb_ref, st_ref, wa_ref, wb_ref, wo_ref, n1_ref, n2_ref, n3_ref,
                      wg_ref, wu_ref, cw_ref, cb_ref, wd_ref, y_ref, cs_ref, gbuf, *, tm, nseq):
    i = pl.program_id(1)
    seg = tm // nseq
    ya = jnp.concatenate([ya_ref[0, g] for g in range(N_PAIRS)], axis=1)
    za = jnp.dot(ya, wa_ref[...], preferred_element_type=F32)
    zb = jnp.dot(yb_ref[0], wb_ref[...], preferred_element_type=F32)
    mix = jax.nn.sigmoid(ga_ref[0].astype(F32)) * za + jax.nn.sigmoid(gb_ref[0].astype(F32)) * zb
    mo = jnp.dot(mix.astype(BF16), wo_ref[...], preferred_element_type=F32)
    x1 = x_ref[0] + _rms(mo, n1_ref[...])
    xn = _rms(x1, n2_ref[...]).astype(BF16)

    starts = [_G0 + j * (seg + _G0) for j in range(nseq)]
    if nseq == 1:
        @pl.when(i == 0)
        def _():
            gbuf[_G0 - 2:_G0, :] = st_ref[0]

        @pl.when(i > 0)
        def _():
            gbuf[_G0 - 2:_G0, :] = gbuf[_G0 + tm - 2:_G0 + tm, :]
    else:
        for j, r0 in enumerate(starts):
            gbuf[r0 - 2:r0, :] = st_ref[j]

    gate = jnp.dot(xn, wg_ref[...], preferred_element_type=F32)
    for j, r0 in enumerate(starts):
        gbuf[r0:r0 + seg, :] = gate[j * seg:(j + 1) * seg]
        cs_ref[j] = gbuf[r0 + seg - 2:r0 + seg, :]
    u = jnp.dot(xn, wu_ref[...], preferred_element_type=F32)
    c = jnp.concatenate([cw_ref[0:1, :] * gbuf[r0 - 2:r0 - 2 + seg, :] + cw_ref[1:2, :] * gbuf[r0 - 1:r0 - 1 + seg, :]
                         + cw_ref[2:3, :] * gbuf[r0:r0 + seg, :] for r0 in starts], axis=0) + cb_ref[...]
    hid = (jax.nn.gelu(c, approximate=True) * u).astype(BF16)
    f = jnp.dot(hid, wd_ref[...], preferred_element_type=F32)
    y_ref[0] = x1 + _rms(f, n3_ref[...])


def _merge_ffn(x, ya, yb, ga, gb, state, wa, wb, wo, n1, n2, n3, wg, wu, cw, cb, wd, *, tm, nseq=1):
    g, s, _ = x.shape
    assert nseq == 1 or s == tm
    tok = lambda w: pl.BlockSpec((1, tm, w), lambda a, b: (a, b, 0))
    per_group = pl.BlockSpec((nseq, CONV_W - 1, D_FF), lambda a, b: (a, 0, 0))
    weights = (wa, wb, wo, n1, n2, n3, wg, wu, cw, cb, wd)
    return pl.pallas_call(
        functools.partial(_merge_ffn_kernel, tm=tm, nseq=nseq), grid=(g, s // tm),
        in_specs=[tok(D_MODEL), pl.BlockSpec((1, N_PAIRS, tm, PAIR_W), lambda a, b: (a, 0, b, 0)), tok(D_VB),
                  tok(D_MODEL), tok(D_MODEL), per_group]
                 + [_resident(w.shape) for w in weights],
        out_specs=(tok(D_MODEL), per_group),
        out_shape=(jax.ShapeDtypeStruct((g, s, D_MODEL), F32),
                   jax.ShapeDtypeStruct((g * nseq, CONV_W - 1, D_FF), F32)),
        scratch_shapes=[pltpu.VMEM((tm + nseq * _G0, D_FF), F32)],
        compiler_params=pltpu.CompilerParams(dimension_semantics=("arbitrary", "arbitrary"),
                                             vmem_limit_bytes=VMEM_LIMIT_V7X),
        name="merge_ffn")(x, ya, yb, ga, gb, state, *weights)


def _rope_tables(pos, scale):
    inv = ROPE_BASE ** (-np.arange(_HALF, dtype=np.float64) / _HALF)
    ang = inv[:, None] * np.asarray(pos, np.float64)[None, :]
    return jnp.asarray(np.cos(ang) * scale, F32), jnp.asarray(np.sin(ang) * scale, F32)


def _band_bias_kernel(ext_ref, o_ref, *, tq, nk):
    w = ext_ref.shape[-1]
    toeplitz = pltpu.roll(jnp.broadcast_to(ext_ref[0], (nk, w)), 0, 1, stride=1, stride_axis=0)[:, :tq]
    qc = (nk - tq + lax.broadcasted_iota(jnp.int32, (nk, tq), 1)) // CHUNK
    kc = lax.broadcasted_iota(jnp.int32, (nk, tq), 0) // CHUNK
    o_ref[0] = jnp.where((kc >= qc - LEFT_CHUNKS) & (kc <= qc), toeplitz, NEG)


def _band_bias(table, tq, nk):
    w = nk + tq
    d_lo, d_hi = 1 - tq, w - tq
    assert -MAX_REL <= d_lo and d_hi >= MAX_REL and w % 128 == 0
    h = table.shape[0]
    ext = jnp.concatenate([table[:, d_lo + MAX_REL:], jnp.broadcast_to(table[:, -1:], (h, d_hi - MAX_REL))], axis=1)
    ext = (jnp.roll(ext, -(nk - 1), axis=1) * LOG2E).reshape(h, 1, w)
    return pl.pallas_call(
        functools.partial(_band_bias_kernel, tq=tq, nk=nk), grid=(h,),
        in_specs=[pl.BlockSpec((1, 1, w), lambda a: (a, 0, 0))],
        out_specs=pl.BlockSpec((1, nk, tq), lambda a: (a, 0, 0)),
        out_shape=jax.ShapeDtypeStruct((h, nk, tq), F32),
        compiler_params=pltpu.CompilerParams(dimension_semantics=("arbitrary",)),
        name="band_bias")(ext)


def _chunk_masks(tk, tq):
    kc = jnp.arange(tq)[:, None] // CHUNK
    qc = jnp.arange(tq)[None, :] // CHUNK
    return jnp.where(kc <= qc, 0.0, NEG).astype(F32).reshape(tq // tk, tk, tq)


def _scaled_cast_kernel(w_ref, o_ref, *, rows, n_scaled, scale):
    r = pl.program_id(0) * rows + lax.broadcasted_iota(jnp.int32, (rows, 1), 0)
    o_ref[...] = (w_ref[0] * jnp.where(r < n_scaled, scale, 1.0)).astype(BF16)


def _scaled_cast(w, n_scaled, scale, *, rows):
    _, d, n = w.shape
    return pl.pallas_call(
        functools.partial(_scaled_cast_kernel, rows=rows, n_scaled=n_scaled, scale=scale), grid=(d // rows,),
        in_specs=[pl.BlockSpec((1, rows, n), lambda a: (0, a, 0))],
        out_specs=pl.BlockSpec((rows, n), lambda a: (a, 0)),
        out_shape=jax.ShapeDtypeStruct((d, n), BF16),
        compiler_params=pltpu.CompilerParams(dimension_semantics=("arbitrary",)),
        name="cast_w_in")(w)


def _prep_weights(w_in, w_uk, w_uv):
    o = np.cumsum((0,) + IN_SIZES)
    d = w_in.shape[1]
    w_t = _scaled_cast(jnp.swapaxes(w_in, 1, 2), int(o[1]), A_SCALE * LOG2E, rows=464)
    row = lambda j: w_t[o[j]:o[j + 1]]
    qn = row(3).reshape(H_B, DH_NOPE, d)
    qr = row(4).reshape(H_B, DH_ROPE, d)
    pad = jnp.zeros((H_B, HEAD_PAD - DH_NOPE - DH_ROPE, d), BF16).at[0].set(row(6))
    wqt = jnp.concatenate([jnp.concatenate([qn, qr, pad], axis=1).reshape(D_QF, d), row(0)], axis=0)
    wm = jnp.concatenate([row(1), row(5), row(7), row(8)], axis=0).T
    wvat = row(2)
    wk = jnp.concatenate([w_uk, jnp.zeros((D_C, H_B, HEAD_PAD - DH_NOPE), F32)], axis=2).reshape(D_C, D_QF).astype(BF16)
    place = jnp.concatenate([jnp.zeros((DH_ROPE, DH_NOPE), F32), jnp.eye(DH_ROPE, dtype=F32),
                             jnp.zeros((DH_ROPE, HEAD_PAD - DH_NOPE - DH_ROPE), F32)], axis=1)
    ek = jnp.tile(place, (1, H_B)).astype(BF16)
    wvt = w_uv.reshape(D_C, D_VB).T.astype(BF16)
    return wm, wqt, wvat, wvat.T, wk, ek, wvt


def kernel(x_prompt, x_sample, cache_a_k, cache_a_v, cache_mla_ckv, cache_mla_krope, state_ffn_conv, norm_mix_pre,
           norm_mix_post, w_in, rel_bias_table, kv_norm, w_uk, w_uv, w_branch_a, w_branch_b, w_out, norm_ffn_pre,
           norm_ffn_post, w_ffn_gate, w_ffn_up, conv_w, conv_b, w_ffn_down):
    assert w_in.shape[0] == 1, "single layer"
    b, s, _ = x_prompt.shape
    db, t, _ = x_sample.shape
    past = cache_mla_ckv.shape[2]
    wcache = cache_a_k.shape[2]
    keep = min(A_WINDOW, s)
    tm1 = 512
    assert keep == tm1 and db * t == tm1 and wcache == A_WINDOW

    wm, wqt, wvat, wva, wk, ek, wvt = _prep_weights(w_in, w_uk[0], w_uv[0])
    row = lambda v: v.reshape(1, -1)
    proj_w = (row(norm_mix_pre[0]), wm, wqt, wvat, wva, row(kv_norm[0]), wk, wvt)
    ffn_w = (w_branch_a[0].astype(BF16), w_branch_b[0].astype(BF16), w_out[0].astype(BF16), row(norm_mix_post[0]),
             row(norm_ffn_pre[0]), row(norm_ffn_post[0]), w_ffn_gate[0].astype(BF16), w_ffn_up[0].astype(BF16),
             conv_w[0], row(conv_b[0]), w_ffn_down[0].astype(BF16))
    table = rel_bias_table[0]

    pos = np.arange(s)
    tabs = _rope_tables(pos, MLA_SCALE * LOG2E) + _rope_tables(pos, 1.0)
    t_mla = tm1
    qa, ka, vat, kaf, vaf, qf, ckv, kr, ga, gb, kf, vt = _in_proj(x_prompt, *proj_w, *tabs, tm=tm1)
    ya = _band_prompt(qa, ka, vat, _band_bias(table, 256, A_WINDOW + 256), tm=tm1, tq=256)
    yb = _mla_attn(qf, kf, vt, _chunk_masks(t_mla, t_mla), tq=t_mla, tk=t_mla, causal=True)
    y_prompt, conv_p = _merge_ffn(x_prompt, ya, yb, ga, gb, jnp.zeros((b, CONV_W - 1, D_FF), F32), *ffn_w, tm=512)

    pos_s = np.tile(past + np.arange(t), db)
    tabs_s = _rope_tables(pos_s, MLA_SCALE * LOG2E) + _rope_tables(pos_s, 1.0)
    qa2, ka2, vat2, kaf2, vaf2, qf2, ckv2, kr2, ga2, gb2, _, _ = _in_proj(x_sample.reshape(1, db * t, D_MODEL), *proj_w,
                                                                    *tabs_s, tm=tm1)
    per_seq = lambda v: v.reshape(db, t, v.shape[-1])
    ckv2, kr2 = per_seq(ckv2[0]), per_seq(kr2[0])
    new_k = jnp.concatenate([cache_a_k[0].reshape(db, wcache, D_A), per_seq(kaf2[0])], axis=1)
    new_v = jnp.concatenate([cache_a_v[0].reshape(db, wcache, D_A), per_seq(vaf2[0])], axis=1)
    qa2 = jnp.transpose(qa2[0].reshape(N_PAIRS, PAIR_W, db, t), (0, 2, 3, 1))
    ya2 = _band_sample(qa2, new_k.astype(BF16), jnp.swapaxes(new_v, 1, 2).astype(BF16),
                       _band_bias(table, t, wcache + t))
    c_all = jnp.concatenate([cache_mla_ckv[0], ckv2], axis=1)
    kr_all = jnp.concatenate([cache_mla_krope[0], kr2], axis=1)
    kf2, vt2 = _kv_up(c_all, kr_all, wk, ek, wvt, tm=past + t, tk=past + t)
    qf2 = jnp.transpose(qf2[0, 0].reshape(D_QF, db, t), (1, 0, 2))[:, None]
    yb2 = _mla_attn(qf2, kf2, vt2, jnp.zeros((1, 8, 128), F32), tq=t, tk=past + t, causal=False, pairs=H_B // 2)
    y_sample, conv_s = _merge_ffn(x_sample.reshape(1, db * t, D_MODEL), ya2.reshape(1, N_PAIRS, db * t, PAIR_W),
                                  yb2.reshape(1, db * t, D_VB), ga2, gb2, state_ffn_conv[0], *ffn_w,
                                  tm=db * t, nseq=db)
    y_sample = y_sample.reshape(db, t, D_MODEL)

    heads = lambda v: v.reshape(1, v.shape[0], v.shape[1], H_A, DH_A)
    return (y_prompt, y_sample,
            heads(kaf), heads(vaf), ckv[None], kr[None], conv_p[None],
            heads(new_k[:, -wcache:]), heads(new_v[:, -wcache:]), ckv2[None], kr2[None], conv_s[None])
```

```python
import functools

import jax
import jax.numpy as jnp
import numpy as np
from jax import lax
from jax.experimental import pallas as pl
from jax.experimental.pallas import tpu as pltpu

D_MODEL = 1024
CHUNK = 64
LEFT_CHUNKS = 8
A_WINDOW = LEFT_CHUNKS * CHUNK
H_A = 8
DH_A = 64
MAX_REL = 256
H_B = 8
DH_NOPE = 64
DH_ROPE = 32
DV_B = 64
D_C = 256
D_FF = 2816
CONV_W = 3
ROPE_BASE = 10000.0
EPS = 1e-6
A_SCALE = DH_A ** -0.5
MLA_SCALE = (DH_NOPE + DH_ROPE) ** -0.5
IN_SIZES = (H_A * DH_A, H_A * DH_A, H_A * DH_A, H_B * DH_NOPE, H_B * DH_ROPE, D_C, DH_ROPE, D_MODEL, D_MODEL)

HEAD_PAD = 128
D_A = H_A * DH_A
D_VB = H_B * DV_B
D_QF = H_B * HEAD_PAD
PAIR_W = 2 * DH_A
N_PAIRS = H_A // 2
NEG = -1e30
LOG2E = 1.4426950408889634
_L_ROWS = 16
VMEM_LIMIT_V7X = 56 * 1024 * 1024

F32 = jnp.float32
BF16 = jnp.bfloat16
_NT = (((1,), (1,)), ((), ()))


def _resident(shape):
    nd = len(shape)
    return pl.BlockSpec(shape, lambda *_: (0,) * nd, pipeline_mode=pl.Buffered(1))


def _rms(x, w):
    return x * lax.rsqrt(jnp.mean(x * x, axis=-1, keepdims=True) + EPS) * w


_C_KA, _C_CKV, _C_GA, _C_GB, _C_END = 0, 512, 768, 1792, 2816
_KR_ROW = DH_NOPE + DH_ROPE
_HALF = DH_ROPE // 2


def _store_pairs(ref, v):
    for g in range(N_PAIRS):
        ref[0, g] = v[:, g * PAIR_W:(g + 1) * PAIR_W]


def _rope_rows(x, c, s):
    x1, x2 = x[:_HALF], x[_HALF:]
    return [x1 * c - x2 * s, x1 * s + x2 * c]


def _in_proj_kernel(x_ref, nw_ref, wm_ref, wqt_ref, wvt_ref, wva_ref, kvn_ref, wuk_ref, wuvt_ref, cq_ref, sq_ref,
                    ck_ref, sk_ref, qa_ref, ka_ref, vat_ref, kaf_ref, vaf_ref, qf_ref, ckv_ref, kr_ref, ga_ref, gb_ref,
                    kf_ref, vt_ref, *, n):
    i = pl.program_id(1)
    xn = _rms(x_ref[0], nw_ref[...]).astype(BF16)

    def proj(lo, hi):
        return jnp.dot(xn, wm_ref[:, lo:hi], preferred_element_type=F32)

    ka = proj(_C_KA, _C_CKV)
    _store_pairs(ka_ref, ka.astype(BF16))
    kaf_ref[0] = ka
    vat_ref[0, 0] = lax.dot_general(wvt_ref[...], xn, _NT, preferred_element_type=F32).astype(BF16)

    def va_tail():
        vaf_ref[0] = jnp.dot(xn, wva_ref[...], preferred_element_type=F32)

    if n == 1:
        va_tail()
    else:
        pl.when(i == n - 1)(va_tail)

    qt = lax.dot_general(wqt_ref[...], xn, _NT, preferred_element_type=F32)
    for g in range(N_PAIRS):
        qa_ref[0, g] = qt[D_QF + g * PAIR_W:D_QF + (g + 1) * PAIR_W].astype(BF16)
    cq, sq = cq_ref[...], sq_ref[...]
    pad = jnp.zeros((HEAD_PAD - _KR_ROW, qt.shape[1]), F32)
    for h in range(H_B):
        slab = qt[h * HEAD_PAD:(h + 1) * HEAD_PAD]
        rows = [slab[:DH_NOPE] * (MLA_SCALE * LOG2E)] + _rope_rows(slab[DH_NOPE:_KR_ROW], cq, sq) + [pad]
        qf_ref[0, 0, h * HEAD_PAD:(h + 1) * HEAD_PAD, :] = jnp.concatenate(rows, axis=0).astype(BF16)
    kr_t = _rope_rows(qt[_KR_ROW:HEAD_PAD], ck_ref[...], sk_ref[...])
    zero_rows = lambda r: jnp.zeros((r, qt.shape[1]), F32)
    kr_ref[0] = jnp.concatenate(kr_t + [zero_rows(HEAD_PAD - DH_ROPE)], axis=0).T[:, :DH_ROPE]

    ckv = _rms(proj(_C_CKV, _C_GA), kvn_ref[...])
    ckv_ref[0] = ckv
    c_bf = ckv.astype(BF16)
    k_nope = jnp.dot(c_bf, wuk_ref[...], preferred_element_type=F32)
    k_rope = jnp.concatenate([zero_rows(DH_NOPE)] + kr_t + [zero_rows(HEAD_PAD - _KR_ROW)], axis=0).T
    for h in range(H_B):
        slab = slice(h * HEAD_PAD, (h + 1) * HEAD_PAD)
        kf_ref[0, :, slab] = (k_nope[:, slab] + k_rope).astype(BF16)
    vt_ref[0, 0] = lax.dot_general(wuvt_ref[...], c_bf, _NT, preferred_element_type=F32).astype(BF16)
    ga_ref[0] = proj(_C_GA, _C_GB).astype(BF16)
    gb_ref[0] = proj(_C_GB, _C_END).astype(BF16)


def _in_proj(x, nw, wm, wqt, wvt, wva, kvn, wuk, wuvt, cq, sq, ck, sk, *, tm):
    g, s, _ = x.shape
    n = s // tm
    tok = lambda w: pl.BlockSpec((1, tm, w), lambda a, b: (a, b, 0))
    tab = pl.BlockSpec((_HALF, tm), lambda a, b: (0, b))
    tail = pl.BlockSpec((1, tm, D_A), lambda a, b: (a, 0, 0))
    out_shape = (
        jax.ShapeDtypeStruct((g, N_PAIRS, PAIR_W, s), BF16),
        jax.ShapeDtypeStruct((g, N_PAIRS, s, PAIR_W), BF16),
        jax.ShapeDtypeStruct((g, n, D_A, tm), BF16),
        jax.ShapeDtypeStruct((g, tm, D_A), F32),
        jax.ShapeDtypeStruct((g, tm, D_A), F32),
        jax.ShapeDtypeStruct((g, n, D_QF, tm), BF16),
        jax.ShapeDtypeStruct((g, s, D_C), F32),
        jax.ShapeDtypeStruct((g, s, DH_ROPE), F32),
        jax.ShapeDtypeStruct((g, s, D_MODEL), BF16),
        jax.ShapeDtypeStruct((g, s, D_MODEL), BF16),
        jax.ShapeDtypeStruct((g, s, D_QF), BF16),
        jax.ShapeDtypeStruct((g, n, D_VB, tm), BF16),
    )
    pairs = pl.BlockSpec((1, N_PAIRS, tm, PAIR_W), lambda a, b: (a, 0, b, 0))
    pairs_t = pl.BlockSpec((1, N_PAIRS, PAIR_W, tm), lambda a, b: (a, 0, 0, b))
    out_specs = (pairs_t, pairs, pl.BlockSpec((1, 1, D_A, tm), lambda a, b: (a, b, 0, 0)), tail, tail,
                 pl.BlockSpec((1, 1, D_QF, tm), lambda a, b: (a, b, 0, 0)), tok(D_C), tok(DH_ROPE), tok(D_MODEL),
                 tok(D_MODEL), tok(D_QF), pl.BlockSpec((1, 1, D_VB, tm), lambda a, b: (a, b, 0, 0)))
    in_specs = [tok(D_MODEL), _resident(nw.shape), _resident(wm.shape), _resident(wqt.shape), _resident(wvt.shape),
                _resident(wva.shape), _resident(kvn.shape), _resident(wuk.shape), _resident(wuvt.shape), tab, tab, tab,
                tab]
    return pl.pallas_call(
        functools.partial(_in_proj_kernel, n=n), grid=(g, n), in_specs=in_specs, out_specs=out_specs, out_shape=out_shape,
        compiler_params=pltpu.CompilerParams(dimension_semantics=("arbitrary", "arbitrary"),
                                             vmem_limit_bytes=VMEM_LIMIT_V7X),
        name="in_proj")(x, nw, wm, wqt, wvt, wva, kvn, wuk, wuvt, cq, sq, ck, sk)


def _kv_up_kernel(ckv_ref, kr_ref, wk_ref, ek_ref, wvt_ref, kf_ref, vt_ref, *, tk):
    c = ckv_ref[0].astype(BF16)
    kf = jnp.dot(c, wk_ref[...], preferred_element_type=F32)
    kf = kf + jnp.dot(kr_ref[0].astype(BF16), ek_ref[...], preferred_element_type=F32)
    kf_ref[0] = kf.astype(BF16)
    for j in range(c.shape[0] // tk):
        vt_ref[0, j] = lax.dot_general(wvt_ref[...], c[j * tk:(j + 1) * tk], _NT,
                                       preferred_element_type=F32).astype(BF16)


def _kv_up(ckv, kr, wk, ek, wvt, *, tm, tk):
    g, s, _ = ckv.shape
    return pl.pallas_call(
        functools.partial(_kv_up_kernel, tk=tk), grid=(g, s // tm),
        in_specs=[pl.BlockSpec((1, tm, D_C), lambda a, b: (a, b, 0)),
                  pl.BlockSpec((1, tm, DH_ROPE), lambda a, b: (a, b, 0)),
                  _resident(wk.shape), _resident(ek.shape), _resident(wvt.shape)],
        out_specs=(pl.BlockSpec((1, tm, D_QF), lambda a, b: (a, b, 0)),
                   pl.BlockSpec((1, tm // tk, D_VB, tk), lambda a, b: (a, b, 0, 0))),
        out_shape=(jax.ShapeDtypeStruct((g, s, D_QF), BF16),
                   jax.ShapeDtypeStruct((g, s // tk, D_VB, tk), BF16)),
        compiler_params=pltpu.CompilerParams(dimension_semantics=("arbitrary", "arbitrary"),
                                             vmem_limit_bytes=VMEM_LIMIT_V7X),
        name="kv_up")(ckv, kr, wk, ek, wvt)


def _band_units(units, bias_ref, n_pairs=H_A // 2):
    def scores(u, g):
        q, k_of, _, _ = units[u]
        lanes = slice(g * 2 * DH_A, (g + 1) * 2 * DH_A)
        lane = lax.broadcasted_iota(jnp.int32, (q.shape[0], 2 * DH_A), 1)
        k2 = k_of(lanes)
        return [lax.dot_general(k2, jnp.where((lane >= DH_A) == (hl == 1), q[:, lanes], jnp.zeros((), BF16)), _NT,
                                preferred_element_type=F32) for hl in range(2)]

    steps = [(u, g) for u in range(len(units)) for g in range(n_pairs)]
    outs = [[] for _ in units]
    pend = scores(*steps[0])
    for j, (u, g) in enumerate(steps):
        nxt = scores(*steps[j + 1]) if j + 1 < len(steps) else None
        _, _, v_of, b0 = units[u]
        for hl in range(2):
            h = 2 * g + hl
            nk = pend[hl].shape[0]
            s = pend[hl] + bias_ref[h, b0:b0 + nk, :]
            p = jnp.exp2(s - jnp.max(s, axis=0, keepdims=True)).astype(BF16)
            v = jnp.concatenate([v_of(slice(h * DH_A, (h + 1) * DH_A)), jnp.ones((_L_ROWS, nk), BF16)], axis=0)
            o = jnp.dot(v, p, preferred_element_type=F32)
            outs[u].append(o[:DH_A] / o[DH_A:DH_A + 1])
        pend = nxt
    return [jnp.concatenate(o, axis=0).T for o in outs]


def _band_prompt_kernel(q_ref, kp_ref, kc_ref, vp_ref, vc_ref, bias_ref, o_ref, s_scr, *, tm, tq):
    subs = [(lo, lo + tq) for lo in range(0, tm, tq)]
    row = lax.broadcasted_iota(jnp.int32, (PAIR_W, tq), 0)

    def tile(first):
        def window(lo, hi):
            return (hi, tm - lo) if first else (tm - lo + hi, 0)

        def scores(g, j):
            lo, hi = subs[j]
            nk, b0 = window(lo, hi)
            q2 = q_ref[0, g, :, lo:hi]
            qcat = jnp.concatenate([jnp.where(row < DH_A, q2, jnp.zeros((), BF16)),
                                    jnp.where(row >= DH_A, q2, jnp.zeros((), BF16))], axis=1)
            k2 = kc_ref[0, g, :hi, :]
            if not first:
                k2 = jnp.concatenate([kp_ref[0, g, lo:, :], k2], axis=0)
            bias = jnp.concatenate([bias_ref[2 * g + hl, b0:b0 + nk, :] for hl in range(2)], axis=1)
            return jnp.dot(k2, qcat, preferred_element_type=F32) + bias

        def stash(j, s):
            s_scr[j, :s.shape[0], :] = s
            return jnp.max(s, axis=0, keepdims=True)

        def consume(g, j, col_max):
            lo, hi = subs[j]
            nk, _ = window(lo, hi)
            outs = []
            for hl in range(2):
                cols = slice(hl * tq, (hl + 1) * tq)
                p = jnp.exp2(s_scr[j, :nk, cols] - col_max[:, cols]).astype(BF16)
                rows = pl.ds(g * PAIR_W + hl * DH_A, DH_A)
                v = vc_ref[0, 0, rows, :hi]
                if not first:
                    v = jnp.concatenate([vp_ref[0, 0, rows, lo:], v], axis=1)
                v = jnp.concatenate([v, jnp.ones((_L_ROWS, nk), BF16)], axis=0)
                o = jnp.dot(v, p, preferred_element_type=F32)
                outs.append(o[:DH_A] / o[DH_A:DH_A + 1])
            o_ref[0, g, lo:hi, :] = jnp.concatenate(outs, axis=0).T.astype(BF16)

        def body(g, col_max):
            new_max = []
            for j in range(len(subs)):
                nxt = scores(g + 1, j)
                consume(g, j, col_max[j])
                new_max.append(stash(j, nxt))
            return tuple(new_max)

        col_max = tuple(stash(j, scores(0, j)) for j in range(len(subs)))
        for g in range(N_PAIRS - 1):
            col_max = body(g, col_max)
        for j in range(len(subs)):
            consume(N_PAIRS - 1, j, col_max[j])

    pl.when(pl.program_id(1) == 0)(functools.partial(tile, True))
    pl.when(pl.program_id(1) > 0)(functools.partial(tile, False))


def _band_prompt(qa, ka, vat, bias, *, tm, tq):
    g, _, _, s = qa.shape
    cur = lambda a, b: (a, 0, b, 0)
    prev = lambda a, b: (a, 0, jnp.maximum(b - 1, 0), 0)
    prev_v = lambda a, b: (a, jnp.maximum(b - 1, 0), 0, 0)
    pairs = lambda idx: pl.BlockSpec((1, N_PAIRS, tm, PAIR_W), idx)
    return pl.pallas_call(
        functools.partial(_band_prompt_kernel, tm=tm, tq=tq), grid=(g, s // tm),
        in_specs=[pl.BlockSpec((1, N_PAIRS, PAIR_W, tm), lambda a, b: (a, 0, 0, b)), pairs(prev), pairs(cur),
                  pl.BlockSpec((1, 1, D_A, tm), prev_v), pl.BlockSpec((1, 1, D_A, tm), lambda a, b: (a, b, 0, 0)),
                  _resident(bias.shape)],
        out_specs=pairs(cur),
        out_shape=jax.ShapeDtypeStruct((g, N_PAIRS, s, PAIR_W), BF16),
        scratch_shapes=[pltpu.VMEM((tm // tq, tm + tq, 2 * tq), F32)],
        compiler_params=pltpu.CompilerParams(dimension_semantics=("arbitrary", "arbitrary"),
                                             vmem_limit_bytes=VMEM_LIMIT_V7X),
        name="band_prompt")(qa, ka, ka, vat, vat, bias)


def _band_sample_kernel(q_ref, k_ref, vt_ref, bias_ref, o_ref):
    q = jnp.concatenate([q_ref[g, 0] for g in range(N_PAIRS)], axis=1)
    unit = (q, lambda lanes: k_ref[0, :, lanes], lambda rows: vt_ref[0, rows, :], 0)
    o = _band_units([unit], bias_ref)[0].astype(BF16)
    for g in range(N_PAIRS):
        o_ref[g, 0] = o[:, g * PAIR_W:(g + 1) * PAIR_W]


def _band_sample(qa, k_all, vt_all, bias):
    _, g, t, _ = qa.shape
    nk = k_all.shape[1]
    pairs = pl.BlockSpec((N_PAIRS, 1, t, PAIR_W), lambda a: (0, a, 0, 0))
    return pl.pallas_call(
        _band_sample_kernel, grid=(g,),
        in_specs=[pairs,
                  pl.BlockSpec((1, nk, D_A), lambda a: (a, 0, 0)),
                  pl.BlockSpec((1, D_A, nk), lambda a: (a, 0, 0)),
                  _resident(bias.shape)],
        out_specs=pairs,
        out_shape=jax.ShapeDtypeStruct((N_PAIRS, g, t, PAIR_W), BF16),
        compiler_params=pltpu.CompilerParams(dimension_semantics=("arbitrary",),
                                             vmem_limit_bytes=VMEM_LIMIT_V7X),
        name="band_sample")(qa, k_all, vt_all, bias)


def _mla_kernel(q_ref, k_ref, vt_ref, mask_ref, o_ref, s_scr, *, tq, tk, causal, pairs):
    nq = q_ref.shape[1]
    nk = k_ref.shape[1] // tk
    ones = jnp.ones((_L_ROWS, tk), BF16)

    def q_rows(hp, i):
        return [q_ref[0, i, (2 * hp + hl) * HEAD_PAD:(2 * hp + hl + 1) * HEAD_PAD, :] for hl in range(2)]

    def scores(hp, qs, kk, hl):
        krow = pl.multiple_of(kk * tk, tk)
        return jnp.dot(k_ref[0, pl.ds(krow, tk), (2 * hp + hl) * HEAD_PAD:(2 * hp + hl + 1) * HEAD_PAD], qs[hl],
                       preferred_element_type=F32)

    def stash(hl, s):
        s_scr[hl] = s
        return jnp.max(s, axis=0, keepdims=True)

    def consume(hp, kk, hl, m, acc, tile_max, mask_j):
        if mask_j is not None:
            s = s_scr[hl] + mask_ref[mask_j]
            m_new = jnp.maximum(m, jnp.max(s, axis=0, keepdims=True))
        else:
            s = s_scr[hl]
            m_new = jnp.maximum(m, tile_max)
        p = jnp.exp2(s - m_new).astype(BF16)
        v = jnp.concatenate([vt_ref[0, kk, (2 * hp + hl) * DV_B:(2 * hp + hl + 1) * DV_B, :], ones], axis=0)
        return m_new, acc * jnp.exp2(m - m_new) + jnp.dot(v, p, preferred_element_type=F32)

    def q_tile(hp, i, first_max):
        qs = q_rows(hp, i)

        def body(t, carry):
            out = []
            for hl in range(2):
                m, acc, tile_max = carry[3 * hl:3 * hl + 3]
                nxt = scores(hp, qs, t + 1, hl)
                out += list(consume(hp, t, hl, m, acc, tile_max, None)) + [stash(hl, nxt)]
            return tuple(out)

        r = tq // tk
        n_full = r * i if causal else nk - 1
        carry = ()
        for hl in range(2):
            carry += (jnp.full((1, tq), NEG, F32), jnp.zeros((DV_B + _L_ROWS, tq), F32), first_max[hl])
        done = 0
        for n in (4, 2, 1):
            left = n_full - done
            trips = left // n if isinstance(left, int) else lax.shift_right_logical(left, n.bit_length() - 1)

            def trip(u, c, n=n, done=done):
                for j in range(n):
                    c = body(done + n * u + j, c)
                return c

            carry = lax.fori_loop(0, trips, trip, carry)
            done = done + trips * n
        tail = [(r * i + j, j) for j in range(r)] if causal else [(nk - 1, None)]
        qs_next = q_rows(hp, jnp.minimum(i + 1, nq - 1)) if nq > 1 else None
        state = [list(carry[3 * hl:3 * hl + 3]) for hl in range(2)]
        for idx, (tile, mask_j) in enumerate(tail):
            for hl in range(2):
                if idx + 1 < len(tail):
                    nxt = scores(hp, qs, tile + 1, hl)
                else:
                    nxt = scores(hp, qs_next, 0, hl) if nq > 1 else None
                m, acc = consume(hp, tile, hl, *state[hl], mask_j)
                state[hl] = [m, acc, stash(hl, nxt) if nxt is not None else state[hl][2]]
        accs = [state[hl][1] for hl in range(2)]
        next_max = [state[hl][2] for hl in range(2)]
        o = jnp.concatenate([acc[:DV_B] / acc[DV_B:DV_B + 1] for acc in accs], axis=0)
        o_ref[0, pl.ds(pl.multiple_of(i * tq, tq), tq), hp * 2 * DV_B:(hp + 1) * 2 * DV_B] = o.T.astype(BF16)
        return tuple(next_max)

    for hp in range(pairs):
        first = q_rows(hp, 0)
        lax.fori_loop(0, nq, functools.partial(q_tile, hp),
                      tuple(stash(hl, scores(hp, first, 0, hl)) for hl in range(2)))


def _mla_attn(qf, kf, vt, mask, *, tq, tk, causal, pairs=1):
    g, nq, _, _ = qf.shape
    sq, sk = nq * tq, kf.shape[1]
    return pl.pallas_call(
        functools.partial(_mla_kernel, tq=tq, tk=tk, causal=causal, pairs=pairs), grid=(g, H_B // 2 // pairs),
        in_specs=[pl.BlockSpec((1, nq, 2 * pairs * HEAD_PAD, tq), lambda a, b: (a, 0, b, 0)),
                  pl.BlockSpec((1, sk, 2 * pairs * HEAD_PAD), lambda a, b: (a, 0, b)),
                  pl.BlockSpec((1, sk // tk, 2 * pairs * DV_B, tk), lambda a, b: (a, 0, b, 0)),
                  _resident(mask.shape)],
        out_specs=pl.BlockSpec((1, sq, 2 * pairs * DV_B), lambda a, b: (a, 0, b)),
        out_shape=jax.ShapeDtypeStruct((g, sq, D_VB), BF16),
        scratch_shapes=[pltpu.VMEM((2, tk, tq), F32)],
        compiler_params=pltpu.CompilerParams(dimension_semantics=("arbitrary", "arbitrary"),
                                             vmem_limit_bytes=VMEM_LIMIT_V7X),
        name="mla_attn")(qf, kf, vt, mask)


_G0 = 8


def _merge_ffn_kernel(x_ref, ya_ref, yb_ref, ga_ref, gb_ref, st_ref, wa_ref, wb_ref, wo_ref, n1_ref, n2_ref, n3_ref,
                      wgu_ref, cw_ref, cb_ref, wd_ref, y_ref, cs_ref, gbuf, *, tm, nseq):
    i = pl.program_id(1)
    seg = tm // nseq
    parts = [(0, tm // 2), (tm // 2, tm)] if nseq == 1 else [(0, tm)]

    mixes = []
    for lo, hi in parts:
        ya = jnp.concatenate([ya_ref[0, g, lo:hi, :] for g in range(N_PAIRS)], axis=1)
        za = jnp.dot(ya, wa_ref[...], preferred_element_type=F32)
        zb = jnp.dot(yb_ref[0, lo:hi, :], wb_ref[...], preferred_element_type=F32)
        mixes.append((jax.nn.sigmoid(ga_ref[0, lo:hi, :].astype(F32)) * za
                      + jax.nn.sigmoid(gb_ref[0, lo:hi, :].astype(F32)) * zb).astype(BF16))
    x1s, xns = [], []
    for (lo, hi), mix in zip(parts, mixes):
        mo = jnp.dot(mix, wo_ref[...], preferred_element_type=F32)
        x1s.append(x_ref[0, lo:hi, :] + _rms(mo, n1_ref[...]))
        xns.append(_rms(x1s[-1], n2_ref[...]).astype(BF16))

    starts = [_G0 + j * (seg + _G0) for j in range(nseq)]
    if nseq == 1:
        @pl.when(i == 0)
        def _():
            gbuf[_G0 - 2:_G0, :] = st_ref[0]

        @pl.when(i > 0)
        def _():
            gbuf[_G0 - 2:_G0, :] = gbuf[_G0 + tm - 2:_G0 + tm, :]
    else:
        for j, r0 in enumerate(starts):
            gbuf[r0 - 2:r0, :] = st_ref[j]

    ups = []
    for (lo, hi), xn in zip(parts, xns):
        gu = jnp.dot(xn, wgu_ref[...], preferred_element_type=F32)
        gate = gu[:, :D_FF]
        if nseq == 1:
            gbuf[_G0 + lo:_G0 + hi, :] = gate
        else:
            for j, r0 in enumerate(starts):
                gbuf[r0:r0 + seg, :] = gate[j * seg:(j + 1) * seg]
        ups.append(gu[:, D_FF:])
    for j, r0 in enumerate(starts):
        cs_ref[j] = gbuf[r0 + seg - 2:r0 + seg, :]

    def conv(r0, rows):
        return (cw_ref[0:1, :] * gbuf[r0 - 2:r0 - 2 + rows, :] + cw_ref[1:2, :] * gbuf[r0 - 1:r0 - 1 + rows, :]
                + cw_ref[2:3, :] * gbuf[r0:r0 + rows, :])

    for (lo, hi), u, x1 in zip(parts, ups, x1s):
        if nseq == 1:
            c = conv(_G0 + lo, hi - lo)
        else:
            c = jnp.concatenate([conv(r0, seg) for r0 in starts], axis=0)
        hid = (jax.nn.gelu(c + cb_ref[...], approximate=True) * u).astype(BF16)
        f = jnp.dot(hid, wd_ref[...], preferred_element_type=F32)
        y_ref[0, lo:hi, :] = x1 + _rms(f, n3_ref[...])


def _merge_ffn(x, ya, yb, ga, gb, state, wa, wb, wo, n1, n2, n3, wgu, cw, cb, wd, *, tm, nseq=1):
    g, s, _ = x.shape
    assert nseq == 1 or s == tm
    tok = lambda w: pl.BlockSpec((1, tm, w), lambda a, b: (a, b, 0))
    per_group = pl.BlockSpec((nseq, CONV_W - 1, D_FF), lambda a, b: (a, 0, 0))
    weights = (wa, wb, wo, n1, n2, n3, wgu, cw, cb, wd)
    return pl.pallas_call(
        functools.partial(_merge_ffn_kernel, tm=tm, nseq=nseq), grid=(g, s // tm),
        in_specs=[tok(D_MODEL), pl.BlockSpec((1, N_PAIRS, tm, PAIR_W), lambda a, b: (a, 0, b, 0)), tok(D_VB),
                  tok(D_MODEL), tok(D_MODEL), per_group]
                 + [_resident(w.shape) for w in weights],
        out_specs=(tok(D_MODEL), per_group),
        out_shape=(jax.ShapeDtypeStruct((g, s, D_MODEL), F32),
                   jax.ShapeDtypeStruct((g * nseq, CONV_W - 1, D_FF), F32)),
        scratch_shapes=[pltpu.VMEM((tm + nseq * _G0, D_FF), F32)],
        compiler_params=pltpu.CompilerParams(dimension_semantics=("arbitrary", "arbitrary"),
                                             vmem_limit_bytes=VMEM_LIMIT_V7X),
        name="merge_ffn")(x, ya, yb, ga, gb, state, *weights)


def _rope_tables(pos, scale):
    inv = ROPE_BASE ** (-np.arange(_HALF, dtype=np.float64) / _HALF)
    ang = inv[:, None] * np.asarray(pos, np.float64)[None, :]
    return jnp.asarray(np.cos(ang) * scale, F32), jnp.asarray(np.sin(ang) * scale, F32)


def _band_bias_kernel(ext_ref, o_ref, *, tq, nk):
    w = ext_ref.shape[-1]
    toeplitz = pltpu.roll(jnp.broadcast_to(ext_ref[0], (nk, w)), 0, 1, stride=1, stride_axis=0)[:, :tq]
    qc = (nk - tq + lax.broadcasted_iota(jnp.int32, (nk, tq), 1)) // CHUNK
    kc = lax.broadcasted_iota(jnp.int32, (nk, tq), 0) // CHUNK
    o_ref[0] = jnp.where((kc >= qc - LEFT_CHUNKS) & (kc <= qc), toeplitz, NEG)


def _band_bias(table, tq, nk):
    w = nk + tq
    d_lo, d_hi = 1 - tq, w - tq
    assert -MAX_REL <= d_lo and d_hi >= MAX_REL and w % 128 == 0
    h = table.shape[0]
    ext = jnp.concatenate([table[:, d_lo + MAX_REL:], jnp.broadcast_to(table[:, -1:], (h, d_hi - MAX_REL))], axis=1)
    ext = (jnp.roll(ext, -(nk - 1), axis=1) * LOG2E).reshape(h, 1, w)
    return pl.pallas_call(
        functools.partial(_band_bias_kernel, tq=tq, nk=nk), grid=(h,),
        in_specs=[pl.BlockSpec((1, 1, w), lambda a: (a, 0, 0))],
        out_specs=pl.BlockSpec((1, nk, tq), lambda a: (a, 0, 0)),
        out_shape=jax.ShapeDtypeStruct((h, nk, tq), F32),
        compiler_params=pltpu.CompilerParams(dimension_semantics=("arbitrary",)),
        name="band_bias")(ext)


def _chunk_masks(tk, tq):
    kc = jnp.arange(tq)[:, None] // CHUNK
    qc = jnp.arange(tq)[None, :] // CHUNK
    return jnp.where(kc <= qc, 0.0, NEG).astype(F32).reshape(tq // tk, tk, tq)


def _scaled_cast_kernel(w_ref, o_ref, *, rows, n_scaled, scale):
    r = pl.program_id(0) * rows + lax.broadcasted_iota(jnp.int32, (rows, 1), 0)
    o_ref[...] = (w_ref[0] * jnp.where(r < n_scaled, scale, 1.0)).astype(BF16)


def _scaled_cast(w, n_scaled, scale, *, rows):
    _, d, n = w.shape
    return pl.pallas_call(
        functools.partial(_scaled_cast_kernel, rows=rows, n_scaled=n_scaled, scale=scale), grid=(d // rows,),
        in_specs=[pl.BlockSpec((1, rows, n), lambda a: (0, a, 0))],
        out_specs=pl.BlockSpec((rows, n), lambda a: (a, 0)),
        out_shape=jax.ShapeDtypeStruct((d, n), BF16),
        compiler_params=pltpu.CompilerParams(dimension_semantics=("arbitrary",)),
        name="cast_w_in")(w)


def _prep_weights(w_in, w_uk, w_uv):
    o = np.cumsum((0,) + IN_SIZES)
    d = w_in.shape[1]
    w_t = _scaled_cast(jnp.swapaxes(w_in, 1, 2), int(o[1]), A_SCALE * LOG2E, rows=464)
    row = lambda j: w_t[o[j]:o[j + 1]]
    qn = row(3).reshape(H_B, DH_NOPE, d)
    qr = row(4).reshape(H_B, DH_ROPE, d)
    pad = jnp.zeros((H_B, HEAD_PAD - DH_NOPE - DH_ROPE, d), BF16).at[0].set(row(6))
    wqt = jnp.concatenate([jnp.concatenate([qn, qr, pad], axis=1).reshape(D_QF, d), row(0)], axis=0)
    wm = jnp.concatenate([row(1), row(5), row(7), row(8)], axis=0).T
    wvat = row(2)
    wk = jnp.concatenate([w_uk, jnp.zeros((D_C, H_B, HEAD_PAD - DH_NOPE), F32)], axis=2).reshape(D_C, D_QF).astype(BF16)
    place = jnp.concatenate([jnp.zeros((DH_ROPE, DH_NOPE), F32), jnp.eye(DH_ROPE, dtype=F32),
                             jnp.zeros((DH_ROPE, HEAD_PAD - DH_NOPE - DH_ROPE), F32)], axis=1)
    ek = jnp.tile(place, (1, H_B)).astype(BF16)
    wvt = w_uv.reshape(D_C, D_VB).T.astype(BF16)
    return wm, wqt, wvat, wvat.T, wk, ek, wvt


def kernel(x_prompt, x_sample, cache_a_k, cache_a_v, cache_mla_ckv, cache_mla_krope, state_ffn_conv, norm_mix_pre,
           norm_mix_post, w_in, rel_bias_table, kv_norm, w_uk, w_uv, w_branch_a, w_branch_b, w_out, norm_ffn_pre,
           norm_ffn_post, w_ffn_gate, w_ffn_up, conv_w, conv_b, w_ffn_down):
    assert w_in.shape[0] == 1, "single layer"
    b, s, _ = x_prompt.shape
    db, t, _ = x_sample.shape
    past = cache_mla_ckv.shape[2]
    wcache = cache_a_k.shape[2]
    keep = min(A_WINDOW, s)
    tm1 = 512
    assert keep == tm1 and db * t == tm1 and wcache == A_WINDOW

    wm, wqt, wvat, wva, wk, ek, wvt = _prep_weights(w_in, w_uk[0], w_uv[0])
    row = lambda v: v.reshape(1, -1)
    proj_w = (row(norm_mix_pre[0]), wm, wqt, wvat, wva, row(kv_norm[0]), wk, wvt)
    ffn_w = (w_branch_a[0].astype(BF16), w_branch_b[0].astype(BF16), w_out[0].astype(BF16), row(norm_mix_post[0]),
             row(norm_ffn_pre[0]), row(norm_ffn_post[0]),
             jnp.concatenate([w_ffn_gate[0].astype(BF16), w_ffn_up[0].astype(BF16)], axis=1),
             conv_w[0], row(conv_b[0]), w_ffn_down[0].astype(BF16))
    table = rel_bias_table[0]

    pos = np.arange(s)
    tabs = _rope_tables(pos, MLA_SCALE * LOG2E) + _rope_tables(pos, 1.0)
    t_mla = tm1
    qa, ka, vat, kaf, vaf, qf, ckv, kr, ga, gb, kf, vt = _in_proj(x_prompt, *proj_w, *tabs, tm=tm1)
    ya = _band_prompt(qa, ka, vat, _band_bias(table, 256, A_WINDOW + 256), tm=tm1, tq=256)
    yb = _mla_attn(qf, kf, vt, _chunk_masks(t_mla, t_mla), tq=t_mla, tk=t_mla, causal=True)
    y_prompt, conv_p = _merge_ffn(x_prompt, ya, yb, ga, gb, jnp.zeros((b, CONV_W - 1, D_FF), F32), *ffn_w, tm=512)

    pos_s = np.tile(past + np.arange(t), db)
    tabs_s = _rope_tables(pos_s, MLA_SCALE * LOG2E) + _rope_tables(pos_s, 1.0)
    qa2, ka2, vat2, kaf2, vaf2, qf2, ckv2, kr2, ga2, gb2, _, _ = _in_proj(x_sample.reshape(1, db * t, D_MODEL), *proj_w,
                                                                    *tabs_s, tm=tm1)
    per_seq = lambda v: v.reshape(db, t, v.shape[-1])
    ckv2, kr2 = per_seq(ckv2[0]), per_seq(kr2[0])
    new_k = jnp.concatenate([cache_a_k[0].reshape(db, wcache, D_A), per_seq(kaf2[0])], axis=1)
    new_v = jnp.concatenate([cache_a_v[0].reshape(db, wcache, D_A), per_seq(vaf2[0])], axis=1)
    qa2 = jnp.transpose(qa2[0].reshape(N_PAIRS, PAIR_W, db, t), (0, 2, 3, 1))
    ya2 = _band_sample(qa2, new_k.astype(BF16), jnp.swapaxes(new_v, 1, 2).astype(BF16),
                       _band_bias(table, t, wcache + t))
    c_all = jnp.concatenate([cache_mla_ckv[0], ckv2], axis=1)
    kr_all = jnp.concatenate([cache_mla_krope[0], kr2], axis=1)
    kf2, vt2 = _kv_up(c_all, kr_all, wk, ek, wvt, tm=past + t, tk=past + t)
    qf2 = jnp.transpose(qf2[0, 0].reshape(D_QF, db, t), (1, 0, 2))[:, None]
    yb2 = _mla_attn(qf2, kf2, vt2, jnp.zeros((1, 8, 128), F32), tq=t, tk=past + t, causal=False, pairs=H_B // 2)
    y_sample, conv_s = _merge_ffn(x_sample.reshape(1, db * t, D_MODEL), ya2.reshape(1, N_PAIRS, db * t, PAIR_W),
                                  yb2.reshape(1, db * t, D_VB), ga2, gb2, state_ffn_conv[0], *ffn_w,
                                  tm=db * t, nseq=db)
    y_sample = y_sample.reshape(db, t, D_MODEL)

    heads = lambda v: v.reshape(1, v.shape[0], v.shape[1], H_A, DH_A)
    return (y_prompt, y_sample,
            heads(kaf), heads(vaf), ckv[None], kr[None], conv_p[None],
            heads(new_k[:, -wcache:]), heads(new_v[:, -wcache:]), ckv2[None], kr2[None], conv_s[None])
```

```python
import functools

import jax
import jax.numpy as jnp
import numpy as np
from jax import lax
from jax.experimental import pallas as pl
from jax.experimental.pallas import tpu as pltpu

D_MODEL = 1024
CHUNK = 64
LEFT_CHUNKS = 8
A_WINDOW = LEFT_CHUNKS * CHUNK
H_A = 8
DH_A = 64
MAX_REL = 256
H_B = 8
DH_NOPE = 64
DH_ROPE = 32
DV_B = 64
D_C = 256
D_FF = 2816
CONV_W = 3
ROPE_BASE = 10000.0
EPS = 1e-6
A_SCALE = DH_A ** -0.5
MLA_SCALE = (DH_NOPE + DH_ROPE) ** -0.5
IN_SIZES = (H_A * DH_A, H_A * DH_A, H_A * DH_A, H_B * DH_NOPE, H_B * DH_ROPE, D_C, DH_ROPE, D_MODEL, D_MODEL)

HEAD_PAD = 128
D_A = H_A * DH_A
D_VB = H_B * DV_B
D_QF = H_B * HEAD_PAD
PAIR_W = 2 * DH_A
N_PAIRS = H_A // 2
NEG = -1e30
LOG2E = 1.4426950408889634
_L_ROWS = 16
VMEM_LIMIT_V7X = 56 * 1024 * 1024

F32 = jnp.float32
BF16 = jnp.bfloat16
_NT = (((1,), (1,)), ((), ()))


def _resident(shape):
    nd = len(shape)
    return pl.BlockSpec(shape, lambda *_: (0,) * nd, pipeline_mode=pl.Buffered(1))


def _rms(x, w):
    return x * lax.rsqrt(jnp.mean(x * x, axis=-1, keepdims=True) + EPS) * w


_C_KA, _C_CKV, _C_GA, _C_GB, _C_END = 0, 512, 768, 1792, 2816
_KR_ROW = DH_NOPE + DH_ROPE
_HALF = DH_ROPE // 2


def _store_pairs(ref, v):
    for g in range(N_PAIRS):
        ref[0, g] = v[:, g * PAIR_W:(g + 1) * PAIR_W]


def _rope_rows(x, c, s):
    x1, x2 = x[:_HALF], x[_HALF:]
    return [x1 * c - x2 * s, x1 * s + x2 * c]


def _in_proj_kernel(x_ref, nw_ref, wm_ref, wqt_ref, wvt_ref, kvn_ref, wuk_ref, wuvt_ref, cq_ref, sq_ref,
                    ck_ref, sk_ref, qa_ref, ka_ref, vat_ref, kaf_ref, vaf_ref, qf_ref, ckv_ref, kr_ref, ga_ref, gb_ref,
                    kf_ref, vt_ref, *, n):
    i = pl.program_id(1)
    xn = _rms(x_ref[0], nw_ref[...]).astype(BF16)

    def proj(lo, hi):
        return jnp.dot(xn, wm_ref[:, lo:hi], preferred_element_type=F32)

    ka = proj(_C_KA, _C_CKV)
    _store_pairs(ka_ref, ka.astype(BF16))
    vat = lax.dot_general(wvt_ref[...], xn, _NT, preferred_element_type=F32)
    vat_ref[0, 0] = vat.astype(BF16)
    vaf_ref[0] = vat

    def ka_tail():
        kaf_ref[0] = ka.T

    if n == 1:
        ka_tail()
    else:
        pl.when(i == n - 1)(ka_tail)

    qt = lax.dot_general(wqt_ref[...], xn, _NT, preferred_element_type=F32)
    for g in range(N_PAIRS):
        qa_ref[0, g] = qt[D_QF + g * PAIR_W:D_QF + (g + 1) * PAIR_W].astype(BF16)
    cq, sq = cq_ref[...], sq_ref[...]
    pad = jnp.zeros((HEAD_PAD - _KR_ROW, qt.shape[1]), F32)
    for h in range(H_B):
        slab = qt[h * HEAD_PAD:(h + 1) * HEAD_PAD]
        rows = [slab[:DH_NOPE] * (MLA_SCALE * LOG2E)] + _rope_rows(slab[DH_NOPE:_KR_ROW], cq, sq) + [pad]
        qf_ref[0, 0, h * HEAD_PAD:(h + 1) * HEAD_PAD, :] = jnp.concatenate(rows, axis=0).astype(BF16)
    kr_t = _rope_rows(qt[_KR_ROW:HEAD_PAD], ck_ref[...], sk_ref[...])
    zero_rows = lambda r: jnp.zeros((r, qt.shape[1]), F32)
    kr_ref[0] = jnp.concatenate(kr_t, axis=0)

    ckv = _rms(proj(_C_CKV, _C_GA), kvn_ref[...])
    ckv_ref[0] = ckv
    c_bf = ckv.astype(BF16)
    k_nope = jnp.dot(c_bf, wuk_ref[...], preferred_element_type=F32)
    k_rope = jnp.concatenate([zero_rows(DH_NOPE)] + kr_t + [zero_rows(HEAD_PAD - _KR_ROW)], axis=0).T
    for h in range(H_B):
        slab = slice(h * HEAD_PAD, (h + 1) * HEAD_PAD)
        kf_ref[0, :, slab] = (k_nope[:, slab] + k_rope).astype(BF16)
    vt_ref[0, 0] = lax.dot_general(wuvt_ref[...], c_bf, _NT, preferred_element_type=F32).astype(BF16)
    ga_ref[0] = proj(_C_GA, _C_GB).astype(BF16)
    gb_ref[0] = proj(_C_GB, _C_END).astype(BF16)


def _in_proj(x, nw, wm, wqt, wvt, kvn, wuk, wuvt, cq, sq, ck, sk, *, tm):
    g, s, _ = x.shape
    n = s // tm
    tok = lambda w: pl.BlockSpec((1, tm, w), lambda a, b: (a, b, 0))
    tab = pl.BlockSpec((_HALF, tm), lambda a, b: (0, b))
    tail = pl.BlockSpec((1, D_A, tm), lambda a, b: (a, 0, 0))
    out_shape = (
        jax.ShapeDtypeStruct((g, N_PAIRS, PAIR_W, s), BF16),
        jax.ShapeDtypeStruct((g, N_PAIRS, s, PAIR_W), BF16),
        jax.ShapeDtypeStruct((g, n, D_A, tm), BF16),
        jax.ShapeDtypeStruct((g, D_A, tm), F32),
        jax.ShapeDtypeStruct((g, D_A, tm), F32),
        jax.ShapeDtypeStruct((g, n, D_QF, tm), BF16),
        jax.ShapeDtypeStruct((g, s, D_C), F32),
        jax.ShapeDtypeStruct((g, DH_ROPE, s), F32),
        jax.ShapeDtypeStruct((g, s, D_MODEL), BF16),
        jax.ShapeDtypeStruct((g, s, D_MODEL), BF16),
        jax.ShapeDtypeStruct((g, s, D_QF), BF16),
        jax.ShapeDtypeStruct((g, n, D_VB, tm), BF16),
    )
    pairs = pl.BlockSpec((1, N_PAIRS, tm, PAIR_W), lambda a, b: (a, 0, b, 0))
    pairs_t = pl.BlockSpec((1, N_PAIRS, PAIR_W, tm), lambda a, b: (a, 0, 0, b))
    out_specs = (pairs_t, pairs, pl.BlockSpec((1, 1, D_A, tm), lambda a, b: (a, b, 0, 0)), tail, tail,
                 pl.BlockSpec((1, 1, D_QF, tm), lambda a, b: (a, b, 0, 0)), tok(D_C),
                 pl.BlockSpec((1, DH_ROPE, tm), lambda a, b: (a, 0, b)), tok(D_MODEL),
                 tok(D_MODEL), tok(D_QF), pl.BlockSpec((1, 1, D_VB, tm), lambda a, b: (a, b, 0, 0)))
    in_specs = [tok(D_MODEL), _resident(nw.shape), _resident(wm.shape), _resident(wqt.shape), _resident(wvt.shape),
                _resident(kvn.shape), _resident(wuk.shape), _resident(wuvt.shape), tab, tab, tab,
                tab]
    return pl.pallas_call(
        functools.partial(_in_proj_kernel, n=n), grid=(g, n), in_specs=in_specs, out_specs=out_specs, out_shape=out_shape,
        compiler_params=pltpu.CompilerParams(dimension_semantics=("arbitrary", "arbitrary"),
                                             vmem_limit_bytes=VMEM_LIMIT_V7X),
        name="in_proj")(x, nw, wm, wqt, wvt, kvn, wuk, wuvt, cq, sq, ck, sk)


def _kv_up_kernel(ckv_ref, kr_ref, wk_ref, ek_ref, wvt_ref, kf_ref, vt_ref, *, tk):
    c = ckv_ref[0].astype(BF16)
    kf = jnp.dot(c, wk_ref[...], preferred_element_type=F32)
    kf = kf + jnp.dot(kr_ref[0].astype(BF16), ek_ref[...], preferred_element_type=F32)
    kf_ref[0] = kf.astype(BF16)
    for j in range(c.shape[0] // tk):
        vt_ref[0, j] = lax.dot_general(wvt_ref[...], c[j * tk:(j + 1) * tk], _NT,
                                       preferred_element_type=F32).astype(BF16)


def _kv_up(ckv, kr, wk, ek, wvt, *, tm, tk):
    g, s, _ = ckv.shape
    return pl.pallas_call(
        functools.partial(_kv_up_kernel, tk=tk), grid=(g, s // tm),
        in_specs=[pl.BlockSpec((1, tm, D_C), lambda a, b: (a, b, 0)),
                  pl.BlockSpec((1, tm, DH_ROPE), lambda a, b: (a, b, 0)),
                  _resident(wk.shape), _resident(ek.shape), _resident(wvt.shape)],
        out_specs=(pl.BlockSpec((1, tm, D_QF), lambda a, b: (a, b, 0)),
                   pl.BlockSpec((1, tm // tk, D_VB, tk), lambda a, b: (a, b, 0, 0))),
        out_shape=(jax.ShapeDtypeStruct((g, s, D_QF), BF16),
                   jax.ShapeDtypeStruct((g, s // tk, D_VB, tk), BF16)),
        compiler_params=pltpu.CompilerParams(dimension_semantics=("arbitrary", "arbitrary"),
                                             vmem_limit_bytes=VMEM_LIMIT_V7X),
        name="kv_up")(ckv, kr, wk, ek, wvt)


def _band_units(units, bias_ref, n_pairs=H_A // 2):
    def scores(u, g):
        q, k_of, _, _ = units[u]
        lanes = slice(g * 2 * DH_A, (g + 1) * 2 * DH_A)
        lane = lax.broadcasted_iota(jnp.int32, (q.shape[0], 2 * DH_A), 1)
        k2 = k_of(lanes)
        return [lax.dot_general(k2, jnp.where((lane >= DH_A) == (hl == 1), q[:, lanes], jnp.zeros((), BF16)), _NT,
                                preferred_element_type=F32) for hl in range(2)]

    steps = [(u, g) for u in range(len(units)) for g in range(n_pairs)]
    outs = [[] for _ in units]
    pend = scores(*steps[0])
    for j, (u, g) in enumerate(steps):
        nxt = scores(*steps[j + 1]) if j + 1 < len(steps) else None
        _, _, v_of, b0 = units[u]
        for hl in range(2):
            h = 2 * g + hl
            nk = pend[hl].shape[0]
            s = pend[hl] + bias_ref[h, b0:b0 + nk, :]
            p = jnp.exp2(s - jnp.max(s, axis=0, keepdims=True)).astype(BF16)
            v = jnp.concatenate([v_of(slice(h * DH_A, (h + 1) * DH_A)), jnp.ones((_L_ROWS, nk), BF16)], axis=0)
            o = jnp.dot(v, p, preferred_element_type=F32)
            outs[u].append(o[:DH_A] / o[DH_A:DH_A + 1])
        pend = nxt
    return [jnp.concatenate(o, axis=0).T for o in outs]


def _band_prompt_kernel(q_ref, kp_ref, kc_ref, vp_ref, vc_ref, bias_ref, o_ref, s_scr, *, tm, tq):
    subs = [(lo, lo + tq) for lo in range(0, tm, tq)]
    row = lax.broadcasted_iota(jnp.int32, (PAIR_W, tq), 0)

    def tile(first):
        def window(lo, hi):
            return (hi, tm - lo) if first else (tm - lo + hi, 0)

        def scores(g, j):
            lo, hi = subs[j]
            nk, b0 = window(lo, hi)
            q2 = q_ref[0, g, :, lo:hi]
            qcat = jnp.concatenate([jnp.where(row < DH_A, q2, jnp.zeros((), BF16)),
                                    jnp.where(row >= DH_A, q2, jnp.zeros((), BF16))], axis=1)
            k2 = kc_ref[0, g, :hi, :]
            if not first:
                k2 = jnp.concatenate([kp_ref[0, g, lo:, :], k2], axis=0)
            bias = jnp.concatenate([bias_ref[2 * g + hl, b0:b0 + nk, :] for hl in range(2)], axis=1)
            return jnp.dot(k2, qcat, preferred_element_type=F32) + bias

        def stash(j, s):
            s_scr[j, :s.shape[0], :] = s
            return jnp.max(s, axis=0, keepdims=True)

        def consume(g, j, col_max):
            lo, hi = subs[j]
            nk, _ = window(lo, hi)
            outs = []
            for hl in range(2):
                cols = slice(hl * tq, (hl + 1) * tq)
                p = jnp.exp2(s_scr[j, :nk, cols] - col_max[:, cols]).astype(BF16)
                rows = pl.ds(g * PAIR_W + hl * DH_A, DH_A)
                v = vc_ref[0, 0, rows, :hi]
                if not first:
                    v = jnp.concatenate([vp_ref[0, 0, rows, lo:], v], axis=1)
                v = jnp.concatenate([v, jnp.ones((_L_ROWS, nk), BF16)], axis=0)
                o = jnp.dot(v, p, preferred_element_type=F32)
                outs.append(o[:DH_A] / o[DH_A:DH_A + 1])
            o_ref[0, g, lo:hi, :] = jnp.concatenate(outs, axis=0).T.astype(BF16)

        def body(g, col_max):
            new_max = []
            for j in range(len(subs)):
                nxt = scores(g + 1, j)
                consume(g, j, col_max[j])
                new_max.append(stash(j, nxt))
            return tuple(new_max)

        col_max = tuple(stash(j, scores(0, j)) for j in range(len(subs)))
        for g in range(N_PAIRS - 1):
            col_max = body(g, col_max)
        for j in range(len(subs)):
            consume(N_PAIRS - 1, j, col_max[j])

    pl.when(pl.program_id(1) == 0)(functools.partial(tile, True))
    pl.when(pl.program_id(1) > 0)(functools.partial(tile, False))


def _band_prompt(qa, ka, vat, bias, *, tm, tq):
    g, _, _, s = qa.shape
    cur = lambda a, b: (a, 0, b, 0)
    prev = lambda a, b: (a, 0, jnp.maximum(b - 1, 0), 0)
    prev_v = lambda a, b: (a, jnp.maximum(b - 1, 0), 0, 0)
    pairs = lambda idx: pl.BlockSpec((1, N_PAIRS, tm, PAIR_W), idx)
    return pl.pallas_call(
        functools.partial(_band_prompt_kernel, tm=tm, tq=tq), grid=(g, s // tm),
        in_specs=[pl.BlockSpec((1, N_PAIRS, PAIR_W, tm), lambda a, b: (a, 0, 0, b)), pairs(prev), pairs(cur),
                  pl.BlockSpec((1, 1, D_A, tm), prev_v), pl.BlockSpec((1, 1, D_A, tm), lambda a, b: (a, b, 0, 0)),
                  _resident(bias.shape)],
        out_specs=pairs(cur),
        out_shape=jax.ShapeDtypeStruct((g, N_PAIRS, s, PAIR_W), BF16),
        scratch_shapes=[pltpu.VMEM((tm // tq, tm + tq, 2 * tq), F32)],
        compiler_params=pltpu.CompilerParams(dimension_semantics=("arbitrary", "arbitrary"),
                                             vmem_limit_bytes=VMEM_LIMIT_V7X),
        name="band_prompt")(qa, ka, ka, vat, vat, bias)


def _band_sample_kernel(q_ref, k_ref, vt_ref, bias_ref, o_ref):
    q = jnp.concatenate([q_ref[g, 0] for g in range(N_PAIRS)], axis=1)
    unit = (q, lambda lanes: k_ref[0, :, lanes], lambda rows: vt_ref[0, rows, :], 0)
    o = _band_units([unit], bias_ref)[0].astype(BF16)
    for g in range(N_PAIRS):
        o_ref[g, 0] = o[:, g * PAIR_W:(g + 1) * PAIR_W]


def _band_sample(qa, k_all, vt_all, bias):
    _, g, t, _ = qa.shape
    nk = k_all.shape[1]
    pairs = pl.BlockSpec((N_PAIRS, 1, t, PAIR_W), lambda a: (0, a, 0, 0))
    return pl.pallas_call(
        _band_sample_kernel, grid=(g,),
        in_specs=[pairs,
                  pl.BlockSpec((1, nk, D_A), lambda a: (a, 0, 0)),
                  pl.BlockSpec((1, D_A, nk), lambda a: (a, 0, 0)),
                  _resident(bias.shape)],
        out_specs=pairs,
        out_shape=jax.ShapeDtypeStruct((N_PAIRS, g, t, PAIR_W), BF16),
        compiler_params=pltpu.CompilerParams(dimension_semantics=("arbitrary",),
                                             vmem_limit_bytes=VMEM_LIMIT_V7X),
        name="band_sample")(qa, k_all, vt_all, bias)


def _mla_kernel(q_ref, k_ref, vt_ref, mask_ref, o_ref, s_scr, *, tq, tk, causal, pairs):
    nq = q_ref.shape[1]
    nk = k_ref.shape[1] // tk
    ones = jnp.ones((_L_ROWS, tk), BF16)

    def q_rows(hp, i):
        return [q_ref[0, i, (2 * hp + hl) * HEAD_PAD:(2 * hp + hl + 1) * HEAD_PAD, :] for hl in range(2)]

    def scores(hp, qs, kk, hl):
        krow = pl.multiple_of(kk * tk, tk)
        return jnp.dot(k_ref[0, pl.ds(krow, tk), (2 * hp + hl) * HEAD_PAD:(2 * hp + hl + 1) * HEAD_PAD], qs[hl],
                       preferred_element_type=F32)

    def stash(hl, s):
        s_scr[hl] = s
        return jnp.max(s, axis=0, keepdims=True)

    def consume(hp, kk, hl, m, acc, tile_max, mask_j):
        if mask_j is not None:
            s = s_scr[hl] + mask_ref[mask_j]
            m_new = jnp.maximum(m, jnp.max(s, axis=0, keepdims=True))
        else:
            s = s_scr[hl]
            m_new = jnp.maximum(m, tile_max)
        p = jnp.exp2(s - m_new).astype(BF16)
        v = jnp.concatenate([vt_ref[0, kk, (2 * hp + hl) * DV_B:(2 * hp + hl + 1) * DV_B, :], ones], axis=0)
        return m_new, acc * jnp.exp2(m - m_new) + jnp.dot(v, p, preferred_element_type=F32)

    def q_tile(hp, i, first_max):
        qs = q_rows(hp, i)

        def body(t, carry):
            out = []
            for hl in range(2):
                m, acc, tile_max = carry[3 * hl:3 * hl + 3]
                nxt = scores(hp, qs, t + 1, hl)
                out += list(consume(hp, t, hl, m, acc, tile_max, None)) + [stash(hl, nxt)]
            return tuple(out)

        r = tq // tk
        n_full = r * i if causal else nk - 1
        carry = ()
        for hl in range(2):
            carry += (jnp.full((1, tq), NEG, F32), jnp.zeros((DV_B + _L_ROWS, tq), F32), first_max[hl])
        done = 0
        for n in (4, 2, 1):
            left = n_full - done
            trips = left // n if isinstance(left, int) else lax.shift_right_logical(left, n.bit_length() - 1)

            def trip(u, c, n=n, done=done):
                for j in range(n):
                    c = body(done + n * u + j, c)
                return c

            carry = lax.fori_loop(0, trips, trip, carry)
            done = done + trips * n
        tail = [(r * i + j, j) for j in range(r)] if causal else [(nk - 1, None)]
        qs_next = q_rows(hp, jnp.minimum(i + 1, nq - 1)) if nq > 1 else None
        state = [list(carry[3 * hl:3 * hl + 3]) for hl in range(2)]
        for idx, (tile, mask_j) in enumerate(tail):
            for hl in range(2):
                if idx + 1 < len(tail):
                    nxt = scores(hp, qs, tile + 1, hl)
                else:
                    nxt = scores(hp, qs_next, 0, hl) if nq > 1 else None
                m, acc = consume(hp, tile, hl, *state[hl], mask_j)
                state[hl] = [m, acc, stash(hl, nxt) if nxt is not None else state[hl][2]]
        accs = [state[hl][1] for hl in range(2)]
        next_max = [state[hl][2] for hl in range(2)]
        o = jnp.concatenate([acc[:DV_B] / acc[DV_B:DV_B + 1] for acc in accs], axis=0)
        o_ref[0, pl.ds(pl.multiple_of(i * tq, tq), tq), hp * 2 * DV_B:(hp + 1) * 2 * DV_B] = o.T.astype(BF16)
        return tuple(next_max)

    for hp in range(pairs):
        first = q_rows(hp, 0)
        lax.fori_loop(0, nq, functools.partial(q_tile, hp),
                      tuple(stash(hl, scores(hp, first, 0, hl)) for hl in range(2)))


def _mla_attn(qf, kf, vt, mask, *, tq, tk, causal, pairs=1):
    g, nq, _, _ = qf.shape
    sq, sk = nq * tq, kf.shape[1]
    return pl.pallas_call(
        functools.partial(_mla_kernel, tq=tq, tk=tk, causal=causal, pairs=pairs), grid=(g, H_B // 2 // pairs),
        in_specs=[pl.BlockSpec((1, nq, 2 * pairs * HEAD_PAD, tq), lambda a, b: (a, 0, b, 0)),
                  pl.BlockSpec((1, sk, 2 * pairs * HEAD_PAD), lambda a, b: (a, 0, b)),
                  pl.BlockSpec((1, sk // tk, 2 * pairs * DV_B, tk), lambda a, b: (a, 0, b, 0)),
                  _resident(mask.shape)],
        out_specs=pl.BlockSpec((1, sq, 2 * pairs * DV_B), lambda a, b: (a, 0, b)),
        out_shape=jax.ShapeDtypeStruct((g, sq, D_VB), BF16),
        scratch_shapes=[pltpu.VMEM((2, tk, tq), F32)],
        compiler_params=pltpu.CompilerParams(dimension_semantics=("arbitrary", "arbitrary"),
                                             vmem_limit_bytes=VMEM_LIMIT_V7X),
        name="mla_attn")(qf, kf, vt, mask)


_G0 = 8


def _merge_ffn_kernel(x_ref, ya_ref, yb_ref, ga_ref, gb_ref, st_ref, wa_ref, wb_ref, wo_ref, n1_ref, n2_ref, n3_ref,
                      wg_ref, wu_ref, cw_ref, cb_ref, wd_ref, y_ref, cs_ref, gbuf, *, tm, nseq):
    i = pl.program_id(1)
    seg = tm // nseq
    ya = jnp.concatenate([ya_ref[0, g] for g in range(N_PAIRS)], axis=1)
    za = jnp.dot(ya, wa_ref[...], preferred_element_type=F32)
    zb = jnp.dot(yb_ref[0], wb_ref[...], preferred_element_type=F32)
    mix = jax.nn.sigmoid(ga_ref[0].astype(F32)) * za + jax.nn.sigmoid(gb_ref[0].astype(F32)) * zb
    mo = jnp.dot(mix.astype(BF16), wo_ref[...], preferred_element_type=F32)
    x1 = x_ref[0] + _rms(mo, n1_ref[...])
    xn = _rms(x1, n2_ref[...]).astype(BF16)

    starts = [_G0 + j * (seg + _G0) for j in range(nseq)]
    if nseq == 1:
        @pl.when(i == 0)
        def _():
            gbuf[_G0 - 2:_G0, :] = st_ref[0]

        @pl.when(i > 0)
        def _():
            gbuf[_G0 - 2:_G0, :] = gbuf[_G0 + tm - 2:_G0 + tm, :]
    else:
        for j, r0 in enumerate(starts):
            gbuf[r0 - 2:r0, :] = st_ref[j]

    gate = jnp.dot(xn, wg_ref[...], preferred_element_type=F32)
    for j, r0 in enumerate(starts):
        gbuf[r0:r0 + seg, :] = gate[j * seg:(j + 1) * seg]
        cs_ref[j] = gbuf[r0 + seg - 2:r0 + seg, :]
    u = jnp.dot(xn, wu_ref[...], preferred_element_type=F32)
    c = jnp.concatenate([cw_ref[0:1, :] * gbuf[r0 - 2:r0 - 2 + seg, :] + cw_ref[1:2, :] * gbuf[r0 - 1:r0 - 1 + seg, :]
                         + cw_ref[2:3, :] * gbuf[r0:r0 + seg, :] for r0 in starts], axis=0) + cb_ref[...]
    hid = (jax.nn.gelu(c, approximate=True) * u).astype(BF16)
    f = jnp.dot(hid, wd_ref[...], preferred_element_type=F32)
    y_ref[0] = x1 + _rms(f, n3_ref[...])


def _merge_ffn(x, ya, yb, ga, gb, state, wa, wb, wo, n1, n2, n3, wg, wu, cw, cb, wd, *, tm, nseq=1):
    g, s, _ = x.shape
    assert nseq == 1 or s == tm
    tok = lambda w: pl.BlockSpec((1, tm, w), lambda a, b: (a, b, 0))
    per_group = pl.BlockSpec((nseq, CONV_W - 1, D_FF), lambda a, b: (a, 0, 0))
    weights = (wa, wb, wo, n1, n2, n3, wg, wu, cw, cb, wd)
    return pl.pallas_call(
        functools.partial(_merge_ffn_kernel, tm=tm, nseq=nseq), grid=(g, s // tm),
        in_specs=[tok(D_MODEL), pl.BlockSpec((1, N_PAIRS, tm, PAIR_W), lambda a, b: (a, 0, b, 0)), tok(D_VB),
                  tok(D_MODEL), tok(D_MODEL), per_group]
                 + [_resident(w.shape) for w in weights],
        out_specs=(tok(D_MODEL), per_group),
        out_shape=(jax.ShapeDtypeStruct((g, s, D_MODEL), F32),
                   jax.ShapeDtypeStruct((g * nseq, CONV_W - 1, D_FF), F32)),
        scratch_shapes=[pltpu.VMEM((tm + nseq * _G0, D_FF), F32)],
        compiler_params=pltpu.CompilerParams(dimension_semantics=("arbitrary", "arbitrary"),
                                             vmem_limit_bytes=VMEM_LIMIT_V7X),
        name="merge_ffn")(x, ya, yb, ga, gb, state, *weights)


def _rope_tables(pos, scale):
    inv = ROPE_BASE ** (-np.arange(_HALF, dtype=np.float64) / _HALF)
    ang = inv[:, None] * np.asarray(pos, np.float64)[None, :]
    return jnp.asarray(np.cos(ang) * scale, F32), jnp.asarray(np.sin(ang) * scale, F32)


def _band_bias_kernel(ext_ref, o_ref, *, tq, nk):
    w = ext_ref.shape[-1]
    toeplitz = pltpu.roll(jnp.broadcast_to(ext_ref[0], (nk, w)), 0, 1, stride=1, stride_axis=0)[:, :tq]
    qc = (nk - tq + lax.broadcasted_iota(jnp.int32, (nk, tq), 1)) // CHUNK
    kc = lax.broadcasted_iota(jnp.int32, (nk, tq), 0) // CHUNK
    o_ref[0] = jnp.where((kc >= qc - LEFT_CHUNKS) & (kc <= qc), toeplitz, NEG)


def _band_bias(table, tq, nk):
    w = nk + tq
    d_lo, d_hi = 1 - tq, w - tq
    assert -MAX_REL <= d_lo and d_hi >= MAX_REL and w % 128 == 0
    h = table.shape[0]
    ext = jnp.concatenate([table[:, d_lo + MAX_REL:], jnp.broadcast_to(table[:, -1:], (h, d_hi - MAX_REL))], axis=1)
    ext = (jnp.roll(ext, -(nk - 1), axis=1) * LOG2E).reshape(h, 1, w)
    return pl.pallas_call(
        functools.partial(_band_bias_kernel, tq=tq, nk=nk), grid=(h,),
        in_specs=[pl.BlockSpec((1, 1, w), lambda a: (a, 0, 0))],
        out_specs=pl.BlockSpec((1, nk, tq), lambda a: (a, 0, 0)),
        out_shape=jax.ShapeDtypeStruct((h, nk, tq), F32),
        compiler_params=pltpu.CompilerParams(dimension_semantics=("arbitrary",)),
        name="band_bias")(ext)


def _chunk_masks(tk, tq):
    kc = jnp.arange(tq)[:, None] // CHUNK
    qc = jnp.arange(tq)[None, :] // CHUNK
    return jnp.where(kc <= qc, 0.0, NEG).astype(F32).reshape(tq // tk, tk, tq)


def _scaled_cast_kernel(w_ref, o_ref, *, rows, n_scaled, scale):
    r = pl.program_id(0) * rows + lax.broadcasted_iota(jnp.int32, (rows, 1), 0)
    o_ref[...] = (w_ref[0] * jnp.where(r < n_scaled, scale, 1.0)).astype(BF16)


def _scaled_cast(w, n_scaled, scale, *, rows):
    _, d, n = w.shape
    return pl.pallas_call(
        functools.partial(_scaled_cast_kernel, rows=rows, n_scaled=n_scaled, scale=scale), grid=(d // rows,),
        in_specs=[pl.BlockSpec((1, rows, n), lambda a: (0, a, 0))],
        out_specs=pl.BlockSpec((rows, n), lambda a: (a, 0)),
        out_shape=jax.ShapeDtypeStruct((d, n), BF16),
        compiler_params=pltpu.CompilerParams(dimension_semantics=("arbitrary",)),
        name="cast_w_in")(w)


def _prep_weights(w_in, w_uk, w_uv):
    o = np.cumsum((0,) + IN_SIZES)
    d = w_in.shape[1]
    w_t = _scaled_cast(jnp.swapaxes(w_in, 1, 2), int(o[1]), A_SCALE * LOG2E, rows=464)
    row = lambda j: w_t[o[j]:o[j + 1]]
    qn = row(3).reshape(H_B, DH_NOPE, d)
    qr = row(4).reshape(H_B, DH_ROPE, d)
    pad = jnp.zeros((H_B, HEAD_PAD - DH_NOPE - DH_ROPE, d), BF16).at[0].set(row(6))
    wqt = jnp.concatenate([jnp.concatenate([qn, qr, pad], axis=1).reshape(D_QF, d), row(0)], axis=0)
    wm = jnp.concatenate([row(1), row(5), row(7), row(8)], axis=0).T
    wvat = row(2)
    wk = jnp.concatenate([w_uk, jnp.zeros((D_C, H_B, HEAD_PAD - DH_NOPE), F32)], axis=2).reshape(D_C, D_QF).astype(BF16)
    place = jnp.concatenate([jnp.zeros((DH_ROPE, DH_NOPE), F32), jnp.eye(DH_ROPE, dtype=F32),
                             jnp.zeros((DH_ROPE, HEAD_PAD - DH_NOPE - DH_ROPE), F32)], axis=1)
    ek = jnp.tile(place, (1, H_B)).astype(BF16)
    wvt = w_uv.reshape(D_C, D_VB).T.astype(BF16)
    return wm, wqt, wvat, wk, ek, wvt


def kernel(x_prompt, x_sample, cache_a_k, cache_a_v, cache_mla_ckv, cache_mla_krope, state_ffn_conv, norm_mix_pre,
           norm_mix_post, w_in, rel_bias_table, kv_norm, w_uk, w_uv, w_branch_a, w_branch_b, w_out, norm_ffn_pre,
           norm_ffn_post, w_ffn_gate, w_ffn_up, conv_w, conv_b, w_ffn_down):
    assert w_in.shape[0] == 1, "single layer"
    b, s, _ = x_prompt.shape
    db, t, _ = x_sample.shape
    past = cache_mla_ckv.shape[2]
    wcache = cache_a_k.shape[2]
    keep = min(A_WINDOW, s)
    tm1 = 512
    assert keep == tm1 and db * t == tm1 and wcache == A_WINDOW

    wm, wqt, wvat, wk, ek, wvt = _prep_weights(w_in, w_uk[0], w_uv[0])
    row = lambda v: v.reshape(1, -1)
    proj_w = (row(norm_mix_pre[0]), wm, wqt, wvat, row(kv_norm[0]), wk, wvt)
    ffn_w = (w_branch_a[0].astype(BF16), w_branch_b[0].astype(BF16), w_out[0].astype(BF16), row(norm_mix_post[0]),
             row(norm_ffn_pre[0]), row(norm_ffn_post[0]), w_ffn_gate[0].astype(BF16), w_ffn_up[0].astype(BF16),
             conv_w[0], row(conv_b[0]), w_ffn_down[0].astype(BF16))
    table = rel_bias_table[0]

    pos = np.arange(s)
    tabs = _rope_tables(pos, MLA_SCALE * LOG2E) + _rope_tables(pos, 1.0)
    t_mla = tm1
    qa, ka, vat, kaf, vaf, qf, ckv, kr, ga, gb, kf, vt = _in_proj(x_prompt, *proj_w, *tabs, tm=tm1)
    ya = _band_prompt(qa, ka, vat, _band_bias(table, 256, A_WINDOW + 256), tm=tm1, tq=256)
    yb = _mla_attn(qf, kf, vt, _chunk_masks(t_mla, t_mla), tq=t_mla, tk=t_mla, causal=True)
    y_prompt, conv_p = _merge_ffn(x_prompt, ya, yb, ga, gb, jnp.zeros((b, CONV_W - 1, D_FF), F32), *ffn_w, tm=512)

    pos_s = np.tile(past + np.arange(t), db)
    tabs_s = _rope_tables(pos_s, MLA_SCALE * LOG2E) + _rope_tables(pos_s, 1.0)
    qa2, ka2, vat2, kaf2, vaf2, qf2, ckv2, kr2, ga2, gb2, _, _ = _in_proj(x_sample.reshape(1, db * t, D_MODEL), *proj_w,
                                                                    *tabs_s, tm=tm1)
    per_seq = lambda v: v.reshape(db, t, v.shape[-1])
    ckv2, kr2 = per_seq(ckv2[0]), per_seq(kr2[0].T)
    new_k = jnp.concatenate([cache_a_k[0].reshape(db, wcache, D_A), per_seq(kaf2[0].T)], axis=1)
    new_v = jnp.concatenate([cache_a_v[0].reshape(db, wcache, D_A), per_seq(vaf2[0].T)], axis=1)
    qa2 = jnp.transpose(qa2[0].reshape(N_PAIRS, PAIR_W, db, t), (0, 2, 3, 1))
    ya2 = _band_sample(qa2, new_k.astype(BF16), jnp.swapaxes(new_v, 1, 2).astype(BF16),
                       _band_bias(table, t, wcache + t))
    c_all = jnp.concatenate([cache_mla_ckv[0], ckv2], axis=1)
    kr_all = jnp.concatenate([cache_mla_krope[0], kr2], axis=1)
    kf2, vt2 = _kv_up(c_all, kr_all, wk, ek, wvt, tm=past + t, tk=past + t)
    qf2 = jnp.transpose(qf2[0, 0].reshape(D_QF, db, t), (1, 0, 2))[:, None]
    yb2 = _mla_attn(qf2, kf2, vt2, jnp.zeros((1, 8, 128), F32), tq=t, tk=past + t, causal=False, pairs=H_B // 2)
    y_sample, conv_s = _merge_ffn(x_sample.reshape(1, db * t, D_MODEL), ya2.reshape(1, N_PAIRS, db * t, PAIR_W),
                                  yb2.reshape(1, db * t, D_VB), ga2, gb2, state_ffn_conv[0], *ffn_w,
                                  tm=db * t, nseq=db)
    y_sample = y_sample.reshape(db, t, D_MODEL)

    heads = lambda v: v.reshape(1, v.shape[0], v.shape[1], H_A, DH_A)
    heads_t = lambda v: jnp.transpose(v.reshape(1, v.shape[0], H_A, DH_A, v.shape[2]), (0, 1, 4, 2, 3))
    return (y_prompt, y_sample,
            heads_t(kaf), heads_t(vaf), ckv[None], jnp.swapaxes(kr, 1, 2)[None], conv_p[None],
            heads(new_k[:, -wcache:]), heads(new_v[:, -wcache:]), ckv2[None], kr2[None], conv_s[None])
```

```python
import functools

import jax
import jax.numpy as jnp
import numpy as np
from jax import lax
from jax.experimental import pallas as pl
from jax.experimental.pallas import tpu as pltpu

D_MODEL = 1024
CHUNK = 64
LEFT_CHUNKS = 8
A_WINDOW = LEFT_CHUNKS * CHUNK
H_A = 8
DH_A = 64
MAX_REL = 256
H_B = 8
DH_NOPE = 64
DH_ROPE = 32
DV_B = 64
D_C = 256
D_FF = 2816
CONV_W = 3
ROPE_BASE = 10000.0
EPS = 1e-6
A_SCALE = DH_A ** -0.5
MLA_SCALE = (DH_NOPE + DH_ROPE) ** -0.5
IN_SIZES = (H_A * DH_A, H_A * DH_A, H_A * DH_A, H_B * DH_NOPE, H_B * DH_ROPE, D_C, DH_ROPE, D_MODEL, D_MODEL)

HEAD_PAD = 128
D_A = H_A * DH_A
D_VB = H_B * DV_B
D_QF = H_B * HEAD_PAD
PAIR_W = 2 * DH_A
N_PAIRS = H_A // 2
NEG = -1e30
LOG2E = 1.4426950408889634
_L_ROWS = 16
VMEM_LIMIT_V7X = 56 * 1024 * 1024

F32 = jnp.float32
BF16 = jnp.bfloat16
_NT = (((1,), (1,)), ((), ()))


def _resident(shape):
    nd = len(shape)
    return pl.BlockSpec(shape, lambda *_: (0,) * nd, pipeline_mode=pl.Buffered(1))


def _rms(x, w):
    return x * lax.rsqrt(jnp.mean(x * x, axis=-1, keepdims=True) + EPS) * w


_C_KA, _C_CKV, _C_GA, _C_GB, _C_END = 0, 512, 768, 1792, 2816
_KR_ROW = DH_NOPE + DH_ROPE
_HALF = DH_ROPE // 2


def _store_pairs(ref, v):
    for g in range(N_PAIRS):
        ref[0, g] = v[:, g * PAIR_W:(g + 1) * PAIR_W]


def _rope_rows(x, c, s):
    x1, x2 = x[:_HALF], x[_HALF:]
    return [x1 * c - x2 * s, x1 * s + x2 * c]


def _in_proj_kernel(x_ref, nw_ref, wm_ref, wqt_ref, wvt_ref, kvn_ref, wuk_ref, wuvt_ref, cq_ref, sq_ref,
                    ck_ref, sk_ref, qa_ref, ka_ref, vat_ref, kaf_ref, vaf_ref, qf_ref, ckv_ref, kr_ref, ga_ref, gb_ref,
                    kf_ref, vt_ref, *, n):
    i = pl.program_id(1)
    xn = _rms(x_ref[0], nw_ref[...]).astype(BF16)

    def proj(lo, hi):
        return jnp.dot(xn, wm_ref[:, lo:hi], preferred_element_type=F32)

    ka = proj(_C_KA, _C_CKV)
    _store_pairs(ka_ref, ka.astype(BF16))
    vat = lax.dot_general(wvt_ref[...], xn, _NT, preferred_element_type=F32)
    vat_ref[0, 0] = vat.astype(BF16)
    vaf_ref[0] = vat

    def ka_tail():
        kaf_ref[0] = ka.T

    if n == 1:
        ka_tail()
    else:
        pl.when(i == n - 1)(ka_tail)

    qt = lax.dot_general(wqt_ref[...], xn, _NT, preferred_element_type=F32)
    for g in range(N_PAIRS):
        qa_ref[0, g] = qt[D_QF + g * PAIR_W:D_QF + (g + 1) * PAIR_W].astype(BF16)
    cq, sq = cq_ref[...], sq_ref[...]
    pad = jnp.zeros((HEAD_PAD - _KR_ROW, qt.shape[1]), F32)
    for h in range(H_B):
        slab = qt[h * HEAD_PAD:(h + 1) * HEAD_PAD]
        rows = [slab[:DH_NOPE] * (MLA_SCALE * LOG2E)] + _rope_rows(slab[DH_NOPE:_KR_ROW], cq, sq) + [pad]
        qf_ref[0, 0, h * HEAD_PAD:(h + 1) * HEAD_PAD, :] = jnp.concatenate(rows, axis=0).astype(BF16)
    kr_t = _rope_rows(qt[_KR_ROW:HEAD_PAD], ck_ref[...], sk_ref[...])
    zero_rows = lambda r: jnp.zeros((r, qt.shape[1]), F32)
    kr_ref[0] = jnp.concatenate(kr_t, axis=0)

    ckv = _rms(proj(_C_CKV, _C_GA), kvn_ref[...])
    ckv_ref[0] = ckv
    c_bf = ckv.astype(BF16)
    k_nope = jnp.dot(c_bf, wuk_ref[...], preferred_element_type=F32)
    k_rope = jnp.concatenate([zero_rows(DH_NOPE)] + kr_t + [zero_rows(HEAD_PAD - _KR_ROW)], axis=0).T
    for h in range(H_B):
        slab = slice(h * HEAD_PAD, (h + 1) * HEAD_PAD)
        kf_ref[0, :, slab] = (k_nope[:, slab] + k_rope).astype(BF16)
    vt_ref[0, 0] = lax.dot_general(wuvt_ref[...], c_bf, _NT, preferred_element_type=F32).astype(BF16)
    ga_ref[0] = proj(_C_GA, _C_GB).astype(BF16)
    gb_ref[0] = proj(_C_GB, _C_END).astype(BF16)


def _in_proj(x, nw, wm, wqt, wvt, kvn, wuk, wuvt, cq, sq, ck, sk, *, tm):
    g, s, _ = x.shape
    n = s // tm
    tok = lambda w: pl.BlockSpec((1, tm, w), lambda a, b: (a, b, 0))
    tab = pl.BlockSpec((_HALF, tm), lambda a, b: (0, b))
    tail = pl.BlockSpec((1, D_A, tm), lambda a, b: (a, 0, 0))
    out_shape = (
        jax.ShapeDtypeStruct((g, N_PAIRS, PAIR_W, s), BF16),
        jax.ShapeDtypeStruct((g, N_PAIRS, s, PAIR_W), BF16),
        jax.ShapeDtypeStruct((g, n, D_A, tm), BF16),
        jax.ShapeDtypeStruct((g, D_A, tm), F32),
        jax.ShapeDtypeStruct((g, D_A, tm), F32),
        jax.ShapeDtypeStruct((g, n, D_QF, tm), BF16),
        jax.ShapeDtypeStruct((g, s, D_C), F32),
        jax.ShapeDtypeStruct((g, DH_ROPE, s), F32),
        jax.ShapeDtypeStruct((g, s, D_MODEL), BF16),
        jax.ShapeDtypeStruct((g, s, D_MODEL), BF16),
        jax.ShapeDtypeStruct((g, s, D_QF), BF16),
        jax.ShapeDtypeStruct((g, n, D_VB, tm), BF16),
    )
    pairs = pl.BlockSpec((1, N_PAIRS, tm, PAIR_W), lambda a, b: (a, 0, b, 0))
    pairs_t = pl.BlockSpec((1, N_PAIRS, PAIR_W, tm), lambda a, b: (a, 0, 0, b))
    out_specs = (pairs_t, pairs, pl.BlockSpec((1, 1, D_A, tm), lambda a, b: (a, b, 0, 0)), tail, tail,
                 pl.BlockSpec((1, 1, D_QF, tm), lambda a, b: (a, b, 0, 0)), tok(D_C),
                 pl.BlockSpec((1, DH_ROPE, tm), lambda a, b: (a, 0, b)), tok(D_MODEL),
                 tok(D_MODEL), tok(D_QF), pl.BlockSpec((1, 1, D_VB, tm), lambda a, b: (a, b, 0, 0)))
    in_specs = [tok(D_MODEL), _resident(nw.shape), _resident(wm.shape), _resident(wqt.shape), _resident(wvt.shape),
                _resident(kvn.shape), _resident(wuk.shape), _resident(wuvt.shape), tab, tab, tab,
                tab]
    return pl.pallas_call(
        functools.partial(_in_proj_kernel, n=n), grid=(g, n), in_specs=in_specs, out_specs=out_specs, out_shape=out_shape,
        compiler_params=pltpu.CompilerParams(dimension_semantics=("arbitrary", "arbitrary"),
                                             vmem_limit_bytes=VMEM_LIMIT_V7X),
        name="in_proj")(x, nw, wm, wqt, wvt, kvn, wuk, wuvt, cq, sq, ck, sk)


def _kv_up_kernel(ckv_ref, kr_ref, wk_ref, ek_ref, wvt_ref, kf_ref, vt_ref, *, tk):
    c = ckv_ref[0].astype(BF16)
    kf = jnp.dot(c, wk_ref[...], preferred_element_type=F32)
    kf = kf + jnp.dot(kr_ref[0].astype(BF16), ek_ref[...], preferred_element_type=F32)
    kf_ref[0] = kf.astype(BF16)
    for j in range(c.shape[0] // tk):
        vt_ref[0, j] = lax.dot_general(wvt_ref[...], c[j * tk:(j + 1) * tk], _NT,
                                       preferred_element_type=F32).astype(BF16)


def _kv_up(ckv, kr, wk, ek, wvt, *, tm, tk):
    g, s, _ = ckv.shape
    return pl.pallas_call(
        functools.partial(_kv_up_kernel, tk=tk), grid=(g, s // tm),
        in_specs=[pl.BlockSpec((1, tm, D_C), lambda a, b: (a, b, 0)),
                  pl.BlockSpec((1, tm, DH_ROPE), lambda a, b: (a, b, 0)),
                  _resident(wk.shape), _resident(ek.shape), _resident(wvt.shape)],
        out_specs=(pl.BlockSpec((1, tm, D_QF), lambda a, b: (a, b, 0)),
                   pl.BlockSpec((1, tm // tk, D_VB, tk), lambda a, b: (a, b, 0, 0))),
        out_shape=(jax.ShapeDtypeStruct((g, s, D_QF), BF16),
                   jax.ShapeDtypeStruct((g, s // tk, D_VB, tk), BF16)),
        compiler_params=pltpu.CompilerParams(dimension_semantics=("arbitrary", "arbitrary"),
                                             vmem_limit_bytes=VMEM_LIMIT_V7X),
        name="kv_up")(ckv, kr, wk, ek, wvt)


def _band_prompt_kernel(q_ref, kp_ref, kc_ref, vp_ref, vc_ref, bias_ref, o_ref, s_scr, *, tm, tq):
    subs = [(lo, lo + tq) for lo in range(0, tm, tq)]
    row = lax.broadcasted_iota(jnp.int32, (PAIR_W, tq), 0)

    def tile(first):
        def window(lo, hi):
            return (hi, tm - lo) if first else (tm - lo + hi, 0)

        def scores(g, j):
            lo, hi = subs[j]
            nk, b0 = window(lo, hi)
            q2 = q_ref[0, g, :, lo:hi]
            qcat = jnp.concatenate([jnp.where(row < DH_A, q2, jnp.zeros((), BF16)),
                                    jnp.where(row >= DH_A, q2, jnp.zeros((), BF16))], axis=1)
            k2 = kc_ref[0, g, :hi, :]
            if not first:
                k2 = jnp.concatenate([kp_ref[0, g, lo:, :], k2], axis=0)
            bias = jnp.concatenate([bias_ref[2 * g + hl, b0:b0 + nk, :] for hl in range(2)], axis=1)
            return jnp.dot(k2, qcat, preferred_element_type=F32) + bias

        def stash(j, s):
            s_scr[j, :s.shape[0], :] = s
            return jnp.max(s, axis=0, keepdims=True)

        def consume(g, j, col_max):
            lo, hi = subs[j]
            nk, _ = window(lo, hi)
            outs = []
            for hl in range(2):
                cols = slice(hl * tq, (hl + 1) * tq)
                p = jnp.exp2(s_scr[j, :nk, cols] - col_max[:, cols]).astype(BF16)
                rows = pl.ds(g * PAIR_W + hl * DH_A, DH_A)
                v = vc_ref[0, 0, rows, :hi]
                if not first:
                    v = jnp.concatenate([vp_ref[0, 0, rows, lo:], v], axis=1)
                v = jnp.concatenate([v, jnp.ones((_L_ROWS, nk), BF16)], axis=0)
                o = jnp.dot(v, p, preferred_element_type=F32)
                outs.append(o[:DH_A] / o[DH_A:DH_A + 1])
            o_ref[0, g, lo:hi, :] = jnp.concatenate(outs, axis=0).T.astype(BF16)

        def body(g, col_max):
            new_max = []
            for j in range(len(subs)):
                nxt = scores(g + 1, j)
                consume(g, j, col_max[j])
                new_max.append(stash(j, nxt))
            return tuple(new_max)

        col_max = tuple(stash(j, scores(0, j)) for j in range(len(subs)))
        for g in range(N_PAIRS - 1):
            col_max = body(g, col_max)
        for j in range(len(subs)):
            consume(N_PAIRS - 1, j, col_max[j])

    pl.when(pl.program_id(1) == 0)(functools.partial(tile, True))
    pl.when(pl.program_id(1) > 0)(functools.partial(tile, False))


def _band_prompt(qa, ka, vat, bias, *, tm, tq):
    g, _, _, s = qa.shape
    cur = lambda a, b: (a, 0, b, 0)
    prev = lambda a, b: (a, 0, jnp.maximum(b - 1, 0), 0)
    prev_v = lambda a, b: (a, jnp.maximum(b - 1, 0), 0, 0)
    pairs = lambda idx: pl.BlockSpec((1, N_PAIRS, tm, PAIR_W), idx)
    return pl.pallas_call(
        functools.partial(_band_prompt_kernel, tm=tm, tq=tq), grid=(g, s // tm),
        in_specs=[pl.BlockSpec((1, N_PAIRS, PAIR_W, tm), lambda a, b: (a, 0, 0, b)), pairs(prev), pairs(cur),
                  pl.BlockSpec((1, 1, D_A, tm), prev_v), pl.BlockSpec((1, 1, D_A, tm), lambda a, b: (a, b, 0, 0)),
                  _resident(bias.shape)],
        out_specs=pairs(cur),
        out_shape=jax.ShapeDtypeStruct((g, N_PAIRS, s, PAIR_W), BF16),
        scratch_shapes=[pltpu.VMEM((tm // tq, tm + tq, 2 * tq), F32)],
        compiler_params=pltpu.CompilerParams(dimension_semantics=("arbitrary", "arbitrary"),
                                             vmem_limit_bytes=VMEM_LIMIT_V7X),
        name="band_prompt")(qa, ka, ka, vat, vat, bias)


def _band_sample_kernel(q_ref, kc_ref, kn_ref, vc_ref, vn_ref, bias_ref, o_ref):
    lane = lax.broadcasted_iota(jnp.int32, q_ref.shape[2:], 1)
    for g in range(N_PAIRS):
        rows = slice(g * PAIR_W, (g + 1) * PAIR_W)
        kt = jnp.concatenate([kc_ref[0, rows, :], kn_ref[0, rows, :]], axis=1).astype(BF16)
        q2 = q_ref[g, 0]
        outs = []
        for hl in range(2):
            h = 2 * g + hl
            qm = jnp.where((lane >= DH_A) == (hl == 1), q2, jnp.zeros((), BF16))
            s = jnp.dot(qm, kt, preferred_element_type=F32) + bias_ref[h]
            p = jnp.exp2(s - jnp.max(s, axis=1, keepdims=True))
            hrows = slice(h * DH_A, (h + 1) * DH_A)
            vt = jnp.concatenate([vc_ref[0, hrows, :], vn_ref[0, hrows, :]], axis=1).astype(BF16)
            o = lax.dot_general(p.astype(BF16), vt, _NT, preferred_element_type=F32)
            outs.append(o / jnp.sum(p, axis=1, keepdims=True))
        o_ref[g, 0] = jnp.concatenate(outs, axis=1).astype(BF16)


def _band_sample(qa, kc, kn, vc, vn, bias):
    _, g, t, _ = qa.shape
    pairs = pl.BlockSpec((N_PAIRS, 1, t, PAIR_W), lambda a: (0, a, 0, 0))
    per_seq = lambda v: pl.BlockSpec((1,) + v.shape[1:], lambda a: (a, 0, 0))
    return pl.pallas_call(
        _band_sample_kernel, grid=(g,),
        in_specs=[pairs, per_seq(kc), per_seq(kn), per_seq(vc), per_seq(vn), _resident(bias.shape)],
        out_specs=pairs,
        out_shape=jax.ShapeDtypeStruct((N_PAIRS, g, t, PAIR_W), BF16),
        compiler_params=pltpu.CompilerParams(dimension_semantics=("arbitrary",),
                                             vmem_limit_bytes=VMEM_LIMIT_V7X),
        name="band_sample")(qa, kc, kn, vc, vn, bias)


def _mla_kernel(q_ref, k_ref, vt_ref, mask_ref, o_ref, s_scr, *, tq, tk, causal, pairs):
    nq = q_ref.shape[1]
    nk = k_ref.shape[1] // tk
    ones = jnp.ones((_L_ROWS, tk), BF16)

    def q_rows(hp, i):
        return [q_ref[0, i, (2 * hp + hl) * HEAD_PAD:(2 * hp + hl + 1) * HEAD_PAD, :] for hl in range(2)]

    def scores(hp, qs, kk, hl):
        krow = pl.multiple_of(kk * tk, tk)
        return jnp.dot(k_ref[0, pl.ds(krow, tk), (2 * hp + hl) * HEAD_PAD:(2 * hp + hl + 1) * HEAD_PAD], qs[hl],
                       preferred_element_type=F32)

    def stash(hl, s):
        s_scr[hl] = s
        return jnp.max(s, axis=0, keepdims=True)

    def consume(hp, kk, hl, m, acc, tile_max, mask_j):
        if mask_j is not None:
            s = s_scr[hl] + mask_ref[mask_j]
            m_new = jnp.maximum(m, jnp.max(s, axis=0, keepdims=True))
        else:
            s = s_scr[hl]
            m_new = jnp.maximum(m, tile_max)
        p = jnp.exp2(s - m_new).astype(BF16)
        v = jnp.concatenate([vt_ref[0, kk, (2 * hp + hl) * DV_B:(2 * hp + hl + 1) * DV_B, :], ones], axis=0)
        return m_new, acc * jnp.exp2(m - m_new) + jnp.dot(v, p, preferred_element_type=F32)

    def q_tile(hp, i, first_max):
        qs = q_rows(hp, i)

        def body(t, carry):
            out = []
            for hl in range(2):
                m, acc, tile_max = carry[3 * hl:3 * hl + 3]
                nxt = scores(hp, qs, t + 1, hl)
                out += list(consume(hp, t, hl, m, acc, tile_max, None)) + [stash(hl, nxt)]
            return tuple(out)

        r = tq // tk
        n_full = r * i if causal else nk - 1
        carry = ()
        for hl in range(2):
            carry += (jnp.full((1, tq), NEG, F32), jnp.zeros((DV_B + _L_ROWS, tq), F32), first_max[hl])
        done = 0
        for n in (4, 2, 1):
            left = n_full - done
            trips = left // n if isinstance(left, int) else lax.shift_right_logical(left, n.bit_length() - 1)

            def trip(u, c, n=n, done=done):
                for j in range(n):
                    c = body(done + n * u + j, c)
                return c

            carry = lax.fori_loop(0, trips, trip, carry)
            done = done + trips * n
        tail = [(r * i + j, j) for j in range(r)] if causal else [(nk - 1, None)]
        qs_next = q_rows(hp, jnp.minimum(i + 1, nq - 1)) if nq > 1 else None
        state = [list(carry[3 * hl:3 * hl + 3]) for hl in range(2)]
        for idx, (tile, mask_j) in enumerate(tail):
            for hl in range(2):
                if idx + 1 < len(tail):
                    nxt = scores(hp, qs, tile + 1, hl)
                else:
                    nxt = scores(hp, qs_next, 0, hl) if nq > 1 else None
                m, acc = consume(hp, tile, hl, *state[hl], mask_j)
                state[hl] = [m, acc, stash(hl, nxt) if nxt is not None else state[hl][2]]
        accs = [state[hl][1] for hl in range(2)]
        next_max = [state[hl][2] for hl in range(2)]
        o = jnp.concatenate([acc[:DV_B] / acc[DV_B:DV_B + 1] for acc in accs], axis=0)
        o_ref[0, pl.ds(pl.multiple_of(i * tq, tq), tq), hp * 2 * DV_B:(hp + 1) * 2 * DV_B] = o.T.astype(BF16)
        return tuple(next_max)

    for hp in range(pairs):
        first = q_rows(hp, 0)
        lax.fori_loop(0, nq, functools.partial(q_tile, hp),
                      tuple(stash(hl, scores(hp, first, 0, hl)) for hl in range(2)))


def _mla_attn(qf, kf, vt, mask, *, tq, tk, causal, pairs=1):
    g, nq, _, _ = qf.shape
    sq, sk = nq * tq, kf.shape[1]
    return pl.pallas_call(
        functools.partial(_mla_kernel, tq=tq, tk=tk, causal=causal, pairs=pairs), grid=(g, H_B // 2 // pairs),
        in_specs=[pl.BlockSpec((1, nq, 2 * pairs * HEAD_PAD, tq), lambda a, b: (a, 0, b, 0)),
                  pl.BlockSpec((1, sk, 2 * pairs * HEAD_PAD), lambda a, b: (a, 0, b)),
                  pl.BlockSpec((1, sk // tk, 2 * pairs * DV_B, tk), lambda a, b: (a, 0, b, 0)),
                  _resident(mask.shape)],
        out_specs=pl.BlockSpec((1, sq, 2 * pairs * DV_B), lambda a, b: (a, 0, b)),
        out_shape=jax.ShapeDtypeStruct((g, sq, D_VB), BF16),
        scratch_shapes=[pltpu.VMEM((2, tk, tq), F32)],
        compiler_params=pltpu.CompilerParams(dimension_semantics=("arbitrary", "arbitrary"),
                                             vmem_limit_bytes=VMEM_LIMIT_V7X),
        name="mla_attn")(qf, kf, vt, mask)


_G0 = 8


def _merge_ffn_kernel(x_ref, ya_ref, yb_ref, ga_ref, gb_ref, st_ref, wa_ref, wb_ref, wo_ref, n1_ref, n2_ref, n3_ref,
                      wg_ref, wu_ref, cw_ref, cb_ref, wd_ref, y_ref, cs_ref, gbuf, *, tm, nseq):
    i = pl.program_id(1)
    seg = tm // nseq
    ya = jnp.concatenate([ya_ref[0, g] for g in range(N_PAIRS)], axis=1)
    za = jnp.dot(ya, wa_ref[...], preferred_element_type=F32)
    zb = jnp.dot(yb_ref[0], wb_ref[...], preferred_element_type=F32)
    mix = jax.nn.sigmoid(ga_ref[0].astype(F32)) * za + jax.nn.sigmoid(gb_ref[0].astype(F32)) * zb
    mo = jnp.dot(mix.astype(BF16), wo_ref[...], preferred_element_type=F32)
    x1 = x_ref[0] + _rms(mo, n1_ref[...])
    xn = _rms(x1, n2_ref[...]).astype(BF16)

    starts = [_G0 + j * (seg + _G0) for j in range(nseq)]
    if nseq == 1:
        @pl.when(i == 0)
        def _():
            gbuf[_G0 - 2:_G0, :] = st_ref[0]

        @pl.when(i > 0)
        def _():
            gbuf[_G0 - 2:_G0, :] = gbuf[_G0 + tm - 2:_G0 + tm, :]
    else:
        for j, r0 in enumerate(starts):
            gbuf[r0 - 2:r0, :] = st_ref[j]

    gate = jnp.dot(xn, wg_ref[...], preferred_element_type=F32)
    for j, r0 in enumerate(starts):
        gbuf[r0:r0 + seg, :] = gate[j * seg:(j + 1) * seg]
        cs_ref[j] = gbuf[r0 + seg - 2:r0 + seg, :]
    u = jnp.dot(xn, wu_ref[...], preferred_element_type=F32)
    c = jnp.concatenate([cw_ref[0:1, :] * gbuf[r0 - 2:r0 - 2 + seg, :] + cw_ref[1:2, :] * gbuf[r0 - 1:r0 - 1 + seg, :]
                         + cw_ref[2:3, :] * gbuf[r0:r0 + seg, :] for r0 in starts], axis=0) + cb_ref[...]
    hid = (jax.nn.gelu(c, approximate=True) * u).astype(BF16)
    f = jnp.dot(hid, wd_ref[...], preferred_element_type=F32)
    y_ref[0] = x1 + _rms(f, n3_ref[...])


def _merge_ffn(x, ya, yb, ga, gb, state, wa, wb, wo, n1, n2, n3, wg, wu, cw, cb, wd, *, tm, nseq=1):
    g, s, _ = x.shape
    assert nseq == 1 or s == tm
    tok = lambda w: pl.BlockSpec((1, tm, w), lambda a, b: (a, b, 0))
    per_group = pl.BlockSpec((nseq, CONV_W - 1, D_FF), lambda a, b: (a, 0, 0))
    weights = (wa, wb, wo, n1, n2, n3, wg, wu, cw, cb, wd)
    return pl.pallas_call(
        functools.partial(_merge_ffn_kernel, tm=tm, nseq=nseq), grid=(g, s // tm),
        in_specs=[tok(D_MODEL), pl.BlockSpec((1, N_PAIRS, tm, PAIR_W), lambda a, b: (a, 0, b, 0)), tok(D_VB),
                  tok(D_MODEL), tok(D_MODEL), per_group]
                 + [_resident(w.shape) for w in weights],
        out_specs=(tok(D_MODEL), per_group),
        out_shape=(jax.ShapeDtypeStruct((g, s, D_MODEL), F32),
                   jax.ShapeDtypeStruct((g * nseq, CONV_W - 1, D_FF), F32)),
        scratch_shapes=[pltpu.VMEM((tm + nseq * _G0, D_FF), F32)],
        compiler_params=pltpu.CompilerParams(dimension_semantics=("arbitrary", "arbitrary"),
                                             vmem_limit_bytes=VMEM_LIMIT_V7X),
        name="merge_ffn")(x, ya, yb, ga, gb, state, *weights)


def _rope_tables(pos, scale):
    inv = ROPE_BASE ** (-np.arange(_HALF, dtype=np.float64) / _HALF)
    ang = inv[:, None] * np.asarray(pos, np.float64)[None, :]
    return jnp.asarray(np.cos(ang) * scale, F32), jnp.asarray(np.sin(ang) * scale, F32)


def _band_bias_kernel(ext_ref, o_ref, *, tq, nk, q_major):
    w = ext_ref.shape[-1]
    toeplitz = pltpu.roll(jnp.broadcast_to(ext_ref[0], (nk, w)), 0, 1, stride=1, stride_axis=0)[:, :tq]
    qc = (nk - tq + lax.broadcasted_iota(jnp.int32, (nk, tq), 1)) // CHUNK
    kc = lax.broadcasted_iota(jnp.int32, (nk, tq), 0) // CHUNK
    bias = jnp.where((kc >= qc - LEFT_CHUNKS) & (kc <= qc), toeplitz, NEG)
    o_ref[0] = bias.T if q_major else bias


def _band_bias(table, tq, nk, q_major=False):
    w = nk + tq
    d_lo, d_hi = 1 - tq, w - tq
    assert -MAX_REL <= d_lo and d_hi >= MAX_REL and w % 128 == 0
    h = table.shape[0]
    ext = jnp.concatenate([table[:, d_lo + MAX_REL:], jnp.broadcast_to(table[:, -1:], (h, d_hi - MAX_REL))], axis=1)
    ext = (jnp.roll(ext, -(nk - 1), axis=1) * LOG2E).reshape(h, 1, w)
    return pl.pallas_call(
        functools.partial(_band_bias_kernel, tq=tq, nk=nk, q_major=q_major), grid=(h,),
        in_specs=[pl.BlockSpec((1, 1, w), lambda a: (a, 0, 0))],
        out_specs=pl.BlockSpec((1, tq, nk) if q_major else (1, nk, tq), lambda a: (a, 0, 0)),
        out_shape=jax.ShapeDtypeStruct((h, tq, nk) if q_major else (h, nk, tq), F32),
        compiler_params=pltpu.CompilerParams(dimension_semantics=("arbitrary",)),
        name="band_bias")(ext)


def _chunk_masks(tk, tq):
    kc = jnp.arange(tq)[:, None] // CHUNK
    qc = jnp.arange(tq)[None, :] // CHUNK
    return jnp.where(kc <= qc, 0.0, NEG).astype(F32).reshape(tq // tk, tk, tq)


def _scaled_cast_kernel(w_ref, o_ref, *, rows, n_scaled, scale):
    r = pl.program_id(0) * rows + lax.broadcasted_iota(jnp.int32, (rows, 1), 0)
    o_ref[...] = (w_ref[0] * jnp.where(r < n_scaled, scale, 1.0)).astype(BF16)


def _scaled_cast(w, n_scaled, scale, *, rows):
    _, d, n = w.shape
    return pl.pallas_call(
        functools.partial(_scaled_cast_kernel, rows=rows, n_scaled=n_scaled, scale=scale), grid=(d // rows,),
        in_specs=[pl.BlockSpec((1, rows, n), lambda a: (0, a, 0))],
        out_specs=pl.BlockSpec((rows, n), lambda a: (a, 0)),
        out_shape=jax.ShapeDtypeStruct((d, n), BF16),
        compiler_params=pltpu.CompilerParams(dimension_semantics=("arbitrary",)),
        name="cast_w_in")(w)


def _prep_weights(w_in, w_uk, w_uv):
    o = np.cumsum((0,) + IN_SIZES)
    d = w_in.shape[1]
    w_t = _scaled_cast(jnp.swapaxes(w_in, 1, 2), int(o[1]), A_SCALE * LOG2E, rows=464)
    row = lambda j: w_t[o[j]:o[j + 1]]
    qn = row(3).reshape(H_B, DH_NOPE, d)
    qr = row(4).reshape(H_B, DH_ROPE, d)
    pad = jnp.zeros((H_B, HEAD_PAD - DH_NOPE - DH_ROPE, d), BF16).at[0].set(row(6))
    wqt = jnp.concatenate([jnp.concatenate([qn, qr, pad], axis=1).reshape(D_QF, d), row(0)], axis=0)
    wm = jnp.concatenate([row(1), row(5), row(7), row(8)], axis=0).T
    wvat = row(2)
    wk = jnp.concatenate([w_uk, jnp.zeros((D_C, H_B, HEAD_PAD - DH_NOPE), F32)], axis=2).reshape(D_C, D_QF).astype(BF16)
    place = jnp.concatenate([jnp.zeros((DH_ROPE, DH_NOPE), F32), jnp.eye(DH_ROPE, dtype=F32),
                             jnp.zeros((DH_ROPE, HEAD_PAD - DH_NOPE - DH_ROPE), F32)], axis=1)
    ek = jnp.tile(place, (1, H_B)).astype(BF16)
    wvt = w_uv.reshape(D_C, D_VB).T.astype(BF16)
    return wm, wqt, wvat, wk, ek, wvt


def kernel(x_prompt, x_sample, cache_a_k, cache_a_v, cache_mla_ckv, cache_mla_krope, state_ffn_conv, norm_mix_pre,
           norm_mix_post, w_in, rel_bias_table, kv_norm, w_uk, w_uv, w_branch_a, w_branch_b, w_out, norm_ffn_pre,
           norm_ffn_post, w_ffn_gate, w_ffn_up, conv_w, conv_b, w_ffn_down):
    assert w_in.shape[0] == 1, "single layer"
    b, s, _ = x_prompt.shape
    db, t, _ = x_sample.shape
    past = cache_mla_ckv.shape[2]
    wcache = cache_a_k.shape[2]
    keep = min(A_WINDOW, s)
    tm1 = 512
    assert keep == tm1 and db * t == tm1 and wcache == A_WINDOW

    wm, wqt, wvat, wk, ek, wvt = _prep_weights(w_in, w_uk[0], w_uv[0])
    row = lambda v: v.reshape(1, -1)
    proj_w = (row(norm_mix_pre[0]), wm, wqt, wvat, row(kv_norm[0]), wk, wvt)
    ffn_w = (w_branch_a[0].astype(BF16), w_branch_b[0].astype(BF16), w_out[0].astype(BF16), row(norm_mix_post[0]),
             row(norm_ffn_pre[0]), row(norm_ffn_post[0]), w_ffn_gate[0].astype(BF16), w_ffn_up[0].astype(BF16),
             conv_w[0], row(conv_b[0]), w_ffn_down[0].astype(BF16))
    table = rel_bias_table[0]

    pos = np.arange(s)
    tabs = _rope_tables(pos, MLA_SCALE * LOG2E) + _rope_tables(pos, 1.0)
    t_mla = tm1
    qa, ka, vat, kaf, vaf, qf, ckv, kr, ga, gb, kf, vt = _in_proj(x_prompt, *proj_w, *tabs, tm=tm1)
    ya = _band_prompt(qa, ka, vat, _band_bias(table, 256, A_WINDOW + 256), tm=tm1, tq=256)
    yb = _mla_attn(qf, kf, vt, _chunk_masks(t_mla, t_mla), tq=t_mla, tk=t_mla, causal=True)
    y_prompt, conv_p = _merge_ffn(x_prompt, ya, yb, ga, gb, jnp.zeros((b, CONV_W - 1, D_FF), F32), *ffn_w, tm=512)

    pos_s = np.tile(past + np.arange(t), db)
    tabs_s = _rope_tables(pos_s, MLA_SCALE * LOG2E) + _rope_tables(pos_s, 1.0)
    qa2, ka2, vat2, kaf2, vaf2, qf2, ckv2, kr2, ga2, gb2, _, _ = _in_proj(x_sample.reshape(1, db * t, D_MODEL), *proj_w,
                                                                    *tabs_s, tm=tm1)
    per_seq = lambda v: v.reshape(db, t, v.shape[-1])
    ckv2, kr2 = per_seq(ckv2[0]), per_seq(kr2[0].T)
    cached_ft = lambda c: jnp.transpose(c[0], (0, 2, 3, 1)).reshape(db, D_A, wcache)
    new_ft = lambda v: jnp.transpose(v[0].reshape(D_A, db, t), (1, 0, 2))
    kc, vc, kn, vn = cached_ft(cache_a_k), cached_ft(cache_a_v), new_ft(kaf2), new_ft(vaf2)
    qa2 = jnp.transpose(qa2[0].reshape(N_PAIRS, PAIR_W, db, t), (0, 2, 3, 1))
    ya2 = _band_sample(qa2, kc, kn, vc, vn, _band_bias(table, t, wcache + t, q_major=True))
    new_k = jnp.concatenate([kc[:, :, t:], kn], axis=2)
    new_v = jnp.concatenate([vc[:, :, t:], vn], axis=2)
    c_all = jnp.concatenate([cache_mla_ckv[0], ckv2], axis=1)
    kr_all = jnp.concatenate([cache_mla_krope[0], kr2], axis=1)
    kf2, vt2 = _kv_up(c_all, kr_all, wk, ek, wvt, tm=past + t, tk=past + t)
    qf2 = jnp.transpose(qf2[0, 0].reshape(D_QF, db, t), (1, 0, 2))[:, None]
    yb2 = _mla_attn(qf2, kf2, vt2, jnp.zeros((1, 8, 128), F32), tq=t, tk=past + t, causal=False, pairs=H_B // 2)
    y_sample, conv_s = _merge_ffn(x_sample.reshape(1, db * t, D_MODEL), ya2.reshape(1, N_PAIRS, db * t, PAIR_W),
                                  yb2.reshape(1, db * t, D_VB), ga2, gb2, state_ffn_conv[0], *ffn_w,
                                  tm=db * t, nseq=db)
    y_sample = y_sample.reshape(db, t, D_MODEL)

    heads_t = lambda v: jnp.transpose(v.reshape(1, v.shape[0], H_A, DH_A, v.shape[2]), (0, 1, 4, 2, 3))
    return (y_prompt, y_sample,
            heads_t(kaf), heads_t(vaf), ckv[None], jnp.swapaxes(kr, 1, 2)[None], conv_p[None],
            heads_t(new_k), heads_t(new_v), ckv2[None], kr2[None], conv_s[None])
```

```python
import functools

import jax
import jax.numpy as jnp
import numpy as np
from jax import lax
from jax.experimental import pallas as pl
from jax.experimental.pallas import tpu as pltpu

D_MODEL = 1024
CHUNK = 64
LEFT_CHUNKS = 8
A_WINDOW = LEFT_CHUNKS * CHUNK
H_A = 8
DH_A = 64
MAX_REL = 256
H_B = 8
DH_NOPE = 64
DH_ROPE = 32
DV_B = 64
D_C = 256
D_FF = 2816
CONV_W = 3
ROPE_BASE = 10000.0
EPS = 1e-6
A_SCALE = DH_A ** -0.5
MLA_SCALE = (DH_NOPE + DH_ROPE) ** -0.5
IN_SIZES = (H_A * DH_A, H_A * DH_A, H_A * DH_A, H_B * DH_NOPE, H_B * DH_ROPE, D_C, DH_ROPE, D_MODEL, D_MODEL)

HEAD_PAD = 128
D_A = H_A * DH_A
D_VB = H_B * DV_B
D_QF = H_B * HEAD_PAD
PAIR_W = 2 * DH_A
N_PAIRS = H_A // 2
NEG = -1e30
LOG2E = 1.4426950408889634
_L_ROWS = 16
VMEM_LIMIT_V7X = 56 * 1024 * 1024

F32 = jnp.float32
BF16 = jnp.bfloat16
_NT = (((1,), (1,)), ((), ()))


def _resident(shape):
    nd = len(shape)
    return pl.BlockSpec(shape, lambda *_: (0,) * nd, pipeline_mode=pl.Buffered(1))


def _rms(x, w):
    return x * lax.rsqrt(jnp.mean(x * x, axis=-1, keepdims=True) + EPS) * w


_C_KA, _C_CKV, _C_GA, _C_GB, _C_END = 0, 512, 768, 1792, 2816
_KR_ROW = DH_NOPE + DH_ROPE
_HALF = DH_ROPE // 2


def _store_pairs(ref, v):
    for g in range(N_PAIRS):
        ref[0, g] = v[:, g * PAIR_W:(g + 1) * PAIR_W]


def _rope_rows(x, c, s):
    x1, x2 = x[:_HALF], x[_HALF:]
    return [x1 * c - x2 * s, x1 * s + x2 * c]


def _in_proj_kernel(x_ref, nw_ref, wm_ref, wqt_ref, wvt_ref, kvn_ref, wuk_ref, wuvt_ref, cq_ref, sq_ref,
                    ck_ref, sk_ref, qa_ref, ka_ref, vat_ref, kaf_ref, vaf_ref, qf_ref, ckv_ref, kr_ref, ga_ref, gb_ref,
                    kf_ref, vt_ref):
    xn = _rms(x_ref[0], nw_ref[...]).astype(BF16)

    def proj(lo, hi):
        return jnp.dot(xn, wm_ref[:, lo:hi], preferred_element_type=F32)

    ka = proj(_C_KA, _C_CKV)
    _store_pairs(ka_ref, ka.astype(BF16))
    vat = lax.dot_general(wvt_ref[...], xn, _NT, preferred_element_type=F32)
    vat_ref[0, 0] = vat.astype(BF16)
    vaf_ref[0] = vat
    kaf_ref[0] = ka.T

    qt = lax.dot_general(wqt_ref[...], xn, _NT, preferred_element_type=F32)
    for g in range(N_PAIRS):
        qa_ref[0, g] = qt[D_QF + g * PAIR_W:D_QF + (g + 1) * PAIR_W].astype(BF16)
    cq, sq = cq_ref[...], sq_ref[...]
    pad = jnp.zeros((HEAD_PAD - _KR_ROW, qt.shape[1]), F32)
    for h in range(H_B):
        slab = qt[h * HEAD_PAD:(h + 1) * HEAD_PAD]
        rows = [slab[:DH_NOPE] * (MLA_SCALE * LOG2E)] + _rope_rows(slab[DH_NOPE:_KR_ROW], cq, sq) + [pad]
        qf_ref[0, 0, h * HEAD_PAD:(h + 1) * HEAD_PAD, :] = jnp.concatenate(rows, axis=0).astype(BF16)
    kr_t = _rope_rows(qt[_KR_ROW:HEAD_PAD], ck_ref[...], sk_ref[...])
    zero_rows = lambda r: jnp.zeros((r, qt.shape[1]), F32)
    kr_ref[0] = jnp.concatenate(kr_t, axis=0)

    ckv = _rms(proj(_C_CKV, _C_GA), kvn_ref[...])
    ckv_ref[0] = ckv
    c_bf = ckv.astype(BF16)
    k_nope = jnp.dot(c_bf, wuk_ref[...], preferred_element_type=F32)
    k_rope = jnp.concatenate([zero_rows(DH_NOPE)] + kr_t + [zero_rows(HEAD_PAD - _KR_ROW)], axis=0).T
    for h in range(H_B):
        slab = slice(h * HEAD_PAD, (h + 1) * HEAD_PAD)
        kf_ref[0, :, slab] = (k_nope[:, slab] + k_rope).astype(BF16)
    vt_ref[0, 0] = lax.dot_general(wuvt_ref[...], c_bf, _NT, preferred_element_type=F32).astype(BF16)
    ga_ref[0] = proj(_C_GA, _C_GB).astype(BF16)
    gb_ref[0] = proj(_C_GB, _C_END).astype(BF16)


def _in_proj(x, nw, wm, wqt, wvt, kvn, wuk, wuvt, cq, sq, ck, sk, *, tm):
    g, s, _ = x.shape
    n = s // tm
    tok = lambda w: pl.BlockSpec((1, tm, w), lambda a, b: (a, b, 0))
    tab = pl.BlockSpec((_HALF, tm), lambda a, b: (0, b))
    tail = pl.BlockSpec((1, D_A, tm), lambda a, b: (a, 0, 0))
    out_shape = (
        jax.ShapeDtypeStruct((g, N_PAIRS, PAIR_W, s), BF16),
        jax.ShapeDtypeStruct((g, N_PAIRS, s, PAIR_W), BF16),
        jax.ShapeDtypeStruct((g, n, D_A, tm), BF16),
        jax.ShapeDtypeStruct((g, D_A, tm), F32),
        jax.ShapeDtypeStruct((g, D_A, tm), F32),
        jax.ShapeDtypeStruct((g, n, D_QF, tm), BF16),
        jax.ShapeDtypeStruct((g, s, D_C), F32),
        jax.ShapeDtypeStruct((g, DH_ROPE, s), F32),
        jax.ShapeDtypeStruct((g, s, D_MODEL), BF16),
        jax.ShapeDtypeStruct((g, s, D_MODEL), BF16),
        jax.ShapeDtypeStruct((g, s, D_QF), BF16),
        jax.ShapeDtypeStruct((g, n, D_VB, tm), BF16),
    )
    pairs = pl.BlockSpec((1, N_PAIRS, tm, PAIR_W), lambda a, b: (a, 0, b, 0))
    pairs_t = pl.BlockSpec((1, N_PAIRS, PAIR_W, tm), lambda a, b: (a, 0, 0, b))
    out_specs = (pairs_t, pairs, pl.BlockSpec((1, 1, D_A, tm), lambda a, b: (a, b, 0, 0)), tail, tail,
                 pl.BlockSpec((1, 1, D_QF, tm), lambda a, b: (a, b, 0, 0)), tok(D_C),
                 pl.BlockSpec((1, DH_ROPE, tm), lambda a, b: (a, 0, b)), tok(D_MODEL),
                 tok(D_MODEL), tok(D_QF), pl.BlockSpec((1, 1, D_VB, tm), lambda a, b: (a, b, 0, 0)))
    in_specs = [tok(D_MODEL), _resident(nw.shape), _resident(wm.shape), _resident(wqt.shape), _resident(wvt.shape),
                _resident(kvn.shape), _resident(wuk.shape), _resident(wuvt.shape), tab, tab, tab,
                tab]
    return pl.pallas_call(
        _in_proj_kernel, grid=(g, n), in_specs=in_specs, out_specs=out_specs, out_shape=out_shape,
        compiler_params=pltpu.CompilerParams(dimension_semantics=("arbitrary", "arbitrary"),
                                             vmem_limit_bytes=VMEM_LIMIT_V7X),
        name="in_proj")(x, nw, wm, wqt, wvt, kvn, wuk, wuvt, cq, sq, ck, sk)


def _kv_up_kernel(cc_ref, cn_ref, kr_ref, wk_ref, ek_ref, wvt_ref, kf_ref, vt_ref):
    c = jnp.concatenate([cc_ref[0], cn_ref[0]], axis=0).astype(BF16)
    kf = jnp.dot(c, wk_ref[...], preferred_element_type=F32)
    kf = kf + jnp.dot(kr_ref[0].astype(BF16), ek_ref[...], preferred_element_type=F32)
    kf_ref[0] = kf.astype(BF16)
    vt_ref[0, 0] = lax.dot_general(wvt_ref[...], c, _NT, preferred_element_type=F32).astype(BF16)


def _kv_up(c_cache, c_new, kr, wk, ek, wvt):
    g, s = c_cache.shape[0], c_cache.shape[1] + c_new.shape[1]
    per_seq = lambda v: pl.BlockSpec((1,) + v.shape[1:], lambda a: (a, 0, 0))
    return pl.pallas_call(
        _kv_up_kernel, grid=(g,),
        in_specs=[per_seq(c_cache), per_seq(c_new), per_seq(kr),
                  _resident(wk.shape), _resident(ek.shape), _resident(wvt.shape)],
        out_specs=(pl.BlockSpec((1, s, D_QF), lambda a: (a, 0, 0)),
                   pl.BlockSpec((1, 1, D_VB, s), lambda a: (a, 0, 0, 0))),
        out_shape=(jax.ShapeDtypeStruct((g, s, D_QF), BF16),
                   jax.ShapeDtypeStruct((g, 1, D_VB, s), BF16)),
        compiler_params=pltpu.CompilerParams(dimension_semantics=("arbitrary",),
                                             vmem_limit_bytes=VMEM_LIMIT_V7X),
        name="kv_up")(c_cache, c_new, kr, wk, ek, wvt)


def _band_prompt_kernel(q_ref, kp_ref, kc_ref, vp_ref, vc_ref, bias_ref, o_ref, s_scr, *, tm, tq):
    subs = [(lo, lo + tq) for lo in range(0, tm, tq)]
    row = lax.broadcasted_iota(jnp.int32, (PAIR_W, tq), 0)

    def tile(first):
        def window(lo, hi):
            return (hi, tm - lo) if first else (tm - lo + hi, 0)

        def scores(g, j):
            lo, hi = subs[j]
            nk, b0 = window(lo, hi)
            q2 = q_ref[0, g, :, lo:hi]
            qcat = jnp.concatenate([jnp.where(row < DH_A, q2, jnp.zeros((), BF16)),
                                    jnp.where(row >= DH_A, q2, jnp.zeros((), BF16))], axis=1)
            k2 = kc_ref[0, g, :hi, :]
            if not first:
                k2 = jnp.concatenate([kp_ref[0, g, lo:, :], k2], axis=0)
            bias = jnp.concatenate([bias_ref[2 * g + hl, b0:b0 + nk, :] for hl in range(2)], axis=1)
            return jnp.dot(k2, qcat, preferred_element_type=F32) + bias

        def stash(j, s):
            s_scr[j, :s.shape[0], :] = s
            return jnp.max(s, axis=0, keepdims=True)

        def consume(g, j, col_max):
            lo, hi = subs[j]
            nk, _ = window(lo, hi)
            outs = []
            for hl in range(2):
                cols = slice(hl * tq, (hl + 1) * tq)
                p = jnp.exp2(s_scr[j, :nk, cols] - col_max[:, cols]).astype(BF16)
                rows = pl.ds(g * PAIR_W + hl * DH_A, DH_A)
                v = vc_ref[0, 0, rows, :hi]
                if not first:
                    v = jnp.concatenate([vp_ref[0, 0, rows, lo:], v], axis=1)
                v = jnp.concatenate([v, jnp.ones((_L_ROWS, nk), BF16)], axis=0)
                o = jnp.dot(v, p, preferred_element_type=F32)
                outs.append(o[:DH_A] / o[DH_A:DH_A + 1])
            o_ref[0, g, lo:hi, :] = jnp.concatenate(outs, axis=0).T.astype(BF16)

        def body(g, col_max):
            new_max = []
            for j in range(len(subs)):
                nxt = scores(g + 1, j)
                consume(g, j, col_max[j])
                new_max.append(stash(j, nxt))
            return tuple(new_max)

        col_max = tuple(stash(j, scores(0, j)) for j in range(len(subs)))
        for g in range(N_PAIRS - 1):
            col_max = body(g, col_max)
        for j in range(len(subs)):
            consume(N_PAIRS - 1, j, col_max[j])

    pl.when(pl.program_id(1) == 0)(functools.partial(tile, True))
    pl.when(pl.program_id(1) > 0)(functools.partial(tile, False))


def _band_prompt(qa, ka, vat, bias, *, tm, tq):
    g, _, _, s = qa.shape
    cur = lambda a, b: (a, 0, b, 0)
    prev = lambda a, b: (a, 0, jnp.maximum(b - 1, 0), 0)
    prev_v = lambda a, b: (a, jnp.maximum(b - 1, 0), 0, 0)
    pairs = lambda idx: pl.BlockSpec((1, N_PAIRS, tm, PAIR_W), idx)
    return pl.pallas_call(
        functools.partial(_band_prompt_kernel, tm=tm, tq=tq), grid=(g, s // tm),
        in_specs=[pl.BlockSpec((1, N_PAIRS, PAIR_W, tm), lambda a, b: (a, 0, 0, b)), pairs(prev), pairs(cur),
                  pl.BlockSpec((1, 1, D_A, tm), prev_v), pl.BlockSpec((1, 1, D_A, tm), lambda a, b: (a, b, 0, 0)),
                  _resident(bias.shape)],
        out_specs=pairs(cur),
        out_shape=jax.ShapeDtypeStruct((g, N_PAIRS, s, PAIR_W), BF16),
        scratch_shapes=[pltpu.VMEM((tm // tq, tm + tq, 2 * tq), F32)],
        compiler_params=pltpu.CompilerParams(dimension_semantics=("arbitrary", "arbitrary"),
                                             vmem_limit_bytes=VMEM_LIMIT_V7X),
        name="band_prompt")(qa, ka, ka, vat, vat, bias)


def _band_sample_kernel(q_ref, kc_ref, kn_ref, vc_ref, vn_ref, bias_ref, o_ref, k_roll_ref, v_roll_ref):
    t = kn_ref.shape[2]
    k_roll_ref[0] = jnp.concatenate([kc_ref[0, :, t:], kn_ref[0]], axis=1)
    v_roll_ref[0] = jnp.concatenate([vc_ref[0, :, t:], vn_ref[0]], axis=1)
    lane = lax.broadcasted_iota(jnp.int32, q_ref.shape[2:], 1)
    for g in range(N_PAIRS):
        rows = slice(g * PAIR_W, (g + 1) * PAIR_W)
        kt = jnp.concatenate([kc_ref[0, rows, :], kn_ref[0, rows, :]], axis=1).astype(BF16)
        q2 = q_ref[g, 0]
        outs = []
        for hl in range(2):
            h = 2 * g + hl
            qm = jnp.where((lane >= DH_A) == (hl == 1), q2, jnp.zeros((), BF16))
            s = jnp.dot(qm, kt, preferred_element_type=F32) + bias_ref[h]
            p = jnp.exp2(s - jnp.max(s, axis=1, keepdims=True))
            hrows = slice(h * DH_A, (h + 1) * DH_A)
            vt = jnp.concatenate([vc_ref[0, hrows, :], vn_ref[0, hrows, :]], axis=1).astype(BF16)
            o = lax.dot_general(p.astype(BF16), vt, _NT, preferred_element_type=F32)
            outs.append(o / jnp.sum(p, axis=1, keepdims=True))
        o_ref[g, 0] = jnp.concatenate(outs, axis=1).astype(BF16)


def _band_sample(qa, kc, kn, vc, vn, bias):
    _, g, t, _ = qa.shape
    pairs = pl.BlockSpec((N_PAIRS, 1, t, PAIR_W), lambda a: (0, a, 0, 0))
    per_seq = lambda v: pl.BlockSpec((1,) + v.shape[1:], lambda a: (a, 0, 0))
    return pl.pallas_call(
        _band_sample_kernel, grid=(g,),
        in_specs=[pairs, per_seq(kc), per_seq(kn), per_seq(vc), per_seq(vn), _resident(bias.shape)],
        out_specs=(pairs, per_seq(kc), per_seq(vc)),
        out_shape=(jax.ShapeDtypeStruct((N_PAIRS, g, t, PAIR_W), BF16),
                   jax.ShapeDtypeStruct(kc.shape, F32), jax.ShapeDtypeStruct(vc.shape, F32)),
        compiler_params=pltpu.CompilerParams(dimension_semantics=("arbitrary",),
                                             vmem_limit_bytes=VMEM_LIMIT_V7X),
        name="band_sample")(qa, kc, kn, vc, vn, bias)


def _mla_kernel(q_ref, k_ref, vt_ref, mask_ref, o_ref, s_scr, *, tq, tk, causal, pairs):
    nq = q_ref.shape[1]
    nk = k_ref.shape[1] // tk
    ones = jnp.ones((_L_ROWS, tk), BF16)

    def q_rows(hp, i):
        return [q_ref[0, i, (2 * hp + hl) * HEAD_PAD:(2 * hp + hl + 1) * HEAD_PAD, :] for hl in range(2)]

    def scores(hp, qs, kk, hl):
        krow = pl.multiple_of(kk * tk, tk)
        return jnp.dot(k_ref[0, pl.ds(krow, tk), (2 * hp + hl) * HEAD_PAD:(2 * hp + hl + 1) * HEAD_PAD], qs[hl],
                       preferred_element_type=F32)

    def stash(hl, s):
        s_scr[hl] = s
        return jnp.max(s, axis=0, keepdims=True)

    def consume(hp, kk, hl, m, acc, tile_max, mask_j):
        if mask_j is not None:
            s = s_scr[hl] + mask_ref[mask_j]
            m_new = jnp.maximum(m, jnp.max(s, axis=0, keepdims=True))
        else:
            s = s_scr[hl]
            m_new = jnp.maximum(m, tile_max)
        p = jnp.exp2(s - m_new).astype(BF16)
        v = jnp.concatenate([vt_ref[0, kk, (2 * hp + hl) * DV_B:(2 * hp + hl + 1) * DV_B, :], ones], axis=0)
        return m_new, acc * jnp.exp2(m - m_new) + jnp.dot(v, p, preferred_element_type=F32)

    def q_tile(hp, i, first_max):
        qs = q_rows(hp, i)

        def body(t, carry):
            out = []
            for hl in range(2):
                m, acc, tile_max = carry[3 * hl:3 * hl + 3]
                nxt = scores(hp, qs, t + 1, hl)
                out += list(consume(hp, t, hl, m, acc, tile_max, None)) + [stash(hl, nxt)]
            return tuple(out)

        r = tq // tk
        n_full = r * i if causal else nk - 1
        carry = ()
        for hl in range(2):
            carry += (jnp.full((1, tq), NEG, F32), jnp.zeros((DV_B + _L_ROWS, tq), F32), first_max[hl])
        done = 0
        for n in (4, 2, 1):
            left = n_full - done
            trips = left // n if isinstance(left, int) else lax.shift_right_logical(left, n.bit_length() - 1)

            def trip(u, c, n=n, done=done):
                for j in range(n):
                    c = body(done + n * u + j, c)
                return c

            carry = lax.fori_loop(0, trips, trip, carry)
            done = done + trips * n
        tail = [(r * i + j, j) for j in range(r)] if causal else [(nk - 1, None)]
        qs_next = q_rows(hp, jnp.minimum(i + 1, nq - 1)) if nq > 1 else None
        state = [list(carry[3 * hl:3 * hl + 3]) for hl in range(2)]
        for idx, (tile, mask_j) in enumerate(tail):
            for hl in range(2):
                if idx + 1 < len(tail):
                    nxt = scores(hp, qs, tile + 1, hl)
                else:
                    nxt = scores(hp, qs_next, 0, hl) if nq > 1 else None
                m, acc = consume(hp, tile, hl, *state[hl], mask_j)
                state[hl] = [m, acc, stash(hl, nxt) if nxt is not None else state[hl][2]]
        accs = [state[hl][1] for hl in range(2)]
        next_max = [state[hl][2] for hl in range(2)]
        o = jnp.concatenate([acc[:DV_B] / acc[DV_B:DV_B + 1] for acc in accs], axis=0)
        o_ref[0, pl.ds(pl.multiple_of(i * tq, tq), tq), hp * 2 * DV_B:(hp + 1) * 2 * DV_B] = o.T.astype(BF16)
        return tuple(next_max)

    for hp in range(pairs):
        first = q_rows(hp, 0)
        lax.fori_loop(0, nq, functools.partial(q_tile, hp),
                      tuple(stash(hl, scores(hp, first, 0, hl)) for hl in range(2)))


def _mla_attn(qf, kf, vt, mask, *, tq, tk, causal, pairs=1):
    g, nq, _, _ = qf.shape
    sq, sk = nq * tq, kf.shape[1]
    return pl.pallas_call(
        functools.partial(_mla_kernel, tq=tq, tk=tk, causal=causal, pairs=pairs), grid=(g, H_B // 2 // pairs),
        in_specs=[pl.BlockSpec((1, nq, 2 * pairs * HEAD_PAD, tq), lambda a, b: (a, 0, b, 0)),
                  pl.BlockSpec((1, sk, 2 * pairs * HEAD_PAD), lambda a, b: (a, 0, b)),
                  pl.BlockSpec((1, sk // tk, 2 * pairs * DV_B, tk), lambda a, b: (a, 0, b, 0)),
                  _resident(mask.shape)],
        out_specs=pl.BlockSpec((1, sq, 2 * pairs * DV_B), lambda a, b: (a, 0, b)),
        out_shape=jax.ShapeDtypeStruct((g, sq, D_VB), BF16),
        scratch_shapes=[pltpu.VMEM((2, tk, tq), F32)],
        compiler_params=pltpu.CompilerParams(dimension_semantics=("arbitrary", "arbitrary"),
                                             vmem_limit_bytes=VMEM_LIMIT_V7X),
        name="mla_attn")(qf, kf, vt, mask)


_G0 = 8


def _merge_ffn_kernel(x_ref, ya_ref, yb_ref, ga_ref, gb_ref, st_ref, wa_ref, wb_ref, wo_ref, n1_ref, n2_ref, n3_ref,
                      wg_ref, wu_ref, cw_ref, cb_ref, wd_ref, y_ref, cs_ref, gbuf, *, tm, nseq):
    i = pl.program_id(1)
    seg = tm // nseq
    starts = [_G0 + j * (seg + _G0) for j in range(nseq)]
    if nseq == 1:
        @pl.when(i == 0)
        def _():
            gbuf[_G0 - 2:_G0, :] = st_ref[0]

        @pl.when(i > 0)
        def _():
            gbuf[_G0 - 2:_G0, :] = gbuf[_G0 + tm - 2:_G0 + tm, :]
    else:
        for j, r0 in enumerate(starts):
            gbuf[r0 - 2:r0, :] = st_ref[j]

    parts = [(0, tm // 2), (tm // 2, tm)] if nseq == 1 else [(0, tm)]

    mixes = []
    for lo, hi in parts:
        ya = jnp.concatenate([ya_ref[0, g, lo:hi, :] for g in range(N_PAIRS)], axis=1)
        za = jnp.dot(ya, wa_ref[...], preferred_element_type=F32)
        zb = jnp.dot(yb_ref[0, lo:hi, :], wb_ref[...], preferred_element_type=F32)
        mixes.append((jax.nn.sigmoid(ga_ref[0, lo:hi, :].astype(F32)) * za
                      + jax.nn.sigmoid(gb_ref[0, lo:hi, :].astype(F32)) * zb).astype(BF16))
    x1s, xns = [], []
    for (lo, hi), mix in zip(parts, mixes):
        mo = jnp.dot(mix, wo_ref[...], preferred_element_type=F32)
        x1s.append(x_ref[0, lo:hi, :] + _rms(mo, n1_ref[...]))
        xns.append(_rms(x1s[-1], n2_ref[...]).astype(BF16))

    ups = []
    for (lo, hi), xn in zip(parts, xns):
        gate = jnp.dot(xn, wg_ref[...], preferred_element_type=F32)
        if nseq == 1:
            gbuf[_G0 + lo:_G0 + hi, :] = gate
        else:
            for j, r0 in enumerate(starts):
                gbuf[r0:r0 + seg, :] = gate[j * seg:(j + 1) * seg]
        ups.append(jnp.dot(xn, wu_ref[...], preferred_element_type=F32))
    for j, r0 in enumerate(starts):
        cs_ref[j] = gbuf[r0 + seg - 2:r0 + seg, :]

    def conv(r0, rows):
        return (cw_ref[0:1, :] * gbuf[r0 - 2:r0 - 2 + rows, :] + cw_ref[1:2, :] * gbuf[r0 - 1:r0 - 1 + rows, :]
                + cw_ref[2:3, :] * gbuf[r0:r0 + rows, :])

    for (lo, hi), u, x1 in zip(parts, ups, x1s):
        if nseq == 1:
            c = conv(_G0 + lo, hi - lo)
        else:
            c = jnp.concatenate([conv(r0, seg) for r0 in starts], axis=0)
        hid = (jax.nn.gelu(c + cb_ref[...], approximate=True) * u).astype(BF16)
        f = jnp.dot(hid, wd_ref[...], preferred_element_type=F32)
        y_ref[0, lo:hi, :] = x1 + _rms(f, n3_ref[...])


def _merge_ffn(x, ya, yb, ga, gb, state, wa, wb, wo, n1, n2, n3, wg, wu, cw, cb, wd, *, tm, nseq=1):
    g, s, _ = x.shape
    assert nseq == 1 or s == tm
    tok = lambda w: pl.BlockSpec((1, tm, w), lambda a, b: (a, b, 0))
    per_group = pl.BlockSpec((nseq, CONV_W - 1, D_FF), lambda a, b: (a, 0, 0))
    weights = (wa, wb, wo, n1, n2, n3, wg, wu, cw, cb, wd)
    return pl.pallas_call(
        functools.partial(_merge_ffn_kernel, tm=tm, nseq=nseq), grid=(g, s // tm),
        in_specs=[tok(D_MODEL), pl.BlockSpec((1, N_PAIRS, tm, PAIR_W), lambda a, b: (a, 0, b, 0)), tok(D_VB),
                  tok(D_MODEL), tok(D_MODEL), per_group]
                 + [_resident(w.shape) for w in weights],
        out_specs=(tok(D_MODEL), per_group),
        out_shape=(jax.ShapeDtypeStruct((g, s, D_MODEL), F32),
                   jax.ShapeDtypeStruct((g * nseq, CONV_W - 1, D_FF), F32)),
        scratch_shapes=[pltpu.VMEM((tm + nseq * _G0, D_FF), F32)],
        compiler_params=pltpu.CompilerParams(dimension_semantics=("arbitrary", "arbitrary"),
                                             vmem_limit_bytes=VMEM_LIMIT_V7X),
        name="merge_ffn")(x, ya, yb, ga, gb, state, *weights)


def _rope_tables(pos, scale):
    inv = ROPE_BASE ** (-np.arange(_HALF, dtype=np.float64) / _HALF)
    ang = inv[:, None] * np.asarray(pos, np.float64)[None, :]
    return jnp.asarray(np.cos(ang) * scale, F32), jnp.asarray(np.sin(ang) * scale, F32)


def _band_bias_kernel(ext_ref, o_ref, *, tq, nk, q_major):
    w = ext_ref.shape[-1]
    toeplitz = pltpu.roll(jnp.broadcast_to(ext_ref[0], (nk, w)), 0, 1, stride=1, stride_axis=0)[:, :tq]
    qc = (nk - tq + lax.broadcasted_iota(jnp.int32, (nk, tq), 1)) // CHUNK
    kc = lax.broadcasted_iota(jnp.int32, (nk, tq), 0) // CHUNK
    bias = jnp.where((kc >= qc - LEFT_CHUNKS) & (kc <= qc), toeplitz, NEG)
    o_ref[0] = bias.T if q_major else bias


def _band_bias(table, tq, nk, q_major=False):
    w = nk + tq
    d_lo, d_hi = 1 - tq, w - tq
    assert -MAX_REL <= d_lo and d_hi >= MAX_REL and w % 128 == 0
    h = table.shape[0]
    ext = jnp.concatenate([table[:, d_lo + MAX_REL:], jnp.broadcast_to(table[:, -1:], (h, d_hi - MAX_REL))], axis=1)
    ext = (jnp.roll(ext, -(nk - 1), axis=1) * LOG2E).reshape(h, 1, w)
    return pl.pallas_call(
        functools.partial(_band_bias_kernel, tq=tq, nk=nk, q_major=q_major), grid=(h,),
        in_specs=[pl.BlockSpec((1, 1, w), lambda a: (a, 0, 0))],
        out_specs=pl.BlockSpec((1, tq, nk) if q_major else (1, nk, tq), lambda a: (a, 0, 0)),
        out_shape=jax.ShapeDtypeStruct((h, tq, nk) if q_major else (h, nk, tq), F32),
        compiler_params=pltpu.CompilerParams(dimension_semantics=("arbitrary",)),
        name="band_bias")(ext)


def _chunk_masks(tk, tq):
    kc = jnp.arange(tq)[:, None] // CHUNK
    qc = jnp.arange(tq)[None, :] // CHUNK
    return jnp.where(kc <= qc, 0.0, NEG).astype(F32).reshape(tq // tk, tk, tq)


def _scaled_cast_kernel(w_ref, o_ref, *, rows, n_scaled, scale):
    r = pl.program_id(0) * rows + lax.broadcasted_iota(jnp.int32, (rows, 1), 0)
    o_ref[...] = (w_ref[0] * jnp.where(r < n_scaled, scale, 1.0)).astype(BF16)


def _scaled_cast(w, n_scaled, scale, *, rows):
    _, d, n = w.shape
    return pl.pallas_call(
        functools.partial(_scaled_cast_kernel, rows=rows, n_scaled=n_scaled, scale=scale), grid=(d // rows,),
        in_specs=[pl.BlockSpec((1, rows, n), lambda a: (0, a, 0))],
        out_specs=pl.BlockSpec((rows, n), lambda a: (a, 0)),
        out_shape=jax.ShapeDtypeStruct((d, n), BF16),
        compiler_params=pltpu.CompilerParams(dimension_semantics=("arbitrary",)),
        name="cast_w_in")(w)


def _prep_weights(w_in, w_uk, w_uv):
    o = np.cumsum((0,) + IN_SIZES)
    d = w_in.shape[1]
    w_t = _scaled_cast(jnp.swapaxes(w_in, 1, 2), int(o[1]), A_SCALE * LOG2E, rows=464)
    row = lambda j: w_t[o[j]:o[j + 1]]
    qn = row(3).reshape(H_B, DH_NOPE, d)
    qr = row(4).reshape(H_B, DH_ROPE, d)
    pad = jnp.zeros((H_B, HEAD_PAD - DH_NOPE - DH_ROPE, d), BF16).at[0].set(row(6))
    wqt = jnp.concatenate([jnp.concatenate([qn, qr, pad], axis=1).reshape(D_QF, d), row(0)], axis=0)
    wm = jnp.concatenate([row(1), row(5), row(7), row(8)], axis=0).T
    wvat = row(2)
    wk = jnp.concatenate([w_uk, jnp.zeros((D_C, H_B, HEAD_PAD - DH_NOPE), F32)], axis=2).reshape(D_C, D_QF).astype(BF16)
    place = jnp.concatenate([jnp.zeros((DH_ROPE, DH_NOPE), F32), jnp.eye(DH_ROPE, dtype=F32),
                             jnp.zeros((DH_ROPE, HEAD_PAD - DH_NOPE - DH_ROPE), F32)], axis=1)
    ek = jnp.tile(place, (1, H_B)).astype(BF16)
    wvt = w_uv.reshape(D_C, D_VB).T.astype(BF16)
    return wm, wqt, wvat, wk, ek, wvt


def kernel(x_prompt, x_sample, cache_a_k, cache_a_v, cache_mla_ckv, cache_mla_krope, state_ffn_conv, norm_mix_pre,
           norm_mix_post, w_in, rel_bias_table, kv_norm, w_uk, w_uv, w_branch_a, w_branch_b, w_out, norm_ffn_pre,
           norm_ffn_post, w_ffn_gate, w_ffn_up, conv_w, conv_b, w_ffn_down):
    assert w_in.shape[0] == 1, "single layer"
    b, s, _ = x_prompt.shape
    db, t, _ = x_sample.shape
    past = cache_mla_ckv.shape[2]
    wcache = cache_a_k.shape[2]
    keep = min(A_WINDOW, s)
    tm1 = 512
    assert keep == tm1 and db * t == tm1 and wcache == A_WINDOW

    wm, wqt, wvat, wk, ek, wvt = _prep_weights(w_in, w_uk[0], w_uv[0])
    row = lambda v: v.reshape(1, -1)
    proj_w = (row(norm_mix_pre[0]), wm, wqt, wvat, row(kv_norm[0]), wk, wvt)
    ffn_w = (w_branch_a[0].astype(BF16), w_branch_b[0].astype(BF16), w_out[0].astype(BF16), row(norm_mix_post[0]),
             row(norm_ffn_pre[0]), row(norm_ffn_post[0]), w_ffn_gate[0].astype(BF16), w_ffn_up[0].astype(BF16),
             conv_w[0], row(conv_b[0]), w_ffn_down[0].astype(BF16))
    table = rel_bias_table[0]

    pos = np.arange(s)
    tabs = _rope_tables(pos, MLA_SCALE * LOG2E) + _rope_tables(pos, 1.0)
    t_mla = tm1
    qa, ka, vat, kaf, vaf, qf, ckv, kr, ga, gb, kf, vt = _in_proj(x_prompt, *proj_w, *tabs, tm=tm1)
    ya = _band_prompt(qa, ka, vat, _band_bias(table, 256, A_WINDOW + 256), tm=tm1, tq=256)
    yb = _mla_attn(qf, kf, vt, _chunk_masks(t_mla, t_mla), tq=t_mla, tk=t_mla, causal=True)
    y_prompt, conv_p = _merge_ffn(x_prompt, ya, yb, ga, gb, jnp.zeros((b, CONV_W - 1, D_FF), F32), *ffn_w, tm=512)

    pos_s = np.tile(past + np.arange(t), db)
    tabs_s = _rope_tables(pos_s, MLA_SCALE * LOG2E) + _rope_tables(pos_s, 1.0)
    qa2, ka2, vat2, kaf2, vaf2, qf2, ckv2, kr2, ga2, gb2, _, _ = _in_proj(x_sample.reshape(1, db * t, D_MODEL), *proj_w,
                                                                    *tabs_s, tm=tm1)
    per_seq = lambda v: v.reshape(db, t, v.shape[-1])
    ckv2, kr2 = per_seq(ckv2[0]), per_seq(kr2[0].T)
    cached_ft = lambda c: jnp.transpose(c[0], (0, 2, 3, 1)).reshape(db, D_A, wcache)
    new_ft = lambda v: jnp.transpose(v[0].reshape(D_A, db, t), (1, 0, 2))
    kc, vc, kn, vn = cached_ft(cache_a_k), cached_ft(cache_a_v), new_ft(kaf2), new_ft(vaf2)
    qa2 = jnp.transpose(qa2[0].reshape(N_PAIRS, PAIR_W, db, t), (0, 2, 3, 1))
    ya2, new_k, new_v = _band_sample(qa2, kc, kn, vc, vn, _band_bias(table, t, wcache + t, q_major=True))
    kr_all = jnp.concatenate([cache_mla_krope[0], kr2], axis=1)
    kf2, vt2 = _kv_up(cache_mla_ckv[0], ckv2, kr_all, wk, ek, wvt)
    qf2 = jnp.transpose(qf2[0, 0].reshape(D_QF, db, t), (1, 0, 2))[:, None]
    yb2 = _mla_attn(qf2, kf2, vt2, jnp.zeros((1, 8, 128), F32), tq=t, tk=past + t, causal=False, pairs=H_B // 2)
    y_sample, conv_s = _merge_ffn(x_sample.reshape(1, db * t, D_MODEL), ya2.reshape(1, N_PAIRS, db * t, PAIR_W),
                                  yb2.reshape(1, db * t, D_VB), ga2, gb2, state_ffn_conv[0], *ffn_w,
                                  tm=db * t, nseq=db)
    y_sample = y_sample.reshape(db, t, D_MODEL)

    heads_t = lambda v: jnp.transpose(v.reshape(1, v.shape[0], H_A, DH_A, v.shape[2]), (0, 1, 4, 2, 3))
    return (y_prompt, y_sample,
            heads_t(kaf), heads_t(vaf), ckv[None], jnp.swapaxes(kr, 1, 2)[None], conv_p[None],
            heads_t(new_k), heads_t(new_v), ckv2[None], kr2[None], conv_s[None])
```

```python
import functools

import jax
import jax.numpy as jnp
import numpy as np
from jax import lax
from jax.experimental import pallas as pl
from jax.experimental.pallas import tpu as pltpu

D_MODEL = 1024
CHUNK = 64
LEFT_CHUNKS = 8
A_WINDOW = LEFT_CHUNKS * CHUNK
H_A = 8
DH_A = 64
MAX_REL = 256
H_B = 8
DH_NOPE = 64
DH_ROPE = 32
DV_B = 64
D_C = 256
D_FF = 2816
CONV_W = 3
ROPE_BASE = 10000.0
EPS = 1e-6
A_SCALE = DH_A ** -0.5
MLA_SCALE = (DH_NOPE + DH_ROPE) ** -0.5
IN_SIZES = (H_A * DH_A, H_A * DH_A, H_A * DH_A, H_B * DH_NOPE, H_B * DH_ROPE, D_C, DH_ROPE, D_MODEL, D_MODEL)

HEAD_PAD = 128
D_A = H_A * DH_A
D_VB = H_B * DV_B
D_QF = H_B * HEAD_PAD
PAIR_W = 2 * DH_A
N_PAIRS = H_A // 2
NEG = -1e30
LOG2E = 1.4426950408889634
_L_ROWS = 16
VMEM_LIMIT_V7X = 56 * 1024 * 1024

F32 = jnp.float32
BF16 = jnp.bfloat16
_NT = (((1,), (1,)), ((), ()))


def _resident(shape):
    nd = len(shape)
    return pl.BlockSpec(shape, lambda *_: (0,) * nd, pipeline_mode=pl.Buffered(1))


def _rms(x, w):
    return x * lax.rsqrt(jnp.mean(x * x, axis=-1, keepdims=True) + EPS) * w


_C_KA, _C_CKV, _C_GA, _C_GB, _C_END = 0, 512, 768, 1792, 2816
_KR_ROW = DH_NOPE + DH_ROPE
_HALF = DH_ROPE // 2


def _store_pairs(ref, v):
    for g in range(N_PAIRS):
        ref[0, g] = v[:, g * PAIR_W:(g + 1) * PAIR_W]


def _rope_rows(x, c, s):
    x1, x2 = x[:_HALF], x[_HALF:]
    return [x1 * c - x2 * s, x1 * s + x2 * c]


def _in_proj_kernel(x_ref, nw_ref, wm_ref, wqt_ref, wvt_ref, kvn_ref, wuk_ref, wuvt_ref, cq_ref, sq_ref,
                    ck_ref, sk_ref, qa_ref, ka_ref, vat_ref, kaf_ref, vaf_ref, qf_ref, ckv_ref, kr_ref, ga_ref, gb_ref,
                    kf_ref, vt_ref):
    xn = _rms(x_ref[0], nw_ref[...]).astype(BF16)

    def proj(lo, hi):
        return jnp.dot(xn, wm_ref[:, lo:hi], preferred_element_type=F32)

    ka = proj(_C_KA, _C_CKV)
    _store_pairs(ka_ref, ka.astype(BF16))
    vat = lax.dot_general(wvt_ref[...], xn, _NT, preferred_element_type=F32)
    vat_ref[0, 0] = vat.astype(BF16)
    vaf_ref[0] = vat
    kaf_ref[0] = ka.T

    qt = lax.dot_general(wqt_ref[...], xn, _NT, preferred_element_type=F32)
    for g in range(N_PAIRS):
        qa_ref[0, g] = qt[D_QF + g * PAIR_W:D_QF + (g + 1) * PAIR_W].astype(BF16)
    cq, sq = cq_ref[...], sq_ref[...]
    pad = jnp.zeros((HEAD_PAD - _KR_ROW, qt.shape[1]), F32)
    for h in range(H_B):
        slab = qt[h * HEAD_PAD:(h + 1) * HEAD_PAD]
        rows = [slab[:DH_NOPE] * (MLA_SCALE * LOG2E)] + _rope_rows(slab[DH_NOPE:_KR_ROW], cq, sq) + [pad]
        qf_ref[0, 0, h * HEAD_PAD:(h + 1) * HEAD_PAD, :] = jnp.concatenate(rows, axis=0).astype(BF16)
    kr_t = _rope_rows(qt[_KR_ROW:HEAD_PAD], ck_ref[...], sk_ref[...])
    zero_rows = lambda r: jnp.zeros((r, qt.shape[1]), F32)
    kr_ref[0] = jnp.concatenate(kr_t, axis=0)

    ckv = _rms(proj(_C_CKV, _C_GA), kvn_ref[...])
    ckv_ref[0] = ckv
    c_bf = ckv.astype(BF16)
    k_nope = jnp.dot(c_bf, wuk_ref[...], preferred_element_type=F32)
    k_rope = jnp.concatenate([zero_rows(DH_NOPE)] + kr_t + [zero_rows(HEAD_PAD - _KR_ROW)], axis=0).T
    for h in range(H_B):
        slab = slice(h * HEAD_PAD, (h + 1) * HEAD_PAD)
        kf_ref[0, :, slab] = (k_nope[:, slab] + k_rope).astype(BF16)
    vt_ref[0, 0] = lax.dot_general(wuvt_ref[...], c_bf, _NT, preferred_element_type=F32).astype(BF16)
    ga_ref[0] = proj(_C_GA, _C_GB).astype(BF16)
    gb_ref[0] = proj(_C_GB, _C_END).astype(BF16)


def _in_proj(x, nw, wm, wqt, wvt, kvn, wuk, wuvt, cq, sq, ck, sk, *, tm):
    g, s, _ = x.shape
    n = s // tm
    tok = lambda w: pl.BlockSpec((1, tm, w), lambda a, b: (a, b, 0))
    tab = pl.BlockSpec((_HALF, tm), lambda a, b: (0, b))
    tail = pl.BlockSpec((1, D_A, tm), lambda a, b: (a, 0, 0))
    out_shape = (
        jax.ShapeDtypeStruct((g, N_PAIRS, PAIR_W, s), BF16),
        jax.ShapeDtypeStruct((g, N_PAIRS, s, PAIR_W), BF16),
        jax.ShapeDtypeStruct((g, n, D_A, tm), BF16),
        jax.ShapeDtypeStruct((g, D_A, tm), F32),
        jax.ShapeDtypeStruct((g, D_A, tm), F32),
        jax.ShapeDtypeStruct((g, n, D_QF, tm), BF16),
        jax.ShapeDtypeStruct((g, s, D_C), F32),
        jax.ShapeDtypeStruct((g, DH_ROPE, s), F32),
        jax.ShapeDtypeStruct((g, s, D_MODEL), BF16),
        jax.ShapeDtypeStruct((g, s, D_MODEL), BF16),
        jax.ShapeDtypeStruct((g, s, D_QF), BF16),
        jax.ShapeDtypeStruct((g, n, D_VB, tm), BF16),
    )
    pairs = pl.BlockSpec((1, N_PAIRS, tm, PAIR_W), lambda a, b: (a, 0, b, 0))
    pairs_t = pl.BlockSpec((1, N_PAIRS, PAIR_W, tm), lambda a, b: (a, 0, 0, b))
    out_specs = (pairs_t, pairs, pl.BlockSpec((1, 1, D_A, tm), lambda a, b: (a, b, 0, 0)), tail, tail,
                 pl.BlockSpec((1, 1, D_QF, tm), lambda a, b: (a, b, 0, 0)), tok(D_C),
                 pl.BlockSpec((1, DH_ROPE, tm), lambda a, b: (a, 0, b)), tok(D_MODEL),
                 tok(D_MODEL), tok(D_QF), pl.BlockSpec((1, 1, D_VB, tm), lambda a, b: (a, b, 0, 0)))
    in_specs = [tok(D_MODEL), _resident(nw.shape), _resident(wm.shape), _resident(wqt.shape), _resident(wvt.shape),
                _resident(kvn.shape), _resident(wuk.shape), _resident(wuvt.shape), tab, tab, tab,
                tab]
    return pl.pallas_call(
        _in_proj_kernel, grid=(g, n), in_specs=in_specs, out_specs=out_specs, out_shape=out_shape,
        compiler_params=pltpu.CompilerParams(dimension_semantics=("arbitrary", "arbitrary"),
                                             vmem_limit_bytes=VMEM_LIMIT_V7X),
        name="in_proj")(x, nw, wm, wqt, wvt, kvn, wuk, wuvt, cq, sq, ck, sk)


def _kv_up_kernel(cc_ref, cn_ref, kr_ref, wk_ref, ek_ref, wvt_ref, kf_ref, vt_ref):
    c = jnp.concatenate([cc_ref[0], cn_ref[0]], axis=0).astype(BF16)
    kf = jnp.dot(c, wk_ref[...], preferred_element_type=F32)
    kf = kf + jnp.dot(kr_ref[0].astype(BF16), ek_ref[...], preferred_element_type=F32)
    kf_ref[0] = kf.astype(BF16)
    vt_ref[0, 0] = lax.dot_general(wvt_ref[...], c, _NT, preferred_element_type=F32).astype(BF16)


def _kv_up(c_cache, c_new, kr, wk, ek, wvt):
    g, s = c_cache.shape[0], c_cache.shape[1] + c_new.shape[1]
    per_seq = lambda v: pl.BlockSpec((1,) + v.shape[1:], lambda a: (a, 0, 0))
    return pl.pallas_call(
        _kv_up_kernel, grid=(g,),
        in_specs=[per_seq(c_cache), per_seq(c_new), per_seq(kr),
                  _resident(wk.shape), _resident(ek.shape), _resident(wvt.shape)],
        out_specs=(pl.BlockSpec((1, s, D_QF), lambda a: (a, 0, 0)),
                   pl.BlockSpec((1, 1, D_VB, s), lambda a: (a, 0, 0, 0))),
        out_shape=(jax.ShapeDtypeStruct((g, s, D_QF), BF16),
                   jax.ShapeDtypeStruct((g, 1, D_VB, s), BF16)),
        compiler_params=pltpu.CompilerParams(dimension_semantics=("arbitrary",),
                                             vmem_limit_bytes=VMEM_LIMIT_V7X),
        name="kv_up")(c_cache, c_new, kr, wk, ek, wvt)


def _band_prompt_kernel(q_ref, kp_ref, kc_ref, vp_ref, vc_ref, bias_ref, o_ref, s_scr, *, tm, tq):
    w = A_WINDOW
    subs = [(lo, lo + tq) for lo in range(0, tm, tq)]
    row = lax.broadcasted_iota(jnp.int32, (PAIR_W, tq), 0)

    def tile(first):
        def window(lo, hi):
            if lo >= w:
                return [(False, lo - w, hi)], 0
            if first:
                return [(False, 0, hi)], w - lo
            return [(True, lo, w), (False, 0, hi)], 0

        def keys(g, pieces):
            return jnp.concatenate([(kp_ref if p else kc_ref)[0, g, a:b, :] for p, a, b in pieces], axis=0)

        def values_t(rows, pieces):
            out = []
            for p, a, b in pieces:
                if p:
                    out.append(vp_ref[0, 0, rows, a:b])
                else:
                    out += [vc_ref[0, ti, rows, max(a, ti * w) - ti * w:min(b, (ti + 1) * w) - ti * w]
                            for ti in range(a // w, (b - 1) // w + 1)]
            return jnp.concatenate(out, axis=1)

        def scores(g, j):
            lo, hi = subs[j]
            pieces, b0 = window(lo, hi)
            q2 = q_ref[0, g, :, lo:hi]
            qcat = jnp.concatenate([jnp.where(row < DH_A, q2, jnp.zeros((), BF16)),
                                    jnp.where(row >= DH_A, q2, jnp.zeros((), BF16))], axis=1)
            k2 = keys(g, pieces)
            nk = k2.shape[0]
            bias = jnp.concatenate([bias_ref[2 * g + hl, b0:b0 + nk, :] for hl in range(2)], axis=1)
            return jnp.dot(k2, qcat, preferred_element_type=F32) + bias

        def stash(j, s):
            s_scr[j, :s.shape[0], :] = s
            return jnp.max(s, axis=0, keepdims=True)

        def consume(g, j, col_max):
            lo, hi = subs[j]
            pieces, _ = window(lo, hi)
            nk = sum(b - a for _, a, b in pieces)
            outs = []
            for hl in range(2):
                cols = slice(hl * tq, (hl + 1) * tq)
                p = jnp.exp2(s_scr[j, :nk, cols] - col_max[:, cols]).astype(BF16)
                v = values_t(pl.ds(g * PAIR_W + hl * DH_A, DH_A), pieces)
                v = jnp.concatenate([v, jnp.ones((_L_ROWS, nk), BF16)], axis=0)
                o = jnp.dot(v, p, preferred_element_type=F32)
                outs.append(o[:DH_A] / o[DH_A:DH_A + 1])
            o_ref[0, g, lo:hi, :] = jnp.concatenate(outs, axis=0).T.astype(BF16)

        def body(g, col_max):
            new_max = []
            for j in range(len(subs)):
                nxt = scores(g + 1, j)
                consume(g, j, col_max[j])
                new_max.append(stash(j, nxt))
            return tuple(new_max)

        col_max = tuple(stash(j, scores(0, j)) for j in range(len(subs)))
        for g in range(N_PAIRS - 1):
            col_max = body(g, col_max)
        for j in range(len(subs)):
            consume(N_PAIRS - 1, j, col_max[j])

    pl.when(pl.program_id(1) == 0)(functools.partial(tile, True))
    pl.when(pl.program_id(1) > 0)(functools.partial(tile, False))


def _band_prompt(qa, ka, vat, bias, *, tm, tq):
    g, _, _, s = qa.shape
    w = A_WINDOW
    assert vat.shape[-1] == w and tm % w == 0
    before = lambda b: jnp.maximum(b * (tm // w) - 1, 0)
    return pl.pallas_call(
        functools.partial(_band_prompt_kernel, tm=tm, tq=tq), grid=(g, s // tm),
        in_specs=[pl.BlockSpec((1, N_PAIRS, PAIR_W, tm), lambda a, b: (a, 0, 0, b)),
                  pl.BlockSpec((1, N_PAIRS, w, PAIR_W), lambda a, b: (a, 0, before(b), 0)),
                  pl.BlockSpec((1, N_PAIRS, tm, PAIR_W), lambda a, b: (a, 0, b, 0)),
                  pl.BlockSpec((1, 1, D_A, w), lambda a, b: (a, before(b), 0, 0)),
                  pl.BlockSpec((1, tm // w, D_A, w), lambda a, b: (a, b, 0, 0)),
                  _resident(bias.shape)],
        out_specs=pl.BlockSpec((1, N_PAIRS, tm, PAIR_W), lambda a, b: (a, 0, b, 0)),
        out_shape=jax.ShapeDtypeStruct((g, N_PAIRS, s, PAIR_W), BF16),
        scratch_shapes=[pltpu.VMEM((tm // tq, w + tq, 2 * tq), F32)],
        compiler_params=pltpu.CompilerParams(dimension_semantics=("arbitrary", "arbitrary"),
                                             vmem_limit_bytes=VMEM_LIMIT_V7X),
        name="band_prompt")(qa, ka, ka, vat, vat, bias)


def _band_sample_kernel(q_ref, kc_ref, kn_ref, vc_ref, vn_ref, bias_ref, o_ref, k_roll_ref, v_roll_ref):
    t = kn_ref.shape[2]
    k_roll_ref[0] = jnp.concatenate([kc_ref[0, :, t:], kn_ref[0]], axis=1)
    v_roll_ref[0] = jnp.concatenate([vc_ref[0, :, t:], vn_ref[0]], axis=1)
    lane = lax.broadcasted_iota(jnp.int32, q_ref.shape[2:], 1)
    for g in range(N_PAIRS):
        rows = slice(g * PAIR_W, (g + 1) * PAIR_W)
        kt = jnp.concatenate([kc_ref[0, rows, :], kn_ref[0, rows, :]], axis=1).astype(BF16)
        q2 = q_ref[g, 0]
        outs = []
        for hl in range(2):
            h = 2 * g + hl
            qm = jnp.where((lane >= DH_A) == (hl == 1), q2, jnp.zeros((), BF16))
            s = jnp.dot(qm, kt, preferred_element_type=F32) + bias_ref[h]
            p = jnp.exp2(s - jnp.max(s, axis=1, keepdims=True))
            hrows = slice(h * DH_A, (h + 1) * DH_A)
            vt = jnp.concatenate([vc_ref[0, hrows, :], vn_ref[0, hrows, :]], axis=1).astype(BF16)
            o = lax.dot_general(p.astype(BF16), vt, _NT, preferred_element_type=F32)
            outs.append(o / jnp.sum(p, axis=1, keepdims=True))
        o_ref[g, 0] = jnp.concatenate(outs, axis=1).astype(BF16)


def _band_sample(qa, kc, kn, vc, vn, bias):
    _, g, t, _ = qa.shape
    pairs = pl.BlockSpec((N_PAIRS, 1, t, PAIR_W), lambda a: (0, a, 0, 0))
    per_seq = lambda v: pl.BlockSpec((1,) + v.shape[1:], lambda a: (a, 0, 0))
    return pl.pallas_call(
        _band_sample_kernel, grid=(g,),
        in_specs=[pairs, per_seq(kc), per_seq(kn), per_seq(vc), per_seq(vn), _resident(bias.shape)],
        out_specs=(pairs, per_seq(kc), per_seq(vc)),
        out_shape=(jax.ShapeDtypeStruct((N_PAIRS, g, t, PAIR_W), BF16),
                   jax.ShapeDtypeStruct(kc.shape, F32), jax.ShapeDtypeStruct(vc.shape, F32)),
        compiler_params=pltpu.CompilerParams(dimension_semantics=("arbitrary",),
                                             vmem_limit_bytes=VMEM_LIMIT_V7X),
        name="band_sample")(qa, kc, kn, vc, vn, bias)


def _mla_kernel(q_ref, k_ref, vt_ref, mask_ref, o_ref, s_scr, *, tq, tk, causal, pairs):
    nq = q_ref.shape[1]
    nk = k_ref.shape[1] // tk
    ones = jnp.ones((_L_ROWS, tk), BF16)

    def q_rows(hp, i):
        return [q_ref[0, i, (2 * hp + hl) * HEAD_PAD:(2 * hp + hl + 1) * HEAD_PAD, :] for hl in range(2)]

    def scores(hp, qs, kk, hl):
        krow = pl.multiple_of(kk * tk, tk)
        return jnp.dot(k_ref[0, pl.ds(krow, tk), (2 * hp + hl) * HEAD_PAD:(2 * hp + hl + 1) * HEAD_PAD], qs[hl],
                       preferred_element_type=F32)

    def stash(hl, s):
        s_scr[hl] = s
        return jnp.max(s, axis=0, keepdims=True)

    def consume(hp, kk, hl, m, acc, tile_max, mask_j):
        if mask_j is not None:
            s = s_scr[hl] + mask_ref[mask_j]
            m_new = jnp.maximum(m, jnp.max(s, axis=0, keepdims=True))
        else:
            s = s_scr[hl]
            m_new = jnp.maximum(m, tile_max)
        p = jnp.exp2(s - m_new).astype(BF16)
        v = jnp.concatenate([vt_ref[0, kk, (2 * hp + hl) * DV_B:(2 * hp + hl + 1) * DV_B, :], ones], axis=0)
        return m_new, acc * jnp.exp2(m - m_new) + jnp.dot(v, p, preferred_element_type=F32)

    def q_tile(hp, i, first_max):
        qs = q_rows(hp, i)

        def body(t, carry):
            out = []
            for hl in range(2):
                m, acc, tile_max = carry[3 * hl:3 * hl + 3]
                nxt = scores(hp, qs, t + 1, hl)
                out += list(consume(hp, t, hl, m, acc, tile_max, None)) + [stash(hl, nxt)]
            return tuple(out)

        r = tq // tk
        n_full = r * i if causal else nk - 1
        carry = ()
        for hl in range(2):
            carry += (jnp.full((1, tq), NEG, F32), jnp.zeros((DV_B + _L_ROWS, tq), F32), first_max[hl])
        done = 0
        for n in (4, 2, 1):
            left = n_full - done
            trips = left // n if isinstance(left, int) else lax.shift_right_logical(left, n.bit_length() - 1)

            def trip(u, c, n=n, done=done):
                for j in range(n):
                    c = body(done + n * u + j, c)
                return c

            carry = lax.fori_loop(0, trips, trip, carry)
            done = done + trips * n
        tail = [(r * i + j, j) for j in range(r)] if causal else [(nk - 1, None)]
        qs_next = q_rows(hp, jnp.minimum(i + 1, nq - 1)) if nq > 1 else None
        state = [list(carry[3 * hl:3 * hl + 3]) for hl in range(2)]
        for idx, (tile, mask_j) in enumerate(tail):
            for hl in range(2):
                if idx + 1 < len(tail):
                    nxt = scores(hp, qs, tile + 1, hl)
                else:
                    nxt = scores(hp, qs_next, 0, hl) if nq > 1 else None
                m, acc = consume(hp, tile, hl, *state[hl], mask_j)
                state[hl] = [m, acc, stash(hl, nxt) if nxt is not None else state[hl][2]]
        accs = [state[hl][1] for hl in range(2)]
        next_max = [state[hl][2] for hl in range(2)]
        o = jnp.concatenate([acc[:DV_B] / acc[DV_B:DV_B + 1] for acc in accs], axis=0)
        o_ref[0, pl.ds(pl.multiple_of(i * tq, tq), tq), hp * 2 * DV_B:(hp + 1) * 2 * DV_B] = o.T.astype(BF16)
        return tuple(next_max)

    for hp in range(pairs):
        first = q_rows(hp, 0)
        lax.fori_loop(0, nq, functools.partial(q_tile, hp),
                      tuple(stash(hl, scores(hp, first, 0, hl)) for hl in range(2)))


def _mla_attn(qf, kf, vt, mask, *, tq, tk, causal, pairs=1):
    g, nq, _, _ = qf.shape
    sq, sk = nq * tq, kf.shape[1]
    return pl.pallas_call(
        functools.partial(_mla_kernel, tq=tq, tk=tk, causal=causal, pairs=pairs), grid=(g, H_B // 2 // pairs),
        in_specs=[pl.BlockSpec((1, nq, 2 * pairs * HEAD_PAD, tq), lambda a, b: (a, 0, b, 0)),
                  pl.BlockSpec((1, sk, 2 * pairs * HEAD_PAD), lambda a, b: (a, 0, b)),
                  pl.BlockSpec((1, sk // tk, 2 * pairs * DV_B, tk), lambda a, b: (a, 0, b, 0)),
                  _resident(mask.shape)],
        out_specs=pl.BlockSpec((1, sq, 2 * pairs * DV_B), lambda a, b: (a, 0, b)),
        out_shape=jax.ShapeDtypeStruct((g, sq, D_VB), BF16),
        scratch_shapes=[pltpu.VMEM((2, tk, tq), F32)],
        compiler_params=pltpu.CompilerParams(dimension_semantics=("arbitrary", "arbitrary"),
                                             vmem_limit_bytes=VMEM_LIMIT_V7X),
        name="mla_attn")(qf, kf, vt, mask)


_G0 = 8


def _merge_ffn_kernel(x_ref, ya_ref, yb_ref, ga_ref, gb_ref, st_ref, wa_ref, wb_ref, wo_ref, n1_ref, n2_ref, n3_ref,
                      wg_ref, wu_ref, cw_ref, cb_ref, wd_ref, y_ref, cs_ref, gbuf, *, tm, nseq):
    i = pl.program_id(1)
    seg = tm // nseq
    starts = [_G0 + j * (seg + _G0) for j in range(nseq)]
    if nseq == 1:
        @pl.when(i == 0)
        def _():
            gbuf[_G0 - 2:_G0, :] = st_ref[0]

        @pl.when(i > 0)
        def _():
            gbuf[_G0 - 2:_G0, :] = gbuf[_G0 + tm - 2:_G0 + tm, :]
    else:
        for j, r0 in enumerate(starts):
            gbuf[r0 - 2:r0, :] = st_ref[j]

    parts = [(0, tm // 2), (tm // 2, tm)] if nseq == 1 else [(0, tm)]

    mixes = []
    for lo, hi in parts:
        ya = jnp.concatenate([ya_ref[0, g, lo:hi, :] for g in range(N_PAIRS)], axis=1)
        za = jnp.dot(ya, wa_ref[...], preferred_element_type=F32)
        zb = jnp.dot(yb_ref[0, lo:hi, :], wb_ref[...], preferred_element_type=F32)
        mixes.append((jax.nn.sigmoid(ga_ref[0, lo:hi, :].astype(F32)) * za
                      + jax.nn.sigmoid(gb_ref[0, lo:hi, :].astype(F32)) * zb).astype(BF16))
    x1s, xns = [], []
    for (lo, hi), mix in zip(parts, mixes):
        mo = jnp.dot(mix, wo_ref[...], preferred_element_type=F32)
        x1s.append(x_ref[0, lo:hi, :] + _rms(mo, n1_ref[...]))
        xns.append(_rms(x1s[-1], n2_ref[...]).astype(BF16))

    ups = []
    for (lo, hi), xn in zip(parts, xns):
        gate = jnp.dot(xn, wg_ref[...], preferred_element_type=F32)
        if nseq == 1:
            gbuf[_G0 + lo:_G0 + hi, :] = gate
        else:
            for j, r0 in enumerate(starts):
                gbuf[r0:r0 + seg, :] = gate[j * seg:(j + 1) * seg]
        ups.append(jnp.dot(xn, wu_ref[...], preferred_element_type=F32))
    for j, r0 in enumerate(starts):
        cs_ref[j] = gbuf[r0 + seg - 2:r0 + seg, :]

    def conv(r0, rows):
        return (cw_ref[0:1, :] * gbuf[r0 - 2:r0 - 2 + rows, :] + cw_ref[1:2, :] * gbuf[r0 - 1:r0 - 1 + rows, :]
                + cw_ref[2:3, :] * gbuf[r0:r0 + rows, :])

    for (lo, hi), u, x1 in zip(parts, ups, x1s):
        if nseq == 1:
            c = conv(_G0 + lo, hi - lo)
        else:
            c = jnp.concatenate([conv(r0, seg) for r0 in starts], axis=0)
        hid = (jax.nn.gelu(c + cb_ref[...], approximate=True) * u).astype(BF16)
        f = jnp.dot(hid, wd_ref[...], preferred_element_type=F32)
        y_ref[0, lo:hi, :] = x1 + _rms(f, n3_ref[...])


def _merge_ffn(x, ya, yb, ga, gb, state, wa, wb, wo, n1, n2, n3, wg, wu, cw, cb, wd, *, tm, nseq=1):
    g, s, _ = x.shape
    assert nseq == 1 or s == tm
    tok = lambda w: pl.BlockSpec((1, tm, w), lambda a, b: (a, b, 0))
    per_group = pl.BlockSpec((nseq, CONV_W - 1, D_FF), lambda a, b: (a, 0, 0))
    weights = (wa, wb, wo, n1, n2, n3, wg, wu, cw, cb, wd)
    return pl.pallas_call(
        functools.partial(_merge_ffn_kernel, tm=tm, nseq=nseq), grid=(g, s // tm),
        in_specs=[tok(D_MODEL), pl.BlockSpec((1, N_PAIRS, tm, PAIR_W), lambda a, b: (a, 0, b, 0)), tok(D_VB),
                  tok(D_MODEL), tok(D_MODEL), per_group]
                 + [_resident(w.shape) for w in weights],
        out_specs=(tok(D_MODEL), per_group),
        out_shape=(jax.ShapeDtypeStruct((g, s, D_MODEL), F32),
                   jax.ShapeDtypeStruct((g * nseq, CONV_W - 1, D_FF), F32)),
        scratch_shapes=[pltpu.VMEM((tm + nseq * _G0, D_FF), F32)],
        compiler_params=pltpu.CompilerParams(dimension_semantics=("arbitrary", "arbitrary"),
                                             vmem_limit_bytes=VMEM_LIMIT_V7X),
        name="merge_ffn")(x, ya, yb, ga, gb, state, *weights)


def _rope_tables(pos, scale):
    inv = ROPE_BASE ** (-np.arange(_HALF, dtype=np.float64) / _HALF)
    ang = inv[:, None] * np.asarray(pos, np.float64)[None, :]
    return jnp.asarray(np.cos(ang) * scale, F32), jnp.asarray(np.sin(ang) * scale, F32)


def _band_bias_kernel(ext_ref, o_ref, *, tq, nk, q_major):
    w = ext_ref.shape[-1]
    toeplitz = pltpu.roll(jnp.broadcast_to(ext_ref[0], (nk, w)), 0, 1, stride=1, stride_axis=0)[:, :tq]
    qc = (nk - tq + lax.broadcasted_iota(jnp.int32, (nk, tq), 1)) // CHUNK
    kc = lax.broadcasted_iota(jnp.int32, (nk, tq), 0) // CHUNK
    bias = jnp.where((kc >= qc - LEFT_CHUNKS) & (kc <= qc), toeplitz, NEG)
    o_ref[0] = bias.T if q_major else bias


def _band_bias(table, tq, nk, q_major=False):
    w = nk + tq
    d_lo, d_hi = 1 - tq, w - tq
    assert -MAX_REL <= d_lo and d_hi >= MAX_REL and w % 128 == 0
    h = table.shape[0]
    ext = jnp.concatenate([table[:, d_lo + MAX_REL:], jnp.broadcast_to(table[:, -1:], (h, d_hi - MAX_REL))], axis=1)
    ext = (jnp.roll(ext, -(nk - 1), axis=1) * LOG2E).reshape(h, 1, w)
    return pl.pallas_call(
        functools.partial(_band_bias_kernel, tq=tq, nk=nk, q_major=q_major), grid=(h,),
        in_specs=[pl.BlockSpec((1, 1, w), lambda a: (a, 0, 0))],
        out_specs=pl.BlockSpec((1, tq, nk) if q_major else (1, nk, tq), lambda a: (a, 0, 0)),
        out_shape=jax.ShapeDtypeStruct((h, tq, nk) if q_major else (h, nk, tq), F32),
        compiler_params=pltpu.CompilerParams(dimension_semantics=("arbitrary",)),
        name="band_bias")(ext)


def _chunk_masks(tk, tq):
    kc = jnp.arange(tq)[:, None] // CHUNK
    qc = jnp.arange(tq)[None, :] // CHUNK
    return jnp.where(kc <= qc, 0.0, NEG).astype(F32).reshape(tq // tk, tk, tq)


def _scaled_cast_kernel(w_ref, o_ref, *, rows, n_scaled, scale):
    r = pl.program_id(0) * rows + lax.broadcasted_iota(jnp.int32, (rows, 1), 0)
    o_ref[...] = (w_ref[0] * jnp.where(r < n_scaled, scale, 1.0)).astype(BF16)


def _scaled_cast(w, n_scaled, scale, *, rows):
    _, d, n = w.shape
    return pl.pallas_call(
        functools.partial(_scaled_cast_kernel, rows=rows, n_scaled=n_scaled, scale=scale), grid=(d // rows,),
        in_specs=[pl.BlockSpec((1, rows, n), lambda a: (0, a, 0))],
        out_specs=pl.BlockSpec((rows, n), lambda a: (a, 0)),
        out_shape=jax.ShapeDtypeStruct((d, n), BF16),
        compiler_params=pltpu.CompilerParams(dimension_semantics=("arbitrary",)),
        name="cast_w_in")(w)


def _prep_weights(w_in, w_uk, w_uv):
    o = np.cumsum((0,) + IN_SIZES)
    d = w_in.shape[1]
    w_t = _scaled_cast(jnp.swapaxes(w_in, 1, 2), int(o[1]), A_SCALE * LOG2E, rows=464)
    row = lambda j: w_t[o[j]:o[j + 1]]
    qn = row(3).reshape(H_B, DH_NOPE, d)
    qr = row(4).reshape(H_B, DH_ROPE, d)
    pad = jnp.zeros((H_B, HEAD_PAD - DH_NOPE - DH_ROPE, d), BF16).at[0].set(row(6))
    wqt = jnp.concatenate([jnp.concatenate([qn, qr, pad], axis=1).reshape(D_QF, d), row(0)], axis=0)
    wm = jnp.concatenate([row(1), row(5), row(7), row(8)], axis=0).T
    wvat = row(2)
    wk = jnp.concatenate([w_uk, jnp.zeros((D_C, H_B, HEAD_PAD - DH_NOPE), F32)], axis=2).reshape(D_C, D_QF).astype(BF16)
    place = jnp.concatenate([jnp.zeros((DH_ROPE, DH_NOPE), F32), jnp.eye(DH_ROPE, dtype=F32),
                             jnp.zeros((DH_ROPE, HEAD_PAD - DH_NOPE - DH_ROPE), F32)], axis=1)
    ek = jnp.tile(place, (1, H_B)).astype(BF16)
    wvt = w_uv.reshape(D_C, D_VB).T.astype(BF16)
    return wm, wqt, wvat, wk, ek, wvt


def kernel(x_prompt, x_sample, cache_a_k, cache_a_v, cache_mla_ckv, cache_mla_krope, state_ffn_conv, norm_mix_pre,
           norm_mix_post, w_in, rel_bias_table, kv_norm, w_uk, w_uv, w_branch_a, w_branch_b, w_out, norm_ffn_pre,
           norm_ffn_post, w_ffn_gate, w_ffn_up, conv_w, conv_b, w_ffn_down):
    assert w_in.shape[0] == 1, "single layer"
    b, s, _ = x_prompt.shape
    db, t, _ = x_sample.shape
    past = cache_mla_ckv.shape[2]
    wcache = cache_a_k.shape[2]
    keep = min(A_WINDOW, s)
    tm1 = 512
    assert keep == tm1 and db * t == tm1 and wcache == A_WINDOW

    wm, wqt, wvat, wk, ek, wvt = _prep_weights(w_in, w_uk[0], w_uv[0])
    row = lambda v: v.reshape(1, -1)
    proj_w = (row(norm_mix_pre[0]), wm, wqt, wvat, row(kv_norm[0]), wk, wvt)
    ffn_w = (w_branch_a[0].astype(BF16), w_branch_b[0].astype(BF16), w_out[0].astype(BF16), row(norm_mix_post[0]),
             row(norm_ffn_pre[0]), row(norm_ffn_post[0]), w_ffn_gate[0].astype(BF16), w_ffn_up[0].astype(BF16),
             conv_w[0], row(conv_b[0]), w_ffn_down[0].astype(BF16))
    table = rel_bias_table[0]

    pos = np.arange(s)
    tabs = _rope_tables(pos, MLA_SCALE * LOG2E) + _rope_tables(pos, 1.0)
    t_mla = tm1
    qa, ka, vat, kaf, vaf, qf, ckv, kr, ga, gb, kf, vt = _in_proj(x_prompt, *proj_w, *tabs, tm=tm1)
    ya = _band_prompt(qa, ka, vat, _band_bias(table, 256, A_WINDOW + 256), tm=2 * tm1, tq=256)
    yb = _mla_attn(qf, kf, vt, _chunk_masks(t_mla, t_mla), tq=t_mla, tk=t_mla, causal=True)
    y_prompt, conv_p = _merge_ffn(x_prompt, ya, yb, ga, gb, jnp.zeros((b, CONV_W - 1, D_FF), F32), *ffn_w, tm=512)

    pos_s = np.tile(past + np.arange(t), db)
    tabs_s = _rope_tables(pos_s, MLA_SCALE * LOG2E) + _rope_tables(pos_s, 1.0)
    qa2, ka2, vat2, kaf2, vaf2, qf2, ckv2, kr2, ga2, gb2, _, _ = _in_proj(x_sample.reshape(1, db * t, D_MODEL), *proj_w,
                                                                    *tabs_s, tm=tm1)
    per_seq = lambda v: v.reshape(db, t, v.shape[-1])
    ckv2, kr2 = per_seq(ckv2[0]), per_seq(kr2[0].T)
    cached_ft = lambda c: jnp.transpose(c[0], (0, 2, 3, 1)).reshape(db, D_A, wcache)
    new_ft = lambda v: jnp.transpose(v[0].reshape(D_A, db, t), (1, 0, 2))
    kc, vc, kn, vn = cached_ft(cache_a_k), cached_ft(cache_a_v), new_ft(kaf2), new_ft(vaf2)
    qa2 = jnp.transpose(qa2[0].reshape(N_PAIRS, PAIR_W, db, t), (0, 2, 3, 1))
    ya2, new_k, new_v = _band_sample(qa2, kc, kn, vc, vn, _band_bias(table, t, wcache + t, q_major=True))
    kr_all = jnp.concatenate([cache_mla_krope[0], kr2], axis=1)
    kf2, vt2 = _kv_up(cache_mla_ckv[0], ckv2, kr_all, wk, ek, wvt)
    qf2 = jnp.transpose(qf2[0, 0].reshape(D_QF, db, t), (1, 0, 2))[:, None]
    yb2 = _mla_attn(qf2, kf2, vt2, jnp.zeros((1, 8, 128), F32), tq=t, tk=past + t, causal=False, pairs=H_B // 2)
    y_sample, conv_s = _merge_ffn(x_sample.reshape(1, db * t, D_MODEL), ya2.reshape(1, N_PAIRS, db * t, PAIR_W),
                                  yb2.reshape(1, db * t, D_VB), ga2, gb2, state_ffn_conv[0], *ffn_w,
                                  tm=db * t, nseq=db)
    y_sample = y_sample.reshape(db, t, D_MODEL)

    heads_t = lambda v: jnp.transpose(v.reshape(1, v.shape[0], H_A, DH_A, v.shape[2]), (0, 1, 4, 2, 3))
    return (y_prompt, y_sample,
            heads_t(kaf), heads_t(vaf), ckv[None], jnp.swapaxes(kr, 1, 2)[None], conv_p[None],
            heads_t(new_k), heads_t(new_v), ckv2[None], kr2[None], conv_s[None])
```

```python
import functools

import jax
import jax.numpy as jnp
import numpy as np
from jax import lax
from jax.experimental import pallas as pl
from jax.experimental.pallas import tpu as pltpu

D_MODEL = 1024
CHUNK = 64
LEFT_CHUNKS = 8
A_WINDOW = LEFT_CHUNKS * CHUNK
H_A = 8
DH_A = 64
MAX_REL = 256
H_B = 8
DH_NOPE = 64
DH_ROPE = 32
DV_B = 64
D_C = 256
D_FF = 2816
CONV_W = 3
ROPE_BASE = 10000.0
EPS = 1e-6
A_SCALE = DH_A ** -0.5
MLA_SCALE = (DH_NOPE + DH_ROPE) ** -0.5
IN_SIZES = (H_A * DH_A, H_A * DH_A, H_A * DH_A, H_B * DH_NOPE, H_B * DH_ROPE, D_C, DH_ROPE, D_MODEL, D_MODEL)

HEAD_PAD = 128
D_A = H_A * DH_A
D_VB = H_B * DV_B
D_QF = H_B * HEAD_PAD
PAIR_W = 2 * DH_A
N_PAIRS = H_A // 2
NEG = -1e30
LOG2E = 1.4426950408889634
_L_ROWS = 16
VMEM_LIMIT_V7X = 56 * 1024 * 1024

F32 = jnp.float32
BF16 = jnp.bfloat16
_NT = (((1,), (1,)), ((), ()))


def _resident(shape):
    nd = len(shape)
    return pl.BlockSpec(shape, lambda *_: (0,) * nd, pipeline_mode=pl.Buffered(1))


def _rms(x, w):
    return x * lax.rsqrt(jnp.mean(x * x, axis=-1, keepdims=True) + EPS) * w


_C_KA, _C_CKV, _C_GA, _C_GB, _C_END = 0, 512, 768, 1792, 2816
_KR_ROW = DH_NOPE + DH_ROPE
_HALF = DH_ROPE // 2


def _store_pairs(ref, v):
    for g in range(N_PAIRS):
        ref[0, g] = v[:, g * PAIR_W:(g + 1) * PAIR_W]


def _rope_rows(x, c, s):
    x1, x2 = x[:_HALF], x[_HALF:]
    return [x1 * c - x2 * s, x1 * s + x2 * c]


def _in_proj_kernel(x_ref, nw_ref, wm_ref, wqt_ref, wvt_ref, kvn_ref, wuk_ref, wuvt_ref, cq_ref, sq_ref,
                    ck_ref, sk_ref, qa_ref, ka_ref, vat_ref, kaf_ref, vaf_ref, qf_ref, ckv_ref, kr_ref, ga_ref, gb_ref,
                    kf_ref, vt_ref):
    xn = _rms(x_ref[0], nw_ref[...]).astype(BF16)

    def proj(lo, hi):
        return jnp.dot(xn, wm_ref[:, lo:hi], preferred_element_type=F32)

    ka = proj(_C_KA, _C_CKV)
    _store_pairs(ka_ref, ka.astype(BF16))
    vat = lax.dot_general(wvt_ref[...], xn, _NT, preferred_element_type=F32)
    vat_ref[0, 0] = vat.astype(BF16)
    vaf_ref[0] = vat
    kaf_ref[0] = ka.T

    qt = lax.dot_general(wqt_ref[...], xn, _NT, preferred_element_type=F32)
    for g in range(N_PAIRS):
        qa_ref[0, g] = qt[D_QF + g * PAIR_W:D_QF + (g + 1) * PAIR_W].astype(BF16)
    cq, sq = cq_ref[...], sq_ref[...]
    pad = jnp.zeros((HEAD_PAD - _KR_ROW, qt.shape[1]), F32)
    for h in range(H_B):
        slab = qt[h * HEAD_PAD:(h + 1) * HEAD_PAD]
        rows = [slab[:DH_NOPE] * (MLA_SCALE * LOG2E)] + _rope_rows(slab[DH_NOPE:_KR_ROW], cq, sq) + [pad]
        qf_ref[0, 0, h * HEAD_PAD:(h + 1) * HEAD_PAD, :] = jnp.concatenate(rows, axis=0).astype(BF16)
    kr_t = _rope_rows(qt[_KR_ROW:HEAD_PAD], ck_ref[...], sk_ref[...])
    zero_rows = lambda r: jnp.zeros((r, qt.shape[1]), F32)
    kr_ref[0] = jnp.concatenate(kr_t, axis=0)

    ckv = _rms(proj(_C_CKV, _C_GA), kvn_ref[...])
    ckv_ref[0] = ckv
    c_bf = ckv.astype(BF16)
    k_nope = jnp.dot(c_bf, wuk_ref[...], preferred_element_type=F32)
    k_rope = jnp.concatenate([zero_rows(DH_NOPE)] + kr_t + [zero_rows(HEAD_PAD - _KR_ROW)], axis=0).T
    for h in range(H_B):
        slab = slice(h * HEAD_PAD, (h + 1) * HEAD_PAD)
        kf_ref[0, :, slab] = (k_nope[:, slab] + k_rope).astype(BF16)
    vt_ref[0, 0] = lax.dot_general(wuvt_ref[...], c_bf, _NT, preferred_element_type=F32).astype(BF16)
    ga_ref[0] = proj(_C_GA, _C_GB).astype(BF16)
    gb_ref[0] = proj(_C_GB, _C_END).astype(BF16)


def _in_proj(x, nw, wm, wqt, wvt, kvn, wuk, wuvt, cq, sq, ck, sk, *, tm):
    g, s, _ = x.shape
    n = s // tm
    tok = lambda w: pl.BlockSpec((1, tm, w), lambda a, b: (a, b, 0))
    tab = pl.BlockSpec((_HALF, tm), lambda a, b: (0, b))
    tail = pl.BlockSpec((1, D_A, tm), lambda a, b: (a, 0, 0))
    out_shape = (
        jax.ShapeDtypeStruct((g, N_PAIRS, PAIR_W, s), BF16),
        jax.ShapeDtypeStruct((g, N_PAIRS, s, PAIR_W), BF16),
        jax.ShapeDtypeStruct((g, n, D_A, tm), BF16),
        jax.ShapeDtypeStruct((g, D_A, tm), F32),
        jax.ShapeDtypeStruct((g, D_A, tm), F32),
        jax.ShapeDtypeStruct((g, n, D_QF, tm), BF16),
        jax.ShapeDtypeStruct((g, s, D_C), F32),
        jax.ShapeDtypeStruct((g, DH_ROPE, s), F32),
        jax.ShapeDtypeStruct((g, s, D_MODEL), BF16),
        jax.ShapeDtypeStruct((g, s, D_MODEL), BF16),
        jax.ShapeDtypeStruct((g, s, D_QF), BF16),
        jax.ShapeDtypeStruct((g, n, D_VB, tm), BF16),
    )
    pairs = pl.BlockSpec((1, N_PAIRS, tm, PAIR_W), lambda a, b: (a, 0, b, 0))
    pairs_t = pl.BlockSpec((1, N_PAIRS, PAIR_W, tm), lambda a, b: (a, 0, 0, b))
    out_specs = (pairs_t, pairs, pl.BlockSpec((1, 1, D_A, tm), lambda a, b: (a, b, 0, 0)), tail, tail,
                 pl.BlockSpec((1, 1, D_QF, tm), lambda a, b: (a, b, 0, 0)), tok(D_C),
                 pl.BlockSpec((1, DH_ROPE, tm), lambda a, b: (a, 0, b)), tok(D_MODEL),
                 tok(D_MODEL), tok(D_QF), pl.BlockSpec((1, 1, D_VB, tm), lambda a, b: (a, b, 0, 0)))
    in_specs = [tok(D_MODEL), _resident(nw.shape), _resident(wm.shape), _resident(wqt.shape), _resident(wvt.shape),
                _resident(kvn.shape), _resident(wuk.shape), _resident(wuvt.shape), tab, tab, tab,
                tab]
    return pl.pallas_call(
        _in_proj_kernel, grid=(g, n), in_specs=in_specs, out_specs=out_specs, out_shape=out_shape,
        compiler_params=pltpu.CompilerParams(dimension_semantics=("arbitrary", "arbitrary"),
                                             vmem_limit_bytes=VMEM_LIMIT_V7X),
        name="in_proj")(x, nw, wm, wqt, wvt, kvn, wuk, wuvt, cq, sq, ck, sk)


def _kv_up_kernel(cc_ref, cn_ref, kr_ref, wk_ref, ek_ref, wvt_ref, kf_ref, vt_ref):
    c = jnp.concatenate([cc_ref[0], cn_ref[0]], axis=0).astype(BF16)
    kf = jnp.dot(c, wk_ref[...], preferred_element_type=F32)
    kf = kf + jnp.dot(kr_ref[0].astype(BF16), ek_ref[...], preferred_element_type=F32)
    kf_ref[0] = kf.astype(BF16)
    vt_ref[0, 0] = lax.dot_general(wvt_ref[...], c, _NT, preferred_element_type=F32).astype(BF16)


def _kv_up(c_cache, c_new, kr, wk, ek, wvt):
    g, s = c_cache.shape[0], c_cache.shape[1] + c_new.shape[1]
    per_seq = lambda v: pl.BlockSpec((1,) + v.shape[1:], lambda a: (a, 0, 0))
    return pl.pallas_call(
        _kv_up_kernel, grid=(g,),
        in_specs=[per_seq(c_cache), per_seq(c_new), per_seq(kr),
                  _resident(wk.shape), _resident(ek.shape), _resident(wvt.shape)],
        out_specs=(pl.BlockSpec((1, s, D_QF), lambda a: (a, 0, 0)),
                   pl.BlockSpec((1, 1, D_VB, s), lambda a: (a, 0, 0, 0))),
        out_shape=(jax.ShapeDtypeStruct((g, s, D_QF), BF16),
                   jax.ShapeDtypeStruct((g, 1, D_VB, s), BF16)),
        compiler_params=pltpu.CompilerParams(dimension_semantics=("arbitrary",),
                                             vmem_limit_bytes=VMEM_LIMIT_V7X),
        name="kv_up")(c_cache, c_new, kr, wk, ek, wvt)


def _band_prompt_kernel(q_ref, kp_ref, kc_ref, vp_ref, vc_ref, bias_ref, o_ref, s_scr, *, tm, tq):
    subs = [(lo, lo + tq) for lo in range(0, tm, tq)]
    row = lax.broadcasted_iota(jnp.int32, (PAIR_W, tq), 0)

    def tile(first):
        def window(lo, hi):
            return (hi, tm - lo) if first else (tm - lo + hi, 0)

        def scores(g, j):
            lo, hi = subs[j]
            nk, b0 = window(lo, hi)
            q2 = q_ref[0, g, :, lo:hi]
            qcat = jnp.concatenate([jnp.where(row < DH_A, q2, jnp.zeros((), BF16)),
                                    jnp.where(row >= DH_A, q2, jnp.zeros((), BF16))], axis=1)
            k2 = kc_ref[0, g, :hi, :]
            if not first:
                k2 = jnp.concatenate([kp_ref[0, g, lo:, :], k2], axis=0)
            bias = jnp.concatenate([bias_ref[2 * g + hl, b0:b0 + nk, :] for hl in range(2)], axis=1)
            return jnp.dot(k2, qcat, preferred_element_type=F32) + bias

        def stash(j, s):
            s_scr[j, :s.shape[0], :] = s
            return jnp.max(s, axis=0, keepdims=True)

        def consume(g, j, col_max):
            lo, hi = subs[j]
            nk, _ = window(lo, hi)
            outs = []
            for hl in range(2):
                cols = slice(hl * tq, (hl + 1) * tq)
                p = jnp.exp2(s_scr[j, :nk, cols] - col_max[:, cols]).astype(BF16)
                rows = pl.ds(g * PAIR_W + hl * DH_A, DH_A)
                v = vc_ref[0, 0, rows, :hi]
                if not first:
                    v = jnp.concatenate([vp_ref[0, 0, rows, lo:], v], axis=1)
                v = jnp.concatenate([v, jnp.ones((_L_ROWS, nk), BF16)], axis=0)
                o = jnp.dot(v, p, preferred_element_type=F32)
                outs.append(o[:DH_A] / o[DH_A:DH_A + 1])
            o_ref[0, g, lo:hi, :] = jnp.concatenate(outs, axis=0).T.astype(BF16)

        def body(g, col_max):
            new_max = []
            for j in range(len(subs)):
                nxt = scores(g + 1, j)
                consume(g, j, col_max[j])
                new_max.append(stash(j, nxt))
            return tuple(new_max)

        col_max = tuple(stash(j, scores(0, j)) for j in range(len(subs)))
        for g in range(N_PAIRS - 1):
            col_max = body(g, col_max)
        for j in range(len(subs)):
            consume(N_PAIRS - 1, j, col_max[j])

    pl.when(pl.program_id(1) == 0)(functools.partial(tile, True))
    pl.when(pl.program_id(1) > 0)(functools.partial(tile, False))


def _band_prompt(qa, ka, vat, bias, *, tm, tq):
    g, _, _, s = qa.shape
    cur = lambda a, b: (a, 0, b, 0)
    prev = lambda a, b: (a, 0, jnp.maximum(b - 1, 0), 0)
    prev_v = lambda a, b: (a, jnp.maximum(b - 1, 0), 0, 0)
    pairs = lambda idx: pl.BlockSpec((1, N_PAIRS, tm, PAIR_W), idx)
    return pl.pallas_call(
        functools.partial(_band_prompt_kernel, tm=tm, tq=tq), grid=(g, s // tm),
        in_specs=[pl.BlockSpec((1, N_PAIRS, PAIR_W, tm), lambda a, b: (a, 0, 0, b)), pairs(prev), pairs(cur),
                  pl.BlockSpec((1, 1, D_A, tm), prev_v), pl.BlockSpec((1, 1, D_A, tm), lambda a, b: (a, b, 0, 0)),
                  _resident(bias.shape)],
        out_specs=pairs(cur),
        out_shape=jax.ShapeDtypeStruct((g, N_PAIRS, s, PAIR_W), BF16),
        scratch_shapes=[pltpu.VMEM((tm // tq, tm + tq, 2 * tq), F32)],
        compiler_params=pltpu.CompilerParams(dimension_semantics=("arbitrary", "arbitrary"),
                                             vmem_limit_bytes=VMEM_LIMIT_V7X),
        name="band_prompt")(qa, ka, ka, vat, vat, bias)


def _band_sample_kernel(q_ref, kc_ref, kn_ref, vc_ref, vn_ref, bias_ref, o_ref, k_roll_ref, v_roll_ref):
    t = kn_ref.shape[2]
    k_roll_ref[0] = jnp.concatenate([kc_ref[0, :, t:], kn_ref[0]], axis=1)
    v_roll_ref[0] = jnp.concatenate([vc_ref[0, :, t:], vn_ref[0]], axis=1)
    lane = lax.broadcasted_iota(jnp.int32, q_ref.shape[2:], 1)
    for g in range(N_PAIRS):
        rows = slice(g * PAIR_W, (g + 1) * PAIR_W)
        kt = jnp.concatenate([kc_ref[0, rows, :], kn_ref[0, rows, :]], axis=1).astype(BF16)
        q2 = q_ref[g, 0]
        outs = []
        for hl in range(2):
            h = 2 * g + hl
            qm = jnp.where((lane >= DH_A) == (hl == 1), q2, jnp.zeros((), BF16))
            s = jnp.dot(qm, kt, preferred_element_type=F32) + bias_ref[h]
            p = jnp.exp2(s - jnp.max(s, axis=1, keepdims=True))
            hrows = slice(h * DH_A, (h + 1) * DH_A)
            vt = jnp.concatenate([vc_ref[0, hrows, :], vn_ref[0, hrows, :]], axis=1).astype(BF16)
            o = lax.dot_general(p.astype(BF16), vt, _NT, preferred_element_type=F32)
            outs.append(o / jnp.sum(p, axis=1, keepdims=True))
        o_ref[g, 0] = jnp.concatenate(outs, axis=1).astype(BF16)


def _band_sample(qa, kc, kn, vc, vn, bias):
    _, g, t, _ = qa.shape
    pairs = pl.BlockSpec((N_PAIRS, 1, t, PAIR_W), lambda a: (0, a, 0, 0))
    per_seq = lambda v: pl.BlockSpec((1,) + v.shape[1:], lambda a: (a, 0, 0))
    return pl.pallas_call(
        _band_sample_kernel, grid=(g,),
        in_specs=[pairs, per_seq(kc), per_seq(kn), per_seq(vc), per_seq(vn), _resident(bias.shape)],
        out_specs=(pairs, per_seq(kc), per_seq(vc)),
        out_shape=(jax.ShapeDtypeStruct((N_PAIRS, g, t, PAIR_W), BF16),
                   jax.ShapeDtypeStruct(kc.shape, F32), jax.ShapeDtypeStruct(vc.shape, F32)),
        compiler_params=pltpu.CompilerParams(dimension_semantics=("arbitrary",),
                                             vmem_limit_bytes=VMEM_LIMIT_V7X),
        name="band_sample")(qa, kc, kn, vc, vn, bias)


def _mla_kernel(q_ref, k_ref, vt_ref, mask_ref, o_ref, s_scr, *, tq, tk, causal, pairs):
    nq = q_ref.shape[1]
    nk = k_ref.shape[1] // tk
    ones = jnp.ones((_L_ROWS, tk), BF16)

    def q_rows(hp, i):
        return [q_ref[0, i, (2 * hp + hl) * HEAD_PAD:(2 * hp + hl + 1) * HEAD_PAD, :] for hl in range(2)]

    def scores(hp, qs, kk, hl):
        krow = pl.multiple_of(kk * tk, tk)
        return jnp.dot(k_ref[0, pl.ds(krow, tk), (2 * hp + hl) * HEAD_PAD:(2 * hp + hl + 1) * HEAD_PAD], qs[hl],
                       preferred_element_type=F32)

    def stash(hl, s):
        s_scr[hl] = s
        return jnp.max(s, axis=0, keepdims=True)

    def consume(hp, kk, hl, m, acc, tile_max, mask_j):
        if mask_j is not None:
            s = s_scr[hl] + mask_ref[mask_j]
            m_new = jnp.maximum(m, jnp.max(s, axis=0, keepdims=True))
        else:
            s = s_scr[hl]
            m_new = jnp.maximum(m, tile_max)
        p = jnp.exp2(s - m_new).astype(BF16)
        v = jnp.concatenate([vt_ref[0, kk, (2 * hp + hl) * DV_B:(2 * hp + hl + 1) * DV_B, :], ones], axis=0)
        return m_new, acc * jnp.exp2(m - m_new) + jnp.dot(v, p, preferred_element_type=F32)

    def q_tile(hp, i, first_max):
        qs = q_rows(hp, i)

        def body(t, carry):
            out = []
            for hl in range(2):
                m, acc, tile_max = carry[3 * hl:3 * hl + 3]
                nxt = scores(hp, qs, t + 1, hl)
                out += list(consume(hp, t, hl, m, acc, tile_max, None)) + [stash(hl, nxt)]
            return tuple(out)

        r = tq // tk
        n_full = r * i if causal else nk - 1
        carry = ()
        for hl in range(2):
            carry += (jnp.full((1, tq), NEG, F32), jnp.zeros((DV_B + _L_ROWS, tq), F32), first_max[hl])
        done = 0
        for n in (4, 2, 1):
            left = n_full - done
            trips = left // n if isinstance(left, int) else lax.shift_right_logical(left, n.bit_length() - 1)

            def trip(u, c, n=n, done=done):
                for j in range(n):
                    c = body(done + n * u + j, c)
                return c

            carry = lax.fori_loop(0, trips, trip, carry)
            done = done + trips * n
        tail = [(r * i + j, j) for j in range(r)] if causal else [(nk - 1, None)]
        qs_next = q_rows(hp, jnp.minimum(i + 1, nq - 1)) if nq > 1 else None
        state = [list(carry[3 * hl:3 * hl + 3]) for hl in range(2)]
        for idx, (tile, mask_j) in enumerate(tail):
            for hl in range(2):
                if idx + 1 < len(tail):
                    nxt = scores(hp, qs, tile + 1, hl)
                else:
                    nxt = scores(hp, qs_next, 0, hl) if nq > 1 else None
                m, acc = consume(hp, tile, hl, *state[hl], mask_j)
                state[hl] = [m, acc, stash(hl, nxt) if nxt is not None else state[hl][2]]
        accs = [state[hl][1] for hl in range(2)]
        next_max = [state[hl][2] for hl in range(2)]
        o = jnp.concatenate([acc[:DV_B] / acc[DV_B:DV_B + 1] for acc in accs], axis=0)
        o_ref[0, pl.ds(pl.multiple_of(i * tq, tq), tq), hp * 2 * DV_B:(hp + 1) * 2 * DV_B] = o.T.astype(BF16)
        return tuple(next_max)

    for hp in range(pairs):
        first = q_rows(hp, 0)
        lax.fori_loop(0, nq, functools.partial(q_tile, hp),
                      tuple(stash(hl, scores(hp, first, 0, hl)) for hl in range(2)))


def _mla_attn(qf, kf, vt, mask, *, tq, tk, causal, pairs=1):
    g, nq, _, _ = qf.shape
    sq, sk = nq * tq, kf.shape[1]
    return pl.pallas_call(
        functools.partial(_mla_kernel, tq=tq, tk=tk, causal=causal, pairs=pairs), grid=(g, H_B // 2 // pairs),
        in_specs=[pl.BlockSpec((1, nq, 2 * pairs * HEAD_PAD, tq), lambda a, b: (a, 0, b, 0)),
                  pl.BlockSpec((1, sk, 2 * pairs * HEAD_PAD), lambda a, b: (a, 0, b)),
                  pl.BlockSpec((1, sk // tk, 2 * pairs * DV_B, tk), lambda a, b: (a, 0, b, 0)),
                  _resident(mask.shape)],
        out_specs=pl.BlockSpec((1, sq, 2 * pairs * DV_B), lambda a, b: (a, 0, b)),
        out_shape=jax.ShapeDtypeStruct((g, sq, D_VB), BF16),
        scratch_shapes=[pltpu.VMEM((2, tk, tq), F32)],
        compiler_params=pltpu.CompilerParams(dimension_semantics=("arbitrary", "arbitrary"),
                                             vmem_limit_bytes=VMEM_LIMIT_V7X),
        name="mla_attn")(qf, kf, vt, mask)


_G0 = 8


def _merge_ffn_kernel(x_ref, ya_ref, yb_ref, ga_ref, gb_ref, st_ref, wa_ref, wb_ref, wo_ref, n1_ref, n2_ref, n3_ref,
                      wg_ref, wu_ref, cw_ref, cb_ref, wd_ref, y_ref, cs_ref, gbuf, *, tm, nseq):
    i = pl.program_id(1)
    seg = tm // nseq
    starts = [_G0 + j * (seg + _G0) for j in range(nseq)]
    if nseq == 1:
        @pl.when(i == 0)
        def _():
            gbuf[_G0 - 2:_G0, :] = st_ref[0]

        @pl.when(i > 0)
        def _():
            gbuf[_G0 - 2:_G0, :] = gbuf[_G0 + tm - 2:_G0 + tm, :]
    else:
        for j, r0 in enumerate(starts):
            gbuf[r0 - 2:r0, :] = st_ref[j]

    parts = [(0, tm // 2), (tm // 2, tm)] if nseq == 1 else [(0, tm)]

    mixes = []
    for lo, hi in parts:
        ya = jnp.concatenate([ya_ref[0, g, lo:hi, :] for g in range(N_PAIRS)], axis=1)
        za = jnp.dot(ya, wa_ref[...], preferred_element_type=F32)
        zb = jnp.dot(yb_ref[0, lo:hi, :], wb_ref[...], preferred_element_type=F32)
        mixes.append((jax.nn.sigmoid(ga_ref[0, lo:hi, :].astype(F32)) * za
                      + jax.nn.sigmoid(gb_ref[0, lo:hi, :].astype(F32)) * zb).astype(BF16))
    x1s, xns = [], []
    for (lo, hi), mix in zip(parts, mixes):
        mo = jnp.dot(mix, wo_ref[...], preferred_element_type=F32)
        x1s.append(x_ref[0, lo:hi, :] + _rms(mo, n1_ref[...]))
        xns.append(_rms(x1s[-1], n2_ref[...]).astype(BF16))

    ups = []
    for (lo, hi), xn in zip(parts, xns):
        gate = jnp.dot(xn, wg_ref[...], preferred_element_type=F32)
        if nseq == 1:
            gbuf[_G0 + lo:_G0 + hi, :] = gate
        else:
            for j, r0 in enumerate(starts):
                gbuf[r0:r0 + seg, :] = gate[j * seg:(j + 1) * seg]
        ups.append(jnp.dot(xn, wu_ref[...], preferred_element_type=F32))
    for j, r0 in enumerate(starts):
        cs_ref[j] = gbuf[r0 + seg - 2:r0 + seg, :]

    def conv(r0, rows):
        return (cw_ref[0:1, :] * gbuf[r0 - 2:r0 - 2 + rows, :] + cw_ref[1:2, :] * gbuf[r0 - 1:r0 - 1 + rows, :]
                + cw_ref[2:3, :] * gbuf[r0:r0 + rows, :])

    for (lo, hi), u, x1 in zip(parts, ups, x1s):
        if nseq == 1:
            c = conv(_G0 + lo, hi - lo)
        else:
            c = jnp.concatenate([conv(r0, seg) for r0 in starts], axis=0)
        hid = (jax.nn.gelu(c + cb_ref[...], approximate=True) * u).astype(BF16)
        f = jnp.dot(hid, wd_ref[...], preferred_element_type=F32)
        y_ref[0, lo:hi, :] = x1 + _rms(f, n3_ref[...])


def _merge_ffn(x, ya, yb, ga, gb, state, wa, wb, wo, n1, n2, n3, wg, wu, cw, cb, wd, *, tm, nseq=1):
    g, s, _ = x.shape
    assert nseq == 1 or s == tm
    tok = lambda w: pl.BlockSpec((1, tm, w), lambda a, b: (a, b, 0))
    per_group = pl.BlockSpec((nseq, CONV_W - 1, D_FF), lambda a, b: (a, 0, 0))
    weights = (wa, wb, wo, n1, n2, n3, wg, wu, cw, cb, wd)
    return pl.pallas_call(
        functools.partial(_merge_ffn_kernel, tm=tm, nseq=nseq), grid=(g, s // tm),
        in_specs=[tok(D_MODEL), pl.BlockSpec((1, N_PAIRS, tm, PAIR_W), lambda a, b: (a, 0, b, 0)), tok(D_VB),
                  tok(D_MODEL), tok(D_MODEL), per_group]
                 + [_resident(w.shape) for w in weights],
        out_specs=(tok(D_MODEL), per_group),
        out_shape=(jax.ShapeDtypeStruct((g, s, D_MODEL), F32),
                   jax.ShapeDtypeStruct((g * nseq, CONV_W - 1, D_FF), F32)),
        scratch_shapes=[pltpu.VMEM((tm + nseq * _G0, D_FF), F32)],
        compiler_params=pltpu.CompilerParams(dimension_semantics=("arbitrary", "arbitrary"),
                                             vmem_limit_bytes=VMEM_LIMIT_V7X),
        name="merge_ffn")(x, ya, yb, ga, gb, state, *weights)


def _rope_tables(pos, scale):
    inv = ROPE_BASE ** (-np.arange(_HALF, dtype=np.float64) / _HALF)
    ang = inv[:, None] * np.asarray(pos, np.float64)[None, :]
    return jnp.asarray(np.cos(ang) * scale, F32), jnp.asarray(np.sin(ang) * scale, F32)


def _band_bias_kernel(ext_ref, o_ref, *, tq, nk, q_major):
    w = ext_ref.shape[-1]
    qc = (nk - tq + lax.broadcasted_iota(jnp.int32, (nk, tq), 1)) // CHUNK
    kc = lax.broadcasted_iota(jnp.int32, (nk, tq), 0) // CHUNK
    in_band = (kc >= qc - LEFT_CHUNKS) & (kc <= qc)
    for h in range(ext_ref.shape[0]):
        toeplitz = pltpu.roll(jnp.broadcast_to(ext_ref[h], (nk, w)), 0, 1, stride=1, stride_axis=0)[:, :tq]
        bias = jnp.where(in_band, toeplitz, NEG)
        o_ref[h] = bias.T if q_major else bias


def _band_bias(table, tq, nk, q_major=False):
    w = nk + tq
    d_lo, d_hi = 1 - tq, w - tq
    assert -MAX_REL <= d_lo and d_hi >= MAX_REL and w % 128 == 0
    h = table.shape[0]
    ext = jnp.concatenate([table[:, d_lo + MAX_REL:], jnp.broadcast_to(table[:, -1:], (h, d_hi - MAX_REL))], axis=1)
    ext = (jnp.roll(ext, -(nk - 1), axis=1) * LOG2E).reshape(h, 1, w)
    return pl.pallas_call(
        functools.partial(_band_bias_kernel, tq=tq, nk=nk, q_major=q_major),
        out_shape=jax.ShapeDtypeStruct((h, tq, nk) if q_major else (h, nk, tq), F32),
        compiler_params=pltpu.CompilerParams(vmem_limit_bytes=VMEM_LIMIT_V7X),
        name="band_bias")(ext)


def _chunk_masks(tk, tq):
    kc = jnp.arange(tq)[:, None] // CHUNK
    qc = jnp.arange(tq)[None, :] // CHUNK
    return jnp.where(kc <= qc, 0.0, NEG).astype(F32).reshape(tq // tk, tk, tq)


def _scaled_cast_kernel(w_ref, o_ref, *, rows, n_scaled, scale):
    r = pl.program_id(0) * rows + lax.broadcasted_iota(jnp.int32, (rows, 1), 0)
    o_ref[...] = (w_ref[0] * jnp.where(r < n_scaled, scale, 1.0)).astype(BF16)


def _scaled_cast(w, n_scaled, scale, *, rows):
    _, d, n = w.shape
    return pl.pallas_call(
        functools.partial(_scaled_cast_kernel, rows=rows, n_scaled=n_scaled, scale=scale), grid=(d // rows,),
        in_specs=[pl.BlockSpec((1, rows, n), lambda a: (0, a, 0))],
        out_specs=pl.BlockSpec((rows, n), lambda a: (a, 0)),
        out_shape=jax.ShapeDtypeStruct((d, n), BF16),
        compiler_params=pltpu.CompilerParams(dimension_semantics=("arbitrary",)),
        name="cast_w_in")(w)


def _prep_weights(w_in, w_uk, w_uv):
    o = np.cumsum((0,) + IN_SIZES)
    d = w_in.shape[1]
    w_t = _scaled_cast(jnp.swapaxes(w_in, 1, 2), int(o[1]), A_SCALE * LOG2E, rows=928)
    row = lambda j: w_t[o[j]:o[j + 1]]
    qn = row(3).reshape(H_B, DH_NOPE, d)
    qr = row(4).reshape(H_B, DH_ROPE, d)
    pad = jnp.zeros((H_B, HEAD_PAD - DH_NOPE - DH_ROPE, d), BF16).at[0].set(row(6))
    wqt = jnp.concatenate([jnp.concatenate([qn, qr, pad], axis=1).reshape(D_QF, d), row(0)], axis=0)
    wm = jnp.concatenate([row(1), row(5), row(7), row(8)], axis=0).T
    wvat = row(2)
    wk = jnp.concatenate([w_uk, jnp.zeros((D_C, H_B, HEAD_PAD - DH_NOPE), F32)], axis=2).reshape(D_C, D_QF).astype(BF16)
    place = jnp.concatenate([jnp.zeros((DH_ROPE, DH_NOPE), F32), jnp.eye(DH_ROPE, dtype=F32),
                             jnp.zeros((DH_ROPE, HEAD_PAD - DH_NOPE - DH_ROPE), F32)], axis=1)
    ek = jnp.tile(place, (1, H_B)).astype(BF16)
    wvt = w_uv.reshape(D_C, D_VB).T.astype(BF16)
    return wm, wqt, wvat, wk, ek, wvt


def kernel(x_prompt, x_sample, cache_a_k, cache_a_v, cache_mla_ckv, cache_mla_krope, state_ffn_conv, norm_mix_pre,
           norm_mix_post, w_in, rel_bias_table, kv_norm, w_uk, w_uv, w_branch_a, w_branch_b, w_out, norm_ffn_pre,
           norm_ffn_post, w_ffn_gate, w_ffn_up, conv_w, conv_b, w_ffn_down):
    assert w_in.shape[0] == 1, "single layer"
    b, s, _ = x_prompt.shape
    db, t, _ = x_sample.shape
    past = cache_mla_ckv.shape[2]
    wcache = cache_a_k.shape[2]
    keep = min(A_WINDOW, s)
    tm1 = 512
    assert keep == tm1 and db * t == tm1 and wcache == A_WINDOW

    wm, wqt, wvat, wk, ek, wvt = _prep_weights(w_in, w_uk[0], w_uv[0])
    row = lambda v: v.reshape(1, -1)
    proj_w = (row(norm_mix_pre[0]), wm, wqt, wvat, row(kv_norm[0]), wk, wvt)
    ffn_w = (w_branch_a[0].astype(BF16), w_branch_b[0].astype(BF16), w_out[0].astype(BF16), row(norm_mix_post[0]),
             row(norm_ffn_pre[0]), row(norm_ffn_post[0]), w_ffn_gate[0].astype(BF16), w_ffn_up[0].astype(BF16),
             conv_w[0], row(conv_b[0]), w_ffn_down[0].astype(BF16))
    table = rel_bias_table[0]

    pos = np.arange(s)
    tabs = _rope_tables(pos, MLA_SCALE * LOG2E) + _rope_tables(pos, 1.0)
    t_mla = tm1
    qa, ka, vat, kaf, vaf, qf, ckv, kr, ga, gb, kf, vt = _in_proj(x_prompt, *proj_w, *tabs, tm=tm1)
    ya = _band_prompt(qa, ka, vat, _band_bias(table, 256, A_WINDOW + 256), tm=tm1, tq=256)
    yb = _mla_attn(qf, kf, vt, _chunk_masks(t_mla, t_mla), tq=t_mla, tk=t_mla, causal=True)
    y_prompt, conv_p = _merge_ffn(x_prompt, ya, yb, ga, gb, jnp.zeros((b, CONV_W - 1, D_FF), F32), *ffn_w, tm=512)

    pos_s = np.tile(past + np.arange(t), db)
    tabs_s = _rope_tables(pos_s, MLA_SCALE * LOG2E) + _rope_tables(pos_s, 1.0)
    qa2, ka2, vat2, kaf2, vaf2, qf2, ckv2, kr2, ga2, gb2, _, _ = _in_proj(x_sample.reshape(1, db * t, D_MODEL), *proj_w,
                                                                    *tabs_s, tm=tm1)
    per_seq = lambda v: v.reshape(db, t, v.shape[-1])
    ckv2, kr2 = per_seq(ckv2[0]), per_seq(kr2[0].T)
    cached_ft = lambda c: jnp.transpose(c[0], (0, 2, 3, 1)).reshape(db, D_A, wcache)
    new_ft = lambda v: jnp.transpose(v[0].reshape(D_A, db, t), (1, 0, 2))
    kc, vc, kn, vn = cached_ft(cache_a_k), cached_ft(cache_a_v), new_ft(kaf2), new_ft(vaf2)
    qa2 = jnp.transpose(qa2[0].reshape(N_PAIRS, PAIR_W, db, t), (0, 2, 3, 1))
    ya2, new_k, new_v = _band_sample(qa2, kc, kn, vc, vn, _band_bias(table, t, wcache + t, q_major=True))
    kr_all = jnp.concatenate([cache_mla_krope[0], kr2], axis=1)
    kf2, vt2 = _kv_up(cache_mla_ckv[0], ckv2, kr_all, wk, ek, wvt)
    qf2 = jnp.transpose(qf2[0, 0].reshape(D_QF, db, t), (1, 0, 2))[:, None]
    yb2 = _mla_attn(qf2, kf2, vt2, jnp.zeros((1, 8, 128), F32), tq=t, tk=past + t, causal=False, pairs=H_B // 2)
    y_sample, conv_s = _merge_ffn(x_sample.reshape(1, db * t, D_MODEL), ya2.reshape(1, N_PAIRS, db * t, PAIR_W),
                                  yb2.reshape(1, db * t, D_VB), ga2, gb2, state_ffn_conv[0], *ffn_w,
                                  tm=db * t, nseq=db)
    y_sample = y_sample.reshape(db, t, D_MODEL)

    heads_t = lambda v: jnp.transpose(v.reshape(1, v.shape[0], H_A, DH_A, v.shape[2]), (0, 1, 4, 2, 3))
    return (y_prompt, y_sample,
            heads_t(kaf), heads_t(vaf), ckv[None], jnp.swapaxes(kr, 1, 2)[None], conv_p[None],
            heads_t(new_k), heads_t(new_v), ckv2[None], kr2[None], conv_s[None])
```

```python
import functools

import jax
import jax.numpy as jnp
import numpy as np
from jax import lax
from jax.experimental import pallas as pl
from jax.experimental.pallas import tpu as pltpu

D_MODEL = 1024
CHUNK = 64
LEFT_CHUNKS = 8
A_WINDOW = LEFT_CHUNKS * CHUNK
H_A = 8
DH_A = 64
MAX_REL = 256
H_B = 8
DH_NOPE = 64
DH_ROPE = 32
DV_B = 64
D_C = 256
D_FF = 2816
CONV_W = 3
ROPE_BASE = 10000.0
EPS = 1e-6
A_SCALE = DH_A ** -0.5
MLA_SCALE = (DH_NOPE + DH_ROPE) ** -0.5
IN_SIZES = (H_A * DH_A, H_A * DH_A, H_A * DH_A, H_B * DH_NOPE, H_B * DH_ROPE, D_C, DH_ROPE, D_MODEL, D_MODEL)

HEAD_PAD = 128
D_A = H_A * DH_A
D_VB = H_B * DV_B
D_QF = H_B * HEAD_PAD
PAIR_W = 2 * DH_A
N_PAIRS = H_A // 2
NEG = -1e30
LOG2E = 1.4426950408889634
_L_ROWS = 16
VMEM_LIMIT_V7X = 56 * 1024 * 1024

F32 = jnp.float32
BF16 = jnp.bfloat16
_NT = (((1,), (1,)), ((), ()))


def _resident(shape):
    nd = len(shape)
    return pl.BlockSpec(shape, lambda *_: (0,) * nd, pipeline_mode=pl.Buffered(1))


def _rms(x, w):
    return x * lax.rsqrt(jnp.mean(x * x, axis=-1, keepdims=True) + EPS) * w


_C_KA, _C_CKV, _C_GA, _C_GB, _C_END = 0, 512, 768, 1792, 2816
_KR_ROW = DH_NOPE + DH_ROPE
_HALF = DH_ROPE // 2


def _store_pairs(ref, v):
    for g in range(N_PAIRS):
        ref[0, g] = v[:, g * PAIR_W:(g + 1) * PAIR_W]


def _rope_rows(x, c, s):
    x1, x2 = x[:_HALF], x[_HALF:]
    return [x1 * c - x2 * s, x1 * s + x2 * c]


def _in_proj_kernel(x_ref, nw_ref, wm_ref, wqt_ref, wvt_ref, kvn_ref, wuk_ref, wuvt_ref, cq_ref, sq_ref,
                    ck_ref, sk_ref, qa_ref, ka_ref, vat_ref, kaf_ref, vaf_ref, qf_ref, ckv_ref, kr_ref, ga_ref, gb_ref,
                    kf_ref, vt_ref):
    xn = _rms(x_ref[0], nw_ref[...]).astype(BF16)

    def proj(lo, hi):
        return jnp.dot(xn, wm_ref[:, lo:hi], preferred_element_type=F32)

    ka = proj(_C_KA, _C_CKV)
    _store_pairs(ka_ref, ka.astype(BF16))
    vat = lax.dot_general(wvt_ref[...], xn, _NT, preferred_element_type=F32)
    vat_ref[0, 0] = vat.astype(BF16)
    vaf_ref[0] = vat
    kaf_ref[0] = ka.T

    qt = lax.dot_general(wqt_ref[...], xn, _NT, preferred_element_type=F32)
    for g in range(N_PAIRS):
        qa_ref[0, g] = qt[D_QF + g * PAIR_W:D_QF + (g + 1) * PAIR_W].astype(BF16)
    cq, sq = cq_ref[...], sq_ref[...]
    pad = jnp.zeros((HEAD_PAD - _KR_ROW, qt.shape[1]), F32)
    for h in range(H_B):
        slab = qt[h * HEAD_PAD:(h + 1) * HEAD_PAD]
        rows = [slab[:DH_NOPE] * (MLA_SCALE * LOG2E)] + _rope_rows(slab[DH_NOPE:_KR_ROW], cq, sq) + [pad]
        qf_ref[0, 0, h * HEAD_PAD:(h + 1) * HEAD_PAD, :] = jnp.concatenate(rows, axis=0).astype(BF16)
    kr_t = _rope_rows(qt[_KR_ROW:HEAD_PAD], ck_ref[...], sk_ref[...])
    zero_rows = lambda r: jnp.zeros((r, qt.shape[1]), F32)
    kr_ref[0] = jnp.concatenate(kr_t, axis=0)

    ckv_raw = proj(_C_CKV, _C_GA)
    ga_ref[0] = proj(_C_GA, _C_GB).astype(BF16)
    ckv = _rms(ckv_raw, kvn_ref[...])
    ckv_ref[0] = ckv
    c_bf = ckv.astype(BF16)
    k_nope = jnp.dot(c_bf, wuk_ref[...], preferred_element_type=F32)
    k_rope = jnp.concatenate([zero_rows(DH_NOPE)] + kr_t + [zero_rows(HEAD_PAD - _KR_ROW)], axis=0).T
    for h in range(H_B):
        slab = slice(h * HEAD_PAD, (h + 1) * HEAD_PAD)
        kf_ref[0, :, slab] = (k_nope[:, slab] + k_rope).astype(BF16)
    vt_ref[0, 0] = lax.dot_general(wuvt_ref[...], c_bf, _NT, preferred_element_type=F32).astype(BF16)
    gb_ref[0] = proj(_C_GB, _C_END).astype(BF16)


def _in_proj(x, nw, wm, wqt, wvt, kvn, wuk, wuvt, cq, sq, ck, sk, *, tm):
    g, s, _ = x.shape
    n = s // tm
    tok = lambda w: pl.BlockSpec((1, tm, w), lambda a, b: (a, b, 0))
    tab = pl.BlockSpec((_HALF, tm), lambda a, b: (0, b))
    tail = pl.BlockSpec((1, D_A, tm), lambda a, b: (a, 0, 0))
    out_shape = (
        jax.ShapeDtypeStruct((g, N_PAIRS, PAIR_W, s), BF16),
        jax.ShapeDtypeStruct((g, N_PAIRS, s, PAIR_W), BF16),
        jax.ShapeDtypeStruct((g, n, D_A, tm), BF16),
        jax.ShapeDtypeStruct((g, D_A, tm), F32),
        jax.ShapeDtypeStruct((g, D_A, tm), F32),
        jax.ShapeDtypeStruct((g, n, D_QF, tm), BF16),
        jax.ShapeDtypeStruct((g, s, D_C), F32),
        jax.ShapeDtypeStruct((g, DH_ROPE, s), F32),
        jax.ShapeDtypeStruct((g, s, D_MODEL), BF16),
        jax.ShapeDtypeStruct((g, s, D_MODEL), BF16),
        jax.ShapeDtypeStruct((g, s, D_QF), BF16),
        jax.ShapeDtypeStruct((g, n, D_VB, tm), BF16),
    )
    pairs = pl.BlockSpec((1, N_PAIRS, tm, PAIR_W), lambda a, b: (a, 0, b, 0))
    pairs_t = pl.BlockSpec((1, N_PAIRS, PAIR_W, tm), lambda a, b: (a, 0, 0, b))
    out_specs = (pairs_t, pairs, pl.BlockSpec((1, 1, D_A, tm), lambda a, b: (a, b, 0, 0)), tail, tail,
                 pl.BlockSpec((1, 1, D_QF, tm), lambda a, b: (a, b, 0, 0)), tok(D_C),
                 pl.BlockSpec((1, DH_ROPE, tm), lambda a, b: (a, 0, b)), tok(D_MODEL),
                 tok(D_MODEL), tok(D_QF), pl.BlockSpec((1, 1, D_VB, tm), lambda a, b: (a, b, 0, 0)))
    in_specs = [tok(D_MODEL), _resident(nw.shape), _resident(wm.shape), _resident(wqt.shape), _resident(wvt.shape),
                _resident(kvn.shape), _resident(wuk.shape), _resident(wuvt.shape), tab, tab, tab,
                tab]
    return pl.pallas_call(
        _in_proj_kernel, grid=(g, n), in_specs=in_specs, out_specs=out_specs, out_shape=out_shape,
        compiler_params=pltpu.CompilerParams(dimension_semantics=("arbitrary", "arbitrary"),
                                             vmem_limit_bytes=VMEM_LIMIT_V7X),
        name="in_proj")(x, nw, wm, wqt, wvt, kvn, wuk, wuvt, cq, sq, ck, sk)


def _kv_up_kernel(cc_ref, cn_ref, kr_ref, wk_ref, ek_ref, wvt_ref, kf_ref, vt_ref):
    c = jnp.concatenate([cc_ref[0], cn_ref[0]], axis=0).astype(BF16)
    kf = jnp.dot(c, wk_ref[...], preferred_element_type=F32)
    kf = kf + jnp.dot(kr_ref[0].astype(BF16), ek_ref[...], preferred_element_type=F32)
    kf_ref[0] = kf.astype(BF16)
    vt_ref[0, 0] = lax.dot_general(wvt_ref[...], c, _NT, preferred_element_type=F32).astype(BF16)


def _kv_up(c_cache, c_new, kr, wk, ek, wvt):
    g, s = c_cache.shape[0], c_cache.shape[1] + c_new.shape[1]
    per_seq = lambda v: pl.BlockSpec((1,) + v.shape[1:], lambda a: (a, 0, 0))
    return pl.pallas_call(
        _kv_up_kernel, grid=(g,),
        in_specs=[per_seq(c_cache), per_seq(c_new), per_seq(kr),
                  _resident(wk.shape), _resident(ek.shape), _resident(wvt.shape)],
        out_specs=(pl.BlockSpec((1, s, D_QF), lambda a: (a, 0, 0)),
                   pl.BlockSpec((1, 1, D_VB, s), lambda a: (a, 0, 0, 0))),
        out_shape=(jax.ShapeDtypeStruct((g, s, D_QF), BF16),
                   jax.ShapeDtypeStruct((g, 1, D_VB, s), BF16)),
        compiler_params=pltpu.CompilerParams(dimension_semantics=("arbitrary",),
                                             vmem_limit_bytes=VMEM_LIMIT_V7X),
        name="kv_up")(c_cache, c_new, kr, wk, ek, wvt)


def _band_prompt_kernel(q_ref, kp_ref, kc_ref, vp_ref, vc_ref, bias_ref, o_ref, s_scr, *, tm, tq):
    subs = [(lo, lo + tq) for lo in range(0, tm, tq)]
    row = lax.broadcasted_iota(jnp.int32, (PAIR_W, tq), 0)

    def tile(first):
        def window(lo, hi):
            return (hi, tm - lo) if first else (tm - lo + hi, 0)

        def scores(g, j):
            lo, hi = subs[j]
            nk, b0 = window(lo, hi)
            q2 = q_ref[0, g, :, lo:hi]
            qcat = jnp.concatenate([jnp.where(row < DH_A, q2, jnp.zeros((), BF16)),
                                    jnp.where(row >= DH_A, q2, jnp.zeros((), BF16))], axis=1)
            k2 = kc_ref[0, g, :hi, :]
            if not first:
                k2 = jnp.concatenate([kp_ref[0, g, lo:, :], k2], axis=0)
            bias = jnp.concatenate([bias_ref[2 * g + hl, b0:b0 + nk, :] for hl in range(2)], axis=1)
            return jnp.dot(k2, qcat, preferred_element_type=F32) + bias

        def stash(j, s):
            s_scr[j, :s.shape[0], :] = s
            return jnp.max(s, axis=0, keepdims=True)

        def consume(g, j, col_max):
            lo, hi = subs[j]
            nk, _ = window(lo, hi)
            outs = []
            for hl in range(2):
                cols = slice(hl * tq, (hl + 1) * tq)
                p = jnp.exp2(s_scr[j, :nk, cols] - col_max[:, cols]).astype(BF16)
                rows = pl.ds(g * PAIR_W + hl * DH_A, DH_A)
                v = vc_ref[0, 0, rows, :hi]
                if not first:
                    v = jnp.concatenate([vp_ref[0, 0, rows, lo:], v], axis=1)
                v = jnp.concatenate([v, jnp.ones((_L_ROWS, nk), BF16)], axis=0)
                o = jnp.dot(v, p, preferred_element_type=F32)
                outs.append(o[:DH_A] / o[DH_A:DH_A + 1])
            o_ref[0, g, lo:hi, :] = jnp.concatenate(outs, axis=0).T.astype(BF16)

        def body(g, col_max):
            new_max = []
            for j in range(len(subs)):
                nxt = scores(g + 1, j)
                consume(g, j, col_max[j])
                new_max.append(stash(j, nxt))
            return tuple(new_max)

        col_max = tuple(stash(j, scores(0, j)) for j in range(len(subs)))
        for g in range(N_PAIRS - 1):
            col_max = body(g, col_max)
        for j in range(len(subs)):
            consume(N_PAIRS - 1, j, col_max[j])

    pl.when(pl.program_id(1) == 0)(functools.partial(tile, True))
    pl.when(pl.program_id(1) > 0)(functools.partial(tile, False))


def _band_prompt(qa, ka, vat, bias, *, tm, tq):
    g, _, _, s = qa.shape
    cur = lambda a, b: (a, 0, b, 0)
    prev = lambda a, b: (a, 0, jnp.maximum(b - 1, 0), 0)
    prev_v = lambda a, b: (a, jnp.maximum(b - 1, 0), 0, 0)
    pairs = lambda idx: pl.BlockSpec((1, N_PAIRS, tm, PAIR_W), idx)
    return pl.pallas_call(
        functools.partial(_band_prompt_kernel, tm=tm, tq=tq), grid=(g, s // tm),
        in_specs=[pl.BlockSpec((1, N_PAIRS, PAIR_W, tm), lambda a, b: (a, 0, 0, b)), pairs(prev), pairs(cur),
                  pl.BlockSpec((1, 1, D_A, tm), prev_v), pl.BlockSpec((1, 1, D_A, tm), lambda a, b: (a, b, 0, 0)),
                  _resident(bias.shape)],
        out_specs=pairs(cur),
        out_shape=jax.ShapeDtypeStruct((g, N_PAIRS, s, PAIR_W), BF16),
        scratch_shapes=[pltpu.VMEM((tm // tq, tm + tq, 2 * tq), F32)],
        compiler_params=pltpu.CompilerParams(dimension_semantics=("arbitrary", "arbitrary"),
                                             vmem_limit_bytes=VMEM_LIMIT_V7X),
        name="band_prompt")(qa, ka, ka, vat, vat, bias)


def _band_sample_kernel(q_ref, kc_ref, kn_ref, vc_ref, vn_ref, bias_ref, o_ref, k_roll_ref, v_roll_ref):
    t = kn_ref.shape[2]
    k_roll_ref[0] = jnp.concatenate([kc_ref[0, :, t:], kn_ref[0]], axis=1)
    v_roll_ref[0] = jnp.concatenate([vc_ref[0, :, t:], vn_ref[0]], axis=1)
    lane = lax.broadcasted_iota(jnp.int32, q_ref.shape[2:], 1)
    for g in range(N_PAIRS):
        rows = slice(g * PAIR_W, (g + 1) * PAIR_W)
        kt = jnp.concatenate([kc_ref[0, rows, :], kn_ref[0, rows, :]], axis=1).astype(BF16)
        q2 = q_ref[g, 0]
        outs = []
        for hl in range(2):
            h = 2 * g + hl
            qm = jnp.where((lane >= DH_A) == (hl == 1), q2, jnp.zeros((), BF16))
            s = jnp.dot(qm, kt, preferred_element_type=F32) + bias_ref[h]
            p = jnp.exp2(s - jnp.max(s, axis=1, keepdims=True))
            hrows = slice(h * DH_A, (h + 1) * DH_A)
            vt = jnp.concatenate([vc_ref[0, hrows, :], vn_ref[0, hrows, :]], axis=1).astype(BF16)
            o = lax.dot_general(p.astype(BF16), vt, _NT, preferred_element_type=F32)
            outs.append(o / jnp.sum(p, axis=1, keepdims=True))
        o_ref[g, 0] = jnp.concatenate(outs, axis=1).astype(BF16)


def _band_sample(qa, kc, kn, vc, vn, bias):
    _, g, t, _ = qa.shape
    pairs = pl.BlockSpec((N_PAIRS, 1, t, PAIR_W), lambda a: (0, a, 0, 0))
    per_seq = lambda v: pl.BlockSpec((1,) + v.shape[1:], lambda a: (a, 0, 0))
    return pl.pallas_call(
        _band_sample_kernel, grid=(g,),
        in_specs=[pairs, per_seq(kc), per_seq(kn), per_seq(vc), per_seq(vn), _resident(bias.shape)],
        out_specs=(pairs, per_seq(kc), per_seq(vc)),
        out_shape=(jax.ShapeDtypeStruct((N_PAIRS, g, t, PAIR_W), BF16),
                   jax.ShapeDtypeStruct(kc.shape, F32), jax.ShapeDtypeStruct(vc.shape, F32)),
        compiler_params=pltpu.CompilerParams(dimension_semantics=("arbitrary",),
                                             vmem_limit_bytes=VMEM_LIMIT_V7X),
        name="band_sample")(qa, kc, kn, vc, vn, bias)


def _mla_kernel(q_ref, k_ref, vt_ref, mask_ref, o_ref, s_scr, *, tq, tk, causal, pairs):
    nq = q_ref.shape[1]
    nk = k_ref.shape[1] // tk
    ones = jnp.ones((_L_ROWS, tk), BF16)

    def q_rows(hp, i):
        return [q_ref[0, i, (2 * hp + hl) * HEAD_PAD:(2 * hp + hl + 1) * HEAD_PAD, :] for hl in range(2)]

    def scores(hp, qs, kk, hl):
        krow = pl.multiple_of(kk * tk, tk)
        return jnp.dot(k_ref[0, pl.ds(krow, tk), (2 * hp + hl) * HEAD_PAD:(2 * hp + hl + 1) * HEAD_PAD], qs[hl],
                       preferred_element_type=F32)

    def stash(hl, s):
        s_scr[hl] = s
        return jnp.max(s, axis=0, keepdims=True)

    def consume(hp, kk, hl, m, acc, tile_max, mask_j):
        if mask_j is not None:
            s = s_scr[hl] + mask_ref[mask_j]
            m_new = jnp.maximum(m, jnp.max(s, axis=0, keepdims=True))
        else:
            s = s_scr[hl]
            m_new = jnp.maximum(m, tile_max)
        p = jnp.exp2(s - m_new).astype(BF16)
        v = jnp.concatenate([vt_ref[0, kk, (2 * hp + hl) * DV_B:(2 * hp + hl + 1) * DV_B, :], ones], axis=0)
        return m_new, acc * jnp.exp2(m - m_new) + jnp.dot(v, p, preferred_element_type=F32)

    def q_tile(hp, i, first_max):
        qs = q_rows(hp, i)

        def body(t, carry):
            out = []
            for hl in range(2):
                m, acc, tile_max = carry[3 * hl:3 * hl + 3]
                nxt = scores(hp, qs, t + 1, hl)
                out += list(consume(hp, t, hl, m, acc, tile_max, None)) + [stash(hl, nxt)]
            return tuple(out)

        r = tq // tk
        n_full = r * i if causal else nk - 1
        carry = ()
        for hl in range(2):
            carry += (jnp.full((1, tq), NEG, F32), jnp.zeros((DV_B + _L_ROWS, tq), F32), first_max[hl])
        done = 0
        for n in (4, 2, 1):
            left = n_full - done
            trips = left // n if isinstance(left, int) else lax.shift_right_logical(left, n.bit_length() - 1)

            def trip(u, c, n=n, done=done):
                for j in range(n):
                    c = body(done + n * u + j, c)
                return c

            carry = lax.fori_loop(0, trips, trip, carry)
            done = done + trips * n
        tail = [(r * i + j, j) for j in range(r)] if causal else [(nk - 1, None)]
        qs_next = q_rows(hp, jnp.minimum(i + 1, nq - 1)) if nq > 1 else None
        state = [list(carry[3 * hl:3 * hl + 3]) for hl in range(2)]
        for idx, (tile, mask_j) in enumerate(tail):
            for hl in range(2):
                if idx + 1 < len(tail):
                    nxt = scores(hp, qs, tile + 1, hl)
                else:
                    nxt = scores(hp, qs_next, 0, hl) if nq > 1 else None
                m, acc = consume(hp, tile, hl, *state[hl], mask_j)
                state[hl] = [m, acc, stash(hl, nxt) if nxt is not None else state[hl][2]]
        accs = [state[hl][1] for hl in range(2)]
        next_max = [state[hl][2] for hl in range(2)]
        o = jnp.concatenate([acc[:DV_B] / acc[DV_B:DV_B + 1] for acc in accs], axis=0)
        o_ref[0, pl.ds(pl.multiple_of(i * tq, tq), tq), hp * 2 * DV_B:(hp + 1) * 2 * DV_B] = o.T.astype(BF16)
        return tuple(next_max)

    for hp in range(pairs):
        first = q_rows(hp, 0)
        lax.fori_loop(0, nq, functools.partial(q_tile, hp),
                      tuple(stash(hl, scores(hp, first, 0, hl)) for hl in range(2)))


def _mla_attn(qf, kf, vt, mask, *, tq, tk, causal, pairs=1):
    g, nq, _, _ = qf.shape
    sq, sk = nq * tq, kf.shape[1]
    return pl.pallas_call(
        functools.partial(_mla_kernel, tq=tq, tk=tk, causal=causal, pairs=pairs), grid=(g, H_B // 2 // pairs),
        in_specs=[pl.BlockSpec((1, nq, 2 * pairs * HEAD_PAD, tq), lambda a, b: (a, 0, b, 0)),
                  pl.BlockSpec((1, sk, 2 * pairs * HEAD_PAD), lambda a, b: (a, 0, b)),
                  pl.BlockSpec((1, sk // tk, 2 * pairs * DV_B, tk), lambda a, b: (a, 0, b, 0)),
                  _resident(mask.shape)],
        out_specs=pl.BlockSpec((1, sq, 2 * pairs * DV_B), lambda a, b: (a, 0, b)),
        out_shape=jax.ShapeDtypeStruct((g, sq, D_VB), BF16),
        scratch_shapes=[pltpu.VMEM((2, tk, tq), F32)],
        compiler_params=pltpu.CompilerParams(dimension_semantics=("arbitrary", "arbitrary"),
                                             vmem_limit_bytes=VMEM_LIMIT_V7X),
        name="mla_attn")(qf, kf, vt, mask)


_G0 = 8


def _merge_ffn_kernel(x_ref, ya_ref, yb_ref, ga_ref, gb_ref, st_ref, wa_ref, wb_ref, wo_ref, n1_ref, n2_ref, n3_ref,
                      wg_ref, wu_ref, cw_ref, cb_ref, wd_ref, y_ref, cs_ref, gbuf, *, tm, nseq):
    i = pl.program_id(1)
    seg = tm // nseq
    starts = [_G0 + j * (seg + _G0) for j in range(nseq)]
    if nseq == 1:
        @pl.when(i == 0)
        def _():
            gbuf[_G0 - 2:_G0, :] = st_ref[0]

        @pl.when(i > 0)
        def _():
            gbuf[_G0 - 2:_G0, :] = gbuf[_G0 + tm - 2:_G0 + tm, :]
    else:
        for j, r0 in enumerate(starts):
            gbuf[r0 - 2:r0, :] = st_ref[j]

    parts = [(0, tm // 2), (tm // 2, tm)] if nseq == 1 else [(0, tm)]

    mixes = []
    for lo, hi in parts:
        ya = jnp.concatenate([ya_ref[0, g, lo:hi, :] for g in range(N_PAIRS)], axis=1)
        za = jnp.dot(ya, wa_ref[...], preferred_element_type=F32)
        zb = jnp.dot(yb_ref[0, lo:hi, :], wb_ref[...], preferred_element_type=F32)
        mixes.append((jax.nn.sigmoid(ga_ref[0, lo:hi, :].astype(F32)) * za
                      + jax.nn.sigmoid(gb_ref[0, lo:hi, :].astype(F32)) * zb).astype(BF16))
    x1s, xns = [], []
    for (lo, hi), mix in zip(parts, mixes):
        mo = jnp.dot(mix, wo_ref[...], preferred_element_type=F32)
        x1s.append(x_ref[0, lo:hi, :] + _rms(mo, n1_ref[...]))
        xns.append(_rms(x1s[-1], n2_ref[...]).astype(BF16))

    ups = []
    for (lo, hi), xn in zip(parts, xns):
        gate = jnp.dot(xn, wg_ref[...], preferred_element_type=F32)
        if nseq == 1:
            gbuf[_G0 + lo:_G0 + hi, :] = gate
        else:
            for j, r0 in enumerate(starts):
                gbuf[r0:r0 + seg, :] = gate[j * seg:(j + 1) * seg]
        ups.append(jnp.dot(xn, wu_ref[...], preferred_element_type=F32))
    for j, r0 in enumerate(starts):
        cs_ref[j] = gbuf[r0 + seg - 2:r0 + seg, :]

    def conv(r0, rows):
        return (cw_ref[0:1, :] * gbuf[r0 - 2:r0 - 2 + rows, :] + cw_ref[1:2, :] * gbuf[r0 - 1:r0 - 1 + rows, :]
                + cw_ref[2:3, :] * gbuf[r0:r0 + rows, :])

    for (lo, hi), u, x1 in zip(parts, ups, x1s):
        if nseq == 1:
            c = conv(_G0 + lo, hi - lo)
        else:
            c = jnp.concatenate([conv(r0, seg) for r0 in starts], axis=0)
        hid = (jax.nn.gelu(c + cb_ref[...], approximate=True) * u).astype(BF16)
        f = jnp.dot(hid, wd_ref[...], preferred_element_type=F32)
        y_ref[0, lo:hi, :] = x1 + _rms(f, n3_ref[...])


def _merge_ffn(x, ya, yb, ga, gb, state, wa, wb, wo, n1, n2, n3, wg, wu, cw, cb, wd, *, tm, nseq=1):
    g, s, _ = x.shape
    assert nseq == 1 or s == tm
    tok = lambda w: pl.BlockSpec((1, tm, w), lambda a, b: (a, b, 0))
    per_group = pl.BlockSpec((nseq, CONV_W - 1, D_FF), lambda a, b: (a, 0, 0))
    weights = (wa, wb, wo, n1, n2, n3, wg, wu, cw, cb, wd)
    return pl.pallas_call(
        functools.partial(_merge_ffn_kernel, tm=tm, nseq=nseq), grid=(g, s // tm),
        in_specs=[tok(D_MODEL), pl.BlockSpec((1, N_PAIRS, tm, PAIR_W), lambda a, b: (a, 0, b, 0)), tok(D_VB),
                  tok(D_MODEL), tok(D_MODEL), per_group]
                 + [_resident(w.shape) for w in weights],
        out_specs=(tok(D_MODEL), per_group),
        out_shape=(jax.ShapeDtypeStruct((g, s, D_MODEL), F32),
                   jax.ShapeDtypeStruct((g * nseq, CONV_W - 1, D_FF), F32)),
        scratch_shapes=[pltpu.VMEM((tm + nseq * _G0, D_FF), F32)],
        compiler_params=pltpu.CompilerParams(dimension_semantics=("arbitrary", "arbitrary"),
                                             vmem_limit_bytes=VMEM_LIMIT_V7X),
        name="merge_ffn")(x, ya, yb, ga, gb, state, *weights)


def _rope_tables(pos, scale):
    inv = ROPE_BASE ** (-np.arange(_HALF, dtype=np.float64) / _HALF)
    ang = inv[:, None] * np.asarray(pos, np.float64)[None, :]
    return jnp.asarray(np.cos(ang) * scale, F32), jnp.asarray(np.sin(ang) * scale, F32)


def _band_bias_kernel(ext_ref, o_ref, *, tq, nk, q_major):
    w = ext_ref.shape[-1]
    qc = (nk - tq + lax.broadcasted_iota(jnp.int32, (nk, tq), 1)) // CHUNK
    kc = lax.broadcasted_iota(jnp.int32, (nk, tq), 0) // CHUNK
    in_band = (kc >= qc - LEFT_CHUNKS) & (kc <= qc)
    for h in range(ext_ref.shape[0]):
        toeplitz = pltpu.roll(jnp.broadcast_to(ext_ref[h], (nk, w)), 0, 1, stride=1, stride_axis=0)[:, :tq]
        bias = jnp.where(in_band, toeplitz, NEG)
        o_ref[h] = bias.T if q_major else bias


def _band_bias(table, tq, nk, q_major=False):
    w = nk + tq
    d_lo, d_hi = 1 - tq, w - tq
    assert -MAX_REL <= d_lo and d_hi >= MAX_REL and w % 128 == 0
    h = table.shape[0]
    ext = jnp.concatenate([table[:, d_lo + MAX_REL:], jnp.broadcast_to(table[:, -1:], (h, d_hi - MAX_REL))], axis=1)
    ext = (jnp.roll(ext, -(nk - 1), axis=1) * LOG2E).reshape(h, 1, w)
    return pl.pallas_call(
        functools.partial(_band_bias_kernel, tq=tq, nk=nk, q_major=q_major),
        out_shape=jax.ShapeDtypeStruct((h, tq, nk) if q_major else (h, nk, tq), F32),
        compiler_params=pltpu.CompilerParams(vmem_limit_bytes=VMEM_LIMIT_V7X),
        name="band_bias")(ext)


def _chunk_masks(tk, tq):
    kc = jnp.arange(tq)[:, None] // CHUNK
    qc = jnp.arange(tq)[None, :] // CHUNK
    return jnp.where(kc <= qc, 0.0, NEG).astype(F32).reshape(tq // tk, tk, tq)


def _scaled_cast_kernel(w_ref, o_ref, *, rows, n_scaled, scale):
    r = pl.program_id(0) * rows + lax.broadcasted_iota(jnp.int32, (rows, 1), 0)
    o_ref[...] = (w_ref[0] * jnp.where(r < n_scaled, scale, 1.0)).astype(BF16)


def _scaled_cast(w, n_scaled, scale, *, rows):
    _, d, n = w.shape
    return pl.pallas_call(
        functools.partial(_scaled_cast_kernel, rows=rows, n_scaled=n_scaled, scale=scale), grid=(d // rows,),
        in_specs=[pl.BlockSpec((1, rows, n), lambda a: (0, a, 0))],
        out_specs=pl.BlockSpec((rows, n), lambda a: (a, 0)),
        out_shape=jax.ShapeDtypeStruct((d, n), BF16),
        compiler_params=pltpu.CompilerParams(dimension_semantics=("arbitrary",)),
        name="cast_w_in")(w)


def _prep_weights(w_in, w_uk, w_uv):
    o = np.cumsum((0,) + IN_SIZES)
    d = w_in.shape[1]
    w_t = _scaled_cast(jnp.swapaxes(w_in, 1, 2), int(o[1]), A_SCALE * LOG2E, rows=928)
    row = lambda j: w_t[o[j]:o[j + 1]]
    qn = row(3).reshape(H_B, DH_NOPE, d)
    qr = row(4).reshape(H_B, DH_ROPE, d)
    pad = jnp.zeros((H_B, HEAD_PAD - DH_NOPE - DH_ROPE, d), BF16).at[0].set(row(6))
    wqt = jnp.concatenate([jnp.concatenate([qn, qr, pad], axis=1).reshape(D_QF, d), row(0)], axis=0)
    wm = jnp.concatenate([row(1), row(5), row(7), row(8)], axis=0).T
    wvat = row(2)
    wk = jnp.concatenate([w_uk, jnp.zeros((D_C, H_B, HEAD_PAD - DH_NOPE), F32)], axis=2).reshape(D_C, D_QF).astype(BF16)
    place = jnp.concatenate([jnp.zeros((DH_ROPE, DH_NOPE), F32), jnp.eye(DH_ROPE, dtype=F32),
                             jnp.zeros((DH_ROPE, HEAD_PAD - DH_NOPE - DH_ROPE), F32)], axis=1)
    ek = jnp.tile(place, (1, H_B)).astype(BF16)
    wvt = w_uv.reshape(D_C, D_VB).T.astype(BF16)
    return wm, wqt, wvat, wk, ek, wvt


def kernel(x_prompt, x_sample, cache_a_k, cache_a_v, cache_mla_ckv, cache_mla_krope, state_ffn_conv, norm_mix_pre,
           norm_mix_post, w_in, rel_bias_table, kv_norm, w_uk, w_uv, w_branch_a, w_branch_b, w_out, norm_ffn_pre,
           norm_ffn_post, w_ffn_gate, w_ffn_up, conv_w, conv_b, w_ffn_down):
    assert w_in.shape[0] == 1, "single layer"
    b, s, _ = x_prompt.shape
    db, t, _ = x_sample.shape
    past = cache_mla_ckv.shape[2]
    wcache = cache_a_k.shape[2]
    keep = min(A_WINDOW, s)
    tm1 = 512
    assert keep == tm1 and db * t == tm1 and wcache == A_WINDOW

    wm, wqt, wvat, wk, ek, wvt = _prep_weights(w_in, w_uk[0], w_uv[0])
    row = lambda v: v.reshape(1, -1)
    proj_w = (row(norm_mix_pre[0]), wm, wqt, wvat, row(kv_norm[0]), wk, wvt)
    ffn_w = (w_branch_a[0].astype(BF16), w_branch_b[0].astype(BF16), w_out[0].astype(BF16), row(norm_mix_post[0]),
             row(norm_ffn_pre[0]), row(norm_ffn_post[0]), w_ffn_gate[0].astype(BF16), w_ffn_up[0].astype(BF16),
             conv_w[0], row(conv_b[0]), w_ffn_down[0].astype(BF16))
    table = rel_bias_table[0]

    pos = np.arange(s)
    tabs = _rope_tables(pos, MLA_SCALE * LOG2E) + _rope_tables(pos, 1.0)
    t_mla = tm1
    qa, ka, vat, kaf, vaf, qf, ckv, kr, ga, gb, kf, vt = _in_proj(x_prompt, *proj_w, *tabs, tm=tm1)
    ya = _band_prompt(qa, ka, vat, _band_bias(table, 256, A_WINDOW + 256), tm=tm1, tq=256)
    yb = _mla_attn(qf, kf, vt, _chunk_masks(t_mla, t_mla), tq=t_mla, tk=t_mla, causal=True)
    y_prompt, conv_p = _merge_ffn(x_prompt, ya, yb, ga, gb, jnp.zeros((b, CONV_W - 1, D_FF), F32), *ffn_w, tm=512)

    pos_s = np.tile(past + np.arange(t), db)
    tabs_s = _rope_tables(pos_s, MLA_SCALE * LOG2E) + _rope_tables(pos_s, 1.0)
    qa2, ka2, vat2, kaf2, vaf2, qf2, ckv2, kr2, ga2, gb2, _, _ = _in_proj(x_sample.reshape(1, db * t, D_MODEL), *proj_w,
                                                                    *tabs_s, tm=tm1)
    per_seq = lambda v: v.reshape(db, t, v.shape[-1])
    ckv2, kr2 = per_seq(ckv2[0]), per_seq(kr2[0].T)
    cached_ft = lambda c: jnp.transpose(c[0], (0, 2, 3, 1)).reshape(db, D_A, wcache)
    new_ft = lambda v: jnp.transpose(v[0].reshape(D_A, db, t), (1, 0, 2))
    kc, vc, kn, vn = cached_ft(cache_a_k), cached_ft(cache_a_v), new_ft(kaf2), new_ft(vaf2)
    qa2 = jnp.transpose(qa2[0].reshape(N_PAIRS, PAIR_W, db, t), (0, 2, 3, 1))
    ya2, new_k, new_v = _band_sample(qa2, kc, kn, vc, vn, _band_bias(table, t, wcache + t, q_major=True))
    kr_all = jnp.concatenate([cache_mla_krope[0], kr2], axis=1)
    kf2, vt2 = _kv_up(cache_mla_ckv[0], ckv2, kr_all, wk, ek, wvt)
    qf2 = jnp.transpose(qf2[0, 0].reshape(D_QF, db, t), (1, 0, 2))[:, None]
    yb2 = _mla_attn(qf2, kf2, vt2, jnp.zeros((1, 8, 128), F32), tq=t, tk=past + t, causal=False, pairs=H_B // 2)
    y_sample, conv_s = _merge_ffn(x_sample.reshape(1, db * t, D_MODEL), ya2.reshape(1, N_PAIRS, db * t, PAIR_W),
                                  yb2.reshape(1, db * t, D_VB), ga2, gb2, state_ffn_conv[0], *ffn_w,
                                  tm=db * t, nseq=db)
    y_sample = y_sample.reshape(db, t, D_MODEL)

    heads_t = lambda v: jnp.transpose(v.reshape(1, v.shape[0], H_A, DH_A, v.shape[2]), (0, 1, 4, 2, 3))
    return (y_prompt, y_sample,
            heads_t(kaf), heads_t(vaf), ckv[None], jnp.swapaxes(kr, 1, 2)[None], conv_p[None],
            heads_t(new_k), heads_t(new_v), ckv2[None], kr2[None], conv_s[None])
```
